```python
import jax, jax.numpy as jnp
from jax import lax
import numpy as np

D_MODEL = 2048
BATCH = 4
SEQ = 2048
DEPTH = 1
DEC_BATCH = 128
DEC_SEQ = 4
PAST_LEN = 16384
PAGE_SIZE = 128

HG_HEADS = 16
HG_KDIM = 128
HG_VDIM = D_MODEL // HG_HEADS
HG_CHUNK = 64
ML_HEADS = 8
ML_VDIM = D_MODEL // ML_HEADS
ML_QKDIM = ML_VDIM // 2
ML_CHUNK = 64
D_FF = ((8 * D_MODEL // 3 + 255) // 256) * 256
CONV_W = 3
EPS = 1e-6

HG_KW = HG_HEADS * HG_KDIM
HG_VW = HG_HEADS * HG_VDIM
ML_QKW = ML_HEADS * ML_QKDIM
ML_VW = ML_HEADS * ML_VDIM
IN_SPLITS = (HG_KW, HG_KW, HG_VW, HG_VW, ML_QKW, ML_QKW, ML_VW, ML_HEADS, ML_HEADS, ML_VW, D_MODEL, D_MODEL)
D_IN = sum(IN_SPLITS)

kernel_name = 'hgrn2_mlstm_convffn_adaln_step'


def rms_norm(x, w=None):
    x32 = x.astype(jnp.float32)
    y = x32 * lax.rsqrt(jnp.mean(x32 * x32, axis=-1, keepdims=True) + EPS)
    if w is not None:
        y = y * w.astype(jnp.float32)
    return y.astype(x.dtype)


def head_rms_norm(x, w):
    B, T, H, Dh = x.shape
    y = x * lax.rsqrt(jnp.mean(x * x, axis=-1, keepdims=True) + EPS)
    return y.reshape(B, T, H * Dh) * w.astype(jnp.float32)


def to_chunks(a, L, fill):
    B, T = a.shape[:2]
    pad = (-T) % L
    a = jnp.pad(a, [(0, 0), (0, pad)] + [(0, 0)] * (a.ndim - 2), constant_values=fill)
    n = a.shape[1] // L
    a = a.reshape((B, n, L) + a.shape[2:])
    perm = (1, 0, 3, 2) + tuple(range(4, a.ndim))
    return a.transpose(perm)


def from_chunks(o, T):
    n, B, H, L, V = o.shape
    return o.transpose(1, 0, 3, 2, 4).reshape(B, n * L, H, V)[:, :T]


def hgrn2_chunked(q, k, logf, v, S0):
    T = q.shape[1]
    L = min(HG_CHUNK, T)
    causal = jnp.tril(jnp.ones((L, L), dtype=bool))

    def step(S, blk):
        qc, kc, gc, vc = blk
        b = jnp.cumsum(gc, axis=2)
        diff = jnp.where(causal[None, None, :, :, None], b[:, :, :, None, :] - b[:, :, None, :, :], -jnp.inf)
        A = jnp.einsum('bhtk,bhtsk,bhsk->bhts', qc, jnp.exp(diff), kc)
        o = jnp.einsum('bhtk,bhkv->bhtv', qc * jnp.exp(b), S) + jnp.einsum('bhts,bhsv->bhtv', A, vc)
        bL = b[:, :, -1:, :]
        S = jnp.exp(bL[:, :, 0, :])[..., None] * S + jnp.einsum('bhsk,bhsv->bhkv', kc * jnp.exp(bL - b), vc)
        return S, o

    S, o = lax.scan(step, S0, (to_chunks(q, L, 0.0), to_chunks(k, L, 0.0), to_chunks(logf, L, 0.0), to_chunks(v, L, 0.0)))
    return from_chunks(o, T), S


def mlstm_chunked(q, k, v, ig, lf, C0, n0, m0):
    T = q.shape[1]
    L = min(ML_CHUNK, T)
    causal = jnp.tril(jnp.ones((L, L), dtype=bool))

    def step(carry, blk):
        C, n, m = carry
        qc, kc, vc, igc, lfc = blk
        b = jnp.cumsum(lfc, axis=-1)
        Dm = jnp.where(causal, b[..., :, None] - b[..., None, :] + igc[..., None, :], -jnp.inf)
        mt = jnp.maximum(b + m[..., None], jnp.max(Dm, axis=-1))
        inter = jnp.exp(b + m[..., None] - mt)
        Sc = jnp.einsum('bhtd,bhsd->bhts', qc, kc) * jnp.exp(Dm - mt[..., None])
        num = inter[..., None] * jnp.einsum('bhtd,bhdv->bhtv', qc, C) + jnp.einsum('bhts,bhsv->bhtv', Sc, vc)
        den = inter * jnp.einsum('bhtd,bhd->bht', qc, n) + jnp.sum(Sc, axis=-1)
        h = num / jnp.maximum(jnp.abs(den), jnp.exp(-mt))[..., None]
        mL = mt[..., -1]
        decL = jnp.exp(b[..., -1] + m - mL)
        wk = jnp.exp(b[..., -1:] - b + igc - mL[..., None])
        C = decL[..., None, None] * C + jnp.einsum('bhs,bhsd,bhsv->bhdv', wk, kc, vc)
        n = decL[..., None] * n + jnp.einsum('bhs,bhsd->bhd', wk, kc)
        return (C, n, mL), h

    (C, n, m), h = lax.scan(step, (C0, n0, m0), (to_chunks(q, L, 0.0), to_chunks(k, L, 0.0), to_chunks(v, L, 0.0),
                                                 to_chunks(ig, L, -jnp.inf), to_chunks(lf, L, 0.0)))
    return from_chunks(h, T), C, n, m


def token_mixer(h, st_hg, st_C, st_n, st_m, w_in, lb, hg_norm_w, ml_i_bias, ml_f_bias, ml_norm_w, w_out):
    f32 = jnp.float32
    B, T, _ = h.shape
    proj = h @ w_in
    hq, hf, hi, hg, mq, mk, mv, mi, mf, mo, ga, gb = jnp.split(proj, list(np.cumsum(IN_SPLITS)[:-1]), axis=-1)
    lb = lb.reshape(HG_HEADS, HG_KDIM)
    z = hf.astype(f32).reshape(B, T, HG_HEADS, HG_KDIM)
    logf = jnp.log(lb + (1.0 - lb) * jax.nn.sigmoid(z))
    kk = (1.0 - lb) * jax.nn.sigmoid(-z)
    q = jax.nn.silu(hq.astype(f32)).reshape(B, T, HG_HEADS, HG_KDIM) * (HG_KDIM ** -0.5)
    vv = hi.astype(f32).reshape(B, T, HG_HEADS, HG_VDIM)
    o_hg, S_new = hgrn2_chunked(q, kk, logf, vv, st_hg.astype(f32))
    y_a = head_rms_norm(o_hg, hg_norm_w) * jax.nn.silu(hg.astype(f32))
    mq_ = mq.astype(f32).reshape(B, T, ML_HEADS, ML_QKDIM) * (ML_QKDIM ** -0.5)
    mk_ = mk.astype(f32).reshape(B, T, ML_HEADS, ML_QKDIM)
    mv_ = mv.astype(f32).reshape(B, T, ML_HEADS, ML_VDIM)
    ig = mi.astype(f32) + ml_i_bias.astype(f32)
    lf = jax.nn.log_sigmoid(mf.astype(f32) + ml_f_bias.astype(f32))
    h_ml, C_new, n_new, m_new = mlstm_chunked(mq_, mk_, mv_, ig, lf, st_C.astype(f32), st_n.astype(f32), st_m.astype(f32))
    y_b = head_rms_norm(h_ml, ml_norm_w) * jax.nn.sigmoid(mo.astype(f32))
    merged = jax.nn.sigmoid(ga.astype(f32)) * y_a + jax.nn.sigmoid(gb.astype(f32)) * y_b
    out = merged.astype(h.dtype) @ w_out
    return (out, S_new.astype(st_hg.dtype), C_new.astype(st_C.dtype), n_new.astype(st_n.dtype), m_new.astype(st_m.dtype))


def conv_ffn(h, conv_state, conv_w, conv_b, w_up, w_down):
    T = h.shape[1]
    up = h @ w_up
    a, g = up[..., :D_FF], up[..., D_FF:]
    full = jnp.concatenate([conv_state.astype(a.dtype), a], axis=1)
    a_conv = sum(conv_w[j] * full[:, j:j + T] for j in range(CONV_W)) + conv_b
    y = jax.nn.gelu(a_conv, approximate=False) * g
    return y @ w_down, full[:, -(CONV_W - 1):]


def trunk(x, c, st_hg, st_C, st_n, st_m, st_conv, ada_w, ada_b, w_in, hg_lb_logits, hg_norm_w, ml_i_bias,
          ml_f_bias, ml_norm_w, w_out, conv_w, conv_b, w_up, w_down, final_norm_w):
    lbs = jnp.cumsum(jax.nn.softmax(hg_lb_logits.astype(jnp.float32), axis=0), axis=0)
    hg_l, C_l, n_l, m_l, cv_l = [], [], [], [], []
    for l in range(DEPTH):
        mod = jax.nn.silu(c) @ ada_w[l] + ada_b[l]
        sh1, sc1, g1, sh2, sc2, g2 = jnp.split(mod[:, None, :], 6, axis=-1)
        h = rms_norm(x) * (1.0 + sc1) + sh1
        mix, s_hg, s_C, s_n, s_m = token_mixer(h, st_hg[l], st_C[l], st_n[l], st_m[l], w_in[l], lbs[l],
                                               hg_norm_w[l], ml_i_bias[l], ml_f_bias[l], ml_norm_w[l], w_out[l])
        x = x + g1 * mix
        h = rms_norm(x) * (1.0 + sc2) + sh2
        ff, s_cv = conv_ffn(h, st_conv[l], conv_w[l], conv_b[l], w_up[l], w_down[l])
        x = x + g2 * ff
        hg_l.append(s_hg); C_l.append(s_C); n_l.append(s_n); m_l.append(s_m); cv_l.append(s_cv)
    y = rms_norm(x, final_norm_w)
    return y, jnp.stack(hg_l), jnp.stack(C_l), jnp.stack(n_l), jnp.stack(m_l), jnp.stack(cv_l)


def setup_inputs(seed: int = 0) -> dict:
    key = jax.random.key(seed)
    ks = jax.random.split(key, 24)
    nrm = lambda k, shape, s: jax.random.normal(k, shape, jnp.float32) * s
    D = D_MODEL
    return {
        'x_prompt': nrm(ks[0], (BATCH, SEQ, D), 1.0),
        'x_sample': nrm(ks[1], (DEC_BATCH, DEC_SEQ, D), 1.0),
        'c_prompt': nrm(ks[2], (BATCH, D), 1.0),
        'c_sample': nrm(ks[3], (DEC_BATCH, D), 1.0),
        'state_hgrn': nrm(ks[4], (DEPTH, DEC_BATCH, HG_HEADS, HG_KDIM, HG_VDIM), 0.5),
        'state_mlstm_C': nrm(ks[5], (DEPTH, DEC_BATCH, ML_HEADS, ML_QKDIM, ML_VDIM), 0.5),
        'state_mlstm_n': nrm(ks[6], (DEPTH, DEC_BATCH, ML_HEADS, ML_QKDIM), 0.5),
        'state_mlstm_m': nrm(ks[7], (DEPTH, DEC_BATCH, ML_HEADS), 1.0),
        'state_conv': nrm(ks[8], (DEPTH, DEC_BATCH, CONV_W - 1, D_FF), 1.0),
        'ada_w': nrm(ks[9], (DEPTH, D, 6 * D), 0.5 * D ** -0.5),
        'ada_b': nrm(ks[10], (DEPTH, 6 * D), 0.02),
        'w_in': nrm(ks[11], (DEPTH, D, D_IN), D ** -0.5),
        'hg_lb_logits': nrm(ks[12], (DEPTH + 1, HG_KW), 0.1),
        'hg_norm_w': 1.0 + nrm(ks[13], (DEPTH, HG_VW), 0.02),
        'ml_i_bias': nrm(ks[14], (DEPTH, ML_HEADS), 0.1),
        'ml_f_bias': jnp.linspace(3.0, 6.0, ML_HEADS, dtype=jnp.float32)[None, :] + nrm(ks[15], (DEPTH, ML_HEADS), 0.1),
        'ml_norm_w': 1.0 + nrm(ks[16], (DEPTH, ML_VW), 0.02),
        'w_out': nrm(ks[17], (DEPTH, D, D), D ** -0.5),
        'conv_w': nrm(ks[18], (DEPTH, CONV_W, D_FF), CONV_W ** -0.5),
        'conv_b': nrm(ks[19], (DEPTH, D_FF), 0.02),
        'w_up': nrm(ks[20], (DEPTH, D, 2 * D_FF), D ** -0.5),
        'w_down': nrm(ks[21], (DEPTH, D_FF, D), D_FF ** -0.5),
        'final_norm_w': 1.0 + nrm(ks[22], (D,), 0.02),
    }


def reference(x_prompt, x_sample, c_prompt, c_sample, state_hgrn, state_mlstm_C, state_mlstm_n, state_mlstm_m,
              state_conv, ada_w, ada_b, w_in, hg_lb_logits, hg_norm_w, ml_i_bias, ml_f_bias, ml_norm_w, w_out,
              conv_w, conv_b, w_up, w_down, final_norm_w):
    dt = x_prompt.dtype
    z_hg = jnp.zeros((DEPTH, BATCH, HG_HEADS, HG_KDIM, HG_VDIM), dt)
    z_C = jnp.zeros((DEPTH, BATCH, ML_HEADS, ML_QKDIM, ML_VDIM), dt)
    z_n = jnp.zeros((DEPTH, BATCH, ML_HEADS, ML_QKDIM), dt)
    z_m = jnp.zeros((DEPTH, BATCH, ML_HEADS), dt)
    z_cv = jnp.zeros((DEPTH, BATCH, CONV_W - 1, D_FF), dt)
    y_prompt, hg_p, C_p, n_p, m_p, cv_p = trunk(
        x_prompt, c_prompt, z_hg, z_C, z_n, z_m, z_cv, ada_w, ada_b, w_in, hg_lb_logits, hg_norm_w,
        ml_i_bias, ml_f_bias, ml_norm_w, w_out, conv_w, conv_b, w_up, w_down, final_norm_w)
    y_sample, hg_s, C_s, n_s, m_s, cv_s = trunk(
        x_sample, c_sample, state_hgrn, state_mlstm_C, state_mlstm_n, state_mlstm_m, state_conv, ada_w, ada_b,
        w_in, hg_lb_logits, hg_norm_w, ml_i_bias, ml_f_bias, ml_norm_w, w_out, conv_w, conv_b, w_up, w_down,
        final_norm_w)
    return (y_prompt, y_sample, hg_p, hg_s, C_p, C_s, n_p, n_s, m_p, m_s, cv_p, cv_s)
```

```python
import functools

import numpy as np
import jax
import jax.numpy as jnp
from jax import lax
from jax.experimental import pallas as pl
from jax.experimental.pallas import tpu as pltpu

F32 = jnp.float32
BF16 = jnp.bfloat16

EPS = 1e-6
CONV_W = 3
LANES = 128
SUBLANES = 8
VMEM_LIMIT_BYTES = 58 * 1024 * 1024

HG_CHUNK = 64
HG_LEVELS = (32, 16, 8, 4, 2, 1)
ML_CHUNK = 128
GROUP = 4


def _cparams(*sem):
    return pltpu.CompilerParams(dimension_semantics=sem, vmem_limit_bytes=VMEM_LIMIT_BYTES)


def _dot(a, b):
    return jnp.dot(a, b, preferred_element_type=F32)


def _dot_nt(a, b):
    return lax.dot_general(a, b, (((1,), (1,)), ((), ())), preferred_element_type=F32)


def _dot_tn(a, b):
    return lax.dot_general(a, b, (((0,), (0,)), ((), ())), preferred_element_type=F32)


def _split3(x):
    x1 = x.astype(BF16)
    r1 = x - x1.astype(F32)
    x2 = r1.astype(BF16)
    x3 = (r1 - x2.astype(F32)).astype(BF16)
    return x1, x2, x3


def _exact_left_mul(w, x):
    x1, x2, x3 = _split3(x)
    return _dot(w, x1) + _dot(w, x2) + _dot(w, x3)


def _rows(ref, r0, n):
    if ref.shape[0] == 1:
        return ref[...]
    return ref[pl.ds(r0, n), :]


def _norm_mod(x, sc, sh):
    ms = jnp.mean(x * x, axis=-1, keepdims=True)
    return x * lax.rsqrt(ms + EPS) * (1.0 + sc) + sh


def _head_norm(o, w):
    return o * lax.rsqrt(jnp.mean(o * o, axis=-1, keepdims=True) + EPS) * w


def _lower_bound(lg):
    l0, l1 = lg[0:1, :], lg[1:2, :]
    m = jnp.maximum(l0, l1)
    e0, e1 = jnp.exp(l0 - m), jnp.exp(l1 - m)
    return e0 / (e0 + e1)


def _group_last(x, t):
    return jnp.where(t == 3, x, jnp.where(t == 2, pltpu.roll(x, 7, 0),
                     jnp.where(t == 1, pltpu.roll(x, 6, 0), pltpu.roll(x, 5, 0))))


def _mod_kernel(c_ref, w_ref, b_ref, o_ref):
    c = c_ref[...]
    s = (c * jax.nn.sigmoid(c)).astype(BF16)
    o_ref[...] = _dot(s, w_ref[...].astype(BF16)) + b_ref[...]


def _mod_call(c_all, ada_w, ada_b):
    mp, d = c_all.shape
    n = ada_w.shape[1]
    tn = 512
    return pl.pallas_call(
        _mod_kernel,
        grid=(n // tn,),
        in_specs=[pl.BlockSpec((mp, d), lambda j: (0, 0)),
                  pl.BlockSpec((d, tn), lambda j: (0, j)),
                  pl.BlockSpec((1, tn), lambda j: (0, j))],
        out_specs=pl.BlockSpec((mp, tn), lambda j: (0, j)),
        out_shape=jax.ShapeDtypeStruct((mp, n), F32),
        compiler_params=_cparams("arbitrary"),
        name="mod",
    )(c_all, ada_w, ada_b)


def _inproj_kernel(x_ref, sh_ref, sc_ref, w_ref, wg_ref, o_ref, og_ref, h_ref, *, tm, rc):
    @pl.when(pl.program_id(1) == 0)
    def _():
        def body(r, carry):
            r0 = pl.multiple_of(r * rc, rc)
            h = _norm_mod(x_ref[pl.ds(r0, rc), :], _rows(sc_ref, r0, rc), _rows(sh_ref, r0, rc))
            hb = h.astype(BF16)
            h_ref[pl.ds(r0, rc), :] = hb
            og_ref[pl.ds(r0, rc), :] = _dot(hb, wg_ref[...])
            return carry
        lax.fori_loop(0, tm // rc, body, 0)

    o_ref[...] = _dot(h_ref[...], w_ref[...])


def _mod_spec(mod, tm, col, tiles_per_seq):
    d = mod.shape[-1] // 6
    if mod.ndim == 3:
        return pl.BlockSpec((None, 1, d), lambda i, *_: (i // tiles_per_seq, 0, col))
    return pl.BlockSpec((tm, d), lambda i, *_: (i, col))


def _inproj_call(x, mod, w_main, w_gate, tm, tiles_per_seq):
    m, d = x.shape
    n = w_main.shape[1]
    tn = 1024
    kern = functools.partial(_inproj_kernel, tm=tm, rc=256)
    return pl.pallas_call(
        kern,
        grid=(m // tm, n // tn),
        in_specs=[pl.BlockSpec((tm, d), lambda i, j: (i, 0)),
                  _mod_spec(mod, tm, 0, tiles_per_seq),
                  _mod_spec(mod, tm, 1, tiles_per_seq),
                  pl.BlockSpec((d, tn), lambda i, j: (0, j)),
                  pl.BlockSpec((d, LANES), lambda i, j: (0, 0))],
        out_specs=[pl.BlockSpec((tm, tn), lambda i, j: (i, j)),
                   pl.BlockSpec((tm, LANES), lambda i, j: (i, 0))],
        out_shape=[jax.ShapeDtypeStruct((m, n), F32), jax.ShapeDtypeStruct((m, LANES), F32)],
        scratch_shapes=[pltpu.VMEM((tm, d), BF16)],
        compiler_params=_cparams("arbitrary", "arbitrary"),
        name="inproj",
    )(x, mod, mod, w_main, w_gate)


def _hgrn_constants():
    L = HG_CHUNK
    tri = np.tril(np.ones((L, L), np.float32))
    t = np.arange(L)
    blocks = [tri]
    masks = []
    for h in HG_LEVELS:
        mid = (t // (2 * h)) * (2 * h) + h - 1
        blocks.append(tri - tri[mid])
        masks.append((t[:, None] // (2 * h) == t[None, :] // (2 * h)).astype(np.float32))
    blocks.append(tri[L - 1][None, :] - tri)
    masks.append(np.eye(L, dtype=np.float32))
    return jnp.asarray(np.concatenate(blocks, 0), BF16), jnp.asarray(np.stack(masks, 0), F32)


def _hgrn_gates(hq, z, lb):
    omlb = 1.0 - lb
    logf = jnp.log(lb + omlb * jax.nn.sigmoid(z))
    kk = omlb * jax.nn.sigmoid(-z)
    q = hq * jax.nn.sigmoid(hq) * (hq.shape[-1] ** -0.5)
    return q, kk, logf


def _hgrn_p_kernel(hq_ref, hf_ref, hi_ref, hg_ref, lg_ref, nw_ref, wall_ref, mask_ref,
                   ya_ref, so_ref, st_ref, *, n_chunks):
    L = HG_CHUNK
    c = pl.program_id(2)

    @pl.when(c == 0)
    def _():
        st_ref[...] = jnp.zeros_like(st_ref)

    lb = _lower_bound(lg_ref[...])
    nw = nw_ref[...]
    wall = wall_ref[...]
    rowid = lax.broadcasted_iota(jnp.int32, (L, hq_ref.shape[1]), 0)
    n_lv = len(HG_LEVELS)

    for ci in range(n_chunks):
        r0 = ci * L
        q, kk, logf = _hgrn_gates(hq_ref[r0:r0 + L, :], hf_ref[r0:r0 + L, :], lb)
        v = hi_ref[r0:r0 + L, :]
        vb = v.astype(BF16)
        d = _exact_left_mul(wall, logf)
        b = d[0:L]
        a = mask_ref[n_lv] * _dot_nt(q.astype(BF16), kk.astype(BF16))
        for li, h in enumerate(HG_LEVELS):
            e = jnp.exp(-jnp.abs(d[(li + 1) * L:(li + 2) * L]))
            second = (rowid & h) != 0
            p = jnp.where(second, q, kk) * e
            qh = jnp.where(second, p, 0.0).astype(BF16)
            kh = jnp.where(second, 0.0, p).astype(BF16)
            a = a + mask_ref[li] * _dot_nt(qh, kh)
        qt = (q * jnp.exp(b)).astype(BF16)
        kt = (kk * jnp.exp(d[(n_lv + 1) * L:(n_lv + 2) * L])).astype(BF16)
        st = st_ref[...]
        o = _dot(a.astype(BF16), vb) + _dot_nt(qt, st.astype(BF16))
        st_ref[...] = st * jnp.exp(b[L - 1:L, :]) + _dot_tn(vb, kt)
        hg = hg_ref[r0:r0 + L, :]
        ya_ref[r0:r0 + L, :] = _head_norm(o, nw) * (hg * jax.nn.sigmoid(hg))

    @pl.when(c == pl.num_programs(2) - 1)
    def _():
        so_ref[...] = st_ref[...].T


def _hgrn_p_call(proj, lb_logits, norm_w, n_seq, seq_len, heads, kdim, col0, rows_per_step):
    wall, masks = _hgrn_constants()
    nc = seq_len // rows_per_step
    kern = functools.partial(_hgrn_p_kernel, n_chunks=rows_per_step // HG_CHUNK)

    def col(k):
        return pl.BlockSpec((rows_per_step, kdim), lambda b, h, c: (b * nc + c, col0 + k * heads + h))

    return pl.pallas_call(
        kern,
        grid=(n_seq, heads, nc),
        in_specs=[col(0), col(1), col(2), col(3),
                  pl.BlockSpec((2, kdim), lambda b, h, c: (0, h)),
                  pl.BlockSpec((1, kdim), lambda b, h, c: (0, h)),
                  pl.BlockSpec(wall.shape, lambda b, h, c: (0, 0)),
                  pl.BlockSpec(masks.shape, lambda b, h, c: (0, 0, 0))],
        out_specs=[pl.BlockSpec((rows_per_step, kdim), lambda b, h, c: (b * nc + c, h)),
                   pl.BlockSpec((None, None, kdim, kdim), lambda b, h, c: (b, h, 0, 0))],
        out_shape=[jax.ShapeDtypeStruct((n_seq * seq_len, heads * kdim), F32),
                   jax.ShapeDtypeStruct((n_seq, heads, kdim, kdim), F32)],
        scratch_shapes=[pltpu.VMEM((kdim, kdim), F32)],
        compiler_params=_cparams("arbitrary", "arbitrary", "arbitrary"),
        name="hgrn_prompt",
    )(proj, proj, proj, proj, lb_logits, norm_w, wall, masks)


def _hgrn_s_kernel(hq_ref, hf_ref, hi_ref, hg_ref, lg_ref, nw_ref, s_ref, ya_ref, so_ref, *, n_pairs):
    kdim = hq_ref.shape[1]
    lb = _lower_bound(lg_ref[...])
    nw = nw_ref[...]
    rowid = lax.broadcasted_iota(jnp.int32, (SUBLANES, kdim), 0)
    t = rowid & (GROUP - 1)
    first = rowid < GROUP
    ones_a = jnp.where(jnp.logical_and(jnp.logical_not(first), t <= 2), 1.0, 0.0)
    ones_b = jnp.where(jnp.logical_and(first, t <= 2), 1.0, 0.0)

    def pair(p, carry):
        r0 = pl.multiple_of(p * SUBLANES, SUBLANES)
        q, kk, logf = _hgrn_gates(hq_ref[pl.ds(r0, SUBLANES), :], hf_ref[pl.ds(r0, SUBLANES), :], lb)
        v = hi_ref[pl.ds(r0, SUBLANES), :]
        b = logf
        for dlt in range(1, GROUP):
            b = b + jnp.where(t >= dlt, pltpu.roll(logf, dlt, 0), 0.0)
        o = jnp.sum(q * kk, axis=1, keepdims=True) * v
        for dlt in range(1, GROUP):
            x = q * pltpu.roll(kk, dlt, 0) * jnp.exp(b - pltpu.roll(b, dlt, 0))
            a = jnp.sum(jnp.where(t >= dlt, x, 0.0), axis=1, keepdims=True)
            o = o + a * pltpu.roll(v, dlt, 0)
        b_last = _group_last(b, t)
        qt = (q * jnp.exp(b)).astype(BF16)
        kt = kk * jnp.exp(b_last - b)
        d1, d2, d3 = _split3(jnp.exp(b_last))
        swap = lambda y: pltpu.roll(y.astype(F32), GROUP, 0)
        dsplit = jnp.where(t == 0, swap(d1), jnp.where(t == 1, swap(d2), jnp.where(t == 2, swap(d3), 0.0)))
        for half, (is_mine, ones) in enumerate(((first, ones_a), (jnp.logical_not(first), ones_b))):
            s0 = s_ref[2 * p + half]
            lhs = jnp.where(is_mine, kt, dsplit).astype(BF16)
            rhs = jnp.concatenate([jnp.where(is_mine, v, 0.0), ones], axis=1).astype(BF16)
            upd = _dot_tn(lhs, rhs)
            so_ref[2 * p + half] = s0 * upd[:, kdim:] + upd[:, :kdim]
            o = o + jnp.where(is_mine, _dot(qt, s0.astype(BF16)), 0.0)
        hg = hg_ref[pl.ds(r0, SUBLANES), :]
        ya_ref[pl.ds(r0, SUBLANES), :] = _head_norm(o, nw) * (hg * jax.nn.sigmoid(hg))
        return carry

    lax.fori_loop(0, n_pairs, pair, 0)


def _hgrn_s_call(proj, lb_logits, norm_w, state, heads, kdim, col0, seqs_per_step):
    n_seq = state.shape[0]
    rows = seqs_per_step * GROUP
    kern = functools.partial(_hgrn_s_kernel, n_pairs=seqs_per_step // 2)

    def col(k):
        return pl.BlockSpec((rows, kdim), lambda i, h: (i, col0 + k * heads + h))

    st_spec = pl.BlockSpec((seqs_per_step, None, kdim, kdim), lambda i, h: (i, h, 0, 0))
    return pl.pallas_call(
        kern,
        grid=(n_seq // seqs_per_step, heads),
        in_specs=[col(0), col(1), col(2), col(3),
                  pl.BlockSpec((2, kdim), lambda i, h: (0, h)),
                  pl.BlockSpec((1, kdim), lambda i, h: (0, h)),
                  st_spec],
        out_specs=[pl.BlockSpec((rows, kdim), lambda i, h: (i, h)), st_spec],
        out_shape=[jax.ShapeDtypeStruct((n_seq * GROUP, heads * kdim), F32),
                   jax.ShapeDtypeStruct(state.shape, F32)],
        compiler_params=_cparams("arbitrary", "arbitrary"),
        name="hgrn_sample",
    )(proj, proj, proj, proj, lb_logits, norm_w, state)


def _lane_pick(x, lane, idx):
    return jnp.broadcast_to(jnp.sum(jnp.where(lane == idx, x, 0.0), axis=1, keepdims=True), x.shape)


def _mlstm_p_kernel(q_ref, k_ref, v_ref, og_ref, g_ref, bias_ref, nw_ref, tri_ref,
                    yb_ref, co_ref, no_ref, mo_ref, c_s, n_s, m_s, *, heads):
    L = ML_CHUNK
    h = pl.program_id(1)
    c = pl.program_id(2)

    @pl.when(c == 0)
    def _():
        c_s[...] = jnp.zeros_like(c_s)
        n_s[...] = jnp.zeros_like(n_s)
        m_s[...] = jnp.zeros_like(m_s)

    g = g_ref[...] + bias_ref[...]
    lane = lax.broadcasted_iota(jnp.int32, g.shape, 1)
    row = lax.broadcasted_iota(jnp.int32, g.shape, 0)
    b_all = _exact_left_mul(tri_ref[...], jax.nn.log_sigmoid(g))
    ig = _lane_pick(g, lane, h)
    b = _lane_pick(b_all, lane, heads + h)
    m_prev = m_s[...]
    gs = (ig - b).T
    dm = jnp.where(lane <= row, b + gs, -jnp.inf)
    mt = jnp.maximum(b + m_prev, jnp.max(dm, axis=1, keepdims=True))
    inter = jnp.exp(b + m_prev - mt)
    q = q_ref[...] * (q_ref.shape[1] ** -0.5)
    k = k_ref[...]
    vb = v_ref[...].astype(BF16)
    qb = q.astype(BF16)
    sc = _dot_nt(qb, k.astype(BF16)) * jnp.exp(dm - mt)
    c0 = c_s[...]
    n0 = n_s[...]
    num = inter[:, 0:1] * _dot(qb, c0.astype(BF16)) + _dot(sc.astype(BF16), vb)
    den = inter[:, 0:1] * jnp.sum(q * n0, axis=1, keepdims=True) + jnp.sum(sc, axis=1, keepdims=True)
    hh = num / jnp.maximum(jnp.abs(den), jnp.exp(-mt[:, 0:1]))
    og = og_ref[...]
    yb_ref[...] = _head_norm(hh, nw_ref[...]) * jax.nn.sigmoid(og)

    m_last = mt[L - 1:L, :]
    b_last = b[L - 1:L, :]
    dec = jnp.exp(b_last + m_prev - m_last)
    wk = jnp.exp(b_last - b + ig - m_last)
    kw = wk * k
    c_new = jnp.concatenate([dec, dec], axis=1) * c0 + _dot_tn(kw.astype(BF16), vb)
    n_new = dec * n0 + jnp.sum(kw, axis=0, keepdims=True)
    c_s[...] = c_new
    n_s[...] = n_new
    m_s[...] = m_last

    @pl.when(c == pl.num_programs(2) - 1)
    def _():
        co_ref[...] = c_new
        no_ref[...] = n_new
        mo_ref[...] = m_last


def _mlstm_p_call(proj, gates, bias, norm_w, n_seq, seq_len, heads, qk, vd, colq, colk, colv, colo):
    L = ML_CHUNK
    assert qk == L and qk == LANES
    nc = seq_len // L
    tri = jnp.asarray(np.tril(np.ones((L, L), np.float32)), BF16)
    kern = functools.partial(_mlstm_p_kernel, heads=heads)
    row = lambda b, h, c: b * nc + c
    return pl.pallas_call(
        kern,
        grid=(n_seq, heads, nc),
        in_specs=[pl.BlockSpec((L, qk), lambda b, h, c: (row(b, h, c), colq + h)),
                  pl.BlockSpec((L, qk), lambda b, h, c: (row(b, h, c), colk + h)),
                  pl.BlockSpec((L, vd), lambda b, h, c: (row(b, h, c), colv + h)),
                  pl.BlockSpec((L, vd), lambda b, h, c: (row(b, h, c), colo + h)),
                  pl.BlockSpec((L, LANES), lambda b, h, c: (row(b, h, c), 0)),
                  pl.BlockSpec((1, LANES), lambda b, h, c: (0, 0)),
                  pl.BlockSpec((1, vd), lambda b, h, c: (0, h)),
                  pl.BlockSpec((L, L), lambda b, h, c: (0, 0))],
        out_specs=[pl.BlockSpec((L, vd), lambda b, h, c: (row(b, h, c), h)),
                   pl.BlockSpec((None, None, qk, vd), lambda b, h, c: (b, h, 0, 0)),
                   pl.BlockSpec((None, None, 1, qk), lambda b, h, c: (b, h, 0, 0)),
                   pl.BlockSpec((None, None, 1, LANES), lambda b, h, c: (b, h, 0, 0))],
        out_shape=[jax.ShapeDtypeStruct((n_seq * seq_len, heads * vd), F32),
                   jax.ShapeDtypeStruct((n_seq, heads, qk, vd), F32),
                   jax.ShapeDtypeStruct((n_seq, heads, 1, qk), F32),
                   jax.ShapeDtypeStruct((n_seq, heads, 1, LANES), F32)],
        scratch_shapes=[pltpu.VMEM((qk, vd), F32), pltpu.VMEM((1, qk), F32), pltpu.VMEM((1, LANES), F32)],
        compiler_params=_cparams("arbitrary", "arbitrary", "arbitrary"),
        name="mlstm_prompt",
    )(proj, proj, proj, proj, gates, bias, norm_w, tri)


def _mlstm_s_kernel(q_ref, k_ref, v_ref, og_ref, g_ref, bias_ref, m_ref, nw_ref, c_ref, n_ref,
                    yb_ref, co_ref, no_ref, mo_ref, *, heads, n_pairs):
    h = pl.program_id(1)
    qk = q_ref.shape[1]
    nw = nw_ref[...]
    bias = bias_ref[...]
    lane = lax.broadcasted_iota(jnp.int32, (SUBLANES, LANES), 1)
    rowid = lax.broadcasted_iota(jnp.int32, (SUBLANES, LANES), 0)
    t = rowid & (GROUP - 1)
    first = rowid < GROUP
    first_col = lax.broadcasted_iota(jnp.int32, (SUBLANES, 1), 0) < GROUP

    def pair(p, carry):
        r0 = pl.multiple_of(p * SUBLANES, SUBLANES)
        g = g_ref[pl.ds(r0, SUBLANES), :] + bias
        ig = _lane_pick(g, lane, h)
        lf = _lane_pick(jax.nn.log_sigmoid(g), lane, heads + h)
        m_prev = _lane_pick(m_ref[pl.ds(r0, SUBLANES), :], lane, h)
        b = lf
        for dlt in range(1, GROUP):
            b = b + jnp.where(t >= dlt, pltpu.roll(lf, dlt, 0), 0.0)
        q = q_ref[pl.ds(r0, SUBLANES), :] * (qk ** -0.5)
        k = k_ref[pl.ds(r0, SUBLANES), :]
        v = v_ref[pl.ds(r0, SUBLANES), :]
        dms = [ig] + [jnp.where(t >= dlt, b - pltpu.roll(b, dlt, 0) + pltpu.roll(ig, dlt, 0), -jnp.inf)
                      for dlt in range(1, GROUP)]
        mt = jnp.maximum(b + m_prev, functools.reduce(jnp.maximum, dms))
        inter = jnp.exp(b + m_prev - mt)
        num = jnp.zeros(v.shape, F32)
        den = jnp.zeros((SUBLANES, 1), F32)
        for dlt in range(GROUP):
            kd = k if dlt == 0 else pltpu.roll(k, dlt, 0)
            vd_ = v if dlt == 0 else pltpu.roll(v, dlt, 0)
            s = jnp.sum(q * kd, axis=1, keepdims=True) * jnp.exp(dms[dlt] - mt)[:, 0:1]
            num = num + s * vd_
            den = den + s
        m_last = _group_last(mt, t)
        b_last = _group_last(b, t)
        dec = jnp.exp(b_last + m_prev - m_last)
        kw = jnp.exp(b_last - b + ig - m_last) * k
        qb = q.astype(BF16)
        vb = v.astype(BF16)
        qn = jnp.zeros((SUBLANES, 1), F32)
        inter_num = jnp.zeros(v.shape, F32)
        for half, (is_mine, mine_col) in enumerate(((first, first_col),
                                                    (jnp.logical_not(first), jnp.logical_not(first_col)))):
            s_idx = 2 * p + half
            c0 = c_ref[s_idx]
            n0 = n_ref[pl.ds(s_idx, 1), :]
            dec_row = dec[half * GROUP:half * GROUP + 1, :]
            kw_mine = jnp.where(is_mine, kw, 0.0)
            co_ref[s_idx] = jnp.concatenate([dec_row, dec_row], axis=1) * c0 + _dot_tn(kw_mine.astype(BF16), vb)
            no_ref[pl.ds(s_idx, 1), :] = dec_row * n0 + jnp.sum(kw_mine, axis=0, keepdims=True)
            inter_num = inter_num + jnp.where(mine_col, _dot(qb, c0.astype(BF16)), 0.0)
            qn = qn + jnp.where(mine_col, jnp.sum(q * n0, axis=1, keepdims=True), 0.0)
        num = num + inter[:, 0:1] * inter_num
        den = den + inter[:, 0:1] * qn
        hh = num / jnp.maximum(jnp.abs(den), jnp.exp(-mt)[:, 0:1])
        og = og_ref[pl.ds(r0, SUBLANES), :]
        yb_ref[pl.ds(r0, SUBLANES), :] = _head_norm(hh, nw) * jax.nn.sigmoid(og)
        mo_ref[pl.ds(r0, SUBLANES), :] = m_last
        return carry

    lax.fori_loop(0, n_pairs, pair, 0)


def _mlstm_s_call(proj, gates, bias, m_rows, norm_w, c_state, n_state, heads, qk, vd,
                  colq, colk, colv, colo, seqs_per_step):
    n_seq = c_state.shape[0]
    rows = seqs_per_step * GROUP
    kern = functools.partial(_mlstm_s_kernel, heads=heads, n_pairs=seqs_per_step // 2)
    c_spec = pl.BlockSpec((seqs_per_step, None, qk, vd), lambda i, h: (i, h, 0, 0))
    n_spec = pl.BlockSpec((None, seqs_per_step, qk), lambda i, h: (h, i, 0))
    return pl.pallas_call(
        kern,
        grid=(n_seq // seqs_per_step, heads),
        in_specs=[pl.BlockSpec((rows, qk), lambda i, h: (i, colq + h)),
                  pl.BlockSpec((rows, qk), lambda i, h: (i, colk + h)),
                  pl.BlockSpec((rows, vd), lambda i, h: (i, colv + h)),
                  pl.BlockSpec((rows, vd), lambda i, h: (i, colo + h)),
                  pl.BlockSpec((rows, LANES), lambda i, h: (i, 0)),
                  pl.BlockSpec((1, LANES), lambda i, h: (0, 0)),
                  pl.BlockSpec((rows, LANES), lambda i, h: (i, 0)),
                  pl.BlockSpec((1, vd), lambda i, h: (0, h)),
                  c_spec, n_spec],
        out_specs=[pl.BlockSpec((rows, vd), lambda i, h: (i, h)),
                   c_spec, n_spec,
                   pl.BlockSpec((None, rows, LANES), lambda i, h: (h, i, 0))],
        out_shape=[jax.ShapeDtypeStruct((n_seq * GROUP, heads * vd), F32),
                   jax.ShapeDtypeStruct(c_state.shape, F32),
                   jax.ShapeDtypeStruct(n_state.shape, F32),
                   jax.ShapeDtypeStruct((heads, n_seq * GROUP, LANES), F32)],
        compiler_params=_cparams("arbitrary", "arbitrary"),
        name="mlstm_sample",
    )(proj, proj, proj, proj, gates, bias, m_rows, norm_w, c_state, n_state)


def _outproj_kernel(ya_ref, yb_ref, ga_ref, gb_ref, x_ref, g1_ref, w_ref, o_ref):
    merged = jax.nn.sigmoid(ga_ref[...]) * ya_ref[...] + jax.nn.sigmoid(gb_ref[...]) * yb_ref[...]
    o_ref[...] = x_ref[...] + g1_ref[...] * _dot(merged.astype(BF16), w_ref[...])


def _outproj_call(ya, yb, proj, x, mod, w_out, col_ga, col_gb, tm, tiles_per_seq):
    m, d = x.shape
    blk = lambda col: pl.BlockSpec((tm, d), lambda i: (i, col))
    return pl.pallas_call(
        _outproj_kernel,
        grid=(m // tm,),
        in_specs=[blk(0), blk(0), blk(col_ga), blk(col_gb), blk(0),
                  _mod_spec(mod, tm, 2, tiles_per_seq),
                  pl.BlockSpec((d, d), lambda i: (0, 0))],
        out_specs=blk(0),
        out_shape=jax.ShapeDtypeStruct((m, d), F32),
        compiler_params=_cparams("arbitrary"),
        name="outproj",
    )(ya, yb, proj, proj, x, mod, w_out)


def _ffn_kernel(*refs, tm, rc, tiles_per_seq, grouped):
    if grouped:
        (x_ref, sh_ref, sc_ref, g2_ref, wa_ref, wg_ref, wd_ref, cw_ref, cb_ref, fw_ref, p1_ref, p2_ref,
         y_ref, a_ref, h_s, acc_s) = refs
    else:
        (x_ref, sh_ref, sc_ref, g2_ref, wa_ref, wg_ref, wd_ref, cw_ref, cb_ref, fw_ref,
         y_ref, a_ref, h_s, acc_s, carry_s) = refs
    i = pl.program_id(0)
    j = pl.program_id(1)

    @pl.when(j == 0)
    def _():
        def body(r, carry):
            r0 = pl.multiple_of(r * rc, rc)
            h = _norm_mod(x_ref[pl.ds(r0, rc), :], _rows(sc_ref, r0, rc), _rows(sh_ref, r0, rc))
            h_s[pl.ds(r0, rc), :] = h.astype(BF16)
            return carry
        lax.fori_loop(0, tm // rc, body, 0)
        acc_s[...] = jnp.zeros_like(acc_s)

    hb = h_s[...]
    a = _dot(hb, wa_ref[...])
    g = _dot(hb, wg_ref[...])
    rowid = lax.broadcasted_iota(jnp.int32, a.shape, 0)
    r1 = pltpu.roll(a, 1, 0)
    r2 = pltpu.roll(a, 2, 0)
    if grouped:
        t = rowid & (GROUP - 1)
        prev1 = jnp.where(t == 0, p1_ref[...], r1)
        prev2 = jnp.where(t <= 1, p2_ref[...], r2)
        a_ref[...] = a
    else:
        tail = a[tm - SUBLANES:tm, :]
        car = jnp.where(i % tiles_per_seq == 0, 0.0, carry_s[j])
        c1 = car[SUBLANES - 1:SUBLANES, :]
        c2 = car[SUBLANES - 2:SUBLANES - 1, :]
        prev1 = jnp.where(rowid == 0, c1, r1)
        prev2 = jnp.where(rowid == 0, c2, jnp.where(rowid == 1, c1, r2))
        carry_s[j] = tail
        a_ref[...] = tail
    cw = cw_ref[...]
    ac = cw[0:1, :] * prev2 + cw[1:2, :] * prev1 + cw[2:3, :] * a + cb_ref[...]
    yv = 0.5 * ac * (1.0 + lax.erf(ac * (2.0 ** -0.5))) * g
    acc_s[...] += _dot(yv.astype(BF16), wd_ref[...])

    @pl.when(j == pl.num_programs(1) - 1)
    def _():
        def body(r, carry):
            r0 = pl.multiple_of(r * rc, rc)
            x2 = x_ref[pl.ds(r0, rc), :] + _rows(g2_ref, r0, rc) * acc_s[pl.ds(r0, rc), :]
            ms = jnp.mean(x2 * x2, axis=-1, keepdims=True)
            y_ref[pl.ds(r0, rc), :] = x2 * lax.rsqrt(ms + EPS) * fw_ref[...]
            return carry
        lax.fori_loop(0, tm // rc, body, 0)


def _ffn_call(x, mod, w_up, w_down, conv_w, conv_b, final_w, tm, tiles_per_seq, p1=None, p2=None):
    m, d = x.shape
    f = w_down.shape[0]
    tf = 512
    nj = f // tf
    grouped = p1 is not None
    kern = functools.partial(_ffn_kernel, tm=tm, rc=min(tm, 256), tiles_per_seq=tiles_per_seq, grouped=grouped)
    in_specs = [pl.BlockSpec((tm, d), lambda i, j: (i, 0)),
                _mod_spec(mod, tm, 3, tiles_per_seq),
                _mod_spec(mod, tm, 4, tiles_per_seq),
                _mod_spec(mod, tm, 5, tiles_per_seq),
                pl.BlockSpec((d, tf), lambda i, j: (0, j)),
                pl.BlockSpec((d, tf), lambda i, j: (0, nj + j)),
                pl.BlockSpec((tf, d), lambda i, j: (j, 0)),
                pl.BlockSpec((CONV_W, tf), lambda i, j: (0, j)),
                pl.BlockSpec((1, tf), lambda i, j: (0, j)),
                pl.BlockSpec((1, d), lambda i, j: (0, 0))]
    args = [x, mod, mod, mod, w_up, w_up, w_down, conv_w, conv_b, final_w]
    scratch = [pltpu.VMEM((tm, d), BF16), pltpu.VMEM((tm, d), F32)]
    if grouped:
        in_specs += [pl.BlockSpec((tm, tf), lambda i, j: (i, j))] * 2
        args += [p1, p2]
        a_spec = pl.BlockSpec((tm, tf), lambda i, j: (i, j))
        a_shape = jax.ShapeDtypeStruct((m, f), F32)
    else:
        scratch.append(pltpu.VMEM((nj, SUBLANES, tf), F32))
        a_spec = pl.BlockSpec((SUBLANES, tf), lambda i, j: (i, j))
        a_shape = jax.ShapeDtypeStruct((m // tm * SUBLANES, f), F32)
    return pl.pallas_call(
        kern,
        grid=(m // tm, nj),
        in_specs=in_specs,
        out_specs=[pl.BlockSpec((tm, d), lambda i, j: (i, 0)), a_spec],
        out_shape=[jax.ShapeDtypeStruct((m, d), F32), a_shape],
        scratch_shapes=scratch,
        compiler_params=_cparams("arbitrary", "arbitrary"),
        name="ffn",
    )(*args)


def kernel(x_prompt, x_sample, c_prompt, c_sample, state_hgrn, state_mlstm_C, state_mlstm_n, state_mlstm_m,
           state_conv, ada_w, ada_b, w_in, hg_lb_logits, hg_norm_w, ml_i_bias, ml_f_bias, ml_norm_w, w_out,
           conv_w, conv_b, w_up, w_down, final_norm_w):
    n_p, seq, d = x_prompt.shape
    n_s, dec_seq, _ = x_sample.shape
    depth, _, hg_heads, hg_k, hg_v = state_hgrn.shape
    _, _, ml_heads, ml_qk, ml_v = state_mlstm_C.shape
    f = w_down.shape[1]
    assert depth == 1 and dec_seq == GROUP and hg_k == hg_v == LANES and 2 * ml_heads <= LANES
    assert hg_lb_logits.shape[0] == 2

    hg_w = hg_heads * hg_k
    gate0 = 4 * hg_w + 2 * ml_heads * ml_qk + ml_heads * ml_v
    gate1 = gate0 + 2 * ml_heads
    w = w_in[0]
    w_main = jnp.concatenate([w[:, :gate0], w[:, gate1:]], axis=1).astype(BF16)
    w_gate = jnp.pad(w[:, gate0:gate1], ((0, 0), (0, LANES - 2 * ml_heads))).astype(BF16)
    col_mq = 4 * hg_w // ml_qk
    col_mk = col_mq + ml_heads
    col_mv = (4 * hg_w + 2 * ml_heads * ml_qk) // ml_v
    col_mo = col_mv + ml_heads
    col_ga = gate0 // d + 1
    col_gb = col_ga + 1
    assert gate0 % d == 0 and (gate0 // d + 1) * d == gate0 + ml_heads * ml_v
    gate_bias = jnp.pad(jnp.concatenate([ml_i_bias[0], ml_f_bias[0]]), (0, LANES - 2 * ml_heads))[None, :]

    n_all = n_p + n_s
    c_all = jnp.pad(jnp.concatenate([c_prompt, c_sample], axis=0), ((0, (-n_all) % SUBLANES), (0, 0)))
    mod = _mod_call(c_all, ada_w[0], ada_b)
    mod_p = mod[:n_p][:, None, :]
    mod_s = jnp.repeat(mod[n_p:n_all], GROUP, axis=0)

    w_out_b = w_out[0].astype(BF16)
    w_up_b = w_up[0].astype(BF16)
    w_down_b = w_down[0].astype(BF16)
    hg_nw = hg_norm_w
    ml_nw = ml_norm_w
    fw = final_norm_w[None, :]

    xp = x_prompt.reshape(n_p * seq, d)
    tm_p = 1024
    proj_p, gates_p = _inproj_call(xp, mod_p, w_main, w_gate, tm_p, seq // tm_p)
    ya_p, hg_p = _hgrn_p_call(proj_p, hg_lb_logits, hg_nw, n_p, seq, hg_heads, hg_k, 0, 256)
    yb_p, c_p, nn_p, m_p = _mlstm_p_call(proj_p, gates_p, gate_bias, ml_nw, n_p, seq, ml_heads, ml_qk, ml_v,
                                         col_mq, col_mk, col_mv, col_mo)
    tm_o = 256
    x1_p = _outproj_call(ya_p, yb_p, proj_p, xp, mod_p, w_out_b, col_ga, col_gb, tm_o, seq // tm_o)
    tm_f = 512
    tiles = seq // tm_f
    y_p, atail = _ffn_call(x1_p, mod_p, w_up_b, w_down_b, conv_w[0], conv_b, fw, tm_f, tiles)
    cv_p = atail.reshape(n_p, tiles, SUBLANES, f)[:, tiles - 1, SUBLANES - (CONV_W - 1):]

    xs = x_sample.reshape(n_s * GROUP, d)
    m_s_rows = n_s * GROUP
    proj_s, gates_s = _inproj_call(xs, mod_s, w_main, w_gate, m_s_rows, 1)
    ya_s, hg_s = _hgrn_s_call(proj_s, hg_lb_logits, hg_nw, state_hgrn[0], hg_heads, hg_k, 0, 32)
    m_rows = jnp.pad(jnp.repeat(state_mlstm_m[0], GROUP, axis=0), ((0, 0), (0, LANES - ml_heads)))
    n_hb = jnp.transpose(state_mlstm_n[0], (1, 0, 2))
    yb_s, c_s, nn_s, m_s = _mlstm_s_call(proj_s, gates_s, gate_bias, m_rows, ml_nw, state_mlstm_C[0], n_hb,
                                         ml_heads, ml_qk, ml_v, col_mq, col_mk, col_mv, col_mo, 32)
    x1_s = _outproj_call(ya_s, yb_s, proj_s, xs, mod_s, w_out_b, col_ga, col_gb, tm_o, 1)
    cst = state_conv[0]
    zero = jnp.zeros_like(cst[:, :1])
    p1 = jnp.concatenate([cst[:, 1:2], zero, zero, zero], axis=1).reshape(m_s_rows, f)
    p2 = jnp.concatenate([cst, zero, zero], axis=1).reshape(m_s_rows, f)
    y_s, a_s = _ffn_call(x1_s, mod_s, w_up_b, w_down_b, conv_w[0], conv_b, fw, m_s_rows, 1, p1, p2)
    cv_s = a_s.reshape(n_s, GROUP, f)[:, GROUP - (CONV_W - 1):]

    return (y_p.reshape(n_p, seq, d), y_s.reshape(n_s, GROUP, d),
            hg_p[None], hg_s[None],
            c_p[None], c_s[None],
            nn_p.reshape(1, n_p, ml_heads, ml_qk), jnp.transpose(nn_s, (1, 0, 2))[None],
            m_p[:, :, 0, 0][None], jnp.transpose(m_s[:, ::GROUP, 0])[None],
            cv_p[None], cv_s[None])
```

```python
import functools

import numpy as np
import jax
import jax.numpy as jnp
from jax import lax
from jax.experimental import pallas as pl
from jax.experimental.pallas import tpu as pltpu

F32 = jnp.float32
BF16 = jnp.bfloat16

EPS = 1e-6
CONV_W = 3
LANES = 128
SUBLANES = 8
VMEM_LIMIT_BYTES = 58 * 1024 * 1024

MXU_DEPTH = 256
HG_CHUNK = 64
HG_SAFE_EXP = 60.0
HG_LEVELS = (32, 16, 8, 4, 2, 1)
ML_CHUNK = 128
GROUP = 4


def _cparams(*sem):
    return pltpu.CompilerParams(dimension_semantics=sem, vmem_limit_bytes=VMEM_LIMIT_BYTES)


def _dot(a, b):
    return jnp.dot(a, b, preferred_element_type=F32)


def _dot_nt(a, b):
    return lax.dot_general(a, b, (((1,), (1,)), ((), ())), preferred_element_type=F32)


def _dot_tn(a, b):
    return lax.dot_general(a, b, (((0,), (0,)), ((), ())), preferred_element_type=F32)


def _split3(x):
    x1 = x.astype(BF16)
    r1 = x - x1.astype(F32)
    x2 = r1.astype(BF16)
    x3 = (r1 - x2.astype(F32)).astype(BF16)
    return x1, x2, x3


def _exact_left_mul(w, x):
    x1, x2, x3 = _split3(x)
    return _dot(w, x1) + _dot(w, x2) + _dot(w, x3)


def _rows(ref, r0, n):
    if ref.shape[0] == 1:
        return ref[...]
    return ref[pl.ds(r0, n), :]


def _norm_mod(x, sc, sh):
    ms = jnp.mean(x * x, axis=-1, keepdims=True)
    return x * lax.rsqrt(ms + EPS) * (1.0 + sc) + sh


def _head_norm(o, w):
    return o * lax.rsqrt(jnp.mean(o * o, axis=-1, keepdims=True) + EPS) * w


def _lower_bound(lg):
    l0, l1 = lg[0:1, :], lg[1:2, :]
    m = jnp.maximum(l0, l1)
    e0, e1 = jnp.exp(l0 - m), jnp.exp(l1 - m)
    return e0 / (e0 + e1)


def _group_last(x, t, up):
    return jnp.where(t == 3, x, jnp.where(t == 2, up(x, 1), jnp.where(t == 1, up(x, 2), up(x, 3))))


def _mod_kernel(c_ref, w_ref, b_ref, o_ref):
    c = c_ref[...]
    s = (c * jax.nn.sigmoid(c)).astype(BF16)
    o_ref[...] = _dot(s, w_ref[...].astype(BF16)) + b_ref[...]


def _mod_call(c_all, ada_w, ada_b):
    mp, d = c_all.shape
    n = ada_w.shape[1]
    tn = 512
    return pl.pallas_call(
        _mod_kernel,
        grid=(n // tn,),
        in_specs=[pl.BlockSpec((mp, d), lambda j: (0, 0)),
                  pl.BlockSpec((d, tn), lambda j: (0, j)),
                  pl.BlockSpec((1, tn), lambda j: (0, j))],
        out_specs=pl.BlockSpec((mp, tn), lambda j: (0, j)),
        out_shape=jax.ShapeDtypeStruct((mp, n), F32),
        compiler_params=_cparams("arbitrary"),
        name="mod",
    )(c_all, ada_w, ada_b)


def _inproj_kernel(x_ref, sh_ref, sc_ref, w_ref, wg_ref, o_ref, og_ref, h_ref, *, tm, rc):
    @pl.when(pl.program_id(1) == 0)
    def _():
        def body(r, carry):
            r0 = pl.multiple_of(r * rc, rc)
            h = _norm_mod(x_ref[pl.ds(r0, rc), :], _rows(sc_ref, r0, rc), _rows(sh_ref, r0, rc))
            hb = h.astype(BF16)
            h_ref[pl.ds(r0, rc), :] = hb
            og_ref[pl.ds(r0, rc), :] = _dot(hb, wg_ref[...])
            return carry
        lax.fori_loop(0, tm // rc, body, 0)

    o_ref[...] = _dot(h_ref[...], w_ref[...])


def _mod_spec(mod, tm, col, tiles_per_seq):
    d = mod.shape[-1] // 6
    if mod.ndim == 3:
        return pl.BlockSpec((None, 1, d), lambda i, *_: (i // tiles_per_seq, 0, col))
    return pl.BlockSpec((tm, d), lambda i, *_: (i, col))


def _inproj_call(x, mod, w_main, w_gate, tm, tiles_per_seq):
    m, d = x.shape
    n = w_main.shape[1]
    tn = 1024
    kern = functools.partial(_inproj_kernel, tm=tm, rc=256)
    return pl.pallas_call(
        kern,
        grid=(m // tm, n // tn),
        in_specs=[pl.BlockSpec((tm, d), lambda i, j: (i, 0)),
                  _mod_spec(mod, tm, 0, tiles_per_seq),
                  _mod_spec(mod, tm, 1, tiles_per_seq),
                  pl.BlockSpec((d, tn), lambda i, j: (0, j)),
                  pl.BlockSpec((d, LANES), lambda i, j: (0, 0))],
        out_specs=[pl.BlockSpec((tm, tn), lambda i, j: (i, j)),
                   pl.BlockSpec((tm, LANES), lambda i, j: (i, 0))],
        out_shape=[jax.ShapeDtypeStruct((m, n), F32), jax.ShapeDtypeStruct((m, LANES), F32)],
        scratch_shapes=[pltpu.VMEM((tm, d), BF16)],
        compiler_params=_cparams("arbitrary", "arbitrary"),
        name="inproj",
    )(x, mod, mod, w_main, w_gate)


def _hgrn_constants():
    L = HG_CHUNK
    tri = np.tril(np.ones((L, L), np.float32))
    t = np.arange(L)
    blocks = []
    masks = []
    for h in HG_LEVELS:
        mid = (t // (2 * h)) * (2 * h) + h - 1
        blocks.append(tri - tri[mid])
        masks.append((t[:, None] // (2 * h) == t[None, :] // (2 * h)).astype(np.float32))
    blocks.append(tri[L - 1][None, :] - tri)
    masks.append(np.eye(L, dtype=np.float32))
    return jnp.asarray(np.concatenate(blocks, 0), BF16), jnp.asarray(np.stack(masks, 0), F32)


def _hgrn_gates(hq, z, lb):
    omlb = 1.0 - lb
    logf = jnp.log(lb + omlb * jax.nn.sigmoid(z))
    kk = omlb * jax.nn.sigmoid(-z)
    q = hq * jax.nn.sigmoid(hq) * (hq.shape[-1] ** -0.5)
    return q, kk, logf


def _hgrn_p_kernel(hq_ref, hf_ref, hi_ref, hg_ref, lg_ref, nw_ref, trib_ref, wall_ref, mask_ref,
                   ya_ref, so_ref, st_ref, *, n_chunks):
    L = HG_CHUNK
    c = pl.program_id(2)
    tb = trib_ref.shape[0]

    @pl.when(c == 0)
    def _():
        st_ref[...] = jnp.zeros_like(st_ref)

    lb = _lower_bound(lg_ref[...])
    nw = nw_ref[...]
    q, kk, logf = _hgrn_gates(hq_ref[...], hf_ref[...], lb)
    trib = trib_ref[...]
    b = jnp.concatenate([_exact_left_mul(trib, logf[r:r + tb]) for r in range(0, n_chunks * L, tb)], axis=0)
    worst = jnp.max(-b)

    def finish(ci, o):
        hg = hg_ref[ci * L:(ci + 1) * L, :]
        ya_ref[ci * L:(ci + 1) * L, :] = _head_norm(o, nw) * (hg * jax.nn.sigmoid(hg))

    @pl.when(worst <= HG_SAFE_EXP)
    def _():
        eb = jnp.exp(b)
        qt = (q * eb).astype(BF16)
        kn = kk * jnp.exp(-b)
        knb = kn.astype(BF16)
        row = lax.broadcasted_iota(jnp.int32, (L, L), 0)
        colid = lax.broadcasted_iota(jnp.int32, (L, L), 1)
        causal = colid <= row
        vbs, o_intra, kts = [], [], []
        for ci in range(n_chunks):
            sl = slice(ci * L, (ci + 1) * L)
            vb = hi_ref[sl, :].astype(BF16)
            a = jnp.where(causal, _dot_nt(qt[sl], knb[sl]), 0.0)
            vbs.append(vb)
            o_intra.append(_dot(a.astype(BF16), vb))
            kts.append((kn[sl] * eb[(ci + 1) * L - 1:(ci + 1) * L, :]).astype(BF16))
        st = st_ref[...]
        for ci in range(n_chunks):
            sl = slice(ci * L, (ci + 1) * L)
            finish(ci, o_intra[ci] + _dot_nt(qt[sl], st.astype(BF16)))
            st = st * eb[(ci + 1) * L - 1:(ci + 1) * L, :] + _dot_tn(vbs[ci], kts[ci])
        st_ref[...] = st

    @pl.when(jnp.logical_not(worst <= HG_SAFE_EXP))
    def _():
        wall = wall_ref[...]
        rowid = lax.broadcasted_iota(jnp.int32, (L, hq_ref.shape[1]), 0)
        n_lv = len(HG_LEVELS)
        for ci in range(n_chunks):
            sl = slice(ci * L, (ci + 1) * L)
            qc, kc, bc = q[sl], kk[sl], b[sl]
            vb = hi_ref[sl, :].astype(BF16)
            d = _exact_left_mul(wall, logf[sl])
            a = mask_ref[n_lv] * _dot_nt(qc.astype(BF16), kc.astype(BF16))
            for li, h in enumerate(HG_LEVELS):
                e = jnp.exp(-jnp.abs(d[li * L:(li + 1) * L]))
                second = (rowid & h) != 0
                p = jnp.where(second, qc, kc) * e
                qh = jnp.where(second, p, 0.0).astype(BF16)
                kh = jnp.where(second, 0.0, p).astype(BF16)
                a = a + mask_ref[li] * _dot_nt(qh, kh)
            qt = (qc * jnp.exp(bc)).astype(BF16)
            kt = (kc * jnp.exp(d[n_lv * L:(n_lv + 1) * L])).astype(BF16)
            st = st_ref[...]
            finish(ci, _dot(a.astype(BF16), vb) + _dot_nt(qt, st.astype(BF16)))
            st_ref[...] = st * jnp.exp(bc[L - 1:L, :]) + _dot_tn(vb, kt)

    @pl.when(c == pl.num_programs(2) - 1)
    def _():
        so_ref[...] = st_ref[...].T


def _hgrn_p_call(proj, lb_logits, norm_w, n_seq, seq_len, heads, kdim, col0, rows_per_step):
    wall, masks = _hgrn_constants()
    per_mat = min(rows_per_step, MXU_DEPTH) // HG_CHUNK
    trib = jnp.asarray(np.kron(np.eye(per_mat, dtype=np.float32),
                               np.tril(np.ones((HG_CHUNK, HG_CHUNK), np.float32))), BF16)
    nc = seq_len // rows_per_step
    kern = functools.partial(_hgrn_p_kernel, n_chunks=rows_per_step // HG_CHUNK)

    def col(k):
        return pl.BlockSpec((rows_per_step, kdim), lambda b, h, c: (b * nc + c, col0 + k * heads + h))

    return pl.pallas_call(
        kern,
        grid=(n_seq, heads, nc),
        in_specs=[col(0), col(1), col(2), col(3),
                  pl.BlockSpec((2, kdim), lambda b, h, c: (0, h)),
                  pl.BlockSpec((1, kdim), lambda b, h, c: (0, h)),
                  pl.BlockSpec(trib.shape, lambda b, h, c: (0, 0)),
                  pl.BlockSpec(wall.shape, lambda b, h, c: (0, 0)),
                  pl.BlockSpec(masks.shape, lambda b, h, c: (0, 0, 0))],
        out_specs=[pl.BlockSpec((rows_per_step, kdim), lambda b, h, c: (b * nc + c, h)),
                   pl.BlockSpec((None, None, kdim, kdim), lambda b, h, c: (b, h, 0, 0))],
        out_shape=[jax.ShapeDtypeStruct((n_seq * seq_len, heads * kdim), F32),
                   jax.ShapeDtypeStruct((n_seq, heads, kdim, kdim), F32)],
        scratch_shapes=[pltpu.VMEM((kdim, kdim), F32)],
        compiler_params=_cparams("arbitrary", "arbitrary", "arbitrary"),
        name="hgrn_prompt",
    )(proj, proj, proj, proj, lb_logits, norm_w, trib, wall, masks)


def _hgrn_s_kernel(hq_ref, hf_ref, hi_ref, hg_ref, lg_ref, nw_ref, s_ref, ya_ref, so_ref, *, n_pairs):
    kdim = hq_ref.shape[1]
    rows = n_pairs * SUBLANES
    lb = _lower_bound(lg_ref[...])
    rowid = lax.broadcasted_iota(jnp.int32, (rows, kdim), 0)
    t = rowid & (GROUP - 1)
    first = (rowid & GROUP) == 0
    first8 = lax.broadcasted_iota(jnp.int32, (SUBLANES, kdim), 0) < GROUP
    down = lambda y, j: pltpu.roll(y, j, 0)
    up = lambda y, j: pltpu.roll(y, rows - j, 0)

    q, kk, logf = _hgrn_gates(hq_ref[...], hf_ref[...], lb)
    v = hi_ref[...]
    b = logf
    for dlt in range(1, GROUP):
        b = b + jnp.where(t >= dlt, down(logf, dlt), 0.0)
    o = jnp.sum(q * kk, axis=1, keepdims=True) * v
    for dlt in range(1, GROUP):
        x = q * down(kk, dlt) * jnp.exp(b - down(b, dlt))
        a = jnp.sum(jnp.where(t >= dlt, x, 0.0), axis=1, keepdims=True)
        o = o + a * down(v, dlt)
    b_last = _group_last(b, t, up)
    qt = q * jnp.exp(b)
    kt = kk * jnp.exp(b_last - b)
    d1, d2, d3 = _split3(jnp.exp(b_last))
    swap = lambda y: jnp.where(first, up(y.astype(F32), GROUP), down(y.astype(F32), GROUP))
    dsplit = jnp.where(t == 0, swap(d1), jnp.where(t == 1, swap(d2), jnp.where(t == 2, swap(d3), 0.0)))
    ones = jnp.where(t <= 2, 1.0, 0.0)
    lhs = (jnp.where(first, kt, dsplit), jnp.where(first, dsplit, kt))
    rhs = (jnp.concatenate([jnp.where(first, v, 0.0), jnp.where(first, 0.0, ones)], axis=1),
           jnp.concatenate([jnp.where(first, 0.0, v), jnp.where(first, ones, 0.0)], axis=1))
    o_inter = []
    for p in range(n_pairs):
        sl = slice(p * SUBLANES, (p + 1) * SUBLANES)
        qb = qt[sl].astype(BF16)
        parts = []
        for half in range(2):
            s0 = s_ref[2 * p + half]
            upd = _dot_tn(lhs[half][sl].astype(BF16), rhs[half][sl].astype(BF16))
            so_ref[2 * p + half] = s0 * upd[:, kdim:] + upd[:, :kdim]
            parts.append(_dot(qb, s0.astype(BF16)))
        o_inter.append(jnp.where(first8, parts[0], parts[1]))
    o = o + jnp.concatenate(o_inter, axis=0)
    hg = hg_ref[...]
    ya_ref[...] = _head_norm(o, nw_ref[...]) * (hg * jax.nn.sigmoid(hg))


def _hgrn_s_call(proj, lb_logits, norm_w, state, heads, kdim, col0, seqs_per_step):
    n_seq = state.shape[0]
    rows = seqs_per_step * GROUP
    kern = functools.partial(_hgrn_s_kernel, n_pairs=seqs_per_step // 2)

    def col(k):
        return pl.BlockSpec((rows, kdim), lambda i, h: (i, col0 + k * heads + h))

    st_spec = pl.BlockSpec((seqs_per_step, None, kdim, kdim), lambda i, h: (i, h, 0, 0))
    return pl.pallas_call(
        kern,
        grid=(n_seq // seqs_per_step, heads),
        in_specs=[col(0), col(1), col(2), col(3),
                  pl.BlockSpec((2, kdim), lambda i, h: (0, h)),
                  pl.BlockSpec((1, kdim), lambda i, h: (0, h)),
                  st_spec],
        out_specs=[pl.BlockSpec((rows, kdim), lambda i, h: (i, h)), st_spec],
        out_shape=[jax.ShapeDtypeStruct((n_seq * GROUP, heads * kdim), F32),
                   jax.ShapeDtypeStruct(state.shape, F32)],
        compiler_params=_cparams("arbitrary", "arbitrary"),
        name="hgrn_sample",
    )(proj, proj, proj, proj, lb_logits, norm_w, state)


def _lane_pick(x, lane, idx):
    return jnp.broadcast_to(jnp.sum(jnp.where(lane == idx, x, 0.0), axis=1, keepdims=True), x.shape)


def _interleave(chains):
    live = list(chains)
    while live:
        still = []
        for g in live:
            try:
                next(g)
                still.append(g)
            except StopIteration:
                pass
        live = still


def _mlstm_p_kernel(q_ref, k_ref, v_ref, og_ref, g_ref, bias_ref, nw_ref, tri_ref,
                    yb_ref, co_ref, no_ref, mo_ref, c_s, n_s, m_s, *, heads, qk, vd):
    L = ML_CHUNK
    c = pl.program_id(1)

    @pl.when(c == 0)
    def _():
        c_s[...] = jnp.zeros_like(c_s)
        n_s[...] = jnp.zeros_like(n_s)
        m_s[...] = jnp.zeros_like(m_s)

    g = g_ref[...] + bias_ref[...]
    lane = lax.broadcasted_iota(jnp.int32, g.shape, 1)
    row = lax.broadcasted_iota(jnp.int32, g.shape, 0)
    b_all = _exact_left_mul(tri_ref[...], jax.nn.log_sigmoid(g))

    def head(hd):
        qs = slice(hd * qk, (hd + 1) * qk)
        vs = slice(hd * vd, (hd + 1) * vd)
        ig = _lane_pick(g, lane, hd)
        b = _lane_pick(b_all, lane, heads + hd)
        m_prev = m_s[hd]
        gs = (ig - b).T
        yield
        dm = jnp.where(lane <= row, b + gs, -jnp.inf)
        mt = jnp.maximum(b + m_prev, jnp.max(dm, axis=1, keepdims=True))
        inter = jnp.exp(b + m_prev - mt)
        q = q_ref[:, qs] * (qk ** -0.5)
        k = k_ref[:, qs]
        vb = v_ref[:, vs].astype(BF16)
        qb = q.astype(BF16)
        sc = _dot_nt(qb, k.astype(BF16)) * jnp.exp(dm - mt)
        yield
        c0 = c_s[hd]
        n0 = n_s[hd]
        num = inter[:, 0:1] * _dot(qb, c0.astype(BF16)) + _dot(sc.astype(BF16), vb)
        den = inter[:, 0:1] * jnp.sum(q * n0, axis=1, keepdims=True) + jnp.sum(sc, axis=1, keepdims=True)
        yield
        hh = num / jnp.maximum(jnp.abs(den), jnp.exp(-mt[:, 0:1]))
        yb_ref[:, vs] = _head_norm(hh, nw_ref[:, vs]) * jax.nn.sigmoid(og_ref[:, vs])
        yield
        m_last = mt[L - 1:L, :]
        b_last = b[L - 1:L, :]
        dec = jnp.exp(b_last + m_prev - m_last)
        kw = jnp.exp(b_last - b + ig - m_last) * k
        c_s[hd] = jnp.concatenate([dec, dec], axis=1) * c0 + _dot_tn(kw.astype(BF16), vb)
        n_s[hd] = dec * n0 + jnp.sum(kw, axis=0, keepdims=True)
        m_s[hd] = m_last

    _interleave([head(hd) for hd in range(heads)])

    @pl.when(c == pl.num_programs(1) - 1)
    def _():
        co_ref[...] = c_s[...]
        no_ref[...] = n_s[...]
        mo_ref[...] = m_s[...]


def _mlstm_p_call(proj, gates, bias, norm_w, n_seq, seq_len, heads, qk, vd, colq, colk, colv, colo):
    L = ML_CHUNK
    assert qk == L and qk == LANES
    nc = seq_len // L
    qw, vw = heads * qk, heads * vd
    assert colq % qw == 0 and colk % qw == 0 and colv % vw == 0 and colo % vw == 0
    tri = jnp.asarray(np.tril(np.ones((L, L), np.float32)), BF16)
    kern = functools.partial(_mlstm_p_kernel, heads=heads, qk=qk, vd=vd)
    return pl.pallas_call(
        kern,
        grid=(n_seq, nc),
        in_specs=[pl.BlockSpec((L, qw), lambda b, c: (b * nc + c, colq // qw)),
                  pl.BlockSpec((L, qw), lambda b, c: (b * nc + c, colk // qw)),
                  pl.BlockSpec((L, vw), lambda b, c: (b * nc + c, colv // vw)),
                  pl.BlockSpec((L, vw), lambda b, c: (b * nc + c, colo // vw)),
                  pl.BlockSpec((L, LANES), lambda b, c: (b * nc + c, 0)),
                  pl.BlockSpec((1, LANES), lambda b, c: (0, 0)),
                  pl.BlockSpec((1, vw), lambda b, c: (0, 0)),
                  pl.BlockSpec((L, L), lambda b, c: (0, 0))],
        out_specs=[pl.BlockSpec((L, vw), lambda b, c: (b * nc + c, 0)),
                   pl.BlockSpec((None, heads, qk, vd), lambda b, c: (b, 0, 0, 0)),
                   pl.BlockSpec((None, heads, 1, qk), lambda b, c: (b, 0, 0, 0)),
                   pl.BlockSpec((None, heads, 1, LANES), lambda b, c: (b, 0, 0, 0))],
        out_shape=[jax.ShapeDtypeStruct((n_seq * seq_len, vw), F32),
                   jax.ShapeDtypeStruct((n_seq, heads, qk, vd), F32),
                   jax.ShapeDtypeStruct((n_seq, heads, 1, qk), F32),
                   jax.ShapeDtypeStruct((n_seq, heads, 1, LANES), F32)],
        scratch_shapes=[pltpu.VMEM((heads, qk, vd), F32), pltpu.VMEM((heads, 1, qk), F32),
                        pltpu.VMEM((heads, 1, LANES), F32)],
        compiler_params=_cparams("arbitrary", "arbitrary"),
        name="mlstm_prompt",
    )(proj, proj, proj, proj, gates, bias, norm_w, tri)


def _mlstm_s_kernel(q_ref, k_ref, v_ref, og_ref, g_ref, bias_ref, m_ref, nw_ref, c_ref, n_ref,
                    yb_ref, co_ref, no_ref, mo_ref, *, heads, n_pairs):
    h = pl.program_id(1)
    qk = q_ref.shape[1]
    rows = n_pairs * SUBLANES
    lane = lax.broadcasted_iota(jnp.int32, (rows, LANES), 1)
    rowid = lax.broadcasted_iota(jnp.int32, (rows, LANES), 0)
    t = rowid & (GROUP - 1)
    first = (rowid & GROUP) == 0
    first8 = lax.broadcasted_iota(jnp.int32, (SUBLANES, 1), 0) < GROUP
    down = lambda y, j: pltpu.roll(y, j, 0)
    up = lambda y, j: pltpu.roll(y, rows - j, 0)

    g = g_ref[...] + bias_ref[...]
    ig = _lane_pick(g, lane, h)
    lf = _lane_pick(jax.nn.log_sigmoid(g), lane, heads + h)
    m_prev = _lane_pick(m_ref[...], lane, h)
    b = lf
    for dlt in range(1, GROUP):
        b = b + jnp.where(t >= dlt, down(lf, dlt), 0.0)
    q = q_ref[...] * (qk ** -0.5)
    k = k_ref[...]
    v = v_ref[...]
    dms = [ig] + [jnp.where(t >= dlt, b - down(b, dlt) + down(ig, dlt), -jnp.inf) for dlt in range(1, GROUP)]
    mt = jnp.maximum(b + m_prev, functools.reduce(jnp.maximum, dms))
    inter = jnp.exp(b + m_prev - mt)[:, 0:1]
    num = jnp.zeros(v.shape, F32)
    den = jnp.zeros((rows, 1), F32)
    for dlt in range(GROUP):
        kd = k if dlt == 0 else down(k, dlt)
        vd_ = v if dlt == 0 else down(v, dlt)
        s = jnp.sum(q * kd, axis=1, keepdims=True) * jnp.exp(dms[dlt] - mt)[:, 0:1]
        num = num + s * vd_
        den = den + s
    m_last = _group_last(mt, t, up)
    b_last = _group_last(b, t, up)
    dec = jnp.exp(b_last + m_prev - m_last)
    kw = jnp.exp(b_last - b + ig - m_last) * k
    n0 = n_ref[...]
    no_ref[...] = dec * n0 + kw + down(kw, 1) + down(kw, 2) + down(kw, 3)
    mo_ref[...] = m_last
    den = den + inter * jnp.sum(q * n0, axis=1, keepdims=True)

    kw_half = (jnp.where(first, kw, 0.0), jnp.where(first, 0.0, kw))
    dec2 = jnp.concatenate([dec, dec], axis=1)
    num_inter = []
    for p in range(n_pairs):
        sl = slice(p * SUBLANES, (p + 1) * SUBLANES)
        qb = q[sl].astype(BF16)
        vb = v[sl].astype(BF16)
        parts = []
        for half in range(2):
            c0 = c_ref[2 * p + half]
            r = p * SUBLANES + half * GROUP
            co_ref[2 * p + half] = dec2[r:r + 1, :] * c0 + _dot_tn(kw_half[half][sl].astype(BF16), vb)
            parts.append(_dot(qb, c0.astype(BF16)))
        num_inter.append(jnp.where(first8, parts[0], parts[1]))
    num = num + inter * jnp.concatenate(num_inter, axis=0)
    hh = num / jnp.maximum(jnp.abs(den), jnp.exp(-mt)[:, 0:1])
    yb_ref[...] = _head_norm(hh, nw_ref[...]) * jax.nn.sigmoid(og_ref[...])


def _mlstm_s_call(proj, gates, bias, m_rows, norm_w, c_state, n_state, heads, qk, vd,
                  colq, colk, colv, colo, seqs_per_step):
    n_seq = c_state.shape[0]
    rows = seqs_per_step * GROUP
    kern = functools.partial(_mlstm_s_kernel, heads=heads, n_pairs=seqs_per_step // 2)
    c_spec = pl.BlockSpec((seqs_per_step, None, qk, vd), lambda i, h: (i, h, 0, 0))
    n_spec = pl.BlockSpec((None, rows, qk), lambda i, h: (h, i, 0))
    return pl.pallas_call(
        kern,
        grid=(n_seq // seqs_per_step, heads),
        in_specs=[pl.BlockSpec((rows, qk), lambda i, h: (i, colq // qk + h)),
                  pl.BlockSpec((rows, qk), lambda i, h: (i, colk // qk + h)),
                  pl.BlockSpec((rows, vd), lambda i, h: (i, colv // vd + h)),
                  pl.BlockSpec((rows, vd), lambda i, h: (i, colo // vd + h)),
                  pl.BlockSpec((rows, LANES), lambda i, h: (i, 0)),
                  pl.BlockSpec((1, LANES), lambda i, h: (0, 0)),
                  pl.BlockSpec((rows, LANES), lambda i, h: (i, 0)),
                  pl.BlockSpec((1, vd), lambda i, h: (0, h)),
                  c_spec, n_spec],
        out_specs=[pl.BlockSpec((rows, vd), lambda i, h: (i, h)),
                   c_spec, n_spec,
                   pl.BlockSpec((None, rows, LANES), lambda i, h: (h, i, 0))],
        out_shape=[jax.ShapeDtypeStruct((n_seq * GROUP, heads * vd), F32),
                   jax.ShapeDtypeStruct(c_state.shape, F32),
                   jax.ShapeDtypeStruct(n_state.shape, F32),
                   jax.ShapeDtypeStruct((heads, n_seq * GROUP, LANES), F32)],
        compiler_params=_cparams("arbitrary", "arbitrary"),
        name="mlstm_sample",
    )(proj, proj, proj, proj, gates, bias, m_rows, norm_w, c_state, n_state)


def _outproj_kernel(ya_ref, yb_ref, ga_ref, gb_ref, x_ref, g1_ref, w_ref, o_ref):
    merged = jax.nn.sigmoid(ga_ref[...]) * ya_ref[...] + jax.nn.sigmoid(gb_ref[...]) * yb_ref[...]
    o_ref[...] = x_ref[...] + g1_ref[...] * _dot(merged.astype(BF16), w_ref[...])


def _outproj_call(ya, yb, proj, x, mod, w_out, col_ga, col_gb, tm, tiles_per_seq):
    m, d = x.shape
    blk = lambda col: pl.BlockSpec((tm, d), lambda i: (i, col))
    return pl.pallas_call(
        _outproj_kernel,
        grid=(m // tm,),
        in_specs=[blk(0), blk(0), blk(col_ga), blk(col_gb), blk(0),
                  _mod_spec(mod, tm, 2, tiles_per_seq),
                  pl.BlockSpec((d, d), lambda i: (0, 0))],
        out_specs=blk(0),
        out_shape=jax.ShapeDtypeStruct((m, d), F32),
        compiler_params=_cparams("arbitrary"),
        name="outproj",
    )(ya, yb, proj, proj, x, mod, w_out)


def _ffn_kernel(*refs, tm, rc, tiles_per_seq, grouped):
    if grouped:
        (x_ref, sh_ref, sc_ref, g2_ref, wa_ref, wg_ref, wd_ref, cw_ref, cb_ref, fw_ref, p1_ref, p2_ref,
         y_ref, a_ref, h_s, acc_s) = refs
    else:
        (x_ref, sh_ref, sc_ref, g2_ref, wa_ref, wg_ref, wd_ref, cw_ref, cb_ref, fw_ref,
         y_ref, a_ref, h_s, acc_s, carry_s) = refs
    i = pl.program_id(0)
    j = pl.program_id(1)

    @pl.when(j == 0)
    def _():
        def body(r, carry):
            r0 = pl.multiple_of(r * rc, rc)
            h = _norm_mod(x_ref[pl.ds(r0, rc), :], _rows(sc_ref, r0, rc), _rows(sh_ref, r0, rc))
            h_s[pl.ds(r0, rc), :] = h.astype(BF16)
            return carry
        lax.fori_loop(0, tm // rc, body, 0)
        acc_s[...] = jnp.zeros_like(acc_s)

    hb = h_s[...]
    a = _dot(hb, wa_ref[...])
    g = _dot(hb, wg_ref[...])
    rowid = lax.broadcasted_iota(jnp.int32, a.shape, 0)
    r1 = pltpu.roll(a, 1, 0)
    r2 = pltpu.roll(a, 2, 0)
    if grouped:
        t = rowid & (GROUP - 1)
        prev1 = jnp.where(t == 0, p1_ref[...], r1)
        prev2 = jnp.where(t <= 1, p2_ref[...], r2)
        a_ref[...] = a
    else:
        tail = a[tm - SUBLANES:tm, :]
        car = jnp.where(i % tiles_per_seq == 0, 0.0, carry_s[j])
        c1 = car[SUBLANES - 1:SUBLANES, :]
        c2 = car[SUBLANES - 2:SUBLANES - 1, :]
        prev1 = jnp.where(rowid == 0, c1, r1)
        prev2 = jnp.where(rowid == 0, c2, jnp.where(rowid == 1, c1, r2))
        carry_s[j] = tail
        a_ref[...] = tail
    cw = cw_ref[...]
    ac = cw[0:1, :] * prev2 + cw[1:2, :] * prev1 + cw[2:3, :] * a + cb_ref[...]
    yv = 0.5 * ac * (1.0 + lax.erf(ac * (2.0 ** -0.5))) * g
    acc_s[...] += _dot(yv.astype(BF16), wd_ref[...])

    @pl.when(j == pl.num_programs(1) - 1)
    def _():
        def body(r, carry):
            r0 = pl.multiple_of(r * rc, rc)
            x2 = x_ref[pl.ds(r0, rc), :] + _rows(g2_ref, r0, rc) * acc_s[pl.ds(r0, rc), :]
            ms = jnp.mean(x2 * x2, axis=-1, keepdims=True)
            y_ref[pl.ds(r0, rc), :] = x2 * lax.rsqrt(ms + EPS) * fw_ref[...]
            return carry
        lax.fori_loop(0, tm // rc, body, 0)


def _ffn_call(x, mod, w_up, w_down, conv_w, conv_b, final_w, tm, tiles_per_seq, p1=None, p2=None):
    m, d = x.shape
    f = w_down.shape[0]
    tf = 512
    nj = f // tf
    grouped = p1 is not None
    kern = functools.partial(_ffn_kernel, tm=tm, rc=min(tm, 256), tiles_per_seq=tiles_per_seq, grouped=grouped)
    in_specs = [pl.BlockSpec((tm, d), lambda i, j: (i, 0)),
                _mod_spec(mod, tm, 3, tiles_per_seq),
                _mod_spec(mod, tm, 4, tiles_per_seq),
                _mod_spec(mod, tm, 5, tiles_per_seq),
                pl.BlockSpec((d, tf), lambda i, j: (0, j)),
                pl.BlockSpec((d, tf), lambda i, j: (0, nj + j)),
                pl.BlockSpec((tf, d), lambda i, j: (j, 0)),
                pl.BlockSpec((CONV_W, tf), lambda i, j: (0, j)),
                pl.BlockSpec((1, tf), lambda i, j: (0, j)),
                pl.BlockSpec((1, d), lambda i, j: (0, 0))]
    args = [x, mod, mod, mod, w_up, w_up, w_down, conv_w, conv_b, final_w]
    scratch = [pltpu.VMEM((tm, d), BF16), pltpu.VMEM((tm, d), F32)]
    if grouped:
        in_specs += [pl.BlockSpec((tm, tf), lambda i, j: (i, j))] * 2
        args += [p1, p2]
        a_spec = pl.BlockSpec((tm, tf), lambda i, j: (i, j))
        a_shape = jax.ShapeDtypeStruct((m, f), F32)
    else:
        scratch.append(pltpu.VMEM((nj, SUBLANES, tf), F32))
        a_spec = pl.BlockSpec((SUBLANES, tf), lambda i, j: (i, j))
        a_shape = jax.ShapeDtypeStruct((m // tm * SUBLANES, f), F32)
    return pl.pallas_call(
        kern,
        grid=(m // tm, nj),
        in_specs=in_specs,
        out_specs=[pl.BlockSpec((tm, d), lambda i, j: (i, 0)), a_spec],
        out_shape=[jax.ShapeDtypeStruct((m, d), F32), a_shape],
        scratch_shapes=scratch,
        compiler_params=_cparams("arbitrary", "arbitrary"),
        name="ffn",
    )(*args)


def kernel(x_prompt, x_sample, c_prompt, c_sample, state_hgrn, state_mlstm_C, state_mlstm_n, state_mlstm_m,
           state_conv, ada_w, ada_b, w_in, hg_lb_logits, hg_norm_w, ml_i_bias, ml_f_bias, ml_norm_w, w_out,
           conv_w, conv_b, w_up, w_down, final_norm_w):
    n_p, seq, d = x_prompt.shape
    n_s, dec_seq, _ = x_sample.shape
    depth, _, hg_heads, hg_k, hg_v = state_hgrn.shape
    _, _, ml_heads, ml_qk, ml_v = state_mlstm_C.shape
    f = w_down.shape[1]
    assert depth == 1 and dec_seq == GROUP and hg_k == hg_v == LANES and 2 * ml_heads <= LANES
    assert hg_lb_logits.shape[0] == 2

    hg_w = hg_heads * hg_k
    gate0 = 4 * hg_w + 2 * ml_heads * ml_qk + ml_heads * ml_v
    gate1 = gate0 + 2 * ml_heads
    w = w_in[0]
    w_main = jnp.concatenate([w[:, :gate0], w[:, gate1:]], axis=1).astype(BF16)
    w_gate = jnp.pad(w[:, gate0:gate1], ((0, 0), (0, LANES - 2 * ml_heads))).astype(BF16)
    col_mq = 4 * hg_w
    col_mk = col_mq + ml_heads * ml_qk
    col_mv = col_mk + ml_heads * ml_qk
    col_mo = col_mv + ml_heads * ml_v
    col_ga = gate0 // d + 1
    col_gb = col_ga + 1
    assert gate0 % d == 0 and (gate0 // d + 1) * d == gate0 + ml_heads * ml_v
    gate_bias = jnp.pad(jnp.concatenate([ml_i_bias[0], ml_f_bias[0]]), (0, LANES - 2 * ml_heads))[None, :]

    n_all = n_p + n_s
    c_all = jnp.pad(jnp.concatenate([c_prompt, c_sample], axis=0), ((0, (-n_all) % SUBLANES), (0, 0)))
    mod = _mod_call(c_all, ada_w[0], ada_b)
    mod_p = mod[:n_p][:, None, :]
    mod_s = jnp.repeat(mod[n_p:n_all], GROUP, axis=0)

    w_out_b = w_out[0].astype(BF16)
    w_up_b = w_up[0].astype(BF16)
    w_down_b = w_down[0].astype(BF16)
    hg_nw = hg_norm_w
    ml_nw = ml_norm_w
    fw = final_norm_w[None, :]

    xp = x_prompt.reshape(n_p * seq, d)
    tm_p = 1024
    proj_p, gates_p = _inproj_call(xp, mod_p, w_main, w_gate, tm_p, seq // tm_p)
    ya_p, hg_p = _hgrn_p_call(proj_p, hg_lb_logits, hg_nw, n_p, seq, hg_heads, hg_k, 0, 256)
    yb_p, c_p, nn_p, m_p = _mlstm_p_call(proj_p, gates_p, gate_bias, ml_nw, n_p, seq, ml_heads, ml_qk, ml_v,
                                         col_mq, col_mk, col_mv, col_mo)
    tm_o = 256
    x1_p = _outproj_call(ya_p, yb_p, proj_p, xp, mod_p, w_out_b, col_ga, col_gb, tm_o, seq // tm_o)
    tm_f = 512
    tiles = seq // tm_f
    y_p, atail = _ffn_call(x1_p, mod_p, w_up_b, w_down_b, conv_w[0], conv_b, fw, tm_f, tiles)
    cv_p = atail.reshape(n_p, tiles, SUBLANES, f)[:, tiles - 1, SUBLANES - (CONV_W - 1):]

    xs = x_sample.reshape(n_s * GROUP, d)
    m_s_rows = n_s * GROUP
    proj_s, gates_s = _inproj_call(xs, mod_s, w_main, w_gate, m_s_rows, 1)
    ya_s, hg_s = _hgrn_s_call(proj_s, hg_lb_logits, hg_nw, state_hgrn[0], hg_heads, hg_k, 0, 32)
    m_rows = jnp.pad(jnp.repeat(state_mlstm_m[0], GROUP, axis=0), ((0, 0), (0, LANES - ml_heads)))
    n_hb = jnp.repeat(jnp.transpose(state_mlstm_n[0], (1, 0, 2)), GROUP, axis=1)
    yb_s, c_s, nn_s, m_s = _mlstm_s_call(proj_s, gates_s, gate_bias, m_rows, ml_nw, state_mlstm_C[0], n_hb,
                                         ml_heads, ml_qk, ml_v, col_mq, col_mk, col_mv, col_mo, 32)
    x1_s = _outproj_call(ya_s, yb_s, proj_s, xs, mod_s, w_out_b, col_ga, col_gb, tm_o, 1)
    cst = state_conv[0]
    zero = jnp.zeros_like(cst[:, :1])
    p1 = jnp.concatenate([cst[:, 1:2], zero, zero, zero], axis=1).reshape(m_s_rows, f)
    p2 = jnp.concatenate([cst, zero, zero], axis=1).reshape(m_s_rows, f)
    y_s, a_s = _ffn_call(x1_s, mod_s, w_up_b, w_down_b, conv_w[0], conv_b, fw, m_s_rows, 1, p1, p2)
    cv_s = a_s.reshape(n_s, GROUP, f)[:, GROUP - (CONV_W - 1):]

    return (y_p.reshape(n_p, seq, d), y_s.reshape(n_s, GROUP, d),
            hg_p[None], hg_s[None],
            c_p[None], c_s[None],
            nn_p.reshape(1, n_p, ml_heads, ml_qk), jnp.transpose(nn_s[:, GROUP - 1::GROUP], (1, 0, 2))[None],
            m_p[:, :, 0, 0][None], jnp.transpose(m_s[:, ::GROUP, 0])[None],
            cv_p[None], cv_s[None])
```

```python
import functools

import numpy as np
import jax
import jax.numpy as jnp
from jax import lax
from jax.experimental import pallas as pl
from jax.experimental.pallas import tpu as pltpu

F32 = jnp.float32
BF16 = jnp.bfloat16

EPS = 1e-6
CONV_W = 3
LANES = 128
SUBLANES = 8
VMEM_LIMIT_BYTES = 58 * 1024 * 1024

MXU_DEPTH = 256
HG_CHUNK = 64
HG_SAFE_EXP = 60.0
HG_LEVELS = (32, 16, 8, 4, 2, 1)
ML_CHUNK = 128
GROUP = 4


def _cparams(*sem):
    return pltpu.CompilerParams(dimension_semantics=sem, vmem_limit_bytes=VMEM_LIMIT_BYTES)


def _dot(a, b):
    return jnp.dot(a, b, preferred_element_type=F32)


def _dot_nt(a, b):
    return lax.dot_general(a, b, (((1,), (1,)), ((), ())), preferred_element_type=F32)


def _dot_tn(a, b):
    return lax.dot_general(a, b, (((0,), (0,)), ((), ())), preferred_element_type=F32)


def _split3(x):
    x1 = x.astype(BF16)
    r1 = x - x1.astype(F32)
    x2 = r1.astype(BF16)
    x3 = (r1 - x2.astype(F32)).astype(BF16)
    return x1, x2, x3


def _exact_left_mul(w, x):
    x1, x2, x3 = _split3(x)
    return _dot(w, x1) + _dot(w, x2) + _dot(w, x3)


def _exact_left_mul3(w3, x):
    return _dot(w3, jnp.concatenate(_split3(x), axis=0))


def _rows(ref, r0, n):
    if ref.shape[0] == 1:
        return ref[...]
    return ref[pl.ds(r0, n), :]


def _norm_mod(x, sc, sh):
    ms = jnp.mean(x * x, axis=-1, keepdims=True)
    return x * lax.rsqrt(ms + EPS) * (1.0 + sc) + sh


def _head_norm(o, w):
    return o * lax.rsqrt(jnp.mean(o * o, axis=-1, keepdims=True) + EPS) * w


def _lower_bound(lg):
    l0, l1 = lg[0:1, :], lg[1:2, :]
    m = jnp.maximum(l0, l1)
    e0, e1 = jnp.exp(l0 - m), jnp.exp(l1 - m)
    return e0 / (e0 + e1)


def _group_last(x, t, up):
    return jnp.where(t == 3, x, jnp.where(t == 2, up(x, 1), jnp.where(t == 1, up(x, 2), up(x, 3))))


def _mod_kernel(c_ref, w_ref, b_ref, o_ref):
    c = c_ref[...]
    s = (c * jax.nn.sigmoid(c)).astype(BF16)
    o_ref[...] = _dot(s, w_ref[...].astype(BF16)) + b_ref[...]


def _mod_call(c_all, ada_w, ada_b):
    mp, d = c_all.shape
    n = ada_w.shape[1]
    tn = 512
    return pl.pallas_call(
        _mod_kernel,
        grid=(n // tn,),
        in_specs=[pl.BlockSpec((mp, d), lambda j: (0, 0)),
                  pl.BlockSpec((d, tn), lambda j: (0, j)),
                  pl.BlockSpec((1, tn), lambda j: (0, j))],
        out_specs=pl.BlockSpec((mp, tn), lambda j: (0, j)),
        out_shape=jax.ShapeDtypeStruct((mp, n), F32),
        compiler_params=_cparams("arbitrary"),
        name="mod",
    )(c_all, ada_w, ada_b)


def _inproj_kernel(x_ref, sh_ref, sc_ref, wa_ref, wb_ref, wg_ref, o_ref, og_ref, h_ref, *, tm, rc, na):
    j = pl.program_id(1)

    @pl.when(j == 0)
    def _():
        def body(r, carry):
            r0 = pl.multiple_of(r * rc, rc)
            h = _norm_mod(x_ref[pl.ds(r0, rc), :], _rows(sc_ref, r0, rc), _rows(sh_ref, r0, rc))
            hb = h.astype(BF16)
            h_ref[pl.ds(r0, rc), :] = hb
            gg = _dot_nt(hb, wg_ref[...])
            og_ref[pl.ds(r0, rc), :] = jnp.concatenate(
                [gg, jnp.zeros((rc, og_ref.shape[1] - gg.shape[1]), F32)], axis=1)
            return carry
        lax.fori_loop(0, tm // rc, body, 0)

    @pl.when(j < na)
    def _():
        o_ref[...] = _dot_nt(h_ref[...], wa_ref[...])

    @pl.when(j >= na)
    def _():
        o_ref[...] = _dot_nt(h_ref[...], wb_ref[...])


def _mod_spec(mod, tm, col, tiles_per_seq):
    d = mod.shape[-1] // 6
    if mod.ndim == 3:
        return pl.BlockSpec((None, 1, d), lambda i, *_: (i // tiles_per_seq, 0, col))
    return pl.BlockSpec((tm, d), lambda i, *_: (i, col))


def _inproj_call(x, mod, wt, gate0, n_gate, tm, tiles_per_seq):
    m, d = x.shape
    tn = 1024
    bf16_rows = 2 * SUBLANES
    gate1 = gate0 + n_gate
    assert n_gate == bf16_rows and gate0 % tn == 0 and (wt.shape[0] - gate1) % tn == 0
    na, nb = gate0 // tn, (wt.shape[0] - gate1) // tn
    n = (na + nb) * tn
    kern = functools.partial(_inproj_kernel, tm=tm, rc=256, na=na)
    return pl.pallas_call(
        kern,
        grid=(m // tm, na + nb),
        in_specs=[pl.BlockSpec((tm, d), lambda i, j: (i, 0)),
                  _mod_spec(mod, tm, 0, tiles_per_seq),
                  _mod_spec(mod, tm, 1, tiles_per_seq),
                  pl.BlockSpec((tn, d), lambda i, j: (jnp.minimum(j, na - 1), 0)),
                  pl.BlockSpec((pl.Element(tn), pl.Element(d)),
                               lambda i, j: (pl.multiple_of(gate1 + jnp.maximum(j - na, 0) * tn, bf16_rows), 0)),
                  pl.BlockSpec((n_gate, d), lambda i, j: (gate0 // n_gate, 0))],
        out_specs=[pl.BlockSpec((tm, tn), lambda i, j: (i, j)),
                   pl.BlockSpec((tm, LANES), lambda i, j: (i, 0))],
        out_shape=[jax.ShapeDtypeStruct((m, n), F32), jax.ShapeDtypeStruct((m, LANES), F32)],
        scratch_shapes=[pltpu.VMEM((tm, d), BF16)],
        compiler_params=_cparams("arbitrary", "arbitrary"),
        name="inproj",
    )(x, mod, mod, wt, wt, wt)


def _hgrn_constants():
    L = HG_CHUNK
    tri = np.tril(np.ones((L, L), np.float32))
    t = np.arange(L)
    blocks = []
    masks = []
    for h in HG_LEVELS:
        mid = (t // (2 * h)) * (2 * h) + h - 1
        blocks.append(tri - tri[mid])
        masks.append((t[:, None] // (2 * h) == t[None, :] // (2 * h)).astype(np.float32))
    blocks.append(tri[L - 1][None, :] - tri)
    masks.append(np.eye(L, dtype=np.float32))
    return jnp.asarray(np.concatenate(blocks, 0), BF16), jnp.asarray(np.stack(masks, 0), F32)


def _hgrn_gates(hq, z, lb):
    omlb = 1.0 - lb
    logf = jnp.log(lb + omlb * jax.nn.sigmoid(z))
    kk = omlb * jax.nn.sigmoid(-z)
    q = hq * jax.nn.sigmoid(hq) * (hq.shape[-1] ** -0.5)
    return q, kk, logf


def _hgrn_p_kernel(hq_ref, hf_ref, hi_ref, hg_ref, lg_ref, nw_ref, trib_ref, wall_ref, mask_ref,
                   ya_ref, so_ref, st_ref, *, n_chunks):
    L = HG_CHUNK
    c = pl.program_id(2)
    tb = trib_ref.shape[0]
    last_rows = lambda y: [y[r:r + 1, :] for r in range(L - 1, n_chunks * L, L)]

    @pl.when(c == 0)
    def _():
        st_ref[...] = jnp.zeros_like(st_ref)

    lb = _lower_bound(lg_ref[...])
    nw = nw_ref[...]
    q, kk, logf = _hgrn_gates(hq_ref[...], hf_ref[...], lb)
    trib = trib_ref[...]
    b = jnp.concatenate([_exact_left_mul3(trib, logf[r:r + tb]) for r in range(0, n_chunks * L, tb)], axis=0)
    worst = -jnp.min(jnp.concatenate(last_rows(b), axis=0))

    def finish(ci, o):
        hg = hg_ref[ci * L:(ci + 1) * L, :]
        ya_ref[ci * L:(ci + 1) * L, :] = _head_norm(o, nw) * (hg * jax.nn.sigmoid(hg))

    @pl.when(worst <= HG_SAFE_EXP)
    def _():
        eb = jnp.exp(b)
        qt = (q * eb).astype(BF16)
        kn = kk * jnp.exp(-b)
        knb = kn.astype(BF16)
        row = lax.broadcasted_iota(jnp.int32, (L, L), 0)
        colid = lax.broadcasted_iota(jnp.int32, (L, L), 1)
        causal = colid <= row
        o_intra, incr, dec = [], [], []
        for ci in range(n_chunks):
            sl = slice(ci * L, (ci + 1) * L)
            vb = hi_ref[sl, :].astype(BF16)
            a = jnp.where(causal, _dot_nt(qt[sl], knb[sl]), 0.0)
            o_intra.append(_dot(a.astype(BF16), vb))
            dec.append(eb[(ci + 1) * L - 1:(ci + 1) * L, :])
            incr.append(_dot_tn(vb, (kn[sl] * dec[ci]).astype(BF16)))
        sts = [st_ref[...]]
        for ci in range(n_chunks):
            sts.append(sts[ci] * dec[ci] + incr[ci])
        st_ref[...] = sts[n_chunks]
        for ci in range(n_chunks):
            sl = slice(ci * L, (ci + 1) * L)
            finish(ci, o_intra[ci] + _dot_nt(qt[sl], sts[ci].astype(BF16)))

    @pl.when(jnp.logical_not(worst <= HG_SAFE_EXP))
    def _():
        wall = wall_ref[...]
        rowid = lax.broadcasted_iota(jnp.int32, (L, hq_ref.shape[1]), 0)
        n_lv = len(HG_LEVELS)
        for ci in range(n_chunks):
            sl = slice(ci * L, (ci + 1) * L)
            qc, kc, bc = q[sl], kk[sl], b[sl]
            vb = hi_ref[sl, :].astype(BF16)
            d = _exact_left_mul(wall, logf[sl])
            a = mask_ref[n_lv] * _dot_nt(qc.astype(BF16), kc.astype(BF16))
            for li, h in enumerate(HG_LEVELS):
                e = jnp.exp(-jnp.abs(d[li * L:(li + 1) * L]))
                second = (rowid & h) != 0
                p = jnp.where(second, qc, kc) * e
                qh = jnp.where(second, p, 0.0).astype(BF16)
                kh = jnp.where(second, 0.0, p).astype(BF16)
                a = a + mask_ref[li] * _dot_nt(qh, kh)
            qt = (qc * jnp.exp(bc)).astype(BF16)
            kt = (kc * jnp.exp(d[n_lv * L:(n_lv + 1) * L])).astype(BF16)
            st = st_ref[...]
            finish(ci, _dot(a.astype(BF16), vb) + _dot_nt(qt, st.astype(BF16)))
            st_ref[...] = st * jnp.exp(bc[L - 1:L, :]) + _dot_tn(vb, kt)

    @pl.when(c == pl.num_programs(2) - 1)
    def _():
        so_ref[...] = st_ref[...].T


def _hgrn_p_call(proj, lb_logits, norm_w, n_seq, seq_len, heads, kdim, col0, rows_per_step):
    wall, masks = _hgrn_constants()
    per_mat = min(rows_per_step, MXU_DEPTH) // HG_CHUNK
    trib = np.kron(np.eye(per_mat, dtype=np.float32), np.tril(np.ones((HG_CHUNK, HG_CHUNK), np.float32)))
    trib = jnp.asarray(np.concatenate([trib, trib, trib], axis=1), BF16)
    nc = seq_len // rows_per_step
    kern = functools.partial(_hgrn_p_kernel, n_chunks=rows_per_step // HG_CHUNK)

    def col(k):
        return pl.BlockSpec((rows_per_step, kdim), lambda b, h, c: (b * nc + c, col0 + k * heads + h))

    return pl.pallas_call(
        kern,
        grid=(n_seq, heads, nc),
        in_specs=[col(0), col(1), col(2), col(3),
                  pl.BlockSpec((2, kdim), lambda b, h, c: (0, h)),
                  pl.BlockSpec((1, kdim), lambda b, h, c: (0, h)),
                  pl.BlockSpec(trib.shape, lambda b, h, c: (0, 0)),
                  pl.BlockSpec(wall.shape, lambda b, h, c: (0, 0)),
                  pl.BlockSpec(masks.shape, lambda b, h, c: (0, 0, 0))],
        out_specs=[pl.BlockSpec((rows_per_step, kdim), lambda b, h, c: (b * nc + c, h)),
                   pl.BlockSpec((None, None, kdim, kdim), lambda b, h, c: (b, h, 0, 0))],
        out_shape=[jax.ShapeDtypeStruct((n_seq * seq_len, heads * kdim), F32),
                   jax.ShapeDtypeStruct((n_seq, heads, kdim, kdim), F32)],
        scratch_shapes=[pltpu.VMEM((kdim, kdim), F32)],
        compiler_params=_cparams("arbitrary", "arbitrary", "arbitrary"),
        name="hgrn_prompt",
    )(proj, proj, proj, proj, lb_logits, norm_w, trib, wall, masks)


def _hgrn_s_kernel(hq_ref, hf_ref, hi_ref, hg_ref, lg_ref, nw_ref, s_ref, ya_ref, so_ref, *, n_pairs):
    kdim = hq_ref.shape[1]
    rows = n_pairs * SUBLANES
    lb = _lower_bound(lg_ref[...])
    rowid = lax.broadcasted_iota(jnp.int32, (rows, kdim), 0)
    t = rowid & (GROUP - 1)
    first = (rowid & GROUP) == 0
    first8 = lax.broadcasted_iota(jnp.int32, (SUBLANES, kdim), 0) < GROUP
    down = lambda y, j: pltpu.roll(y, j, 0)
    up = lambda y, j: pltpu.roll(y, rows - j, 0)

    q, kk, logf = _hgrn_gates(hq_ref[...], hf_ref[...], lb)
    v = hi_ref[...]
    b = logf
    for dlt in range(1, GROUP):
        b = b + jnp.where(t >= dlt, down(logf, dlt), 0.0)
    o = jnp.sum(q * kk, axis=1, keepdims=True) * v
    for dlt in range(1, GROUP):
        x = q * down(kk, dlt) * jnp.exp(b - down(b, dlt))
        a = jnp.sum(jnp.where(t >= dlt, x, 0.0), axis=1, keepdims=True)
        o = o + a * down(v, dlt)
    b_last = _group_last(b, t, up)
    qt = q * jnp.exp(b)
    kt = kk * jnp.exp(b_last - b)
    d1, d2, d3 = _split3(jnp.exp(b_last))
    swap = lambda y: jnp.where(first, up(y.astype(F32), GROUP), down(y.astype(F32), GROUP))
    dsplit = jnp.where(t == 0, swap(d1), jnp.where(t == 1, swap(d2), jnp.where(t == 2, swap(d3), 0.0)))
    ones = jnp.where(t <= 2, 1.0, 0.0)
    lhs = (jnp.where(first, kt, dsplit), jnp.where(first, dsplit, kt))
    rhs = (jnp.concatenate([jnp.where(first, v, 0.0), jnp.where(first, 0.0, ones)], axis=1),
           jnp.concatenate([jnp.where(first, 0.0, v), jnp.where(first, ones, 0.0)], axis=1))
    o_inter = []
    for p in range(n_pairs):
        sl = slice(p * SUBLANES, (p + 1) * SUBLANES)
        qb = qt[sl].astype(BF16)
        parts = []
        for half in range(2):
            s0 = s_ref[2 * p + half]
            upd = _dot_tn(lhs[half][sl].astype(BF16), rhs[half][sl].astype(BF16))
            so_ref[2 * p + half] = s0 * upd[:, kdim:] + upd[:, :kdim]
            parts.append(_dot(qb, s0.astype(BF16)))
        o_inter.append(jnp.where(first8, parts[0], parts[1]))
    o = o + jnp.concatenate(o_inter, axis=0)
    hg = hg_ref[...]
    ya_ref[...] = _head_norm(o, nw_ref[...]) * (hg * jax.nn.sigmoid(hg))


def _hgrn_s_call(proj, lb_logits, norm_w, state, heads, kdim, col0, seqs_per_step):
    n_seq = state.shape[0]
    rows = seqs_per_step * GROUP
    kern = functools.partial(_hgrn_s_kernel, n_pairs=seqs_per_step // 2)

    def col(k):
        return pl.BlockSpec((rows, kdim), lambda i, h: (i, col0 + k * heads + h))

    st_spec = pl.BlockSpec((seqs_per_step, None, kdim, kdim), lambda i, h: (i, h, 0, 0))
    return pl.pallas_call(
        kern,
        grid=(n_seq // seqs_per_step, heads),
        in_specs=[col(0), col(1), col(2), col(3),
                  pl.BlockSpec((2, kdim), lambda i, h: (0, h)),
                  pl.BlockSpec((1, kdim), lambda i, h: (0, h)),
                  st_spec],
        out_specs=[pl.BlockSpec((rows, kdim), lambda i, h: (i, h)), st_spec],
        out_shape=[jax.ShapeDtypeStruct((n_seq * GROUP, heads * kdim), F32),
                   jax.ShapeDtypeStruct(state.shape, F32)],
        compiler_params=_cparams("arbitrary", "arbitrary"),
        name="hgrn_sample",
    )(proj, proj, proj, proj, lb_logits, norm_w, state)


def _lane_pick(x, lane, idx):
    return jnp.broadcast_to(jnp.sum(jnp.where(lane == idx, x, 0.0), axis=1, keepdims=True), x.shape)


def _interleave(chains):
    live = list(chains)
    while live:
        still = []
        for g in live:
            try:
                next(g)
                still.append(g)
            except StopIteration:
                pass
        live = still


def _mlstm_p_kernel(q_ref, k_ref, v_ref, og_ref, g_ref, bias_ref, nw_ref, tri_ref,
                    yb_ref, co_ref, no_ref, mo_ref, c_s, n_s, m_s, *, heads, qk, vd):
    L = ML_CHUNK
    c = pl.program_id(1)

    @pl.when(c == 0)
    def _():
        c_s[...] = jnp.zeros_like(c_s)
        n_s[...] = jnp.zeros_like(n_s)
        m_s[...] = jnp.zeros_like(m_s)

    g = g_ref[...] + bias_ref[...]
    lane = lax.broadcasted_iota(jnp.int32, g.shape, 1)
    row = lax.broadcasted_iota(jnp.int32, g.shape, 0)
    b_all = _exact_left_mul(tri_ref[...], jax.nn.log_sigmoid(g))

    def head(hd):
        qs = slice(hd * qk, (hd + 1) * qk)
        vs = slice(hd * vd, (hd + 1) * vd)
        ig = _lane_pick(g, lane, hd)
        b = _lane_pick(b_all, lane, heads + hd)
        m_prev = m_s[hd]
        gs = (ig - b).T
        yield
        dm = jnp.where(lane <= row, b + gs, -jnp.inf)
        mt = jnp.maximum(b + m_prev, jnp.max(dm, axis=1, keepdims=True))
        inter = jnp.exp(b + m_prev - mt)
        q = q_ref[:, qs] * (qk ** -0.5)
        k = k_ref[:, qs]
        vb = v_ref[:, vs].astype(BF16)
        qb = q.astype(BF16)
        sc = _dot_nt(qb, k.astype(BF16)) * jnp.exp(dm - mt)
        yield
        c0 = c_s[hd]
        n0 = n_s[hd]
        num = inter[:, 0:1] * _dot(qb, c0.astype(BF16)) + _dot(sc.astype(BF16), vb)
        den = inter[:, 0:1] * jnp.sum(q * n0, axis=1, keepdims=True) + jnp.sum(sc, axis=1, keepdims=True)
        yield
        hh = num / jnp.maximum(jnp.abs(den), jnp.exp(-mt[:, 0:1]))
        yb_ref[:, vs] = _head_norm(hh, nw_ref[:, vs]) * jax.nn.sigmoid(og_ref[:, vs])
        yield
        m_last = mt[L - 1:L, :]
        b_last = b[L - 1:L, :]
        dec = jnp.exp(b_last + m_prev - m_last)
        kw = jnp.exp(b_last - b + ig - m_last) * k
        c_s[hd] = jnp.concatenate([dec, dec], axis=1) * c0 + _dot_tn(kw.astype(BF16), vb)
        n_s[hd] = dec * n0 + jnp.sum(kw, axis=0, keepdims=True)
        m_s[hd] = m_last

    _interleave([head(hd) for hd in range(heads)])

    @pl.when(c == pl.num_programs(1) - 1)
    def _():
        co_ref[...] = c_s[...]
        no_ref[...] = n_s[...]
        mo_ref[...] = m_s[...]


def _mlstm_p_call(proj, gates, bias, norm_w, n_seq, seq_len, heads, qk, vd, colq, colk, colv, colo):
    L = ML_CHUNK
    assert qk == L and qk == LANES
    nc = seq_len // L
    qw, vw = heads * qk, heads * vd
    assert colq % qw == 0 and colk % qw == 0 and colv % vw == 0 and colo % vw == 0
    tri = jnp.asarray(np.tril(np.ones((L, L), np.float32)), BF16)
    kern = functools.partial(_mlstm_p_kernel, heads=heads, qk=qk, vd=vd)
    return pl.pallas_call(
        kern,
        grid=(n_seq, nc),
        in_specs=[pl.BlockSpec((L, qw), lambda b, c: (b * nc + c, colq // qw)),
                  pl.BlockSpec((L, qw), lambda b, c: (b * nc + c, colk // qw)),
                  pl.BlockSpec((L, vw), lambda b, c: (b * nc + c, colv // vw)),
                  pl.BlockSpec((L, vw), lambda b, c: (b * nc + c, colo // vw)),
                  pl.BlockSpec((L, LANES), lambda b, c: (b * nc + c, 0)),
                  pl.BlockSpec((1, LANES), lambda b, c: (0, 0)),
                  pl.BlockSpec((1, vw), lambda b, c: (0, 0)),
                  pl.BlockSpec((L, L), lambda b, c: (0, 0))],
        out_specs=[pl.BlockSpec((L, vw), lambda b, c: (b * nc + c, 0)),
                   pl.BlockSpec((None, heads, qk, vd), lambda b, c: (b, 0, 0, 0)),
                   pl.BlockSpec((None, heads, 1, qk), lambda b, c: (b, 0, 0, 0)),
                   pl.BlockSpec((None, heads, 1, LANES), lambda b, c: (b, 0, 0, 0))],
        out_shape=[jax.ShapeDtypeStruct((n_seq * seq_len, vw), F32),
                   jax.ShapeDtypeStruct((n_seq, heads, qk, vd), F32),
                   jax.ShapeDtypeStruct((n_seq, heads, 1, qk), F32),
                   jax.ShapeDtypeStruct((n_seq, heads, 1, LANES), F32)],
        scratch_shapes=[pltpu.VMEM((heads, qk, vd), F32), pltpu.VMEM((heads, 1, qk), F32),
                        pltpu.VMEM((heads, 1, LANES), F32)],
        compiler_params=_cparams("arbitrary", "arbitrary"),
        name="mlstm_prompt",
    )(proj, proj, proj, proj, gates, bias, norm_w, tri)


def _mlstm_s_kernel(q_ref, k_ref, v_ref, og_ref, g_ref, bias_ref, m_ref, nw_ref, c_ref, n_ref,
                    yb_ref, co_ref, no_ref, mo_ref, *, heads, n_pairs):
    h = pl.program_id(1)
    qk = q_ref.shape[1]
    rows = n_pairs * SUBLANES
    lane = lax.broadcasted_iota(jnp.int32, (rows, LANES), 1)
    rowid = lax.broadcasted_iota(jnp.int32, (rows, LANES), 0)
    t = rowid & (GROUP - 1)
    first = (rowid & GROUP) == 0
    first8 = lax.broadcasted_iota(jnp.int32, (SUBLANES, 1), 0) < GROUP
    down = lambda y, j: pltpu.roll(y, j, 0)
    up = lambda y, j: pltpu.roll(y, rows - j, 0)

    g = g_ref[...] + bias_ref[...]
    ig = _lane_pick(g, lane, h)
    lf = _lane_pick(jax.nn.log_sigmoid(g), lane, heads + h)
    m_prev = _lane_pick(m_ref[...], lane, h)
    b = lf
    for dlt in range(1, GROUP):
        b = b + jnp.where(t >= dlt, down(lf, dlt), 0.0)
    q = q_ref[...] * (qk ** -0.5)
    k = k_ref[...]
    v = v_ref[...]
    dms = [ig] + [jnp.where(t >= dlt, b - down(b, dlt) + down(ig, dlt), -jnp.inf) for dlt in range(1, GROUP)]
    mt = jnp.maximum(b + m_prev, functools.reduce(jnp.maximum, dms))
    inter = jnp.exp(b + m_prev - mt)[:, 0:1]
    num = jnp.zeros(v.shape, F32)
    den = jnp.zeros((rows, 1), F32)
    for dlt in range(GROUP):
        kd = k if dlt == 0 else down(k, dlt)
        vd_ = v if dlt == 0 else down(v, dlt)
        s = jnp.sum(q * kd, axis=1, keepdims=True) * jnp.exp(dms[dlt] - mt)[:, 0:1]
        num = num + s * vd_
        den = den + s
    m_last = _group_last(mt, t, up)
    b_last = _group_last(b, t, up)
    dec = jnp.exp(b_last + m_prev - m_last)
    kw = jnp.exp(b_last - b + ig - m_last) * k
    n0 = n_ref[...]
    no_ref[...] = dec * n0 + kw + down(kw, 1) + down(kw, 2) + down(kw, 3)
    mo_ref[...] = m_last
    den = den + inter * jnp.sum(q * n0, axis=1, keepdims=True)

    kw_half = (jnp.where(first, kw, 0.0), jnp.where(first, 0.0, kw))
    dec2 = jnp.concatenate([dec, dec], axis=1)
    num_inter = []
    for p in range(n_pairs):
        sl = slice(p * SUBLANES, (p + 1) * SUBLANES)
        qb = q[sl].astype(BF16)
        vb = v[sl].astype(BF16)
        parts = []
        for half in range(2):
            c0 = c_ref[2 * p + half]
            r = p * SUBLANES + half * GROUP
            co_ref[2 * p + half] = dec2[r:r + 1, :] * c0 + _dot_tn(kw_half[half][sl].astype(BF16), vb)
            parts.append(_dot(qb, c0.astype(BF16)))
        num_inter.append(jnp.where(first8, parts[0], parts[1]))
    num = num + inter * jnp.concatenate(num_inter, axis=0)
    hh = num / jnp.maximum(jnp.abs(den), jnp.exp(-mt)[:, 0:1])
    yb_ref[...] = _head_norm(hh, nw_ref[...]) * jax.nn.sigmoid(og_ref[...])


def _mlstm_s_call(proj, gates, bias, m_rows, norm_w, c_state, n_state, heads, qk, vd,
                  colq, colk, colv, colo, seqs_per_step):
    n_seq = c_state.shape[0]
    rows = seqs_per_step * GROUP
    kern = functools.partial(_mlstm_s_kernel, heads=heads, n_pairs=seqs_per_step // 2)
    c_spec = pl.BlockSpec((seqs_per_step, None, qk, vd), lambda i, h: (i, h, 0, 0))
    n_spec = pl.BlockSpec((None, rows, qk), lambda i, h: (h, i, 0))
    return pl.pallas_call(
        kern,
        grid=(n_seq // seqs_per_step, heads),
        in_specs=[pl.BlockSpec((rows, qk), lambda i, h: (i, colq // qk + h)),
                  pl.BlockSpec((rows, qk), lambda i, h: (i, colk // qk + h)),
                  pl.BlockSpec((rows, vd), lambda i, h: (i, colv // vd + h)),
                  pl.BlockSpec((rows, vd), lambda i, h: (i, colo // vd + h)),
                  pl.BlockSpec((rows, LANES), lambda i, h: (i, 0)),
                  pl.BlockSpec((1, LANES), lambda i, h: (0, 0)),
                  pl.BlockSpec((rows, LANES), lambda i, h: (i, 0)),
                  pl.BlockSpec((1, vd), lambda i, h: (0, h)),
                  c_spec, n_spec],
        out_specs=[pl.BlockSpec((rows, vd), lambda i, h: (i, h)),
                   c_spec, n_spec,
                   pl.BlockSpec((None, rows, LANES), lambda i, h: (h, i, 0))],
        out_shape=[jax.ShapeDtypeStruct((n_seq * GROUP, heads * vd), F32),
                   jax.ShapeDtypeStruct(c_state.shape, F32),
                   jax.ShapeDtypeStruct(n_state.shape, F32),
                   jax.ShapeDtypeStruct((heads, n_seq * GROUP, LANES), F32)],
        compiler_params=_cparams("arbitrary", "arbitrary"),
        name="mlstm_sample",
    )(proj, proj, proj, proj, gates, bias, m_rows, norm_w, c_state, n_state)


def _outproj_kernel(ya_ref, yb_ref, ga_ref, gb_ref, x_ref, g1_ref, w_ref, o_ref):
    merged = jax.nn.sigmoid(ga_ref[...]) * ya_ref[...] + jax.nn.sigmoid(gb_ref[...]) * yb_ref[...]
    o_ref[...] = x_ref[...] + g1_ref[...] * _dot(merged.astype(BF16), w_ref[...])


def _outproj_call(ya, yb, proj, x, mod, w_out, col_ga, col_gb, tm, tiles_per_seq):
    m, d = x.shape
    blk = lambda col: pl.BlockSpec((tm, d), lambda i: (i, col))
    return pl.pallas_call(
        _outproj_kernel,
        grid=(m // tm,),
        in_specs=[blk(0), blk(0), blk(col_ga), blk(col_gb), blk(0),
                  _mod_spec(mod, tm, 2, tiles_per_seq),
                  pl.BlockSpec((d, d), lambda i: (0, 0))],
        out_specs=blk(0),
        out_shape=jax.ShapeDtypeStruct((m, d), F32),
        compiler_params=_cparams("arbitrary"),
        name="outproj",
    )(ya, yb, proj, proj, x, mod, w_out)


def _ffn_kernel(*refs, tm, rc, tiles_per_seq, grouped):
    if grouped:
        (x_ref, sh_ref, sc_ref, g2_ref, wa_ref, wg_ref, wd_ref, cw_ref, cb_ref, fw_ref, p1_ref, p2_ref,
         y_ref, a_ref, h_s, acc_s) = refs
    else:
        (x_ref, sh_ref, sc_ref, g2_ref, wa_ref, wg_ref, wd_ref, cw_ref, cb_ref, fw_ref,
         y_ref, a_ref, h_s, acc_s, carry_s) = refs
    i = pl.program_id(0)
    j = pl.program_id(1)

    @pl.when(j == 0)
    def _():
        def body(r, carry):
            r0 = pl.multiple_of(r * rc, rc)
            h = _norm_mod(x_ref[pl.ds(r0, rc), :], _rows(sc_ref, r0, rc), _rows(sh_ref, r0, rc))
            h_s[pl.ds(r0, rc), :] = h.astype(BF16)
            return carry
        lax.fori_loop(0, tm // rc, body, 0)
        acc_s[...] = jnp.zeros_like(acc_s)

    hb = h_s[...]
    a = _dot(hb, wa_ref[...])
    g = _dot(hb, wg_ref[...])
    rowid = lax.broadcasted_iota(jnp.int32, a.shape, 0)
    r1 = pltpu.roll(a, 1, 0)
    r2 = pltpu.roll(a, 2, 0)
    if grouped:
        t = rowid & (GROUP - 1)
        prev1 = jnp.where(t == 0, p1_ref[...], r1)
        prev2 = jnp.where(t <= 1, p2_ref[...], r2)
        a_ref[...] = a
    else:
        tail = a[tm - SUBLANES:tm, :]
        car = jnp.where(i % tiles_per_seq == 0, 0.0, carry_s[j])
        c1 = car[SUBLANES - 1:SUBLANES, :]
        c2 = car[SUBLANES - 2:SUBLANES - 1, :]
        prev1 = jnp.where(rowid == 0, c1, r1)
        prev2 = jnp.where(rowid == 0, c2, jnp.where(rowid == 1, c1, r2))
        carry_s[j] = tail
        a_ref[...] = tail
    cw = cw_ref[...]
    ac = cw[0:1, :] * prev2 + cw[1:2, :] * prev1 + cw[2:3, :] * a + cb_ref[...]
    yv = 0.5 * ac * (1.0 + lax.erf(ac * (2.0 ** -0.5))) * g
    acc_s[...] += _dot(yv.astype(BF16), wd_ref[...])

    @pl.when(j == pl.num_programs(1) - 1)
    def _():
        def body(r, carry):
            r0 = pl.multiple_of(r * rc, rc)
            x2 = x_ref[pl.ds(r0, rc), :] + _rows(g2_ref, r0, rc) * acc_s[pl.ds(r0, rc), :]
            ms = jnp.mean(x2 * x2, axis=-1, keepdims=True)
            y_ref[pl.ds(r0, rc), :] = x2 * lax.rsqrt(ms + EPS) * fw_ref[...]
            return carry
        lax.fori_loop(0, tm // rc, body, 0)


def _ffn_call(x, mod, w_up, w_down, conv_w, conv_b, final_w, tm, tiles_per_seq, p1=None, p2=None):
    m, d = x.shape
    f = w_down.shape[0]
    tf = 512
    nj = f // tf
    grouped = p1 is not None
    kern = functools.partial(_ffn_kernel, tm=tm, rc=min(tm, 256), tiles_per_seq=tiles_per_seq, grouped=grouped)
    in_specs = [pl.BlockSpec((tm, d), lambda i, j: (i, 0)),
                _mod_spec(mod, tm, 3, tiles_per_seq),
                _mod_spec(mod, tm, 4, tiles_per_seq),
                _mod_spec(mod, tm, 5, tiles_per_seq),
                pl.BlockSpec((d, tf), lambda i, j: (0, j)),
                pl.BlockSpec((d, tf), lambda i, j: (0, nj + j)),
                pl.BlockSpec((tf, d), lambda i, j: (j, 0)),
                pl.BlockSpec((CONV_W, tf), lambda i, j: (0, j)),
                pl.BlockSpec((1, tf), lambda i, j: (0, j)),
                pl.BlockSpec((1, d), lambda i, j: (0, 0))]
    args = [x, mod, mod, mod, w_up, w_up, w_down, conv_w, conv_b, final_w]
    scratch = [pltpu.VMEM((tm, d), BF16), pltpu.VMEM((tm, d), F32)]
    if grouped:
        in_specs += [pl.BlockSpec((tm, tf), lambda i, j: (i, j))] * 2
        args += [p1, p2]
        a_spec = pl.BlockSpec((tm, tf), lambda i, j: (i, j))
        a_shape = jax.ShapeDtypeStruct((m, f), F32)
    else:
        scratch.append(pltpu.VMEM((nj, SUBLANES, tf), F32))
        a_spec = pl.BlockSpec((SUBLANES, tf), lambda i, j: (i, j))
        a_shape = jax.ShapeDtypeStruct((m // tm * SUBLANES, f), F32)
    return pl.pallas_call(
        kern,
        grid=(m // tm, nj),
        in_specs=in_specs,
        out_specs=[pl.BlockSpec((tm, d), lambda i, j: (i, 0)), a_spec],
        out_shape=[jax.ShapeDtypeStruct((m, d), F32), a_shape],
        scratch_shapes=scratch,
        compiler_params=_cparams("arbitrary", "arbitrary"),
        name="ffn",
    )(*args)


def kernel(x_prompt, x_sample, c_prompt, c_sample, state_hgrn, state_mlstm_C, state_mlstm_n, state_mlstm_m,
           state_conv, ada_w, ada_b, w_in, hg_lb_logits, hg_norm_w, ml_i_bias, ml_f_bias, ml_norm_w, w_out,
           conv_w, conv_b, w_up, w_down, final_norm_w):
    n_p, seq, d = x_prompt.shape
    n_s, dec_seq, _ = x_sample.shape
    depth, _, hg_heads, hg_k, hg_v = state_hgrn.shape
    _, _, ml_heads, ml_qk, ml_v = state_mlstm_C.shape
    f = w_down.shape[1]
    assert depth == 1 and dec_seq == GROUP and hg_k == hg_v == LANES and 2 * ml_heads <= LANES
    assert hg_lb_logits.shape[0] == 2

    hg_w = hg_heads * hg_k
    gate0 = 4 * hg_w + 2 * ml_heads * ml_qk + ml_heads * ml_v
    gate1 = gate0 + 2 * ml_heads
    wt = jnp.swapaxes(w_in[0], 0, 1).astype(BF16)
    col_mq = 4 * hg_w
    col_mk = col_mq + ml_heads * ml_qk
    col_mv = col_mk + ml_heads * ml_qk
    col_mo = col_mv + ml_heads * ml_v
    col_ga = gate0 // d + 1
    col_gb = col_ga + 1
    assert gate0 % d == 0 and (gate0 // d + 1) * d == gate0 + ml_heads * ml_v
    gate_bias = jnp.pad(jnp.concatenate([ml_i_bias[0], ml_f_bias[0]]), (0, LANES - 2 * ml_heads))[None, :]

    n_all = n_p + n_s
    c_all = jnp.pad(jnp.concatenate([c_prompt, c_sample], axis=0), ((0, (-n_all) % SUBLANES), (0, 0)))
    mod = _mod_call(c_all, ada_w[0], ada_b)
    mod_p = mod[:n_p][:, None, :]
    mod_s = jnp.repeat(mod[n_p:n_all], GROUP, axis=0)

    w_out_b = w_out[0].astype(BF16)
    w_up_b = w_up[0].astype(BF16)
    w_down_b = w_down[0].astype(BF16)
    hg_nw = hg_norm_w
    ml_nw = ml_norm_w
    fw = final_norm_w[None, :]

    xp = x_prompt.reshape(n_p * seq, d)
    tm_p = 1024
    proj_p, gates_p = _inproj_call(xp, mod_p, wt, gate0, gate1 - gate0, tm_p, seq // tm_p)
    ya_p, hg_p = _hgrn_p_call(proj_p, hg_lb_logits, hg_nw, n_p, seq, hg_heads, hg_k, 0, 1024)
    yb_p, c_p, nn_p, m_p = _mlstm_p_call(proj_p, gates_p, gate_bias, ml_nw, n_p, seq, ml_heads, ml_qk, ml_v,
                                         col_mq, col_mk, col_mv, col_mo)
    tm_o = 256
    x1_p = _outproj_call(ya_p, yb_p, proj_p, xp, mod_p, w_out_b, col_ga, col_gb, tm_o, seq // tm_o)
    tm_f = 512
    tiles = seq // tm_f
    y_p, atail = _ffn_call(x1_p, mod_p, w_up_b, w_down_b, conv_w[0], conv_b, fw, tm_f, tiles)
    cv_p = atail.reshape(n_p, tiles, SUBLANES, f)[:, tiles - 1, SUBLANES - (CONV_W - 1):]

    xs = x_sample.reshape(n_s * GROUP, d)
    m_s_rows = n_s * GROUP
    proj_s, gates_s = _inproj_call(xs, mod_s, wt, gate0, gate1 - gate0, m_s_rows, 1)
    ya_s, hg_s = _hgrn_s_call(proj_s, hg_lb_logits, hg_nw, state_hgrn[0], hg_heads, hg_k, 0, 32)
    m_rows = jnp.pad(jnp.repeat(state_mlstm_m[0], GROUP, axis=0), ((0, 0), (0, LANES - ml_heads)))
    n_hb = jnp.repeat(jnp.transpose(state_mlstm_n[0], (1, 0, 2)), GROUP, axis=1)
    yb_s, c_s, nn_s, m_s = _mlstm_s_call(proj_s, gates_s, gate_bias, m_rows, ml_nw, state_mlstm_C[0], n_hb,
                                         ml_heads, ml_qk, ml_v, col_mq, col_mk, col_mv, col_mo, 32)
    x1_s = _outproj_call(ya_s, yb_s, proj_s, xs, mod_s, w_out_b, col_ga, col_gb, tm_o, 1)
    cst = state_conv[0]
    zero = jnp.zeros_like(cst[:, :1])
    p1 = jnp.concatenate([cst[:, 1:2], zero, zero, zero], axis=1).reshape(m_s_rows, f)
    p2 = jnp.concatenate([cst, zero, zero], axis=1).reshape(m_s_rows, f)
    y_s, a_s = _ffn_call(x1_s, mod_s, w_up_b, w_down_b, conv_w[0], conv_b, fw, m_s_rows, 1, p1, p2)
    cv_s = a_s.reshape(n_s, GROUP, f)[:, GROUP - (CONV_W - 1):]

    return (y_p.reshape(n_p, seq, d), y_s.reshape(n_s, GROUP, d),
            hg_p[None], hg_s[None],
            c_p[None], c_s[None],
            nn_p.reshape(1, n_p, ml_heads, ml_qk), jnp.transpose(nn_s[:, GROUP - 1::GROUP], (1, 0, 2))[None],
            m_p[:, :, 0, 0][None], jnp.transpose(m_s[:, ::GROUP, 0])[None],
            cv_p[None], cv_s[None])
```

```python
import functools

import numpy as np
import jax
import jax.numpy as jnp
from jax import lax
from jax.experimental import pallas as pl
from jax.experimental.pallas import tpu as pltpu

F32 = jnp.float32
BF16 = jnp.bfloat16

EPS = 1e-6
CONV_W = 3
LANES = 128
SUBLANES = 8
VMEM_LIMIT_BYTES = 58 * 1024 * 1024

MXU_DEPTH = 256
HG_CHUNK = 64
HG_SAFE_EXP = 60.0
HG_LEVELS = (32, 16, 8, 4, 2, 1)
ML_CHUNK = 128
GROUP = 4


def _cparams(*sem):
    return pltpu.CompilerParams(dimension_semantics=sem, vmem_limit_bytes=VMEM_LIMIT_BYTES)


def _dot(a, b):
    return jnp.dot(a, b, preferred_element_type=F32)


def _dot_nt(a, b):
    return lax.dot_general(a, b, (((1,), (1,)), ((), ())), preferred_element_type=F32)


def _dot_tn(a, b):
    return lax.dot_general(a, b, (((0,), (0,)), ((), ())), preferred_element_type=F32)


def _split3(x):
    x1 = x.astype(BF16)
    r1 = x - x1.astype(F32)
    x2 = r1.astype(BF16)
    x3 = (r1 - x2.astype(F32)).astype(BF16)
    return x1, x2, x3


def _exact_left_mul(w, x):
    x1, x2, x3 = _split3(x)
    return _dot(w, x1) + _dot(w, x2) + _dot(w, x3)


def _exact_left_mul3(w3, x):
    return _dot(w3, jnp.concatenate(_split3(x), axis=0))


def _rows(ref, r0, n):
    if ref.shape[0] == 1:
        return ref[...]
    return ref[pl.ds(r0, n), :]


def _norm_mod(x, sc, sh):
    ms = jnp.mean(x * x, axis=-1, keepdims=True)
    return x * lax.rsqrt(ms + EPS) * (1.0 + sc) + sh


def _head_norm(o, w):
    return o * lax.rsqrt(jnp.mean(o * o, axis=-1, keepdims=True) + EPS) * w


def _lower_bound(lg):
    l0, l1 = lg[0:1, :], lg[1:2, :]
    m = jnp.maximum(l0, l1)
    e0, e1 = jnp.exp(l0 - m), jnp.exp(l1 - m)
    return e0 / (e0 + e1)


def _group_last(x, t, up):
    return jnp.where(t == 3, x, jnp.where(t == 2, up(x, 1), jnp.where(t == 1, up(x, 2), up(x, 3))))


def _mod_kernel(c_ref, w_ref, b_ref, o_ref):
    c = c_ref[...]
    s = (c * jax.nn.sigmoid(c)).astype(BF16)
    o_ref[...] = _dot(s, w_ref[...].astype(BF16)) + b_ref[...]


def _mod_call(c_all, ada_w, ada_b):
    mp, d = c_all.shape
    n = ada_w.shape[1]
    tn = 512
    return pl.pallas_call(
        _mod_kernel,
        grid=(n // tn,),
        in_specs=[pl.BlockSpec((mp, d), lambda j: (0, 0)),
                  pl.BlockSpec((d, tn), lambda j: (0, j)),
                  pl.BlockSpec((1, tn), lambda j: (0, j))],
        out_specs=pl.BlockSpec((mp, tn), lambda j: (0, j)),
        out_shape=jax.ShapeDtypeStruct((mp, n), F32),
        compiler_params=_cparams("arbitrary"),
        name="mod",
    )(c_all, ada_w, ada_b)


def _inproj_kernel(x_ref, sh_ref, sc_ref, wa_ref, wb_ref, wg_ref, o_ref, og_ref, h_ref, *, tm, rc, na):
    j = pl.program_id(1)

    @pl.when(j == 0)
    def _():
        def body(r, carry):
            r0 = pl.multiple_of(r * rc, rc)
            h = _norm_mod(x_ref[pl.ds(r0, rc), :], _rows(sc_ref, r0, rc), _rows(sh_ref, r0, rc))
            hb = h.astype(BF16)
            h_ref[pl.ds(r0, rc), :] = hb
            gg = _dot_nt(hb, wg_ref[...])
            og_ref[pl.ds(r0, rc), :] = jnp.concatenate(
                [gg, jnp.zeros((rc, og_ref.shape[1] - gg.shape[1]), F32)], axis=1)
            return carry
        lax.fori_loop(0, tm // rc, body, 0)

    @pl.when(j < na)
    def _():
        o_ref[...] = _dot_nt(h_ref[...], wa_ref[...])

    @pl.when(j >= na)
    def _():
        o_ref[...] = _dot_nt(h_ref[...], wb_ref[...])


def _mod_spec(mod, tm, col, tiles_per_seq):
    d = mod.shape[-1] // 6
    if mod.ndim == 3:
        return pl.BlockSpec((None, 1, d), lambda i, *_: (i // tiles_per_seq, 0, col))
    return pl.BlockSpec((tm, d), lambda i, *_: (i, col))


def _inproj_call(x, mod, wt, gate0, n_gate, tm, tiles_per_seq):
    m, d = x.shape
    tn = 1024
    bf16_rows = 2 * SUBLANES
    gate1 = gate0 + n_gate
    assert n_gate == bf16_rows and gate0 % tn == 0 and (wt.shape[0] - gate1) % tn == 0
    na, nb = gate0 // tn, (wt.shape[0] - gate1) // tn
    n = (na + nb) * tn
    kern = functools.partial(_inproj_kernel, tm=tm, rc=256, na=na)
    return pl.pallas_call(
        kern,
        grid=(m // tm, na + nb),
        in_specs=[pl.BlockSpec((tm, d), lambda i, j: (i, 0)),
                  _mod_spec(mod, tm, 0, tiles_per_seq),
                  _mod_spec(mod, tm, 1, tiles_per_seq),
                  pl.BlockSpec((tn, d), lambda i, j: (jnp.minimum(j, na - 1), 0)),
                  pl.BlockSpec((pl.Element(tn), pl.Element(d)),
                               lambda i, j: (pl.multiple_of(gate1 + jnp.maximum(j - na, 0) * tn, bf16_rows), 0)),
                  pl.BlockSpec((n_gate, d), lambda i, j: (gate0 // n_gate, 0))],
        out_specs=[pl.BlockSpec((tm, tn), lambda i, j: (i, j)),
                   pl.BlockSpec((tm, LANES), lambda i, j: (i, 0))],
        out_shape=[jax.ShapeDtypeStruct((m, n), F32), jax.ShapeDtypeStruct((m, LANES), F32)],
        scratch_shapes=[pltpu.VMEM((tm, d), BF16)],
        compiler_params=_cparams("arbitrary", "arbitrary"),
        name="inproj",
    )(x, mod, mod, wt, wt, wt)


def _hgrn_constants():
    L = HG_CHUNK
    tri = np.tril(np.ones((L, L), np.float32))
    t = np.arange(L)
    blocks = []
    masks = []
    for h in HG_LEVELS:
        mid = (t // (2 * h)) * (2 * h) + h - 1
        blocks.append(tri - tri[mid])
        masks.append((t[:, None] // (2 * h) == t[None, :] // (2 * h)).astype(np.float32))
    blocks.append(tri[L - 1][None, :] - tri)
    masks.append(np.eye(L, dtype=np.float32))
    return jnp.asarray(np.concatenate(blocks, 0), BF16), jnp.asarray(np.stack(masks, 0), F32)


def _hgrn_gates(hq, z, lb):
    omlb = 1.0 - lb
    logf = jnp.log(lb + omlb * jax.nn.sigmoid(z))
    kk = omlb * jax.nn.sigmoid(-z)
    q = hq * jax.nn.sigmoid(hq) * (hq.shape[-1] ** -0.5)
    return q, kk, logf


def _hgrn_p_kernel(hq_ref, hf_ref, hi_ref, hg_ref, ga_ref, lg_ref, nw_ref, trib_ref, wall_ref, mask_ref,
                   ya_ref, so_ref, st_ref, *, n_chunks):
    L = HG_CHUNK
    c = pl.program_id(2)
    tb = trib_ref.shape[0]
    last_rows = lambda y: [y[r:r + 1, :] for r in range(L - 1, n_chunks * L, L)]

    @pl.when(c == 0)
    def _():
        st_ref[...] = jnp.zeros_like(st_ref)

    lb = _lower_bound(lg_ref[...])
    nw = nw_ref[...]
    q, kk, logf = _hgrn_gates(hq_ref[...], hf_ref[...], lb)
    trib = trib_ref[...]
    b = jnp.concatenate([_exact_left_mul3(trib, logf[r:r + tb]) for r in range(0, n_chunks * L, tb)], axis=0)
    worst = -jnp.min(jnp.concatenate(last_rows(b), axis=0))

    def finish(ci, o):
        hg = hg_ref[ci * L:(ci + 1) * L, :]
        ya = _head_norm(o, nw) * (hg * jax.nn.sigmoid(hg))
        ya_ref[ci * L:(ci + 1) * L, :] = jax.nn.sigmoid(ga_ref[ci * L:(ci + 1) * L, :]) * ya

    @pl.when(worst <= HG_SAFE_EXP)
    def _():
        eb = jnp.exp(b)
        qt = (q * eb).astype(BF16)
        kn = kk * jnp.exp(-b)
        knb = kn.astype(BF16)
        row = lax.broadcasted_iota(jnp.int32, (L, L), 0)
        colid = lax.broadcasted_iota(jnp.int32, (L, L), 1)
        causal = colid <= row
        o_intra, incr, dec = [], [], []
        for ci in range(n_chunks):
            sl = slice(ci * L, (ci + 1) * L)
            vb = hi_ref[sl, :].astype(BF16)
            a = jnp.where(causal, _dot_nt(qt[sl], knb[sl]), 0.0)
            o_intra.append(_dot(a.astype(BF16), vb))
            dec.append(eb[(ci + 1) * L - 1:(ci + 1) * L, :])
            incr.append(_dot_tn(vb, (kn[sl] * dec[ci]).astype(BF16)))
        sts = [st_ref[...]]
        for ci in range(n_chunks):
            sts.append(sts[ci] * dec[ci] + incr[ci])
        st_ref[...] = sts[n_chunks]
        for ci in range(n_chunks):
            sl = slice(ci * L, (ci + 1) * L)
            finish(ci, o_intra[ci] + _dot_nt(qt[sl], sts[ci].astype(BF16)))

    @pl.when(jnp.logical_not(worst <= HG_SAFE_EXP))
    def _():
        wall = wall_ref[...]
        rowid = lax.broadcasted_iota(jnp.int32, (L, hq_ref.shape[1]), 0)
        n_lv = len(HG_LEVELS)
        for ci in range(n_chunks):
            sl = slice(ci * L, (ci + 1) * L)
            qc, kc, bc = q[sl], kk[sl], b[sl]
            vb = hi_ref[sl, :].astype(BF16)
            d = _exact_left_mul(wall, logf[sl])
            a = mask_ref[n_lv] * _dot_nt(qc.astype(BF16), kc.astype(BF16))
            for li, h in enumerate(HG_LEVELS):
                e = jnp.exp(-jnp.abs(d[li * L:(li + 1) * L]))
                second = (rowid & h) != 0
                p = jnp.where(second, qc, kc) * e
                qh = jnp.where(second, p, 0.0).astype(BF16)
                kh = jnp.where(second, 0.0, p).astype(BF16)
                a = a + mask_ref[li] * _dot_nt(qh, kh)
            qt = (qc * jnp.exp(bc)).astype(BF16)
            kt = (kc * jnp.exp(d[n_lv * L:(n_lv + 1) * L])).astype(BF16)
            st = st_ref[...]
            finish(ci, _dot(a.astype(BF16), vb) + _dot_nt(qt, st.astype(BF16)))
            st_ref[...] = st * jnp.exp(bc[L - 1:L, :]) + _dot_tn(vb, kt)

    @pl.when(c == pl.num_programs(2) - 1)
    def _():
        so_ref[...] = st_ref[...].T


def _hgrn_p_call(proj, lb_logits, norm_w, n_seq, seq_len, heads, kdim, col0, col_ga, rows_per_step):
    wall, masks = _hgrn_constants()
    per_mat = min(rows_per_step, MXU_DEPTH) // HG_CHUNK
    trib = np.kron(np.eye(per_mat, dtype=np.float32), np.tril(np.ones((HG_CHUNK, HG_CHUNK), np.float32)))
    trib = jnp.asarray(np.concatenate([trib, trib, trib], axis=1), BF16)
    nc = seq_len // rows_per_step
    kern = functools.partial(_hgrn_p_kernel, n_chunks=rows_per_step // HG_CHUNK)

    def col(k):
        return pl.BlockSpec((rows_per_step, kdim), lambda b, h, c: (b * nc + c, col0 + k * heads + h))

    return pl.pallas_call(
        kern,
        grid=(n_seq, heads, nc),
        in_specs=[col(0), col(1), col(2), col(3),
                  pl.BlockSpec((rows_per_step, kdim), lambda b, h, c: (b * nc + c, col_ga + h)),
                  pl.BlockSpec((2, kdim), lambda b, h, c: (0, h)),
                  pl.BlockSpec((1, kdim), lambda b, h, c: (0, h)),
                  pl.BlockSpec(trib.shape, lambda b, h, c: (0, 0)),
                  pl.BlockSpec(wall.shape, lambda b, h, c: (0, 0)),
                  pl.BlockSpec(masks.shape, lambda b, h, c: (0, 0, 0))],
        out_specs=[pl.BlockSpec((rows_per_step, kdim), lambda b, h, c: (b * nc + c, h)),
                   pl.BlockSpec((None, None, kdim, kdim), lambda b, h, c: (b, h, 0, 0))],
        out_shape=[jax.ShapeDtypeStruct((n_seq * seq_len, heads * kdim), F32),
                   jax.ShapeDtypeStruct((n_seq, heads, kdim, kdim), F32)],
        scratch_shapes=[pltpu.VMEM((kdim, kdim), F32)],
        compiler_params=_cparams("arbitrary", "arbitrary", "arbitrary"),
        name="hgrn_prompt",
    )(proj, proj, proj, proj, proj, lb_logits, norm_w, trib, wall, masks)


def _hgrn_s_kernel(hq_ref, hf_ref, hi_ref, hg_ref, ga_ref, lg_ref, nw_ref, s_ref, ya_ref, so_ref, *, n_pairs):
    kdim = hq_ref.shape[1]
    rows = n_pairs * SUBLANES
    lb = _lower_bound(lg_ref[...])
    rowid = lax.broadcasted_iota(jnp.int32, (rows, kdim), 0)
    t = rowid & (GROUP - 1)
    first = (rowid & GROUP) == 0
    first8 = lax.broadcasted_iota(jnp.int32, (SUBLANES, kdim), 0) < GROUP
    down = lambda y, j: pltpu.roll(y, j, 0)
    up = lambda y, j: pltpu.roll(y, rows - j, 0)

    q, kk, logf = _hgrn_gates(hq_ref[...], hf_ref[...], lb)
    v = hi_ref[...]
    b = logf
    for dlt in range(1, GROUP):
        b = b + jnp.where(t >= dlt, down(logf, dlt), 0.0)
    o = jnp.sum(q * kk, axis=1, keepdims=True) * v
    for dlt in range(1, GROUP):
        x = q * down(kk, dlt) * jnp.exp(b - down(b, dlt))
        a = jnp.sum(jnp.where(t >= dlt, x, 0.0), axis=1, keepdims=True)
        o = o + a * down(v, dlt)
    b_last = _group_last(b, t, up)
    qt = q * jnp.exp(b)
    kt = kk * jnp.exp(b_last - b)
    d1, d2, d3 = _split3(jnp.exp(b_last))
    swap = lambda y: jnp.where(first, up(y.astype(F32), GROUP), down(y.astype(F32), GROUP))
    dsplit = jnp.where(t == 0, swap(d1), jnp.where(t == 1, swap(d2), jnp.where(t == 2, swap(d3), 0.0)))
    ones = jnp.where(t <= 2, 1.0, 0.0)
    lhs = (jnp.where(first, kt, dsplit), jnp.where(first, dsplit, kt))
    rhs = (jnp.concatenate([jnp.where(first, v, 0.0), jnp.where(first, 0.0, ones)], axis=1),
           jnp.concatenate([jnp.where(first, 0.0, v), jnp.where(first, ones, 0.0)], axis=1))
    o_inter = []
    for p in range(n_pairs):
        sl = slice(p * SUBLANES, (p + 1) * SUBLANES)
        qb = qt[sl].astype(BF16)
        parts = []
        for half in range(2):
            s0 = s_ref[2 * p + half]
            upd = _dot_tn(lhs[half][sl].astype(BF16), rhs[half][sl].astype(BF16))
            so_ref[2 * p + half] = s0 * upd[:, kdim:] + upd[:, :kdim]
            parts.append(_dot(qb, s0.astype(BF16)))
        o_inter.append(jnp.where(first8, parts[0], parts[1]))
    o = o + jnp.concatenate(o_inter, axis=0)
    hg = hg_ref[...]
    ya_ref[...] = jax.nn.sigmoid(ga_ref[...]) * (_head_norm(o, nw_ref[...]) * (hg * jax.nn.sigmoid(hg)))


def _hgrn_s_call(proj, lb_logits, norm_w, state, heads, kdim, col0, col_ga, seqs_per_step):
    n_seq = state.shape[0]
    rows = seqs_per_step * GROUP
    kern = functools.partial(_hgrn_s_kernel, n_pairs=seqs_per_step // 2)

    def col(k):
        return pl.BlockSpec((rows, kdim), lambda i, h: (i, col0 + k * heads + h))

    st_spec = pl.BlockSpec((seqs_per_step, None, kdim, kdim), lambda i, h: (i, h, 0, 0))
    return pl.pallas_call(
        kern,
        grid=(n_seq // seqs_per_step, heads),
        in_specs=[col(0), col(1), col(2), col(3),
                  pl.BlockSpec((rows, kdim), lambda i, h: (i, col_ga + h)),
                  pl.BlockSpec((2, kdim), lambda i, h: (0, h)),
                  pl.BlockSpec((1, kdim), lambda i, h: (0, h)),
                  st_spec],
        out_specs=[pl.BlockSpec((rows, kdim), lambda i, h: (i, h)), st_spec],
        out_shape=[jax.ShapeDtypeStruct((n_seq * GROUP, heads * kdim), F32),
                   jax.ShapeDtypeStruct(state.shape, F32)],
        compiler_params=_cparams("arbitrary", "arbitrary"),
        name="hgrn_sample",
    )(proj, proj, proj, proj, proj, lb_logits, norm_w, state)


def _lane_pick(x, lane, idx):
    return jnp.broadcast_to(jnp.sum(jnp.where(lane == idx, x, 0.0), axis=1, keepdims=True), x.shape)


def _interleave(chains):
    live = list(chains)
    while live:
        still = []
        for g in live:
            try:
                next(g)
                still.append(g)
            except StopIteration:
                pass
        live = still


def _mlstm_p_kernel(q_ref, k_ref, v_ref, og_ref, gb_ref, ya_ref, g_ref, bias_ref, nw_ref, tri_ref,
                    mg_ref, co_ref, no_ref, mo_ref, c_s, n_s, m_s, *, heads, qk, vd):
    L = ML_CHUNK
    c = pl.program_id(1)

    @pl.when(c == 0)
    def _():
        c_s[...] = jnp.zeros_like(c_s)
        n_s[...] = jnp.zeros_like(n_s)
        m_s[...] = jnp.zeros_like(m_s)

    g = g_ref[...] + bias_ref[...]
    lane = lax.broadcasted_iota(jnp.int32, g.shape, 1)
    row = lax.broadcasted_iota(jnp.int32, g.shape, 0)
    b_all = _exact_left_mul(tri_ref[...], jax.nn.log_sigmoid(g))

    def head(hd):
        qs = slice(hd * qk, (hd + 1) * qk)
        vs = slice(hd * vd, (hd + 1) * vd)
        ig = _lane_pick(g, lane, hd)
        b = _lane_pick(b_all, lane, heads + hd)
        m_prev = m_s[hd]
        gs = (ig - b).T
        yield
        dm = jnp.where(lane <= row, b + gs, -jnp.inf)
        mt = jnp.maximum(b + m_prev, jnp.max(dm, axis=1, keepdims=True))
        inter = jnp.exp(b + m_prev - mt)
        q = q_ref[:, qs] * (qk ** -0.5)
        k = k_ref[:, qs]
        vb = v_ref[:, vs].astype(BF16)
        qb = q.astype(BF16)
        sc = _dot_nt(qb, k.astype(BF16)) * jnp.exp(dm - mt)
        yield
        c0 = c_s[hd]
        n0 = n_s[hd]
        num = inter[:, 0:1] * _dot(qb, c0.astype(BF16)) + _dot(sc.astype(BF16), vb)
        den = inter[:, 0:1] * jnp.sum(q * n0, axis=1, keepdims=True) + jnp.sum(sc, axis=1, keepdims=True)
        yield
        hh = num / jnp.maximum(jnp.abs(den), jnp.exp(-mt[:, 0:1]))
        yb = _head_norm(hh, nw_ref[:, vs]) * jax.nn.sigmoid(og_ref[:, vs])
        mg_ref[:, vs] = (ya_ref[:, vs] + jax.nn.sigmoid(gb_ref[:, vs]) * yb).astype(mg_ref.dtype)
        yield
        m_last = mt[L - 1:L, :]
        b_last = b[L - 1:L, :]
        dec = jnp.exp(b_last + m_prev - m_last)
        kw = jnp.exp(b_last - b + ig - m_last) * k
        c_s[hd] = jnp.concatenate([dec, dec], axis=1) * c0 + _dot_tn(kw.astype(BF16), vb)
        n_s[hd] = dec * n0 + jnp.sum(kw, axis=0, keepdims=True)
        m_s[hd] = m_last

    _interleave([head(hd) for hd in range(heads)])

    @pl.when(c == pl.num_programs(1) - 1)
    def _():
        co_ref[...] = c_s[...]
        no_ref[...] = n_s[...]
        mo_ref[...] = m_s[...]


def _mlstm_p_call(proj, ya, gates, bias, norm_w, n_seq, seq_len, heads, qk, vd, colq, colk, colv, colo, colg):
    L = ML_CHUNK
    assert qk == L and qk == LANES
    nc = seq_len // L
    qw, vw = heads * qk, heads * vd
    assert colq % qw == 0 and colk % qw == 0 and colv % vw == 0 and colo % vw == 0 and colg % vw == 0
    tri = jnp.asarray(np.tril(np.ones((L, L), np.float32)), BF16)
    kern = functools.partial(_mlstm_p_kernel, heads=heads, qk=qk, vd=vd)
    return pl.pallas_call(
        kern,
        grid=(n_seq, nc),
        in_specs=[pl.BlockSpec((L, qw), lambda b, c: (b * nc + c, colq // qw)),
                  pl.BlockSpec((L, qw), lambda b, c: (b * nc + c, colk // qw)),
                  pl.BlockSpec((L, vw), lambda b, c: (b * nc + c, colv // vw)),
                  pl.BlockSpec((L, vw), lambda b, c: (b * nc + c, colo // vw)),
                  pl.BlockSpec((L, vw), lambda b, c: (b * nc + c, colg // vw)),
                  pl.BlockSpec((L, vw), lambda b, c: (b * nc + c, 0)),
                  pl.BlockSpec((L, LANES), lambda b, c: (b * nc + c, 0)),
                  pl.BlockSpec((1, LANES), lambda b, c: (0, 0)),
                  pl.BlockSpec((1, vw), lambda b, c: (0, 0)),
                  pl.BlockSpec((L, L), lambda b, c: (0, 0))],
        out_specs=[pl.BlockSpec((L, vw), lambda b, c: (b * nc + c, 0)),
                   pl.BlockSpec((None, heads, qk, vd), lambda b, c: (b, 0, 0, 0)),
                   pl.BlockSpec((None, heads, 1, qk), lambda b, c: (b, 0, 0, 0)),
                   pl.BlockSpec((None, heads, 1, LANES), lambda b, c: (b, 0, 0, 0))],
        out_shape=[jax.ShapeDtypeStruct((n_seq * seq_len, vw), BF16),
                   jax.ShapeDtypeStruct((n_seq, heads, qk, vd), F32),
                   jax.ShapeDtypeStruct((n_seq, heads, 1, qk), F32),
                   jax.ShapeDtypeStruct((n_seq, heads, 1, LANES), F32)],
        scratch_shapes=[pltpu.VMEM((heads, qk, vd), F32), pltpu.VMEM((heads, 1, qk), F32),
                        pltpu.VMEM((heads, 1, LANES), F32)],
        compiler_params=_cparams("arbitrary", "arbitrary"),
        name="mlstm_prompt",
    )(proj, proj, proj, proj, proj, ya, gates, bias, norm_w, tri)


def _mlstm_s_kernel(q_ref, k_ref, v_ref, og_ref, gb_ref, ya_ref, g_ref, bias_ref, m_ref, nw_ref, c_ref, n_ref,
                    mg_ref, co_ref, no_ref, mo_ref, *, heads, n_pairs):
    h = pl.program_id(1)
    qk = q_ref.shape[1]
    rows = n_pairs * SUBLANES
    lane = lax.broadcasted_iota(jnp.int32, (rows, LANES), 1)
    rowid = lax.broadcasted_iota(jnp.int32, (rows, LANES), 0)
    t = rowid & (GROUP - 1)
    first = (rowid & GROUP) == 0
    first8 = lax.broadcasted_iota(jnp.int32, (SUBLANES, 1), 0) < GROUP
    down = lambda y, j: pltpu.roll(y, j, 0)
    up = lambda y, j: pltpu.roll(y, rows - j, 0)

    g = g_ref[...] + bias_ref[...]
    ig = _lane_pick(g, lane, h)
    lf = _lane_pick(jax.nn.log_sigmoid(g), lane, heads + h)
    m_prev = _lane_pick(m_ref[...], lane, h)
    b = lf
    for dlt in range(1, GROUP):
        b = b + jnp.where(t >= dlt, down(lf, dlt), 0.0)
    q = q_ref[...] * (qk ** -0.5)
    k = k_ref[...]
    v = v_ref[...]
    dms = [ig] + [jnp.where(t >= dlt, b - down(b, dlt) + down(ig, dlt), -jnp.inf) for dlt in range(1, GROUP)]
    mt = jnp.maximum(b + m_prev, functools.reduce(jnp.maximum, dms))
    inter = jnp.exp(b + m_prev - mt)[:, 0:1]
    num = jnp.zeros(v.shape, F32)
    den = jnp.zeros((rows, 1), F32)
    for dlt in range(GROUP):
        kd = k if dlt == 0 else down(k, dlt)
        vd_ = v if dlt == 0 else down(v, dlt)
        s = jnp.sum(q * kd, axis=1, keepdims=True) * jnp.exp(dms[dlt] - mt)[:, 0:1]
        num = num + s * vd_
        den = den + s
    m_last = _group_last(mt, t, up)
    b_last = _group_last(b, t, up)
    dec = jnp.exp(b_last + m_prev - m_last)
    kw = jnp.exp(b_last - b + ig - m_last) * k
    n0 = n_ref[...]
    no_ref[...] = dec * n0 + kw + down(kw, 1) + down(kw, 2) + down(kw, 3)
    mo_ref[...] = m_last
    den = den + inter * jnp.sum(q * n0, axis=1, keepdims=True)

    kw_half = (jnp.where(first, kw, 0.0), jnp.where(first, 0.0, kw))
    dec2 = jnp.concatenate([dec, dec], axis=1)
    num_inter = []
    for p in range(n_pairs):
        sl = slice(p * SUBLANES, (p + 1) * SUBLANES)
        qb = q[sl].astype(BF16)
        vb = v[sl].astype(BF16)
        parts = []
        for half in range(2):
            c0 = c_ref[2 * p + half]
            r = p * SUBLANES + half * GROUP
            co_ref[2 * p + half] = dec2[r:r + 1, :] * c0 + _dot_tn(kw_half[half][sl].astype(BF16), vb)
            parts.append(_dot(qb, c0.astype(BF16)))
        num_inter.append(jnp.where(first8, parts[0], parts[1]))
    num = num + inter * jnp.concatenate(num_inter, axis=0)
    hh = num / jnp.maximum(jnp.abs(den), jnp.exp(-mt)[:, 0:1])
    yb = _head_norm(hh, nw_ref[...]) * jax.nn.sigmoid(og_ref[...])
    mg_ref[...] = (ya_ref[...] + jax.nn.sigmoid(gb_ref[...]) * yb).astype(mg_ref.dtype)


def _mlstm_s_call(proj, ya, gates, bias, m_rows, norm_w, c_state, n_state, heads, qk, vd,
                  colq, colk, colv, colo, colg, seqs_per_step):
    n_seq = c_state.shape[0]
    rows = seqs_per_step * GROUP
    kern = functools.partial(_mlstm_s_kernel, heads=heads, n_pairs=seqs_per_step // 2)
    c_spec = pl.BlockSpec((seqs_per_step, None, qk, vd), lambda i, h: (i, h, 0, 0))
    n_spec = pl.BlockSpec((None, rows, qk), lambda i, h: (h, i, 0))
    return pl.pallas_call(
        kern,
        grid=(n_seq // seqs_per_step, heads),
        in_specs=[pl.BlockSpec((rows, qk), lambda i, h: (i, colq // qk + h)),
                  pl.BlockSpec((rows, qk), lambda i, h: (i, colk // qk + h)),
                  pl.BlockSpec((rows, vd), lambda i, h: (i, colv // vd + h)),
                  pl.BlockSpec((rows, vd), lambda i, h: (i, colo // vd + h)),
                  pl.BlockSpec((rows, vd), lambda i, h: (i, colg // vd + h)),
                  pl.BlockSpec((rows, vd), lambda i, h: (i, h)),
                  pl.BlockSpec((rows, LANES), lambda i, h: (i, 0)),
                  pl.BlockSpec((1, LANES), lambda i, h: (0, 0)),
                  pl.BlockSpec((rows, LANES), lambda i, h: (i, 0)),
                  pl.BlockSpec((1, vd), lambda i, h: (0, h)),
                  c_spec, n_spec],
        out_specs=[pl.BlockSpec((rows, vd), lambda i, h: (i, h)),
                   c_spec, n_spec,
                   pl.BlockSpec((None, rows, LANES), lambda i, h: (h, i, 0))],
        out_shape=[jax.ShapeDtypeStruct((n_seq * GROUP, heads * vd), BF16),
                   jax.ShapeDtypeStruct(c_state.shape, F32),
                   jax.ShapeDtypeStruct(n_state.shape, F32),
                   jax.ShapeDtypeStruct((heads, n_seq * GROUP, LANES), F32)],
        compiler_params=_cparams("arbitrary", "arbitrary"),
        name="mlstm_sample",
    )(proj, proj, proj, proj, proj, ya, gates, bias, m_rows, norm_w, c_state, n_state)


def _outproj_kernel(mg_ref, x_ref, g1_ref, w_ref, o_ref):
    o_ref[...] = x_ref[...] + g1_ref[...] * _dot(mg_ref[...], w_ref[...])


def _outproj_call(merged, x, mod, w_out, tm, tiles_per_seq):
    m, d = x.shape
    blk = pl.BlockSpec((tm, d), lambda i: (i, 0))
    return pl.pallas_call(
        _outproj_kernel,
        grid=(m // tm,),
        in_specs=[blk, blk, _mod_spec(mod, tm, 2, tiles_per_seq), pl.BlockSpec((d, d), lambda i: (0, 0))],
        out_specs=blk,
        out_shape=jax.ShapeDtypeStruct((m, d), F32),
        compiler_params=_cparams("arbitrary"),
        name="outproj",
    )(merged, x, mod, w_out)


def _ffn_kernel(*refs, tm, rc, tiles_per_seq, grouped):
    if grouped:
        (x_ref, sh_ref, sc_ref, g2_ref, wa_ref, wg_ref, wd_ref, cw_ref, cb_ref, fw_ref, p2_ref,
         y_ref, cv2_ref, cv3_ref, h_s, acc_s, a_s) = refs
    else:
        (x_ref, sh_ref, sc_ref, g2_ref, wa_ref, wg_ref, wd_ref, cw_ref, cb_ref, fw_ref,
         y_ref, a_ref, h_s, acc_s, carry_s) = refs
    i = pl.program_id(0)
    j = pl.program_id(1)

    @pl.when(j == 0)
    def _():
        def body(r, carry):
            r0 = pl.multiple_of(r * rc, rc)
            h = _norm_mod(x_ref[pl.ds(r0, rc), :], _rows(sc_ref, r0, rc), _rows(sh_ref, r0, rc))
            h_s[pl.ds(r0, rc), :] = h.astype(BF16)
            return carry
        lax.fori_loop(0, tm // rc, body, 0)
        acc_s[...] = jnp.zeros_like(acc_s)

    hb = h_s[...]
    a = _dot(hb, wa_ref[...])
    g = _dot(hb, wg_ref[...])
    rowid = lax.broadcasted_iota(jnp.int32, a.shape, 0)
    r1 = pltpu.roll(a, 1, 0)
    r2 = pltpu.roll(a, 2, 0)
    if grouped:
        t = rowid & (GROUP - 1)
        p2 = p2_ref[...]
        prev1 = jnp.where(t == 0, pltpu.roll(p2, tm - 1, 0), r1)
        prev2 = jnp.where(t <= 1, p2, r2)
        for s in range(a.shape[1] // LANES):
            a_s[s] = a[:, s * LANES:(s + 1) * LANES]
        for s in range(a.shape[1] // LANES):
            cv2_ref[:, s * LANES:(s + 1) * LANES] = a_s[s, pl.ds(GROUP - 2, tm // GROUP, stride=GROUP), :]
            cv3_ref[:, s * LANES:(s + 1) * LANES] = a_s[s, pl.ds(GROUP - 1, tm // GROUP, stride=GROUP), :]
    else:
        tail = a[tm - SUBLANES:tm, :]
        car = jnp.where(i % tiles_per_seq == 0, 0.0, carry_s[j])
        c1 = car[SUBLANES - 1:SUBLANES, :]
        c2 = car[SUBLANES - 2:SUBLANES - 1, :]
        prev1 = jnp.where(rowid == 0, c1, r1)
        prev2 = jnp.where(rowid == 0, c2, jnp.where(rowid == 1, c1, r2))
        carry_s[j] = tail
        a_ref[...] = tail
    cw = cw_ref[...]
    ac = cw[0:1, :] * prev2 + cw[1:2, :] * prev1 + cw[2:3, :] * a + cb_ref[...]
    yv = 0.5 * ac * (1.0 + lax.erf(ac * (2.0 ** -0.5))) * g
    acc_s[...] += _dot(yv.astype(BF16), wd_ref[...])

    @pl.when(j == pl.num_programs(1) - 1)
    def _():
        def body(r, carry):
            r0 = pl.multiple_of(r * rc, rc)
            x2 = x_ref[pl.ds(r0, rc), :] + _rows(g2_ref, r0, rc) * acc_s[pl.ds(r0, rc), :]
            ms = jnp.mean(x2 * x2, axis=-1, keepdims=True)
            y_ref[pl.ds(r0, rc), :] = x2 * lax.rsqrt(ms + EPS) * fw_ref[...]
            return carry
        lax.fori_loop(0, tm // rc, body, 0)


def _ffn_call(x, mod, w_up, w_down, conv_w, conv_b, final_w, tm, tiles_per_seq, p2=None):
    m, d = x.shape
    f = w_down.shape[0]
    tf = 512
    nj = f // tf
    grouped = p2 is not None
    kern = functools.partial(_ffn_kernel, tm=tm, rc=min(tm, 256), tiles_per_seq=tiles_per_seq, grouped=grouped)
    in_specs = [pl.BlockSpec((tm, d), lambda i, j: (i, 0)),
                _mod_spec(mod, tm, 3, tiles_per_seq),
                _mod_spec(mod, tm, 4, tiles_per_seq),
                _mod_spec(mod, tm, 5, tiles_per_seq),
                pl.BlockSpec((d, tf), lambda i, j: (0, j)),
                pl.BlockSpec((d, tf), lambda i, j: (0, nj + j)),
                pl.BlockSpec((tf, d), lambda i, j: (j, 0)),
                pl.BlockSpec((CONV_W, tf), lambda i, j: (0, j)),
                pl.BlockSpec((1, tf), lambda i, j: (0, j)),
                pl.BlockSpec((1, d), lambda i, j: (0, 0))]
    args = [x, mod, mod, mod, w_up, w_up, w_down, conv_w, conv_b, final_w]
    scratch = [pltpu.VMEM((tm, d), BF16), pltpu.VMEM((tm, d), F32)]
    if grouped:
        in_specs.append(pl.BlockSpec((tm, tf), lambda i, j: (i, j)))
        args.append(p2)
        scratch.append(pltpu.VMEM((tf // LANES, tm, LANES), F32))
        a_specs = [pl.BlockSpec((tm // GROUP, tf), lambda i, j: (i, j))] * 2
        a_shapes = [jax.ShapeDtypeStruct((m // GROUP, f), F32)] * 2
    else:
        scratch.append(pltpu.VMEM((nj, SUBLANES, tf), F32))
        a_specs = [pl.BlockSpec((SUBLANES, tf), lambda i, j: (i, j))]
        a_shapes = [jax.ShapeDtypeStruct((m // tm * SUBLANES, f), F32)]
    return pl.pallas_call(
        kern,
        grid=(m // tm, nj),
        in_specs=in_specs,
        out_specs=[pl.BlockSpec((tm, d), lambda i, j: (i, 0))] + a_specs,
        out_shape=[jax.ShapeDtypeStruct((m, d), F32)] + a_shapes,
        scratch_shapes=scratch,
        compiler_params=_cparams("arbitrary", "arbitrary"),
        name="ffn",
    )(*args)


def kernel(x_prompt, x_sample, c_prompt, c_sample, state_hgrn, state_mlstm_C, state_mlstm_n, state_mlstm_m,
           state_conv, ada_w, ada_b, w_in, hg_lb_logits, hg_norm_w, ml_i_bias, ml_f_bias, ml_norm_w, w_out,
           conv_w, conv_b, w_up, w_down, final_norm_w):
    n_p, seq, d = x_prompt.shape
    n_s, dec_seq, _ = x_sample.shape
    depth, _, hg_heads, hg_k, hg_v = state_hgrn.shape
    _, _, ml_heads, ml_qk, ml_v = state_mlstm_C.shape
    f = w_down.shape[1]
    assert depth == 1 and dec_seq == GROUP and hg_k == hg_v == LANES and 2 * ml_heads <= LANES
    assert hg_lb_logits.shape[0] == 2

    hg_w = hg_heads * hg_k
    gate0 = 4 * hg_w + 2 * ml_heads * ml_qk + ml_heads * ml_v
    gate1 = gate0 + 2 * ml_heads
    wt = jnp.swapaxes(w_in[0], 0, 1).astype(BF16)
    col_mq = 4 * hg_w
    col_mk = col_mq + ml_heads * ml_qk
    col_mv = col_mk + ml_heads * ml_qk
    col_mo = col_mv + ml_heads * ml_v
    col_ga = col_mo + ml_heads * ml_v
    col_gb = col_ga + d
    assert col_gb + d == wt.shape[0] - (gate1 - gate0)
    gate_bias = jnp.pad(jnp.concatenate([ml_i_bias[0], ml_f_bias[0]]), (0, LANES - 2 * ml_heads))[None, :]

    m_s_rows = n_s * GROUP
    c_all = jnp.concatenate([jnp.repeat(c_sample, GROUP, axis=0), c_prompt], axis=0)
    c_all = jnp.pad(c_all, ((0, (-c_all.shape[0]) % SUBLANES), (0, 0)))
    mod_s = _mod_call(c_all, ada_w[0], ada_b)
    mod_p = mod_s[m_s_rows:m_s_rows + n_p][:, None, :]

    w_out_b = w_out[0].astype(BF16)
    w_up_b = w_up[0].astype(BF16)
    w_down_b = w_down[0].astype(BF16)
    hg_nw = hg_norm_w
    ml_nw = ml_norm_w
    fw = final_norm_w[None, :]

    xp = x_prompt.reshape(n_p * seq, d)
    tm_p = 1024
    proj_p, gates_p = _inproj_call(xp, mod_p, wt, gate0, gate1 - gate0, tm_p, seq // tm_p)
    ya_p, hg_p = _hgrn_p_call(proj_p, hg_lb_logits, hg_nw, n_p, seq, hg_heads, hg_k, 0, col_ga // hg_k, 1024)
    mg_p, c_p, nn_p, m_p = _mlstm_p_call(proj_p, ya_p, gates_p, gate_bias, ml_nw, n_p, seq, ml_heads, ml_qk, ml_v,
                                         col_mq, col_mk, col_mv, col_mo, col_gb)
    tm_o = 256
    x1_p = _outproj_call(mg_p, xp, mod_p, w_out_b, tm_o, seq // tm_o)
    tm_f = 512
    tiles = seq // tm_f
    y_p, atail = _ffn_call(x1_p, mod_p, w_up_b, w_down_b, conv_w[0], conv_b, fw, tm_f, tiles)
    cv_p = atail.reshape(n_p, tiles, SUBLANES, f)[:, tiles - 1, SUBLANES - (CONV_W - 1):]

    xs = x_sample.reshape(n_s * GROUP, d)
    proj_s, gates_s = _inproj_call(xs, mod_s, wt, gate0, gate1 - gate0, m_s_rows, 1)
    ya_s, hg_s = _hgrn_s_call(proj_s, hg_lb_logits, hg_nw, state_hgrn[0], hg_heads, hg_k, 0, col_ga // hg_k, 32)
    m_rows = jnp.pad(jnp.repeat(state_mlstm_m[0], GROUP, axis=0), ((0, 0), (0, LANES - ml_heads)))
    n_hb = jnp.repeat(jnp.transpose(state_mlstm_n[0], (1, 0, 2)), GROUP, axis=1)
    mg_s, c_s, nn_s, m_s = _mlstm_s_call(proj_s, ya_s, gates_s, gate_bias, m_rows, ml_nw, state_mlstm_C[0], n_hb,
                                         ml_heads, ml_qk, ml_v, col_mq, col_mk, col_mv, col_mo, col_gb, 32)
    x1_s = _outproj_call(mg_s, xs, mod_s, w_out_b, tm_o, 1)
    p2 = jnp.pad(state_conv[0], ((0, 0), (0, GROUP - (CONV_W - 1)), (0, 0))).reshape(m_s_rows, f)
    y_s, cv2, cv3 = _ffn_call(x1_s, mod_s, w_up_b, w_down_b, conv_w[0], conv_b, fw, m_s_rows, 1, p2)
    cv_s = jnp.stack([cv2, cv3], axis=1)

    return (y_p.reshape(n_p, seq, d), y_s.reshape(n_s, GROUP, d),
            hg_p[None], hg_s[None],
            c_p[None], c_s[None],
            nn_p.reshape(1, n_p, ml_heads, ml_qk), jnp.transpose(nn_s[:, GROUP - 1::GROUP], (1, 0, 2))[None],
            m_p[:, :, 0, 0][None], jnp.transpose(m_s[:, ::GROUP, 0])[None],
            cv_p[None], cv_s[None])
```

```python
import functools

import numpy as np
import jax
import jax.numpy as jnp
from jax import lax
from jax.experimental import pallas as pl
from jax.experimental.pallas import tpu as pltpu

F32 = jnp.float32
BF16 = jnp.bfloat16

EPS = 1e-6
CONV_W = 3
LANES = 128
SUBLANES = 8
VMEM_LIMIT_BYTES = 58 * 1024 * 1024

MXU_DEPTH = 256
HG_CHUNK = 64
HG_SAFE_EXP = 60.0
HG_LEVELS = (32, 16, 8, 4, 2, 1)
ML_CHUNK = 128
GROUP = 4


def _cparams(*sem):
    return pltpu.CompilerParams(dimension_semantics=sem, vmem_limit_bytes=VMEM_LIMIT_BYTES)


def _dot(a, b):
    return jnp.dot(a, b, preferred_element_type=F32)


def _dot_nt(a, b):
    return lax.dot_general(a, b, (((1,), (1,)), ((), ())), preferred_element_type=F32)


def _dot_tn(a, b):
    return lax.dot_general(a, b, (((0,), (0,)), ((), ())), preferred_element_type=F32)


def _split3(x):
    x1 = x.astype(BF16)
    r1 = x - x1.astype(F32)
    x2 = r1.astype(BF16)
    x3 = (r1 - x2.astype(F32)).astype(BF16)
    return x1, x2, x3


def _exact_left_mul(w, x):
    x1, x2, x3 = _split3(x)
    return _dot(w, x1) + _dot(w, x2) + _dot(w, x3)


def _exact_left_mul3(w3, x):
    return _dot(w3, jnp.concatenate(_split3(x), axis=0))


def _rows(ref, r0, n):
    if ref.shape[0] == 1:
        return ref[...]
    return ref[pl.ds(r0, n), :]


def _norm_mod(x, sc, sh):
    ms = jnp.mean(x * x, axis=-1, keepdims=True)
    return x * lax.rsqrt(ms + EPS) * (1.0 + sc) + sh


def _head_norm(o, w):
    return o * lax.rsqrt(jnp.mean(o * o, axis=-1, keepdims=True) + EPS) * w


def _lower_bound(lg):
    l0, l1 = lg[0:1, :], lg[1:2, :]
    m = jnp.maximum(l0, l1)
    e0, e1 = jnp.exp(l0 - m), jnp.exp(l1 - m)
    return e0 / (e0 + e1)


def _group_last(x, t, up):
    return jnp.where(t == 3, x, jnp.where(t == 2, up(x, 1), jnp.where(t == 1, up(x, 2), up(x, 3))))


def _mod_kernel(c_ref, w_ref, b_ref, o_ref):
    c = c_ref[...]
    s = (c * jax.nn.sigmoid(c)).astype(BF16)
    o_ref[...] = _dot(s, w_ref[...].astype(BF16)) + b_ref[...]


def _mod_call(c_all, ada_w, ada_b):
    mp, d = c_all.shape
    n = ada_w.shape[1]
    tn = 512
    return pl.pallas_call(
        _mod_kernel,
        grid=(n // tn,),
        in_specs=[pl.BlockSpec((mp, d), lambda j: (0, 0)),
                  pl.BlockSpec((d, tn), lambda j: (0, j)),
                  pl.BlockSpec((1, tn), lambda j: (0, j))],
        out_specs=pl.BlockSpec((mp, tn), lambda j: (0, j)),
        out_shape=jax.ShapeDtypeStruct((mp, n), F32),
        compiler_params=_cparams("arbitrary"),
        name="mod",
    )(c_all, ada_w, ada_b)


def _inproj_kernel(*refs, tm, rc, na, emit_bf16):
    if emit_bf16:
        x_ref, sh_ref, sc_ref, wa_ref, wb_ref, wg_ref, o_ref, og_ref, wao_ref, wbo_ref, wgo_ref, h_ref = refs
    else:
        x_ref, sh_ref, sc_ref, wa_ref, wb_ref, wg_ref, o_ref, og_ref, h_ref = refs
        wao_ref = wbo_ref = wgo_ref = None
    j = pl.program_id(1)

    def weight(src, dst):
        w = src[...]
        if emit_bf16:
            w = w.astype(BF16)
            dst[...] = w
        return w

    @pl.when(j == 0)
    def _():
        wg = weight(wg_ref, wgo_ref)

        def body(r, carry):
            r0 = pl.multiple_of(r * rc, rc)
            h = _norm_mod(x_ref[pl.ds(r0, rc), :], _rows(sc_ref, r0, rc), _rows(sh_ref, r0, rc))
            hb = h.astype(BF16)
            h_ref[pl.ds(r0, rc), :] = hb
            gg = _dot_nt(hb, wg)
            og_ref[pl.ds(r0, rc), :] = jnp.concatenate(
                [gg, jnp.zeros((rc, og_ref.shape[1] - gg.shape[1]), F32)], axis=1)
            return carry
        lax.fori_loop(0, tm // rc, body, 0)

    @pl.when(j < na)
    def _():
        o_ref[...] = _dot_nt(h_ref[...], weight(wa_ref, wao_ref))

    @pl.when(j >= na)
    def _():
        o_ref[...] = _dot_nt(h_ref[...], weight(wb_ref, wbo_ref))


def _mod_spec(mod, tm, col, tiles_per_seq):
    d = mod.shape[-1] // 6
    if mod.ndim == 3:
        return pl.BlockSpec((None, 1, d), lambda i, *_: (i // tiles_per_seq, 0, col))
    return pl.BlockSpec((tm, d), lambda i, *_: (i, col))


def _inproj_call(x, mod, w, gate0, n_gate, tm, tn, tiles_per_seq):
    m, d = x.shape
    emit_bf16 = not isinstance(w, tuple)
    gate1 = gate0 + n_gate
    n_rows = w.shape[0] if emit_bf16 else w[0].shape[0] + w[1].shape[0] + n_gate
    assert n_gate == 2 * SUBLANES and gate0 % tn == 0 and (n_rows - gate1) % tn == 0
    na, nb = gate0 // tn, (n_rows - gate1) // tn
    n = (na + nb) * tn
    seg_a = pl.BlockSpec((tn, d), lambda i, j: (jnp.minimum(j, na - 1), 0))
    seg_b = pl.BlockSpec((tn, d), lambda i, j: (jnp.maximum(j - na, 0), 0))
    seg_g = pl.BlockSpec((n_gate, d), lambda i, j: (0, 0))
    out_specs = [pl.BlockSpec((tm, tn), lambda i, j: (i, j)), pl.BlockSpec((tm, LANES), lambda i, j: (i, 0))]
    out_shape = [jax.ShapeDtypeStruct((m, n), F32), jax.ShapeDtypeStruct((m, LANES), F32)]
    if emit_bf16:
        assert m == tm
        w_specs = [seg_a,
                   pl.BlockSpec((pl.Element(tn), pl.Element(d)),
                                lambda i, j: (pl.multiple_of(gate1 + jnp.maximum(j - na, 0) * tn, SUBLANES), 0)),
                   pl.BlockSpec((n_gate, d), lambda i, j: (gate0 // n_gate, 0))]
        w_args = [w, w, w]
        out_specs += [seg_a, seg_b, seg_g]
        out_shape += [jax.ShapeDtypeStruct((gate0, d), BF16), jax.ShapeDtypeStruct((n_rows - gate1, d), BF16),
                      jax.ShapeDtypeStruct((n_gate, d), BF16)]
    else:
        w_specs = [seg_a, seg_b, seg_g]
        w_args = list(w)
    kern = functools.partial(_inproj_kernel, tm=tm, rc=256, na=na, emit_bf16=emit_bf16)
    return pl.pallas_call(
        kern,
        grid=(m // tm, na + nb),
        in_specs=[pl.BlockSpec((tm, d), lambda i, j: (i, 0)),
                  _mod_spec(mod, tm, 0, tiles_per_seq),
                  _mod_spec(mod, tm, 1, tiles_per_seq)] + w_specs,
        out_specs=out_specs,
        out_shape=out_shape,
        scratch_shapes=[pltpu.VMEM((tm, d), BF16)],
        compiler_params=_cparams("arbitrary", "arbitrary"),
        name="inproj",
    )(x, mod, mod, *w_args)


def _hgrn_constants():
    L = HG_CHUNK
    tri = np.tril(np.ones((L, L), np.float32))
    t = np.arange(L)
    blocks = []
    masks = []
    for h in HG_LEVELS:
        mid = (t // (2 * h)) * (2 * h) + h - 1
        blocks.append(tri - tri[mid])
        masks.append((t[:, None] // (2 * h) == t[None, :] // (2 * h)).astype(np.float32))
    blocks.append(tri[L - 1][None, :] - tri)
    masks.append(np.eye(L, dtype=np.float32))
    return jnp.asarray(np.concatenate(blocks, 0), BF16), jnp.asarray(np.stack(masks, 0), F32)


def _hgrn_gates(hq, z, lb):
    omlb = 1.0 - lb
    logf = jnp.log(lb + omlb * jax.nn.sigmoid(z))
    kk = omlb * jax.nn.sigmoid(-z)
    q = hq * jax.nn.sigmoid(hq) * (hq.shape[-1] ** -0.5)
    return q, kk, logf


def _hgrn_p_kernel(hq_ref, hf_ref, hi_ref, hg_ref, ga_ref, lg_ref, nw_ref, trib_ref, wall_ref, mask_ref,
                   ya_ref, so_ref, st_ref, *, n_chunks):
    L = HG_CHUNK
    c = pl.program_id(2)
    tb = trib_ref.shape[0]
    last_rows = lambda y: [y[r:r + 1, :] for r in range(L - 1, n_chunks * L, L)]

    @pl.when(c == 0)
    def _():
        st_ref[...] = jnp.zeros_like(st_ref)

    lb = _lower_bound(lg_ref[...])
    nw = nw_ref[...]
    q, kk, logf = _hgrn_gates(hq_ref[...], hf_ref[...], lb)
    trib = trib_ref[...]
    b = jnp.concatenate([_exact_left_mul3(trib, logf[r:r + tb]) for r in range(0, n_chunks * L, tb)], axis=0)
    worst = -jnp.min(jnp.concatenate(last_rows(b), axis=0))

    def finish(ci, o):
        hg = hg_ref[ci * L:(ci + 1) * L, :]
        ya = _head_norm(o, nw) * (hg * jax.nn.sigmoid(hg))
        ya_ref[ci * L:(ci + 1) * L, :] = jax.nn.sigmoid(ga_ref[ci * L:(ci + 1) * L, :]) * ya

    @pl.when(worst <= HG_SAFE_EXP)
    def _():
        eb = jnp.exp(b)
        qt = (q * eb).astype(BF16)
        kn = kk * jnp.exp(-b)
        knb = kn.astype(BF16)
        row = lax.broadcasted_iota(jnp.int32, (L, L), 0)
        colid = lax.broadcasted_iota(jnp.int32, (L, L), 1)
        causal = colid <= row
        o_intra, incr, dec = [], [], []
        for ci in range(n_chunks):
            sl = slice(ci * L, (ci + 1) * L)
            vb = hi_ref[sl, :].astype(BF16)
            a = jnp.where(causal, _dot_nt(qt[sl], knb[sl]), 0.0)
            o_intra.append(_dot(a.astype(BF16), vb))
            dec.append(eb[(ci + 1) * L - 1:(ci + 1) * L, :])
            incr.append(_dot_tn(vb, (kn[sl] * dec[ci]).astype(BF16)))
        sts = [st_ref[...]]
        for ci in range(n_chunks):
            sts.append(sts[ci] * dec[ci] + incr[ci])
        st_ref[...] = sts[n_chunks]
        for ci in range(n_chunks):
            sl = slice(ci * L, (ci + 1) * L)
            finish(ci, o_intra[ci] + _dot_nt(qt[sl], sts[ci].astype(BF16)))

    @pl.when(jnp.logical_not(worst <= HG_SAFE_EXP))
    def _():
        wall = wall_ref[...]
        rowid = lax.broadcasted_iota(jnp.int32, (L, hq_ref.shape[1]), 0)
        n_lv = len(HG_LEVELS)
        for ci in range(n_chunks):
            sl = slice(ci * L, (ci + 1) * L)
            qc, kc, bc = q[sl], kk[sl], b[sl]
            vb = hi_ref[sl, :].astype(BF16)
            d = _exact_left_mul(wall, logf[sl])
            a = mask_ref[n_lv] * _dot_nt(qc.astype(BF16), kc.astype(BF16))
            for li, h in enumerate(HG_LEVELS):
                e = jnp.exp(-jnp.abs(d[li * L:(li + 1) * L]))
                second = (rowid & h) != 0
                p = jnp.where(second, qc, kc) * e
                qh = jnp.where(second, p, 0.0).astype(BF16)
                kh = jnp.where(second, 0.0, p).astype(BF16)
                a = a + mask_ref[li] * _dot_nt(qh, kh)
            qt = (qc * jnp.exp(bc)).astype(BF16)
            kt = (kc * jnp.exp(d[n_lv * L:(n_lv + 1) * L])).astype(BF16)
            st = st_ref[...]
            finish(ci, _dot(a.astype(BF16), vb) + _dot_nt(qt, st.astype(BF16)))
            st_ref[...] = st * jnp.exp(bc[L - 1:L, :]) + _dot_tn(vb, kt)

    @pl.when(c == pl.num_programs(2) - 1)
    def _():
        so_ref[...] = st_ref[...].T


def _hgrn_p_call(proj, lb_logits, norm_w, n_seq, seq_len, heads, kdim, col0, col_ga, rows_per_step):
    wall, masks = _hgrn_constants()
    per_mat = min(rows_per_step, MXU_DEPTH) // HG_CHUNK
    trib = np.kron(np.eye(per_mat, dtype=np.float32), np.tril(np.ones((HG_CHUNK, HG_CHUNK), np.float32)))
    trib = jnp.asarray(np.concatenate([trib, trib, trib], axis=1), BF16)
    nc = seq_len // rows_per_step
    kern = functools.partial(_hgrn_p_kernel, n_chunks=rows_per_step // HG_CHUNK)

    def col(k):
        return pl.BlockSpec((rows_per_step, kdim), lambda b, h, c: (b * nc + c, col0 + k * heads + h))

    return pl.pallas_call(
        kern,
        grid=(n_seq, heads, nc),
        in_specs=[col(0), col(1), col(2), col(3),
                  pl.BlockSpec((rows_per_step, kdim), lambda b, h, c: (b * nc + c, col_ga + h)),
                  pl.BlockSpec((2, kdim), lambda b, h, c: (0, h)),
                  pl.BlockSpec((1, kdim), lambda b, h, c: (0, h)),
                  pl.BlockSpec(trib.shape, lambda b, h, c: (0, 0)),
                  pl.BlockSpec(wall.shape, lambda b, h, c: (0, 0)),
                  pl.BlockSpec(masks.shape, lambda b, h, c: (0, 0, 0))],
        out_specs=[pl.BlockSpec((rows_per_step, kdim), lambda b, h, c: (b * nc + c, h)),
                   pl.BlockSpec((None, None, kdim, kdim), lambda b, h, c: (b, h, 0, 0))],
        out_shape=[jax.ShapeDtypeStruct((n_seq * seq_len, heads * kdim), F32),
                   jax.ShapeDtypeStruct((n_seq, heads, kdim, kdim), F32)],
        scratch_shapes=[pltpu.VMEM((kdim, kdim), F32)],
        compiler_params=_cparams("arbitrary", "arbitrary", "arbitrary"),
        name="hgrn_prompt",
    )(proj, proj, proj, proj, proj, lb_logits, norm_w, trib, wall, masks)


def _hgrn_s_kernel(hq_ref, hf_ref, hi_ref, hg_ref, ga_ref, lg_ref, nw_ref, s_ref, ya_ref, so_ref, *, n_pairs):
    kdim = hq_ref.shape[1]
    rows = n_pairs * SUBLANES
    lb = _lower_bound(lg_ref[...])
    rowid = lax.broadcasted_iota(jnp.int32, (rows, kdim), 0)
    t = rowid & (GROUP - 1)
    first = (rowid & GROUP) == 0
    first8 = lax.broadcasted_iota(jnp.int32, (SUBLANES, kdim), 0) < GROUP
    down = lambda y, j: pltpu.roll(y, j, 0)
    up = lambda y, j: pltpu.roll(y, rows - j, 0)

    q, kk, logf = _hgrn_gates(hq_ref[...], hf_ref[...], lb)
    v = hi_ref[...]
    b = logf
    for dlt in range(1, GROUP):
        b = b + jnp.where(t >= dlt, down(logf, dlt), 0.0)
    o = jnp.sum(q * kk, axis=1, keepdims=True) * v
    for dlt in range(1, GROUP):
        x = q * down(kk, dlt) * jnp.exp(b - down(b, dlt))
        a = jnp.sum(jnp.where(t >= dlt, x, 0.0), axis=1, keepdims=True)
        o = o + a * down(v, dlt)
    b_last = _group_last(b, t, up)
    qt = q * jnp.exp(b)
    kt = kk * jnp.exp(b_last - b)
    d1, d2, d3 = _split3(jnp.exp(b_last))
    swap = lambda y: jnp.where(first, up(y.astype(F32), GROUP), down(y.astype(F32), GROUP))
    dsplit = jnp.where(t == 0, swap(d1), jnp.where(t == 1, swap(d2), jnp.where(t == 2, swap(d3), 0.0)))
    ones = jnp.where(t <= 2, 1.0, 0.0)
    lhs = (jnp.where(first, kt, dsplit), jnp.where(first, dsplit, kt))
    rhs = (jnp.concatenate([jnp.where(first, v, 0.0), jnp.where(first, 0.0, ones)], axis=1),
           jnp.concatenate([jnp.where(first, 0.0, v), jnp.where(first, ones, 0.0)], axis=1))
    o_inter = []
    for p in range(n_pairs):
        sl = slice(p * SUBLANES, (p + 1) * SUBLANES)
        qb = qt[sl].astype(BF16)
        parts = []
        for half in range(2):
            s0 = s_ref[2 * p + half]
            upd = _dot_tn(lhs[half][sl].astype(BF16), rhs[half][sl].astype(BF16))
            so_ref[2 * p + half] = s0 * upd[:, kdim:] + upd[:, :kdim]
            parts.append(_dot(qb, s0.astype(BF16)))
        o_inter.append(jnp.where(first8, parts[0], parts[1]))
    o = o + jnp.concatenate(o_inter, axis=0)
    hg = hg_ref[...]
    ya_ref[...] = jax.nn.sigmoid(ga_ref[...]) * (_head_norm(o, nw_ref[...]) * (hg * jax.nn.sigmoid(hg)))


def _hgrn_s_call(proj, lb_logits, norm_w, state, heads, kdim, col0, col_ga, seqs_per_step):
    n_seq = state.shape[0]
    rows = seqs_per_step * GROUP
    kern = functools.partial(_hgrn_s_kernel, n_pairs=seqs_per_step // 2)

    def col(k):
        return pl.BlockSpec((rows, kdim), lambda i, h: (i, col0 + k * heads + h))

    st_spec = pl.BlockSpec((seqs_per_step, None, kdim, kdim), lambda i, h: (i, h, 0, 0))
    return pl.pallas_call(
        kern,
        grid=(n_seq // seqs_per_step, heads),
        in_specs=[col(0), col(1), col(2), col(3),
                  pl.BlockSpec((rows, kdim), lambda i, h: (i, col_ga + h)),
                  pl.BlockSpec((2, kdim), lambda i, h: (0, h)),
                  pl.BlockSpec((1, kdim), lambda i, h: (0, h)),
                  st_spec],
        out_specs=[pl.BlockSpec((rows, kdim), lambda i, h: (i, h)), st_spec],
        out_shape=[jax.ShapeDtypeStruct((n_seq * GROUP, heads * kdim), F32),
                   jax.ShapeDtypeStruct(state.shape, F32)],
        compiler_params=_cparams("arbitrary", "arbitrary"),
        name="hgrn_sample",
    )(proj, proj, proj, proj, proj, lb_logits, norm_w, state)


def _lane_pick(x, lane, idx):
    return jnp.broadcast_to(jnp.sum(jnp.where(lane == idx, x, 0.0), axis=1, keepdims=True), x.shape)


def _interleave(chains):
    live = list(chains)
    while live:
        still = []
        for g in live:
            try:
                next(g)
                still.append(g)
            except StopIteration:
                pass
        live = still


def _mlstm_p_kernel(q_ref, k_ref, v_ref, og_ref, gb_ref, ya_ref, g_ref, bias_ref, nw_ref, tri_ref,
                    mg_ref, co_ref, no_ref, mo_ref, c_s, n_s, m_s, *, heads, qk, vd):
    L = ML_CHUNK
    c = pl.program_id(1)

    @pl.when(c == 0)
    def _():
        c_s[...] = jnp.zeros_like(c_s)
        n_s[...] = jnp.zeros_like(n_s)
        m_s[...] = jnp.zeros_like(m_s)

    g = g_ref[...] + bias_ref[...]
    lane = lax.broadcasted_iota(jnp.int32, g.shape, 1)
    row = lax.broadcasted_iota(jnp.int32, g.shape, 0)
    b_all = _exact_left_mul(tri_ref[...], jax.nn.log_sigmoid(g))

    def head(hd):
        qs = slice(hd * qk, (hd + 1) * qk)
        vs = slice(hd * vd, (hd + 1) * vd)
        ig = _lane_pick(g, lane, hd)
        b = _lane_pick(b_all, lane, heads + hd)
        m_prev = m_s[hd]
        gs = (ig - b).T
        yield
        dm = jnp.where(lane <= row, b + gs, -jnp.inf)
        mt = jnp.maximum(b + m_prev, jnp.max(dm, axis=1, keepdims=True))
        inter = jnp.exp(b + m_prev - mt)
        q = q_ref[:, qs] * (qk ** -0.5)
        k = k_ref[:, qs]
        vb = v_ref[:, vs].astype(BF16)
        qb = q.astype(BF16)
        sc = _dot_nt(qb, k.astype(BF16)) * jnp.exp(dm - mt)
        yield
        c0 = c_s[hd]
        n0 = n_s[hd]
        num = inter[:, 0:1] * _dot(qb, c0.astype(BF16)) + _dot(sc.astype(BF16), vb)
        den = inter[:, 0:1] * jnp.sum(q * n0, axis=1, keepdims=True) + jnp.sum(sc, axis=1, keepdims=True)
        yield
        hh = num / jnp.maximum(jnp.abs(den), jnp.exp(-mt[:, 0:1]))
        yb = _head_norm(hh, nw_ref[:, vs]) * jax.nn.sigmoid(og_ref[:, vs])
        mg_ref[:, vs] = (ya_ref[:, vs] + jax.nn.sigmoid(gb_ref[:, vs]) * yb).astype(mg_ref.dtype)
        yield
        m_last = mt[L - 1:L, :]
        b_last = b[L - 1:L, :]
        dec = jnp.exp(b_last + m_prev - m_last)
        kw = jnp.exp(b_last - b + ig - m_last) * k
        c_s[hd] = jnp.concatenate([dec, dec], axis=1) * c0 + _dot_tn(kw.astype(BF16), vb)
        n_s[hd] = dec * n0 + jnp.sum(kw, axis=0, keepdims=True)
        m_s[hd] = m_last

    _interleave([head(hd) for hd in range(heads)])

    @pl.when(c == pl.num_programs(1) - 1)
    def _():
        co_ref[...] = c_s[...]
        no_ref[...] = n_s[...]
        mo_ref[...] = m_s[...]


def _mlstm_p_call(proj, ya, gates, bias, norm_w, n_seq, seq_len, heads, qk, vd, colq, colk, colv, colo, colg):
    L = ML_CHUNK
    assert qk == L and qk == LANES
    nc = seq_len // L
    qw, vw = heads * qk, heads * vd
    assert colq % qw == 0 and colk % qw == 0 and colv % vw == 0 and colo % vw == 0 and colg % vw == 0
    tri = jnp.asarray(np.tril(np.ones((L, L), np.float32)), BF16)
    kern = functools.partial(_mlstm_p_kernel, heads=heads, qk=qk, vd=vd)
    return pl.pallas_call(
        kern,
        grid=(n_seq, nc),
        in_specs=[pl.BlockSpec((L, qw), lambda b, c: (b * nc + c, colq // qw)),
                  pl.BlockSpec((L, qw), lambda b, c: (b * nc + c, colk // qw)),
                  pl.BlockSpec((L, vw), lambda b, c: (b * nc + c, colv // vw)),
                  pl.BlockSpec((L, vw), lambda b, c: (b * nc + c, colo // vw)),
                  pl.BlockSpec((L, vw), lambda b, c: (b * nc + c, colg // vw)),
                  pl.BlockSpec((L, vw), lambda b, c: (b * nc + c, 0)),
                  pl.BlockSpec((L, LANES), lambda b, c: (b * nc + c, 0)),
                  pl.BlockSpec((1, LANES), lambda b, c: (0, 0)),
                  pl.BlockSpec((1, vw), lambda b, c: (0, 0)),
                  pl.BlockSpec((L, L), lambda b, c: (0, 0))],
        out_specs=[pl.BlockSpec((L, vw), lambda b, c: (b * nc + c, 0)),
                   pl.BlockSpec((None, heads, qk, vd), lambda b, c: (b, 0, 0, 0)),
                   pl.BlockSpec((None, heads, 1, qk), lambda b, c: (b, 0, 0, 0)),
                   pl.BlockSpec((None, heads, 1, LANES), lambda b, c: (b, 0, 0, 0))],
        out_shape=[jax.ShapeDtypeStruct((n_seq * seq_len, vw), BF16),
                   jax.ShapeDtypeStruct((n_seq, heads, qk, vd), F32),
                   jax.ShapeDtypeStruct((n_seq, heads, 1, qk), F32),
                   jax.ShapeDtypeStruct((n_seq, heads, 1, LANES), F32)],
        scratch_shapes=[pltpu.VMEM((heads, qk, vd), F32), pltpu.VMEM((heads, 1, qk), F32),
                        pltpu.VMEM((heads, 1, LANES), F32)],
        compiler_params=_cparams("arbitrary", "arbitrary"),
        name="mlstm_prompt",
    )(proj, proj, proj, proj, proj, ya, gates, bias, norm_w, tri)


def _mlstm_s_kernel(q_ref, k_ref, v_ref, og_ref, gb_ref, ya_ref, g_ref, bias_ref, m_ref, nw_ref, c_ref, n_ref,
                    mg_ref, co_ref, no_ref, mo_ref, *, heads, n_pairs):
    h = pl.program_id(1)
    qk = q_ref.shape[1]
    rows = n_pairs * SUBLANES
    lane = lax.broadcasted_iota(jnp.int32, (rows, LANES), 1)
    rowid = lax.broadcasted_iota(jnp.int32, (rows, LANES), 0)
    t = rowid & (GROUP - 1)
    first = (rowid & GROUP) == 0
    first8 = lax.broadcasted_iota(jnp.int32, (SUBLANES, 1), 0) < GROUP
    down = lambda y, j: pltpu.roll(y, j, 0)
    up = lambda y, j: pltpu.roll(y, rows - j, 0)

    g = g_ref[...] + bias_ref[...]
    ig = _lane_pick(g, lane, h)
    lf = _lane_pick(jax.nn.log_sigmoid(g), lane, heads + h)
    m_prev = _lane_pick(m_ref[...], lane, h)
    b = lf
    for dlt in range(1, GROUP):
        b = b + jnp.where(t >= dlt, down(lf, dlt), 0.0)
    q = q_ref[...] * (qk ** -0.5)
    k = k_ref[...]
    v = v_ref[...]
    dms = [ig] + [jnp.where(t >= dlt, b - down(b, dlt) + down(ig, dlt), -jnp.inf) for dlt in range(1, GROUP)]
    mt = jnp.maximum(b + m_prev, functools.reduce(jnp.maximum, dms))
    inter = jnp.exp(b + m_prev - mt)[:, 0:1]
    num = jnp.zeros(v.shape, F32)
    den = jnp.zeros((rows, 1), F32)
    for dlt in range(GROUP):
        kd = k if dlt == 0 else down(k, dlt)
        vd_ = v if dlt == 0 else down(v, dlt)
        s = jnp.sum(q * kd, axis=1, keepdims=True) * jnp.exp(dms[dlt] - mt)[:, 0:1]
        num = num + s * vd_
        den = den + s
    m_last = _group_last(mt, t, up)
    b_last = _group_last(b, t, up)
    dec = jnp.exp(b_last + m_prev - m_last)
    kw = jnp.exp(b_last - b + ig - m_last) * k
    n0 = n_ref[...]
    no_ref[...] = dec * n0 + kw + down(kw, 1) + down(kw, 2) + down(kw, 3)
    mo_ref[...] = m_last
    den = den + inter * jnp.sum(q * n0, axis=1, keepdims=True)

    kw_half = (jnp.where(first, kw, 0.0), jnp.where(first, 0.0, kw))
    dec2 = jnp.concatenate([dec, dec], axis=1)
    num_inter = []
    for p in range(n_pairs):
        sl = slice(p * SUBLANES, (p + 1) * SUBLANES)
        qb = q[sl].astype(BF16)
        vb = v[sl].astype(BF16)
        parts = []
        for half in range(2):
            c0 = c_ref[2 * p + half]
            r = p * SUBLANES + half * GROUP
            co_ref[2 * p + half] = dec2[r:r + 1, :] * c0 + _dot_tn(kw_half[half][sl].astype(BF16), vb)
            parts.append(_dot(qb, c0.astype(BF16)))
        num_inter.append(jnp.where(first8, parts[0], parts[1]))
    num = num + inter * jnp.concatenate(num_inter, axis=0)
    hh = num / jnp.maximum(jnp.abs(den), jnp.exp(-mt)[:, 0:1])
    yb = _head_norm(hh, nw_ref[...]) * jax.nn.sigmoid(og_ref[...])
    mg_ref[...] = (ya_ref[...] + jax.nn.sigmoid(gb_ref[...]) * yb).astype(mg_ref.dtype)


def _mlstm_s_call(proj, ya, gates, bias, m_rows, norm_w, c_state, n_state, heads, qk, vd,
                  colq, colk, colv, colo, colg, seqs_per_step):
    n_seq = c_state.shape[0]
    rows = seqs_per_step * GROUP
    kern = functools.partial(_mlstm_s_kernel, heads=heads, n_pairs=seqs_per_step // 2)
    c_spec = pl.BlockSpec((seqs_per_step, None, qk, vd), lambda i, h: (i, h, 0, 0))
    n_spec = pl.BlockSpec((None, rows, qk), lambda i, h: (h, i, 0))
    return pl.pallas_call(
        kern,
        grid=(n_seq // seqs_per_step, heads),
        in_specs=[pl.BlockSpec((rows, qk), lambda i, h: (i, colq // qk + h)),
                  pl.BlockSpec((rows, qk), lambda i, h: (i, colk // qk + h)),
                  pl.BlockSpec((rows, vd), lambda i, h: (i, colv // vd + h)),
                  pl.BlockSpec((rows, vd), lambda i, h: (i, colo // vd + h)),
                  pl.BlockSpec((rows, vd), lambda i, h: (i, colg // vd + h)),
                  pl.BlockSpec((rows, vd), lambda i, h: (i, h)),
                  pl.BlockSpec((rows, LANES), lambda i, h: (i, 0)),
                  pl.BlockSpec((1, LANES), lambda i, h: (0, 0)),
                  pl.BlockSpec((rows, LANES), lambda i, h: (i, 0)),
                  pl.BlockSpec((1, vd), lambda i, h: (0, h)),
                  c_spec, n_spec],
        out_specs=[pl.BlockSpec((rows, vd), lambda i, h: (i, h)),
                   c_spec, n_spec,
                   pl.BlockSpec((None, rows, LANES), lambda i, h: (h, i, 0))],
        out_shape=[jax.ShapeDtypeStruct((n_seq * GROUP, heads * vd), BF16),
                   jax.ShapeDtypeStruct(c_state.shape, F32),
                   jax.ShapeDtypeStruct(n_state.shape, F32),
                   jax.ShapeDtypeStruct((heads, n_seq * GROUP, LANES), F32)],
        compiler_params=_cparams("arbitrary", "arbitrary"),
        name="mlstm_sample",
    )(proj, proj, proj, proj, proj, ya, gates, bias, m_rows, norm_w, c_state, n_state)


def _outproj_kernel(mg_ref, x_ref, g1_ref, w_ref, o_ref):
    o_ref[...] = x_ref[...] + g1_ref[...] * _dot(mg_ref[...], w_ref[...])


def _outproj_call(merged, x, mod, w_out, tm, tiles_per_seq):
    m, d = x.shape
    blk = pl.BlockSpec((tm, d), lambda i: (i, 0))
    return pl.pallas_call(
        _outproj_kernel,
        grid=(m // tm,),
        in_specs=[blk, blk, _mod_spec(mod, tm, 2, tiles_per_seq), pl.BlockSpec((d, d), lambda i: (0, 0))],
        out_specs=blk,
        out_shape=jax.ShapeDtypeStruct((m, d), F32),
        compiler_params=_cparams("arbitrary"),
        name="outproj",
    )(merged, x, mod, w_out)


def _ffn_kernel(*refs, tm, rc, tiles_per_seq, grouped, emit_bf16):
    refs = iter(refs)
    x_ref, sh_ref, sc_ref, g2_ref, wa_ref, wg_ref, wd_ref, cw_ref, cb_ref, fw_ref = (next(refs) for _ in range(10))
    p2_ref = next(refs) if grouped else None
    y_ref = next(refs)
    if grouped:
        cv2_ref, cv3_ref = next(refs), next(refs)
    else:
        a_ref = next(refs)
    wao_ref, wgo_ref, wdo_ref = (next(refs), next(refs), next(refs)) if emit_bf16 else (None, None, None)
    h_s, acc_s = next(refs), next(refs)
    a_s = carry_s = next(refs)
    i = pl.program_id(0)
    j = pl.program_id(1)

    def weight(src, dst):
        w = src[...]
        if emit_bf16:
            w = w.astype(BF16)
            dst[...] = w
        return w

    @pl.when(j == 0)
    def _():
        def body(r, carry):
            r0 = pl.multiple_of(r * rc, rc)
            h = _norm_mod(x_ref[pl.ds(r0, rc), :], _rows(sc_ref, r0, rc), _rows(sh_ref, r0, rc))
            h_s[pl.ds(r0, rc), :] = h.astype(BF16)
            return carry
        lax.fori_loop(0, tm // rc, body, 0)
        acc_s[...] = jnp.zeros_like(acc_s)

    hb = h_s[...]
    a = _dot(hb, weight(wa_ref, wao_ref))
    g = _dot(hb, weight(wg_ref, wgo_ref))
    rowid = lax.broadcasted_iota(jnp.int32, a.shape, 0)
    r1 = pltpu.roll(a, 1, 0)
    r2 = pltpu.roll(a, 2, 0)
    if grouped:
        t = rowid & (GROUP - 1)
        p2 = p2_ref[...]
        prev1 = jnp.where(t == 0, pltpu.roll(p2, tm - 1, 0), r1)
        prev2 = jnp.where(t <= 1, p2, r2)
        for s in range(a.shape[1] // LANES):
            a_s[s] = a[:, s * LANES:(s + 1) * LANES]
        for s in range(a.shape[1] // LANES):
            cv2_ref[:, s * LANES:(s + 1) * LANES] = a_s[s, pl.ds(GROUP - 2, tm // GROUP, stride=GROUP), :]
            cv3_ref[:, s * LANES:(s + 1) * LANES] = a_s[s, pl.ds(GROUP - 1, tm // GROUP, stride=GROUP), :]
    else:
        tail = a[tm - SUBLANES:tm, :]
        car = jnp.where(i % tiles_per_seq == 0, 0.0, carry_s[j])
        c1 = car[SUBLANES - 1:SUBLANES, :]
        c2 = car[SUBLANES - 2:SUBLANES - 1, :]
        prev1 = jnp.where(rowid == 0, c1, r1)
        prev2 = jnp.where(rowid == 0, c2, jnp.where(rowid == 1, c1, r2))
        carry_s[j] = tail
        a_ref[...] = tail
    cw = cw_ref[...]
    ac = cw[0:1, :] * prev2 + cw[1:2, :] * prev1 + cw[2:3, :] * a + cb_ref[...]
    yv = 0.5 * ac * (1.0 + lax.erf(ac * (2.0 ** -0.5))) * g
    acc_s[...] += _dot(yv.astype(BF16), weight(wd_ref, wdo_ref))

    @pl.when(j == pl.num_programs(1) - 1)
    def _():
        def body(r, carry):
            r0 = pl.multiple_of(r * rc, rc)
            x2 = x_ref[pl.ds(r0, rc), :] + _rows(g2_ref, r0, rc) * acc_s[pl.ds(r0, rc), :]
            ms = jnp.mean(x2 * x2, axis=-1, keepdims=True)
            y_ref[pl.ds(r0, rc), :] = x2 * lax.rsqrt(ms + EPS) * fw_ref[...]
            return carry
        lax.fori_loop(0, tm // rc, body, 0)


def _ffn_call(x, mod, w_up, w_down, conv_w, conv_b, final_w, tm, tf, tiles_per_seq, p2=None):
    m, d = x.shape
    f = w_down.shape[0]
    nj = f // tf
    grouped = p2 is not None
    emit_bf16 = not isinstance(w_up, tuple)
    kern = functools.partial(_ffn_kernel, tm=tm, rc=min(tm, 256), tiles_per_seq=tiles_per_seq, grouped=grouped,
                             emit_bf16=emit_bf16)
    half_spec = pl.BlockSpec((d, tf), lambda i, j: (0, j))
    down_spec = pl.BlockSpec((tf, d), lambda i, j: (j, 0))
    if emit_bf16:
        assert m == tm
        up_specs = [half_spec, pl.BlockSpec((d, tf), lambda i, j: (0, nj + j))]
        up_args = [w_up, w_up]
    else:
        up_specs = [half_spec, half_spec]
        up_args = list(w_up)
    in_specs = [pl.BlockSpec((tm, d), lambda i, j: (i, 0)),
                _mod_spec(mod, tm, 3, tiles_per_seq),
                _mod_spec(mod, tm, 4, tiles_per_seq),
                _mod_spec(mod, tm, 5, tiles_per_seq)] + up_specs + [
                down_spec,
                pl.BlockSpec((CONV_W, tf), lambda i, j: (0, j)),
                pl.BlockSpec((1, tf), lambda i, j: (0, j)),
                pl.BlockSpec((1, d), lambda i, j: (0, 0))]
    args = [x, mod, mod, mod] + up_args + [w_down, conv_w, conv_b, final_w]
    scratch = [pltpu.VMEM((tm, d), BF16), pltpu.VMEM((tm, d), F32)]
    if grouped:
        in_specs.append(pl.BlockSpec((tm, tf), lambda i, j: (i, j)))
        args.append(p2)
        scratch.append(pltpu.VMEM((tf // LANES, tm, LANES), F32))
        a_specs = [pl.BlockSpec((tm // GROUP, tf), lambda i, j: (i, j))] * 2
        a_shapes = [jax.ShapeDtypeStruct((m // GROUP, f), F32)] * 2
    else:
        scratch.append(pltpu.VMEM((nj, SUBLANES, tf), F32))
        a_specs = [pl.BlockSpec((SUBLANES, tf), lambda i, j: (i, j))]
        a_shapes = [jax.ShapeDtypeStruct((m // tm * SUBLANES, f), F32)]
    if emit_bf16:
        a_specs += [half_spec, half_spec, down_spec]
        a_shapes += [jax.ShapeDtypeStruct((d, f), BF16), jax.ShapeDtypeStruct((d, f), BF16),
                     jax.ShapeDtypeStruct((f, d), BF16)]
    return pl.pallas_call(
        kern,
        grid=(m // tm, nj),
        in_specs=in_specs,
        out_specs=[pl.BlockSpec((tm, d), lambda i, j: (i, 0))] + a_specs,
        out_shape=[jax.ShapeDtypeStruct((m, d), F32)] + a_shapes,
        scratch_shapes=scratch,
        compiler_params=_cparams("arbitrary", "arbitrary"),
        name="ffn",
    )(*args)


def kernel(x_prompt, x_sample, c_prompt, c_sample, state_hgrn, state_mlstm_C, state_mlstm_n, state_mlstm_m,
           state_conv, ada_w, ada_b, w_in, hg_lb_logits, hg_norm_w, ml_i_bias, ml_f_bias, ml_norm_w, w_out,
           conv_w, conv_b, w_up, w_down, final_norm_w):
    n_p, seq, d = x_prompt.shape
    n_s, dec_seq, _ = x_sample.shape
    depth, _, hg_heads, hg_k, hg_v = state_hgrn.shape
    _, _, ml_heads, ml_qk, ml_v = state_mlstm_C.shape
    f = w_down.shape[1]
    assert depth == 1 and dec_seq == GROUP and hg_k == hg_v == LANES and 2 * ml_heads <= LANES
    assert hg_lb_logits.shape[0] == 2

    hg_w = hg_heads * hg_k
    gate0 = 4 * hg_w + 2 * ml_heads * ml_qk + ml_heads * ml_v
    gate1 = gate0 + 2 * ml_heads
    wt = jnp.swapaxes(w_in[0], 0, 1)
    col_mq = 4 * hg_w
    col_mk = col_mq + ml_heads * ml_qk
    col_mv = col_mk + ml_heads * ml_qk
    col_mo = col_mv + ml_heads * ml_v
    col_ga = col_mo + ml_heads * ml_v
    col_gb = col_ga + d
    assert col_gb + d == wt.shape[0] - (gate1 - gate0)
    gate_bias = jnp.pad(jnp.concatenate([ml_i_bias[0], ml_f_bias[0]]), (0, LANES - 2 * ml_heads))[None, :]

    m_s_rows = n_s * GROUP
    c_all = jnp.concatenate([jnp.repeat(c_sample, GROUP, axis=0), c_prompt], axis=0)
    c_all = jnp.pad(c_all, ((0, (-c_all.shape[0]) % SUBLANES), (0, 0)))
    mod_s = _mod_call(c_all, ada_w[0], ada_b)
    mod_p = mod_s[m_s_rows:m_s_rows + n_p][:, None, :]

    w_out_b = w_out[0].astype(BF16)
    hg_nw = hg_norm_w
    ml_nw = ml_norm_w
    fw = final_norm_w[None, :]
    tm_o = 256

    xs = x_sample.reshape(n_s * GROUP, d)
    proj_s, gates_s, *wt_b = _inproj_call(xs, mod_s, wt, gate0, gate1 - gate0, m_s_rows, 512, 1)
    ya_s, hg_s = _hgrn_s_call(proj_s, hg_lb_logits, hg_nw, state_hgrn[0], hg_heads, hg_k, 0, col_ga // hg_k, 32)
    m_rows = jnp.pad(jnp.repeat(state_mlstm_m[0], GROUP, axis=0), ((0, 0), (0, LANES - ml_heads)))
    n_hb = jnp.repeat(jnp.transpose(state_mlstm_n[0], (1, 0, 2)), GROUP, axis=1)
    mg_s, c_s, nn_s, m_s = _mlstm_s_call(proj_s, ya_s, gates_s, gate_bias, m_rows, ml_nw, state_mlstm_C[0], n_hb,
                                         ml_heads, ml_qk, ml_v, col_mq, col_mk, col_mv, col_mo, col_gb, 32)
    x1_s = _outproj_call(mg_s, xs, mod_s, w_out_b, tm_o, 1)
    p2 = jnp.pad(state_conv[0], ((0, 0), (0, GROUP - (CONV_W - 1)), (0, 0))).reshape(m_s_rows, f)
    y_s, cv2, cv3, wa_b, wg_b, wd_b = _ffn_call(x1_s, mod_s, w_up[0], w_down[0], conv_w[0], conv_b, fw,
                                                m_s_rows, 256, 1, p2)
    cv_s = jnp.stack([cv2, cv3], axis=1)

    xp = x_prompt.reshape(n_p * seq, d)
    tm_p = 1024
    proj_p, gates_p = _inproj_call(xp, mod_p, tuple(wt_b), gate0, gate1 - gate0, tm_p, 1024, seq // tm_p)
    ya_p, hg_p = _hgrn_p_call(proj_p, hg_lb_logits, hg_nw, n_p, seq, hg_heads, hg_k, 0, col_ga // hg_k, 1024)
    mg_p, c_p, nn_p, m_p = _mlstm_p_call(proj_p, ya_p, gates_p, gate_bias, ml_nw, n_p, seq, ml_heads, ml_qk, ml_v,
                                         col_mq, col_mk, col_mv, col_mo, col_gb)
    x1_p = _outproj_call(mg_p, xp, mod_p, w_out_b, tm_o, seq // tm_o)
    tm_f = 512
    tiles = seq // tm_f
    y_p, atail = _ffn_call(x1_p, mod_p, (wa_b, wg_b), wd_b, conv_w[0], conv_b, fw, tm_f, 512, tiles)
    cv_p = atail.reshape(n_p, tiles, SUBLANES, f)[:, tiles - 1, SUBLANES - (CONV_W - 1):]

    return (y_p.reshape(n_p, seq, d), y_s.reshape(n_s, GROUP, d),
            hg_p[None], hg_s[None],
            c_p[None], c_s[None],
            nn_p.reshape(1, n_p, ml_heads, ml_qk), jnp.transpose(nn_s[:, GROUP - 1::GROUP], (1, 0, 2))[None],
            m_p[:, :, 0, 0][None], jnp.transpose(m_s[:, ::GROUP, 0])[None],
            cv_p[None], cv_s[None])
```

```python
import functools

import numpy as np
import jax
import jax.numpy as jnp
from jax import lax
from jax.experimental import pallas as pl
from jax.experimental.pallas import tpu as pltpu

F32 = jnp.float32
BF16 = jnp.bfloat16

EPS = 1e-6
CONV_W = 3
LANES = 128
SUBLANES = 8
VMEM_LIMIT_BYTES = 58 * 1024 * 1024

MXU_DEPTH = 256
HG_CHUNK = 64
HG_SAFE_EXP = 60.0
HG_LEVELS = (32, 16, 8, 4, 2, 1)
ML_CHUNK = 128
ML_HEADS_INTERLEAVED = 8
GROUP = 4


def _cparams(*sem):
    return pltpu.CompilerParams(dimension_semantics=sem, vmem_limit_bytes=VMEM_LIMIT_BYTES)


def _dot(a, b):
    return jnp.dot(a, b, preferred_element_type=F32)


def _dot_nt(a, b):
    return lax.dot_general(a, b, (((1,), (1,)), ((), ())), preferred_element_type=F32)


def _dot_tn(a, b):
    return lax.dot_general(a, b, (((0,), (0,)), ((), ())), preferred_element_type=F32)


def _split3(x):
    x1 = x.astype(BF16)
    r1 = x - x1.astype(F32)
    x2 = r1.astype(BF16)
    x3 = (r1 - x2.astype(F32)).astype(BF16)
    return x1, x2, x3


def _exact_left_mul(w, x):
    x1, x2, x3 = _split3(x)
    return _dot(w, x1) + _dot(w, x2) + _dot(w, x3)


def _exact_left_mul3(w3, x):
    return _dot(w3, jnp.concatenate(_split3(x), axis=0))


def _rows(ref, r0, n):
    if ref.shape[0] == 1:
        return ref[...]
    return ref[pl.ds(r0, n), :]


def _norm_mod(x, sc, sh):
    ms = jnp.mean(x * x, axis=-1, keepdims=True)
    return x * lax.rsqrt(ms + EPS) * (1.0 + sc) + sh


def _head_norm(o, w):
    return o * lax.rsqrt(jnp.mean(o * o, axis=-1, keepdims=True) + EPS) * w


def _lower_bound(lg):
    l0, l1 = lg[0:1, :], lg[1:2, :]
    m = jnp.maximum(l0, l1)
    e0, e1 = jnp.exp(l0 - m), jnp.exp(l1 - m)
    return e0 / (e0 + e1)


def _group_last(x, t, up):
    return jnp.where(t == 3, x, jnp.where(t == 2, up(x, 1), jnp.where(t == 1, up(x, 2), up(x, 3))))


def _mod_kernel(c_ref, w_ref, b_ref, o_ref):
    c = c_ref[...]
    s = (c * jax.nn.sigmoid(c)).astype(BF16)
    o_ref[...] = _dot(s, w_ref[...].astype(BF16)) + b_ref[...]


def _mod_call(c_all, ada_w, ada_b):
    mp, d = c_all.shape
    n = ada_w.shape[1]
    tn = 512
    return pl.pallas_call(
        _mod_kernel,
        grid=(n // tn,),
        in_specs=[pl.BlockSpec((mp, d), lambda j: (0, 0)),
                  pl.BlockSpec((d, tn), lambda j: (0, j)),
                  pl.BlockSpec((1, tn), lambda j: (0, j))],
        out_specs=pl.BlockSpec((mp, tn), lambda j: (0, j)),
        out_shape=jax.ShapeDtypeStruct((mp, n), F32),
        compiler_params=_cparams("arbitrary"),
        name="mod",
    )(c_all, ada_w, ada_b)


def _inproj_kernel(*refs, tm, rc, na, emit_bf16):
    if emit_bf16:
        x_ref, sh_ref, sc_ref, wa_ref, wb_ref, wg_ref, o_ref, og_ref, wo_ref, wgo_ref, h_ref = refs
    else:
        x_ref, sh_ref, sc_ref, wa_ref, wg_ref, o_ref, og_ref, h_ref = refs
        wb_ref = wo_ref = wgo_ref = None
    j = pl.program_id(1)

    def weight(src, dst):
        w = src[...]
        if emit_bf16:
            w = w.astype(BF16)
            dst[...] = w
        return w

    @pl.when(j == 0)
    def _():
        wg = weight(wg_ref, wgo_ref)

        def body(r, carry):
            r0 = pl.multiple_of(r * rc, rc)
            h = _norm_mod(x_ref[pl.ds(r0, rc), :], _rows(sc_ref, r0, rc), _rows(sh_ref, r0, rc))
            hb = h.astype(BF16)
            h_ref[pl.ds(r0, rc), :] = hb
            gg = _dot_nt(hb, wg)
            og_ref[pl.ds(r0, rc), :] = jnp.concatenate(
                [gg, jnp.zeros((rc, og_ref.shape[1] - gg.shape[1]), F32)], axis=1)
            return carry
        lax.fori_loop(0, tm // rc, body, 0)

    if emit_bf16:
        @pl.when(j < na)
        def _():
            o_ref[...] = _dot_nt(h_ref[...], weight(wa_ref, wo_ref))

        @pl.when(j >= na)
        def _():
            o_ref[...] = _dot_nt(h_ref[...], weight(wb_ref, wo_ref))
    else:
        o_ref[...] = _dot_nt(h_ref[...], wa_ref[...])


def _mod_spec(mod, tm, col, tiles_per_seq):
    d = mod.shape[-1] // 6
    if mod.ndim == 3:
        return pl.BlockSpec((None, 1, d), lambda i, *_: (i // tiles_per_seq, 0, col))
    return pl.BlockSpec((tm, d), lambda i, *_: (i, col))


def _inproj_call(x, mod, w, gate0, n_gate, tm, tn, tiles_per_seq):
    m, d = x.shape
    emit_bf16 = not isinstance(w, tuple)
    gate1 = gate0 + n_gate
    n = (w.shape[0] - n_gate) if emit_bf16 else w[0].shape[0]
    assert n_gate == 2 * SUBLANES and gate0 % tn == 0 and n % tn == 0
    na = gate0 // tn
    tile = pl.BlockSpec((tn, d), lambda i, j: (j, 0))
    seg_g = pl.BlockSpec((n_gate, d), lambda i, j: (0, 0))
    out_specs = [pl.BlockSpec((tm, tn), lambda i, j: (i, j)), pl.BlockSpec((tm, LANES), lambda i, j: (i, 0))]
    out_shape = [jax.ShapeDtypeStruct((m, n), F32), jax.ShapeDtypeStruct((m, LANES), F32)]
    if emit_bf16:
        assert m == tm
        w_specs = [pl.BlockSpec((tn, d), lambda i, j: (jnp.minimum(j, na - 1), 0)),
                   pl.BlockSpec((pl.Element(tn), pl.Element(d)),
                                lambda i, j: (pl.multiple_of(gate1 + jnp.maximum(j - na, 0) * tn, SUBLANES), 0)),
                   pl.BlockSpec((n_gate, d), lambda i, j: (gate0 // n_gate, 0))]
        w_args = [w, w, w]
        out_specs += [tile, seg_g]
        out_shape += [jax.ShapeDtypeStruct((n, d), BF16), jax.ShapeDtypeStruct((n_gate, d), BF16)]
    else:
        w_specs = [tile, seg_g]
        w_args = list(w)
    kern = functools.partial(_inproj_kernel, tm=tm, rc=256, na=na, emit_bf16=emit_bf16)
    return pl.pallas_call(
        kern,
        grid=(m // tm, n // tn),
        in_specs=[pl.BlockSpec((tm, d), lambda i, j: (i, 0)),
                  _mod_spec(mod, tm, 0, tiles_per_seq),
                  _mod_spec(mod, tm, 1, tiles_per_seq)] + w_specs,
        out_specs=out_specs,
        out_shape=out_shape,
        scratch_shapes=[pltpu.VMEM((tm, d), BF16)],
        compiler_params=_cparams("arbitrary", "arbitrary"),
        name="inproj",
    )(x, mod, mod, *w_args)


def _hgrn_constants():
    L = HG_CHUNK
    tri = np.tril(np.ones((L, L), np.float32))
    t = np.arange(L)
    blocks = []
    masks = []
    for h in HG_LEVELS:
        mid = (t // (2 * h)) * (2 * h) + h - 1
        blocks.append(tri - tri[mid])
        masks.append((t[:, None] // (2 * h) == t[None, :] // (2 * h)).astype(np.float32))
    blocks.append(tri[L - 1][None, :] - tri)
    masks.append(np.eye(L, dtype=np.float32))
    return jnp.asarray(np.concatenate(blocks, 0), BF16), jnp.asarray(np.stack(masks, 0), F32)


def _hgrn_gates(hq, z, lb):
    omlb = 1.0 - lb
    logf = jnp.log(lb + omlb * jax.nn.sigmoid(z))
    kk = omlb * jax.nn.sigmoid(-z)
    q = hq * jax.nn.sigmoid(hq) * (hq.shape[-1] ** -0.5)
    return q, kk, logf


def _hgrn_p_kernel(hq_ref, hf_ref, hi_ref, hg_ref, ga_ref, lg_ref, nw_ref, trib_ref, wall_ref, mask_ref,
                   ya_ref, so_ref, st_ref, *, n_chunks):
    L = HG_CHUNK
    c = pl.program_id(2)
    tb = trib_ref.shape[0]
    last_rows = lambda y: [y[r:r + 1, :] for r in range(L - 1, n_chunks * L, L)]

    @pl.when(c == 0)
    def _():
        st_ref[...] = jnp.zeros_like(st_ref)

    lb = _lower_bound(lg_ref[...])
    nw = nw_ref[...]
    q, kk, logf = _hgrn_gates(hq_ref[...], hf_ref[...], lb)
    trib = trib_ref[...]
    b = jnp.concatenate([_exact_left_mul3(trib, logf[r:r + tb]) for r in range(0, n_chunks * L, tb)], axis=0)
    worst = -jnp.min(jnp.concatenate(last_rows(b), axis=0))

    def finish(ci, o):
        hg = hg_ref[ci * L:(ci + 1) * L, :]
        ya = _head_norm(o, nw) * (hg * jax.nn.sigmoid(hg))
        ya_ref[ci * L:(ci + 1) * L, :] = jax.nn.sigmoid(ga_ref[ci * L:(ci + 1) * L, :]) * ya

    @pl.when(worst <= HG_SAFE_EXP)
    def _():
        eb = jnp.exp(b)
        qt = (q * eb).astype(BF16)
        kn = kk * jnp.exp(-b)
        knb = kn.astype(BF16)
        row = lax.broadcasted_iota(jnp.int32, (L, L), 0)
        colid = lax.broadcasted_iota(jnp.int32, (L, L), 1)
        causal = colid <= row
        o_intra, incr, dec = [], [], []
        for ci in range(n_chunks):
            sl = slice(ci * L, (ci + 1) * L)
            vb = hi_ref[sl, :].astype(BF16)
            a = jnp.where(causal, _dot_nt(qt[sl], knb[sl]), 0.0)
            o_intra.append(_dot(a.astype(BF16), vb))
            dec.append(eb[(ci + 1) * L - 1:(ci + 1) * L, :])
            incr.append(_dot_tn(vb, (kn[sl] * dec[ci]).astype(BF16)))
        sts = [st_ref[...]]
        for ci in range(n_chunks):
            sts.append(sts[ci] * dec[ci] + incr[ci])
        st_ref[...] = sts[n_chunks]
        for ci in range(n_chunks):
            sl = slice(ci * L, (ci + 1) * L)
            finish(ci, o_intra[ci] + _dot_nt(qt[sl], sts[ci].astype(BF16)))

    @pl.when(jnp.logical_not(worst <= HG_SAFE_EXP))
    def _():
        wall = wall_ref[...]
        rowid = lax.broadcasted_iota(jnp.int32, (L, hq_ref.shape[1]), 0)
        n_lv = len(HG_LEVELS)
        for ci in range(n_chunks):
            sl = slice(ci * L, (ci + 1) * L)
            qc, kc, bc = q[sl], kk[sl], b[sl]
            vb = hi_ref[sl, :].astype(BF16)
            d = _exact_left_mul(wall, logf[sl])
            a = mask_ref[n_lv] * _dot_nt(qc.astype(BF16), kc.astype(BF16))
            for li, h in enumerate(HG_LEVELS):
                e = jnp.exp(-jnp.abs(d[li * L:(li + 1) * L]))
                second = (rowid & h) != 0
                p = jnp.where(second, qc, kc) * e
                qh = jnp.where(second, p, 0.0).astype(BF16)
                kh = jnp.where(second, 0.0, p).astype(BF16)
                a = a + mask_ref[li] * _dot_nt(qh, kh)
            qt = (qc * jnp.exp(bc)).astype(BF16)
            kt = (kc * jnp.exp(d[n_lv * L:(n_lv + 1) * L])).astype(BF16)
            st = st_ref[...]
            finish(ci, _dot(a.astype(BF16), vb) + _dot_nt(qt, st.astype(BF16)))
            st_ref[...] = st * jnp.exp(bc[L - 1:L, :]) + _dot_tn(vb, kt)

    @pl.when(c == pl.num_programs(2) - 1)
    def _():
        so_ref[...] = st_ref[...].T


def _hgrn_p_call(proj, lb_logits, norm_w, n_seq, seq_len, heads, kdim, col0, col_ga, rows_per_step):
    wall, masks = _hgrn_constants()
    per_mat = min(rows_per_step, MXU_DEPTH) // HG_CHUNK
    trib = np.kron(np.eye(per_mat, dtype=np.float32), np.tril(np.ones((HG_CHUNK, HG_CHUNK), np.float32)))
    trib = jnp.asarray(np.concatenate([trib, trib, trib], axis=1), BF16)
    nc = seq_len // rows_per_step
    kern = functools.partial(_hgrn_p_kernel, n_chunks=rows_per_step // HG_CHUNK)

    def col(k):
        return pl.BlockSpec((rows_per_step, kdim), lambda b, h, c: (b * nc + c, col0 + k * heads + h))

    return pl.pallas_call(
        kern,
        grid=(n_seq, heads, nc),
        in_specs=[col(0), col(1), col(2), col(3),
                  pl.BlockSpec((rows_per_step, kdim), lambda b, h, c: (b * nc + c, col_ga + h)),
                  pl.BlockSpec((2, kdim), lambda b, h, c: (0, h)),
                  pl.BlockSpec((1, kdim), lambda b, h, c: (0, h)),
                  pl.BlockSpec(trib.shape, lambda b, h, c: (0, 0)),
                  pl.BlockSpec(wall.shape, lambda b, h, c: (0, 0)),
                  pl.BlockSpec(masks.shape, lambda b, h, c: (0, 0, 0))],
        out_specs=[pl.BlockSpec((rows_per_step, kdim), lambda b, h, c: (b * nc + c, h)),
                   pl.BlockSpec((None, None, kdim, kdim), lambda b, h, c: (b, h, 0, 0))],
        out_shape=[jax.ShapeDtypeStruct((n_seq * seq_len, heads * kdim), F32),
                   jax.ShapeDtypeStruct((n_seq, heads, kdim, kdim), F32)],
        scratch_shapes=[pltpu.VMEM((kdim, kdim), F32)],
        compiler_params=_cparams("arbitrary", "arbitrary", "arbitrary"),
        name="hgrn_prompt",
    )(proj, proj, proj, proj, proj, lb_logits, norm_w, trib, wall, masks)


def _hgrn_s_kernel(hq_ref, hf_ref, hi_ref, hg_ref, ga_ref, lg_ref, nw_ref, s_ref, ya_ref, so_ref, *, n_pairs):
    kdim = hq_ref.shape[1]
    rows = n_pairs * SUBLANES
    lb = _lower_bound(lg_ref[...])
    rowid = lax.broadcasted_iota(jnp.int32, (rows, kdim), 0)
    t = rowid & (GROUP - 1)
    first = (rowid & GROUP) == 0
    first8 = lax.broadcasted_iota(jnp.int32, (SUBLANES, kdim), 0) < GROUP
    down = lambda y, j: pltpu.roll(y, j, 0)
    up = lambda y, j: pltpu.roll(y, rows - j, 0)

    q, kk, logf = _hgrn_gates(hq_ref[...], hf_ref[...], lb)
    v = hi_ref[...]
    b = logf
    for dlt in range(1, GROUP):
        b = b + jnp.where(t >= dlt, down(logf, dlt), 0.0)
    o = jnp.sum(q * kk, axis=1, keepdims=True) * v
    for dlt in range(1, GROUP):
        x = q * down(kk, dlt) * jnp.exp(b - down(b, dlt))
        a = jnp.sum(jnp.where(t >= dlt, x, 0.0), axis=1, keepdims=True)
        o = o + a * down(v, dlt)
    b_last = _group_last(b, t, up)
    qt = q * jnp.exp(b)
    kt = kk * jnp.exp(b_last - b)
    d1, d2, d3 = _split3(jnp.exp(b_last))
    swap = lambda y: jnp.where(first, up(y.astype(F32), GROUP), down(y.astype(F32), GROUP))
    dsplit = jnp.where(t == 0, swap(d1), jnp.where(t == 1, swap(d2), jnp.where(t == 2, swap(d3), 0.0)))
    ones = jnp.where(t <= 2, 1.0, 0.0)
    lhs = (jnp.where(first, kt, dsplit), jnp.where(first, dsplit, kt))
    rhs = (jnp.concatenate([jnp.where(first, v, 0.0), jnp.where(first, 0.0, ones)], axis=1),
           jnp.concatenate([jnp.where(first, 0.0, v), jnp.where(first, ones, 0.0)], axis=1))
    o_inter = []
    for p in range(n_pairs):
        sl = slice(p * SUBLANES, (p + 1) * SUBLANES)
        qb = qt[sl].astype(BF16)
        parts = []
        for half in range(2):
            s0 = s_ref[2 * p + half]
            upd = _dot_tn(lhs[half][sl].astype(BF16), rhs[half][sl].astype(BF16))
            so_ref[2 * p + half] = s0 * upd[:, kdim:] + upd[:, :kdim]
            parts.append(_dot(qb, s0.astype(BF16)))
        o_inter.append(jnp.where(first8, parts[0], parts[1]))
    o = o + jnp.concatenate(o_inter, axis=0)
    hg = hg_ref[...]
    ya_ref[...] = jax.nn.sigmoid(ga_ref[...]) * (_head_norm(o, nw_ref[...]) * (hg * jax.nn.sigmoid(hg)))


def _hgrn_s_call(proj, lb_logits, norm_w, state, heads, kdim, col0, col_ga, seqs_per_step):
    n_seq = state.shape[0]
    rows = seqs_per_step * GROUP
    kern = functools.partial(_hgrn_s_kernel, n_pairs=seqs_per_step // 2)

    def col(k):
        return pl.BlockSpec((rows, kdim), lambda i, h: (i, col0 + k * heads + h))

    st_spec = pl.BlockSpec((seqs_per_step, None, kdim, kdim), lambda i, h: (i, h, 0, 0))
    return pl.pallas_call(
        kern,
        grid=(n_seq // seqs_per_step, heads),
        in_specs=[col(0), col(1), col(2), col(3),
                  pl.BlockSpec((rows, kdim), lambda i, h: (i, col_ga + h)),
                  pl.BlockSpec((2, kdim), lambda i, h: (0, h)),
                  pl.BlockSpec((1, kdim), lambda i, h: (0, h)),
                  st_spec],
        out_specs=[pl.BlockSpec((rows, kdim), lambda i, h: (i, h)), st_spec],
        out_shape=[jax.ShapeDtypeStruct((n_seq * GROUP, heads * kdim), F32),
                   jax.ShapeDtypeStruct(state.shape, F32)],
        compiler_params=_cparams("arbitrary", "arbitrary"),
        name="hgrn_sample",
    )(proj, proj, proj, proj, proj, lb_logits, norm_w, state)


def _lane_pick(x, lane, idx):
    return jnp.broadcast_to(jnp.sum(jnp.where(lane == idx, x, 0.0), axis=1, keepdims=True), x.shape)


def _interleave(chains):
    live = list(chains)
    while live:
        still = []
        for g in live:
            try:
                next(g)
                still.append(g)
            except StopIteration:
                pass
        live = still


def _mlstm_p_kernel(q_ref, k_ref, v_ref, og_ref, gb_ref, ya_ref, g_ref, bias_ref, nw_ref, tri_ref,
                    mg_ref, co_ref, no_ref, mo_ref, c_s, n_s, m_s, *, heads, qk, vd):
    L = ML_CHUNK
    c = pl.program_id(1)

    @pl.when(c == 0)
    def _():
        c_s[...] = jnp.zeros_like(c_s)
        n_s[...] = jnp.zeros_like(n_s)
        m_s[...] = jnp.zeros_like(m_s)

    g = g_ref[...] + bias_ref[...]
    lane = lax.broadcasted_iota(jnp.int32, g.shape, 1)
    row = lax.broadcasted_iota(jnp.int32, g.shape, 0)
    b_all = _exact_left_mul(tri_ref[...], jax.nn.log_sigmoid(g))

    def head(hd):
        qs = slice(hd * qk, (hd + 1) * qk)
        vs = slice(hd * vd, (hd + 1) * vd)
        ig = _lane_pick(g, lane, hd)
        b = _lane_pick(b_all, lane, heads + hd)
        m_prev = m_s[hd]
        gs = (ig - b).T
        yield
        dm = jnp.where(lane <= row, b + gs, -jnp.inf)
        mt = jnp.maximum(b + m_prev, jnp.max(dm, axis=1, keepdims=True))
        inter = jnp.exp(b + m_prev - mt)
        q = q_ref[:, qs] * (qk ** -0.5)
        k = k_ref[:, qs]
        vb = v_ref[:, vs].astype(BF16)
        qb = q.astype(BF16)
        sc = _dot_nt(qb, k.astype(BF16)) * jnp.exp(dm - mt)
        yield
        c0 = c_s[hd]
        n0 = n_s[hd]
        num = inter[:, 0:1] * _dot(qb, c0.astype(BF16)) + _dot(sc.astype(BF16), vb)
        den = inter[:, 0:1] * jnp.sum(q * n0, axis=1, keepdims=True) + jnp.sum(sc, axis=1, keepdims=True)
        yield
        hh = num / jnp.maximum(jnp.abs(den), jnp.exp(-mt[:, 0:1]))
        yb = _head_norm(hh, nw_ref[:, vs]) * jax.nn.sigmoid(og_ref[:, vs])
        mg_ref[:, vs] = (ya_ref[:, vs] + jax.nn.sigmoid(gb_ref[:, vs]) * yb).astype(mg_ref.dtype)
        yield
        m_last = mt[L - 1:L, :]
        b_last = b[L - 1:L, :]
        dec = jnp.exp(b_last + m_prev - m_last)
        kw = jnp.exp(b_last - b + ig - m_last) * k
        c_s[hd] = jnp.concatenate([dec, dec], axis=1) * c0 + _dot_tn(kw.astype(BF16), vb)
        n_s[hd] = dec * n0 + jnp.sum(kw, axis=0, keepdims=True)
        m_s[hd] = m_last

    for h0 in range(0, heads, ML_HEADS_INTERLEAVED):
        _interleave([head(hd) for hd in range(h0, min(h0 + ML_HEADS_INTERLEAVED, heads))])

    @pl.when(c == pl.num_programs(1) - 1)
    def _():
        co_ref[...] = c_s[...]
        no_ref[...] = n_s[...]
        mo_ref[...] = m_s[...]


def _mlstm_p_call(proj, ya, gates, bias, norm_w, n_seq, seq_len, heads, qk, vd, colq, colk, colv, colo, colg):
    L = ML_CHUNK
    assert qk == L and qk == LANES
    nc = seq_len // L
    qw, vw = heads * qk, heads * vd
    assert colq % qw == 0 and colk % qw == 0 and colv % vw == 0 and colo % vw == 0 and colg % vw == 0
    tri = jnp.asarray(np.tril(np.ones((L, L), np.float32)), BF16)
    kern = functools.partial(_mlstm_p_kernel, heads=heads, qk=qk, vd=vd)
    return pl.pallas_call(
        kern,
        grid=(n_seq, nc),
        in_specs=[pl.BlockSpec((L, qw), lambda b, c: (b * nc + c, colq // qw)),
                  pl.BlockSpec((L, qw), lambda b, c: (b * nc + c, colk // qw)),
                  pl.BlockSpec((L, vw), lambda b, c: (b * nc + c, colv // vw)),
                  pl.BlockSpec((L, vw), lambda b, c: (b * nc + c, colo // vw)),
                  pl.BlockSpec((L, vw), lambda b, c: (b * nc + c, colg // vw)),
                  pl.BlockSpec((L, vw), lambda b, c: (b * nc + c, 0)),
                  pl.BlockSpec((L, LANES), lambda b, c: (b * nc + c, 0)),
                  pl.BlockSpec((1, LANES), lambda b, c: (0, 0)),
                  pl.BlockSpec((1, vw), lambda b, c: (0, 0)),
                  pl.BlockSpec((L, L), lambda b, c: (0, 0))],
        out_specs=[pl.BlockSpec((L, vw), lambda b, c: (b * nc + c, 0)),
                   pl.BlockSpec((None, heads, qk, vd), lambda b, c: (b, 0, 0, 0)),
                   pl.BlockSpec((None, heads, 1, qk), lambda b, c: (b, 0, 0, 0)),
                   pl.BlockSpec((None, heads, 1, LANES), lambda b, c: (b, 0, 0, 0))],
        out_shape=[jax.ShapeDtypeStruct((n_seq * seq_len, vw), BF16),
                   jax.ShapeDtypeStruct((n_seq, heads, qk, vd), F32),
                   jax.ShapeDtypeStruct((n_seq, heads, 1, qk), F32),
                   jax.ShapeDtypeStruct((n_seq, heads, 1, LANES), F32)],
        scratch_shapes=[pltpu.VMEM((heads, qk, vd), F32), pltpu.VMEM((heads, 1, qk), F32),
                        pltpu.VMEM((heads, 1, LANES), F32)],
        compiler_params=_cparams("arbitrary", "arbitrary"),
        name="mlstm_prompt",
    )(proj, proj, proj, proj, proj, ya, gates, bias, norm_w, tri)


def _mlstm_s_kernel(q_ref, k_ref, v_ref, og_ref, gb_ref, ya_ref, g_ref, bias_ref, m_ref, nw_ref, c_ref, n_ref,
                    mg_ref, co_ref, no_ref, mo_ref, *, heads, n_pairs):
    h = pl.program_id(1)
    qk = q_ref.shape[1]
    rows = n_pairs * SUBLANES
    lane = lax.broadcasted_iota(jnp.int32, (rows, LANES), 1)
    rowid = lax.broadcasted_iota(jnp.int32, (rows, LANES), 0)
    t = rowid & (GROUP - 1)
    first = (rowid & GROUP) == 0
    first8 = lax.broadcasted_iota(jnp.int32, (SUBLANES, 1), 0) < GROUP
    down = lambda y, j: pltpu.roll(y, j, 0)
    up = lambda y, j: pltpu.roll(y, rows - j, 0)

    g = g_ref[...] + bias_ref[...]
    ig = _lane_pick(g, lane, h)
    lf = _lane_pick(jax.nn.log_sigmoid(g), lane, heads + h)
    m_prev = _lane_pick(m_ref[...], lane, h)
    b = lf
    for dlt in range(1, GROUP):
        b = b + jnp.where(t >= dlt, down(lf, dlt), 0.0)
    q = q_ref[...] * (qk ** -0.5)
    k = k_ref[...]
    v = v_ref[...]
    dms = [ig] + [jnp.where(t >= dlt, b - down(b, dlt) + down(ig, dlt), -jnp.inf) for dlt in range(1, GROUP)]
    mt = jnp.maximum(b + m_prev, functools.reduce(jnp.maximum, dms))
    inter = jnp.exp(b + m_prev - mt)[:, 0:1]
    num = jnp.zeros(v.shape, F32)
    den = jnp.zeros((rows, 1), F32)
    for dlt in range(GROUP):
        kd = k if dlt == 0 else down(k, dlt)
        vd_ = v if dlt == 0 else down(v, dlt)
        s = jnp.sum(q * kd, axis=1, keepdims=True) * jnp.exp(dms[dlt] - mt)[:, 0:1]
        num = num + s * vd_
        den = den + s
    m_last = _group_last(mt, t, up)
    b_last = _group_last(b, t, up)
    dec = jnp.exp(b_last + m_prev - m_last)
    kw = jnp.exp(b_last - b + ig - m_last) * k
    n0 = n_ref[...]
    no_ref[...] = dec * n0 + kw + down(kw, 1) + down(kw, 2) + down(kw, 3)
    mo_ref[...] = m_last
    den = den + inter * jnp.sum(q * n0, axis=1, keepdims=True)

    kw_half = (jnp.where(first, kw, 0.0), jnp.where(first, 0.0, kw))
    dec2 = jnp.concatenate([dec, dec], axis=1)
    num_inter = []
    for p in range(n_pairs):
        sl = slice(p * SUBLANES, (p + 1) * SUBLANES)
        qb = q[sl].astype(BF16)
        vb = v[sl].astype(BF16)
        parts = []
        for half in range(2):
            c0 = c_ref[2 * p + half]
            r = p * SUBLANES + half * GROUP
            co_ref[2 * p + half] = dec2[r:r + 1, :] * c0 + _dot_tn(kw_half[half][sl].astype(BF16), vb)
            parts.append(_dot(qb, c0.astype(BF16)))
        num_inter.append(jnp.where(first8, parts[0], parts[1]))
    num = num + inter * jnp.concatenate(num_inter, axis=0)
    hh = num / jnp.maximum(jnp.abs(den), jnp.exp(-mt)[:, 0:1])
    yb = _head_norm(hh, nw_ref[...]) * jax.nn.sigmoid(og_ref[...])
    mg_ref[...] = (ya_ref[...] + jax.nn.sigmoid(gb_ref[...]) * yb).astype(mg_ref.dtype)


def _mlstm_s_call(proj, ya, gates, bias, m_rows, norm_w, c_state, n_state, heads, qk, vd,
                  colq, colk, colv, colo, colg, seqs_per_step):
    n_seq = c_state.shape[0]
    rows = seqs_per_step * GROUP
    kern = functools.partial(_mlstm_s_kernel, heads=heads, n_pairs=seqs_per_step // 2)
    c_spec = pl.BlockSpec((seqs_per_step, None, qk, vd), lambda i, h: (i, h, 0, 0))
    n_spec = pl.BlockSpec((None, rows, qk), lambda i, h: (h, i, 0))
    return pl.pallas_call(
        kern,
        grid=(n_seq // seqs_per_step, heads),
        in_specs=[pl.BlockSpec((rows, qk), lambda i, h: (i, colq // qk + h)),
                  pl.BlockSpec((rows, qk), lambda i, h: (i, colk // qk + h)),
                  pl.BlockSpec((rows, vd), lambda i, h: (i, colv // vd + h)),
                  pl.BlockSpec((rows, vd), lambda i, h: (i, colo // vd + h)),
                  pl.BlockSpec((rows, vd), lambda i, h: (i, colg // vd + h)),
                  pl.BlockSpec((rows, vd), lambda i, h: (i, h)),
                  pl.BlockSpec((rows, LANES), lambda i, h: (i, 0)),
                  pl.BlockSpec((1, LANES), lambda i, h: (0, 0)),
                  pl.BlockSpec((rows, LANES), lambda i, h: (i, 0)),
                  pl.BlockSpec((1, vd), lambda i, h: (0, h)),
                  c_spec, n_spec],
        out_specs=[pl.BlockSpec((rows, vd), lambda i, h: (i, h)),
                   c_spec, n_spec,
                   pl.BlockSpec((None, rows, LANES), lambda i, h: (h, i, 0))],
        out_shape=[jax.ShapeDtypeStruct((n_seq * GROUP, heads * vd), BF16),
                   jax.ShapeDtypeStruct(c_state.shape, F32),
                   jax.ShapeDtypeStruct(n_state.shape, F32),
                   jax.ShapeDtypeStruct((heads, n_seq * GROUP, LANES), F32)],
        compiler_params=_cparams("arbitrary", "arbitrary"),
        name="mlstm_sample",
    )(proj, proj, proj, proj, proj, ya, gates, bias, m_rows, norm_w, c_state, n_state)


def _outproj_kernel(mg_ref, x_ref, g1_ref, w_ref, o_ref):
    o_ref[...] = x_ref[...] + g1_ref[...] * _dot(mg_ref[...], w_ref[...])


def _outproj_call(merged, x, mod, w_out, tm, tiles_per_seq):
    m, d = x.shape
    blk = pl.BlockSpec((tm, d), lambda i: (i, 0))
    return pl.pallas_call(
        _outproj_kernel,
        grid=(m // tm,),
        in_specs=[blk, blk, _mod_spec(mod, tm, 2, tiles_per_seq), pl.BlockSpec((d, d), lambda i: (0, 0))],
        out_specs=blk,
        out_shape=jax.ShapeDtypeStruct((m, d), F32),
        compiler_params=_cparams("arbitrary"),
        name="outproj",
    )(merged, x, mod, w_out)


def _ffn_kernel(*refs, tm, rc, tiles_per_seq, grouped, emit_bf16):
    refs = iter(refs)
    x_ref, sh_ref, sc_ref, g2_ref, wa_ref, wg_ref, wd_ref, cw_ref, cb_ref, fw_ref = (next(refs) for _ in range(10))
    p2_ref = next(refs) if grouped else None
    y_ref = next(refs)
    if grouped:
        cv2_ref, cv3_ref = next(refs), next(refs)
    else:
        a_ref = next(refs)
    wao_ref, wgo_ref, wdo_ref = (next(refs), next(refs), next(refs)) if emit_bf16 else (None, None, None)
    h_s, acc_s = next(refs), next(refs)
    a_s = carry_s = next(refs)
    i = pl.program_id(0)
    j = pl.program_id(1)

    def weight(src, dst):
        w = src[...]
        if emit_bf16:
            w = w.astype(BF16)
            dst[...] = w
        return w

    @pl.when(j == 0)
    def _():
        def body(r, carry):
            r0 = pl.multiple_of(r * rc, rc)
            h = _norm_mod(x_ref[pl.ds(r0, rc), :], _rows(sc_ref, r0, rc), _rows(sh_ref, r0, rc))
            h_s[pl.ds(r0, rc), :] = h.astype(BF16)
            return carry
        lax.fori_loop(0, tm // rc, body, 0)
        acc_s[...] = jnp.zeros_like(acc_s)

    hb = h_s[...]
    a = _dot(hb, weight(wa_ref, wao_ref))
    g = _dot(hb, weight(wg_ref, wgo_ref))
    rowid = lax.broadcasted_iota(jnp.int32, a.shape, 0)
    r1 = pltpu.roll(a, 1, 0)
    r2 = pltpu.roll(a, 2, 0)
    if grouped:
        t = rowid & (GROUP - 1)
        p2 = p2_ref[...]
        prev1 = jnp.where(t == 0, pltpu.roll(p2, tm - 1, 0), r1)
        prev2 = jnp.where(t <= 1, p2, r2)
        for s in range(a.shape[1] // LANES):
            a_s[s] = a[:, s * LANES:(s + 1) * LANES]
        for s in range(a.shape[1] // LANES):
            cv2_ref[:, s * LANES:(s + 1) * LANES] = a_s[s, pl.ds(GROUP - 2, tm // GROUP, stride=GROUP), :]
            cv3_ref[:, s * LANES:(s + 1) * LANES] = a_s[s, pl.ds(GROUP - 1, tm // GROUP, stride=GROUP), :]
    else:
        tail = a[tm - SUBLANES:tm, :]
        car = jnp.where(i % tiles_per_seq == 0, 0.0, carry_s[j])
        c1 = car[SUBLANES - 1:SUBLANES, :]
        c2 = car[SUBLANES - 2:SUBLANES - 1, :]
        prev1 = jnp.where(rowid == 0, c1, r1)
        prev2 = jnp.where(rowid == 0, c2, jnp.where(rowid == 1, c1, r2))
        carry_s[j] = tail
        a_ref[...] = tail
    cw = cw_ref[...]
    ac = cw[0:1, :] * prev2 + cw[1:2, :] * prev1 + cw[2:3, :] * a + cb_ref[...]
    yv = 0.5 * ac * (1.0 + lax.erf(ac * (2.0 ** -0.5))) * g
    acc_s[...] += _dot(yv.astype(BF16), weight(wd_ref, wdo_ref))

    @pl.when(j == pl.num_programs(1) - 1)
    def _():
        def body(r, carry):
            r0 = pl.multiple_of(r * rc, rc)
            x2 = x_ref[pl.ds(r0, rc), :] + _rows(g2_ref, r0, rc) * acc_s[pl.ds(r0, rc), :]
            ms = jnp.mean(x2 * x2, axis=-1, keepdims=True)
            y_ref[pl.ds(r0, rc), :] = x2 * lax.rsqrt(ms + EPS) * fw_ref[...]
            return carry
        lax.fori_loop(0, tm // rc, body, 0)


def _ffn_call(x, mod, w_up, w_down, conv_w, conv_b, final_w, tm, tf, tiles_per_seq, p2=None):
    m, d = x.shape
    f = w_down.shape[0]
    nj = f // tf
    grouped = p2 is not None
    emit_bf16 = not isinstance(w_up, tuple)
    kern = functools.partial(_ffn_kernel, tm=tm, rc=min(tm, 256), tiles_per_seq=tiles_per_seq, grouped=grouped,
                             emit_bf16=emit_bf16)
    half_spec = pl.BlockSpec((d, tf), lambda i, j: (0, j))
    down_spec = pl.BlockSpec((tf, d), lambda i, j: (j, 0))
    if emit_bf16:
        assert m == tm
        up_specs = [half_spec, pl.BlockSpec((d, tf), lambda i, j: (0, nj + j))]
        up_args = [w_up, w_up]
    else:
        up_specs = [half_spec, half_spec]
        up_args = list(w_up)
    in_specs = [pl.BlockSpec((tm, d), lambda i, j: (i, 0)),
                _mod_spec(mod, tm, 3, tiles_per_seq),
                _mod_spec(mod, tm, 4, tiles_per_seq),
                _mod_spec(mod, tm, 5, tiles_per_seq)] + up_specs + [
                down_spec,
                pl.BlockSpec((CONV_W, tf), lambda i, j: (0, j)),
                pl.BlockSpec((1, tf), lambda i, j: (0, j)),
                pl.BlockSpec((1, d), lambda i, j: (0, 0))]
    args = [x, mod, mod, mod] + up_args + [w_down, conv_w, conv_b, final_w]
    scratch = [pltpu.VMEM((tm, d), BF16), pltpu.VMEM((tm, d), F32)]
    if grouped:
        in_specs.append(pl.BlockSpec((tm, tf), lambda i, j: (i, j)))
        args.append(p2)
        scratch.append(pltpu.VMEM((tf // LANES, tm, LANES), F32))
        a_specs = [pl.BlockSpec((tm // GROUP, tf), lambda i, j: (i, j))] * 2
        a_shapes = [jax.ShapeDtypeStruct((m // GROUP, f), F32)] * 2
    else:
        scratch.append(pltpu.VMEM((nj, SUBLANES, tf), F32))
        a_specs = [pl.BlockSpec((SUBLANES, tf), lambda i, j: (i, j))]
        a_shapes = [jax.ShapeDtypeStruct((m // tm * SUBLANES, f), F32)]
    if emit_bf16:
        a_specs += [half_spec, half_spec, down_spec]
        a_shapes += [jax.ShapeDtypeStruct((d, f), BF16), jax.ShapeDtypeStruct((d, f), BF16),
                     jax.ShapeDtypeStruct((f, d), BF16)]
    return pl.pallas_call(
        kern,
        grid=(m // tm, nj),
        in_specs=in_specs,
        out_specs=[pl.BlockSpec((tm, d), lambda i, j: (i, 0))] + a_specs,
        out_shape=[jax.ShapeDtypeStruct((m, d), F32)] + a_shapes,
        scratch_shapes=scratch,
        compiler_params=_cparams("arbitrary", "arbitrary"),
        name="ffn",
    )(*args)


def kernel(x_prompt, x_sample, c_prompt, c_sample, state_hgrn, state_mlstm_C, state_mlstm_n, state_mlstm_m,
           state_conv, ada_w, ada_b, w_in, hg_lb_logits, hg_norm_w, ml_i_bias, ml_f_bias, ml_norm_w, w_out,
           conv_w, conv_b, w_up, w_down, final_norm_w):
    n_p, seq, d = x_prompt.shape
    n_s, dec_seq, _ = x_sample.shape
    depth, _, hg_heads, hg_k, hg_v = state_hgrn.shape
    _, _, ml_heads, ml_qk, ml_v = state_mlstm_C.shape
    f = w_down.shape[1]
    assert depth == 1 and dec_seq == GROUP and hg_k == hg_v == LANES and 2 * ml_heads <= LANES
    assert hg_lb_logits.shape[0] == 2

    hg_w = hg_heads * hg_k
    gate0 = 4 * hg_w + 2 * ml_heads * ml_qk + ml_heads * ml_v
    gate1 = gate0 + 2 * ml_heads
    wt = jnp.swapaxes(w_in[0], 0, 1)
    col_mq = 4 * hg_w
    col_mk = col_mq + ml_heads * ml_qk
    col_mv = col_mk + ml_heads * ml_qk
    col_mo = col_mv + ml_heads * ml_v
    col_ga = col_mo + ml_heads * ml_v
    col_gb = col_ga + d
    assert col_gb + d == wt.shape[0] - (gate1 - gate0)
    gate_bias = jnp.pad(jnp.concatenate([ml_i_bias[0], ml_f_bias[0]]), (0, LANES - 2 * ml_heads))[None, :]

    m_s_rows = n_s * GROUP
    c_all = jnp.concatenate([jnp.repeat(c_sample, GROUP, axis=0), c_prompt], axis=0)
    c_all = jnp.pad(c_all, ((0, (-c_all.shape[0]) % SUBLANES), (0, 0)))
    mod_s = _mod_call(c_all, ada_w[0], ada_b)
    mod_p = mod_s[m_s_rows:m_s_rows + n_p][:, None, :]

    w_out_b = w_out[0].astype(BF16)
    hg_nw = hg_norm_w
    ml_nw = ml_norm_w
    fw = final_norm_w[None, :]
    tm_o = 256

    xs = x_sample.reshape(n_s * GROUP, d)
    proj_s, gates_s, *wt_b = _inproj_call(xs, mod_s, wt, gate0, gate1 - gate0, m_s_rows, 512, 1)
    ya_s, hg_s = _hgrn_s_call(proj_s, hg_lb_logits, hg_nw, state_hgrn[0], hg_heads, hg_k, 0, col_ga // hg_k, 64)
    m_rows = jnp.pad(jnp.repeat(state_mlstm_m[0], GROUP, axis=0), ((0, 0), (0, LANES - ml_heads)))
    n_hb = jnp.repeat(jnp.transpose(state_mlstm_n[0], (1, 0, 2)), GROUP, axis=1)
    mg_s, c_s, nn_s, m_s = _mlstm_s_call(proj_s, ya_s, gates_s, gate_bias, m_rows, ml_nw, state_mlstm_C[0], n_hb,
                                         ml_heads, ml_qk, ml_v, col_mq, col_mk, col_mv, col_mo, col_gb, 64)
    x1_s = _outproj_call(mg_s, xs, mod_s, w_out_b, tm_o, 1)
    p2 = jnp.pad(state_conv[0], ((0, 0), (0, GROUP - (CONV_W - 1)), (0, 0))).reshape(m_s_rows, f)
    y_s, cv2, cv3, wa_b, wg_b, wd_b = _ffn_call(x1_s, mod_s, w_up[0], w_down[0], conv_w[0], conv_b, fw,
                                                m_s_rows, 256, 1, p2)
    cv_s = jnp.stack([cv2, cv3], axis=1)

    xp = x_prompt.reshape(n_p * seq, d)
    tm_p = 1024
    proj_p, gates_p = _inproj_call(xp, mod_p, tuple(wt_b), gate0, gate1 - gate0, tm_p, 2048, seq // tm_p)
    ya_p, hg_p = _hgrn_p_call(proj_p, hg_lb_logits, hg_nw, n_p, seq, hg_heads, hg_k, 0, col_ga // hg_k, 1024)
    mg_p, c_p, nn_p, m_p = _mlstm_p_call(proj_p, ya_p, gates_p, gate_bias, ml_nw, n_p, seq, ml_heads, ml_qk, ml_v,
                                         col_mq, col_mk, col_mv, col_mo, col_gb)
    x1_p = _outproj_call(mg_p, xp, mod_p, w_out_b, tm_o, seq // tm_o)
    tm_f = 512
    tiles = seq // tm_f
    y_p, atail = _ffn_call(x1_p, mod_p, (wa_b, wg_b), wd_b, conv_w[0], conv_b, fw, tm_f, 512, tiles)
    cv_p = atail.reshape(n_p, tiles, SUBLANES, f)[:, tiles - 1, SUBLANES - (CONV_W - 1):]

    return (y_p.reshape(n_p, seq, d), y_s.reshape(n_s, GROUP, d),
            hg_p[None], hg_s[None],
            c_p[None], c_s[None],
            nn_p.reshape(1, n_p, ml_heads, ml_qk), jnp.transpose(nn_s[:, GROUP - 1::GROUP], (1, 0, 2))[None],
            m_p[:, :, 0, 0][None], jnp.transpose(m_s[:, ::GROUP, 0])[None],
            cv_p[None], cv_s[None])
```

```python
import functools

import numpy as np
import jax
import jax.numpy as jnp
from jax import lax
from jax.experimental import pallas as pl
from jax.experimental.pallas import tpu as pltpu

F32 = jnp.float32
BF16 = jnp.bfloat16

EPS = 1e-6
CONV_W = 3
LANES = 128
SUBLANES = 8
VMEM_LIMIT_BYTES = 58 * 1024 * 1024

MXU_DEPTH = 256
HG_CHUNK = 64
HG_SAFE_EXP = 60.0
HG_LEVELS = (32, 16, 8, 4, 2, 1)
ML_CHUNK = 128
ML_HEADS_INTERLEAVED = 8
GROUP = 4


def _cparams(*sem):
    return pltpu.CompilerParams(dimension_semantics=sem, vmem_limit_bytes=VMEM_LIMIT_BYTES)


def _dot(a, b):
    return jnp.dot(a, b, preferred_element_type=F32)


def _dot_nt(a, b):
    return lax.dot_general(a, b, (((1,), (1,)), ((), ())), preferred_element_type=F32)


def _dot_tn(a, b):
    return lax.dot_general(a, b, (((0,), (0,)), ((), ())), preferred_element_type=F32)


def _split3(x):
    x1 = x.astype(BF16)
    r1 = x - x1.astype(F32)
    x2 = r1.astype(BF16)
    x3 = (r1 - x2.astype(F32)).astype(BF16)
    return x1, x2, x3


def _exact_left_mul(w, x):
    x1, x2, x3 = _split3(x)
    return _dot(w, x1) + _dot(w, x2) + _dot(w, x3)


def _exact_left_mul3(w3, x):
    return _dot(w3, jnp.concatenate(_split3(x), axis=0))


def _rows(ref, r0, n):
    if ref.shape[0] == 1:
        return ref[...]
    return ref[pl.ds(r0, n), :]


def _norm_mod(x, sc, sh):
    ms = jnp.mean(x * x, axis=-1, keepdims=True)
    return x * lax.rsqrt(ms + EPS) * (1.0 + sc) + sh


def _head_norm(o, w):
    return o * lax.rsqrt(jnp.mean(o * o, axis=-1, keepdims=True) + EPS) * w


def _lower_bound(lg):
    l0, l1 = lg[0:1, :], lg[1:2, :]
    m = jnp.maximum(l0, l1)
    e0, e1 = jnp.exp(l0 - m), jnp.exp(l1 - m)
    return e0 / (e0 + e1)


def _group_last(x, t, up):
    return jnp.where(t == 3, x, jnp.where(t == 2, up(x, 1), jnp.where(t == 1, up(x, 2), up(x, 3))))


def _mod_kernel(c_ref, w_ref, b_ref, o_ref, rep_s, *, n_rep):
    c = c_ref[...]
    s = (c * jax.nn.sigmoid(c)).astype(BF16)
    res = _dot(s, w_ref[...].astype(BF16)) + b_ref[...]
    for k in range(res.shape[1] // LANES):
        slab = res[:, k * LANES:(k + 1) * LANES]
        for t in range(GROUP):
            rep_s[k, pl.ds(t, n_rep, stride=GROUP), :] = slab[:n_rep]
        rep_s[k, n_rep * GROUP:, :] = slab[n_rep:]
    for k in range(res.shape[1] // LANES):
        o_ref[:, k * LANES:(k + 1) * LANES] = rep_s[k]


def _mod_call(c_all, n_rep, ada_w, ada_b):
    mp, d = c_all.shape
    n = ada_w.shape[1]
    tn = 512
    mo = n_rep * GROUP + (mp - n_rep)
    return pl.pallas_call(
        functools.partial(_mod_kernel, n_rep=n_rep),
        grid=(n // tn,),
        in_specs=[pl.BlockSpec((mp, d), lambda j: (0, 0)),
                  pl.BlockSpec((d, tn), lambda j: (0, j)),
                  pl.BlockSpec((1, tn), lambda j: (0, j))],
        out_specs=pl.BlockSpec((mo, tn), lambda j: (0, j)),
        out_shape=jax.ShapeDtypeStruct((mo, n), F32),
        scratch_shapes=[pltpu.VMEM((tn // LANES, mo, LANES), F32)],
        compiler_params=_cparams("arbitrary"),
        name="mod",
    )(c_all, ada_w, ada_b)


def _inproj_kernel(*refs, tm, rc, na, emit_bf16):
    if emit_bf16:
        x_ref, sh_ref, sc_ref, wa_ref, wb_ref, wg_ref, o_ref, og_ref, wo_ref, wgo_ref, h_ref = refs
    else:
        x_ref, sh_ref, sc_ref, wa_ref, wg_ref, o_ref, og_ref, h_ref = refs
        wb_ref = wo_ref = wgo_ref = None
    j = pl.program_id(1)

    def weight(src, dst):
        w = src[...]
        if emit_bf16:
            w = w.astype(BF16)
            dst[...] = w
        return w

    @pl.when(j == 0)
    def _():
        wg = weight(wg_ref, wgo_ref)

        def body(r, carry):
            r0 = pl.multiple_of(r * rc, rc)
            h = _norm_mod(x_ref[pl.ds(r0, rc), :], _rows(sc_ref, r0, rc), _rows(sh_ref, r0, rc))
            hb = h.astype(BF16)
            h_ref[pl.ds(r0, rc), :] = hb
            gg = _dot_nt(hb, wg)
            og_ref[pl.ds(r0, rc), :] = jnp.concatenate(
                [gg, jnp.zeros((rc, og_ref.shape[1] - gg.shape[1]), F32)], axis=1)
            return carry
        lax.fori_loop(0, tm // rc, body, 0)

    if emit_bf16:
        @pl.when(j < na)
        def _():
            o_ref[...] = _dot_nt(h_ref[...], weight(wa_ref, wo_ref))

        @pl.when(j >= na)
        def _():
            o_ref[...] = _dot_nt(h_ref[...], weight(wb_ref, wo_ref))
    else:
        o_ref[...] = _dot_nt(h_ref[...], wa_ref[...])


def _mod_spec(mod, tm, col, tiles_per_seq):
    d = mod.shape[-1] // 6
    if mod.ndim == 3:
        return pl.BlockSpec((None, 1, d), lambda i, *_: (i // tiles_per_seq, 0, col))
    return pl.BlockSpec((tm, d), lambda i, *_: (i, col))


def _inproj_call(x, mod, w, gate0, n_gate, tm, tn, tiles_per_seq):
    m, d = x.shape
    emit_bf16 = not isinstance(w, tuple)
    gate1 = gate0 + n_gate
    n = (w.shape[0] - n_gate) if emit_bf16 else w[0].shape[0]
    assert n_gate == 2 * SUBLANES and gate0 % tn == 0 and n % tn == 0
    na = gate0 // tn
    tile = pl.BlockSpec((tn, d), lambda i, j: (j, 0))
    seg_g = pl.BlockSpec((n_gate, d), lambda i, j: (0, 0))
    out_specs = [pl.BlockSpec((tm, tn), lambda i, j: (i, j)), pl.BlockSpec((tm, LANES), lambda i, j: (i, 0))]
    out_shape = [jax.ShapeDtypeStruct((m, n), F32), jax.ShapeDtypeStruct((m, LANES), F32)]
    if emit_bf16:
        assert m == tm
        w_specs = [pl.BlockSpec((tn, d), lambda i, j: (jnp.minimum(j, na - 1), 0)),
                   pl.BlockSpec((pl.Element(tn), pl.Element(d)),
                                lambda i, j: (pl.multiple_of(gate1 + jnp.maximum(j - na, 0) * tn, SUBLANES), 0)),
                   pl.BlockSpec((n_gate, d), lambda i, j: (gate0 // n_gate, 0))]
        w_args = [w, w, w]
        out_specs += [tile, seg_g]
        out_shape += [jax.ShapeDtypeStruct((n, d), BF16), jax.ShapeDtypeStruct((n_gate, d), BF16)]
    else:
        w_specs = [tile, seg_g]
        w_args = list(w)
    kern = functools.partial(_inproj_kernel, tm=tm, rc=256, na=na, emit_bf16=emit_bf16)
    return pl.pallas_call(
        kern,
        grid=(m // tm, n // tn),
        in_specs=[pl.BlockSpec((tm, d), lambda i, j: (i, 0)),
                  _mod_spec(mod, tm, 0, tiles_per_seq),
                  _mod_spec(mod, tm, 1, tiles_per_seq)] + w_specs,
        out_specs=out_specs,
        out_shape=out_shape,
        scratch_shapes=[pltpu.VMEM((tm, d), BF16)],
        compiler_params=_cparams("arbitrary", "arbitrary"),
        name="inproj",
    )(x, mod, mod, *w_args)


def _hgrn_constants():
    L = HG_CHUNK
    tri = np.tril(np.ones((L, L), np.float32))
    t = np.arange(L)
    blocks = []
    masks = []
    for h in HG_LEVELS:
        mid = (t // (2 * h)) * (2 * h) + h - 1
        blocks.append(tri - tri[mid])
        masks.append((t[:, None] // (2 * h) == t[None, :] // (2 * h)).astype(np.float32))
    blocks.append(tri[L - 1][None, :] - tri)
    masks.append(np.eye(L, dtype=np.float32))
    return jnp.asarray(np.concatenate(blocks, 0), BF16), jnp.asarray(np.stack(masks, 0), F32)


def _hgrn_gates(hq, z, lb, kdim):
    omlb = 1.0 - lb
    sz = jax.nn.sigmoid(z)
    logf = jnp.log(lb + omlb * sz)
    kk = omlb * (1.0 - sz)
    q = hq * jax.nn.sigmoid(hq) * (kdim ** -0.5)
    return q, kk, logf


def _hgrn_p_kernel(hq_ref, hf_ref, hi_ref, hg_ref, ga_ref, lg_ref, nw_ref, trib_ref, wall_ref, mask_ref,
                   ya_ref, so_ref, st_ref, *, n_chunks, kdim):
    L = HG_CHUNK
    c = pl.program_id(2)
    hps = hq_ref.shape[1] // kdim
    tb = trib_ref.shape[0]
    heads_chunks = [(hd, ci) for ci in range(n_chunks) for hd in range(hps)]
    rows = lambda ci: slice(ci * L, (ci + 1) * L)
    lanes = lambda hd: slice(hd * kdim, (hd + 1) * kdim)

    @pl.when(c == 0)
    def _():
        st_ref[...] = jnp.zeros_like(st_ref)

    lb = _lower_bound(lg_ref[...])
    nw = nw_ref[...]
    q, kk, logf = _hgrn_gates(hq_ref[...], hf_ref[...], lb, kdim)
    trib = trib_ref[...]
    b = jnp.concatenate([_exact_left_mul3(trib, logf[r:r + tb]) for r in range(0, n_chunks * L, tb)], axis=0)
    worst = -jnp.min(jnp.concatenate([b[r:r + 1, :] for r in range(L - 1, n_chunks * L, L)], axis=0))

    def finish(hd, ci, o):
        hg = hg_ref[rows(ci), lanes(hd)]
        ya = _head_norm(o, nw[:, lanes(hd)]) * (hg * jax.nn.sigmoid(hg))
        ya_ref[rows(ci), lanes(hd)] = jax.nn.sigmoid(ga_ref[rows(ci), lanes(hd)]) * ya

    @pl.when(worst <= HG_SAFE_EXP)
    def _():
        eb = jnp.exp(b)
        qt = (q * eb).astype(BF16)
        kn = kk * (1.0 / eb)
        knb = kn.astype(BF16)
        row = lax.broadcasted_iota(jnp.int32, (L, L), 0)
        colid = lax.broadcasted_iota(jnp.int32, (L, L), 1)
        causal = colid <= row
        o_intra, incr, dec = {}, {}, {}
        for hd, ci in heads_chunks:
            vb = hi_ref[rows(ci), lanes(hd)].astype(BF16)
            a = jnp.where(causal, _dot_nt(qt[rows(ci), lanes(hd)], knb[rows(ci), lanes(hd)]), 0.0)
            o_intra[hd, ci] = _dot(a.astype(BF16), vb)
            dec[hd, ci] = eb[(ci + 1) * L - 1:(ci + 1) * L, lanes(hd)]
            incr[hd, ci] = _dot_tn(vb, (kn[rows(ci), lanes(hd)] * dec[hd, ci]).astype(BF16))
        sts = {(hd, 0): st_ref[hd] for hd in range(hps)}
        for hd, ci in heads_chunks:
            sts[hd, ci + 1] = sts[hd, ci] * dec[hd, ci] + incr[hd, ci]
        for hd in range(hps):
            st_ref[hd] = sts[hd, n_chunks]
        for hd, ci in heads_chunks:
            finish(hd, ci, o_intra[hd, ci] + _dot_nt(qt[rows(ci), lanes(hd)], sts[hd, ci].astype(BF16)))

    @pl.when(jnp.logical_not(worst <= HG_SAFE_EXP))
    def _():
        wall = wall_ref[...]
        rowid = lax.broadcasted_iota(jnp.int32, (L, kdim), 0)
        n_lv = len(HG_LEVELS)
        for hd, ci in heads_chunks:
            qc, kc, bc = q[rows(ci), lanes(hd)], kk[rows(ci), lanes(hd)], b[rows(ci), lanes(hd)]
            vb = hi_ref[rows(ci), lanes(hd)].astype(BF16)
            d = _exact_left_mul(wall, logf[rows(ci), lanes(hd)])
            a = mask_ref[n_lv] * _dot_nt(qc.astype(BF16), kc.astype(BF16))
            for li, h in enumerate(HG_LEVELS):
                e = jnp.exp(-jnp.abs(d[li * L:(li + 1) * L]))
                second = (rowid & h) != 0
                p = jnp.where(second, qc, kc) * e
                qh = jnp.where(second, p, 0.0).astype(BF16)
                kh = jnp.where(second, 0.0, p).astype(BF16)
                a = a + mask_ref[li] * _dot_nt(qh, kh)
            qt = (qc * jnp.exp(bc)).astype(BF16)
            kt = (kc * jnp.exp(d[n_lv * L:(n_lv + 1) * L])).astype(BF16)
            st = st_ref[hd]
            finish(hd, ci, _dot(a.astype(BF16), vb) + _dot_nt(qt, st.astype(BF16)))
            st_ref[hd] = st * jnp.exp(bc[L - 1:L, :]) + _dot_tn(vb, kt)

    @pl.when(c == pl.num_programs(2) - 1)
    def _():
        for hd in range(hps):
            so_ref[hd] = st_ref[hd].T


def _hgrn_p_call(proj, lb_logits, norm_w, n_seq, seq_len, heads, kdim, col0, col_ga, rows_per_step, hps):
    wall, masks = _hgrn_constants()
    per_mat = min(rows_per_step, MXU_DEPTH) // HG_CHUNK
    trib = np.kron(np.eye(per_mat, dtype=np.float32), np.tril(np.ones((HG_CHUNK, HG_CHUNK), np.float32)))
    trib = jnp.asarray(np.concatenate([trib, trib, trib], axis=1), BF16)
    nc = seq_len // rows_per_step
    assert heads % hps == 0 and col0 % hps == 0 and col_ga % hps == 0
    kern = functools.partial(_hgrn_p_kernel, n_chunks=rows_per_step // HG_CHUNK, kdim=kdim)
    wide = hps * kdim

    def col(first):
        return pl.BlockSpec((rows_per_step, wide), lambda b, h, c: (b * nc + c, first // hps + h))

    return pl.pallas_call(
        kern,
        grid=(n_seq, heads // hps, nc),
        in_specs=[col(col0), col(col0 + heads), col(col0 + 2 * heads), col(col0 + 3 * heads), col(col_ga),
                  pl.BlockSpec((2, wide), lambda b, h, c: (0, h)),
                  pl.BlockSpec((1, wide), lambda b, h, c: (0, h)),
                  pl.BlockSpec(trib.shape, lambda b, h, c: (0, 0)),
                  pl.BlockSpec(wall.shape, lambda b, h, c: (0, 0)),
                  pl.BlockSpec(masks.shape, lambda b, h, c: (0, 0, 0))],
        out_specs=[pl.BlockSpec((rows_per_step, wide), lambda b, h, c: (b * nc + c, h)),
                   pl.BlockSpec((None, hps, kdim, kdim), lambda b, h, c: (b, h, 0, 0))],
        out_shape=[jax.ShapeDtypeStruct((n_seq * seq_len, heads * kdim), F32),
                   jax.ShapeDtypeStruct((n_seq, heads, kdim, kdim), F32)],
        scratch_shapes=[pltpu.VMEM((hps, kdim, kdim), F32)],
        compiler_params=_cparams("arbitrary", "arbitrary", "arbitrary"),
        name="hgrn_prompt",
    )(proj, proj, proj, proj, proj, lb_logits, norm_w, trib, wall, masks)


def _hgrn_s_kernel(hq_ref, hf_ref, hi_ref, hg_ref, ga_ref, lg_ref, nw_ref, s_ref, ya_ref, so_ref, *, n_pairs):
    kdim = hq_ref.shape[1]
    rows = n_pairs * SUBLANES
    lb = _lower_bound(lg_ref[...])
    rowid = lax.broadcasted_iota(jnp.int32, (rows, kdim), 0)
    t = rowid & (GROUP - 1)
    first = (rowid & GROUP) == 0
    first8 = lax.broadcasted_iota(jnp.int32, (SUBLANES, kdim), 0) < GROUP
    down = lambda y, j: pltpu.roll(y, j, 0)
    up = lambda y, j: pltpu.roll(y, rows - j, 0)

    q, kk, logf = _hgrn_gates(hq_ref[...], hf_ref[...], lb, kdim)
    v = hi_ref[...]
    b = logf
    for dlt in range(1, GROUP):
        b = b + jnp.where(t >= dlt, down(logf, dlt), 0.0)
    o = jnp.sum(q * kk, axis=1, keepdims=True) * v
    for dlt in range(1, GROUP):
        x = q * down(kk, dlt) * jnp.exp(b - down(b, dlt))
        a = jnp.sum(jnp.where(t >= dlt, x, 0.0), axis=1, keepdims=True)
        o = o + a * down(v, dlt)
    b_last = _group_last(b, t, up)
    qt = q * jnp.exp(b)
    kt = kk * jnp.exp(b_last - b)
    d1, d2, d3 = _split3(jnp.exp(b_last))
    swap = lambda y: jnp.where(first, up(y.astype(F32), GROUP), down(y.astype(F32), GROUP))
    dsplit = jnp.where(t == 0, swap(d1), jnp.where(t == 1, swap(d2), jnp.where(t == 2, swap(d3), 0.0)))
    ones = jnp.where(t <= 2, 1.0, 0.0)
    lhs = (jnp.where(first, kt, dsplit), jnp.where(first, dsplit, kt))
    rhs = (jnp.concatenate([jnp.where(first, v, 0.0), jnp.where(first, 0.0, ones)], axis=1),
           jnp.concatenate([jnp.where(first, 0.0, v), jnp.where(first, ones, 0.0)], axis=1))
    o_inter = []
    for p in range(n_pairs):
        sl = slice(p * SUBLANES, (p + 1) * SUBLANES)
        qb = qt[sl].astype(BF16)
        parts = []
        for half in range(2):
            s0 = s_ref[2 * p + half]
            upd = _dot_tn(lhs[half][sl].astype(BF16), rhs[half][sl].astype(BF16))
            so_ref[2 * p + half] = s0 * upd[:, kdim:] + upd[:, :kdim]
            parts.append(_dot(qb, s0.astype(BF16)))
        o_inter.append(jnp.where(first8, parts[0], parts[1]))
    o = o + jnp.concatenate(o_inter, axis=0)
    hg = hg_ref[...]
    ya_ref[...] = jax.nn.sigmoid(ga_ref[...]) * (_head_norm(o, nw_ref[...]) * (hg * jax.nn.sigmoid(hg)))


def _hgrn_s_call(proj, lb_logits, norm_w, state, heads, kdim, col0, col_ga, seqs_per_step):
    n_seq = state.shape[0]
    rows = seqs_per_step * GROUP
    kern = functools.partial(_hgrn_s_kernel, n_pairs=seqs_per_step // 2)

    def col(k):
        return pl.BlockSpec((rows, kdim), lambda i, h: (i, col0 + k * heads + h))

    st_spec = pl.BlockSpec((seqs_per_step, None, kdim, kdim), lambda i, h: (i, h, 0, 0))
    return pl.pallas_call(
        kern,
        grid=(n_seq // seqs_per_step, heads),
        in_specs=[col(0), col(1), col(2), col(3),
                  pl.BlockSpec((rows, kdim), lambda i, h: (i, col_ga + h)),
                  pl.BlockSpec((2, kdim), lambda i, h: (0, h)),
                  pl.BlockSpec((1, kdim), lambda i, h: (0, h)),
                  st_spec],
        out_specs=[pl.BlockSpec((rows, kdim), lambda i, h: (i, h)), st_spec],
        out_shape=[jax.ShapeDtypeStruct((n_seq * GROUP, heads * kdim), F32),
                   jax.ShapeDtypeStruct(state.shape, F32)],
        compiler_params=_cparams("arbitrary", "arbitrary"),
        name="hgrn_sample",
    )(proj, proj, proj, proj, proj, lb_logits, norm_w, state)


def _lane_pick(x, lane, idx):
    return jnp.broadcast_to(jnp.sum(jnp.where(lane == idx, x, 0.0), axis=1, keepdims=True), x.shape)


def _interleave(chains):
    live = list(chains)
    while live:
        still = []
        for g in live:
            try:
                next(g)
                still.append(g)
            except StopIteration:
                pass
        live = still


def _mlstm_p_kernel(q_ref, k_ref, v_ref, og_ref, gb_ref, ya_ref, g_ref, bias_ref, nw_ref, tri_ref,
                    mg_ref, co_ref, no_ref, mo_ref, c_s, n_s, m_s, *, heads, qk, vd):
    L = ML_CHUNK
    c = pl.program_id(1)

    @pl.when(c == 0)
    def _():
        c_s[...] = jnp.zeros_like(c_s)
        n_s[...] = jnp.zeros_like(n_s)
        m_s[...] = jnp.zeros_like(m_s)

    g = g_ref[...] + bias_ref[...]
    lane = lax.broadcasted_iota(jnp.int32, g.shape, 1)
    row = lax.broadcasted_iota(jnp.int32, g.shape, 0)
    b_all = _exact_left_mul(tri_ref[...], jax.nn.log_sigmoid(g))

    def head(hd):
        qs = slice(hd * qk, (hd + 1) * qk)
        vs = slice(hd * vd, (hd + 1) * vd)
        ig = _lane_pick(g, lane, hd)
        b = _lane_pick(b_all, lane, heads + hd)
        m_prev = m_s[hd]
        gs = (ig - b).T
        yield
        dm = jnp.where(lane <= row, b + gs, -jnp.inf)
        mt = jnp.maximum(b + m_prev, jnp.max(dm, axis=1, keepdims=True))
        inter = jnp.exp(b + m_prev - mt)
        q = q_ref[:, qs] * (qk ** -0.5)
        k = k_ref[:, qs]
        vb = v_ref[:, vs].astype(BF16)
        qb = q.astype(BF16)
        sc = _dot_nt(qb, k.astype(BF16)) * jnp.exp(dm - mt)
        yield
        c0 = c_s[hd]
        n0 = n_s[hd]
        num = inter[:, 0:1] * _dot(qb, c0.astype(BF16)) + _dot(sc.astype(BF16), vb)
        den = inter[:, 0:1] * jnp.sum(q * n0, axis=1, keepdims=True) + jnp.sum(sc, axis=1, keepdims=True)
        yield
        hh = num / jnp.maximum(jnp.abs(den), jnp.exp(-mt[:, 0:1]))
        yb = _head_norm(hh, nw_ref[:, vs]) * jax.nn.sigmoid(og_ref[:, vs])
        mg_ref[:, vs] = (ya_ref[:, vs] + jax.nn.sigmoid(gb_ref[:, vs]) * yb).astype(mg_ref.dtype)
        yield
        m_last = mt[L - 1:L, :]
        b_last = b[L - 1:L, :]
        dec = jnp.exp(b_last + m_prev - m_last)
        kw = jnp.exp(b_last - b + ig - m_last) * k
        c_s[hd] = jnp.concatenate([dec, dec], axis=1) * c0 + _dot_tn(kw.astype(BF16), vb)
        n_s[hd] = dec * n0 + jnp.sum(kw, axis=0, keepdims=True)
        m_s[hd] = m_last

    for h0 in range(0, heads, ML_HEADS_INTERLEAVED):
        _interleave([head(hd) for hd in range(h0, min(h0 + ML_HEADS_INTERLEAVED, heads))])

    @pl.when(c == pl.num_programs(1) - 1)
    def _():
        co_ref[...] = c_s[...]
        no_ref[...] = n_s[...]
        mo_ref[...] = m_s[...]


def _mlstm_p_call(proj, ya, gates, bias, norm_w, n_seq, seq_len, heads, qk, vd, colq, colk, colv, colo, colg):
    L = ML_CHUNK
    assert qk == L and qk == LANES
    nc = seq_len // L
    qw, vw = heads * qk, heads * vd
    assert colq % qw == 0 and colk % qw == 0 and colv % vw == 0 and colo % vw == 0 and colg % vw == 0
    tri = jnp.asarray(np.tril(np.ones((L, L), np.float32)), BF16)
    kern = functools.partial(_mlstm_p_kernel, heads=heads, qk=qk, vd=vd)
    return pl.pallas_call(
        kern,
        grid=(n_seq, nc),
        in_specs=[pl.BlockSpec((L, qw), lambda b, c: (b * nc + c, colq // qw)),
                  pl.BlockSpec((L, qw), lambda b, c: (b * nc + c, colk // qw)),
                  pl.BlockSpec((L, vw), lambda b, c: (b * nc + c, colv // vw)),
                  pl.BlockSpec((L, vw), lambda b, c: (b * nc + c, colo // vw)),
                  pl.BlockSpec((L, vw), lambda b, c: (b * nc + c, colg // vw)),
                  pl.BlockSpec((L, vw), lambda b, c: (b * nc + c, 0)),
                  pl.BlockSpec((L, LANES), lambda b, c: (b * nc + c, 0)),
                  pl.BlockSpec((1, LANES), lambda b, c: (0, 0)),
                  pl.BlockSpec((1, vw), lambda b, c: (0, 0)),
                  pl.BlockSpec((L, L), lambda b, c: (0, 0))],
        out_specs=[pl.BlockSpec((L, vw), lambda b, c: (b * nc + c, 0)),
                   pl.BlockSpec((None, heads, qk, vd), lambda b, c: (b, 0, 0, 0)),
                   pl.BlockSpec((None, heads, 1, qk), lambda b, c: (b, 0, 0, 0)),
                   pl.BlockSpec((None, heads, 1, LANES), lambda b, c: (b, 0, 0, 0))],
        out_shape=[jax.ShapeDtypeStruct((n_seq * seq_len, vw), BF16),
                   jax.ShapeDtypeStruct((n_seq, heads, qk, vd), F32),
                   jax.ShapeDtypeStruct((n_seq, heads, 1, qk), F32),
                   jax.ShapeDtypeStruct((n_seq, heads, 1, LANES), F32)],
        scratch_shapes=[pltpu.VMEM((heads, qk, vd), F32), pltpu.VMEM((heads, 1, qk), F32),
                        pltpu.VMEM((heads, 1, LANES), F32)],
        compiler_params=_cparams("arbitrary", "arbitrary"),
        name="mlstm_prompt",
    )(proj, proj, proj, proj, proj, ya, gates, bias, norm_w, tri)


def _mlstm_s_kernel(q_ref, k_ref, v_ref, og_ref, gb_ref, ya_ref, g_ref, bias_ref, m_ref, nw_ref, c_ref, n_ref,
                    mg_ref, co_ref, no_ref, mo_ref, *, heads, n_pairs):
    h = pl.program_id(1)
    qk = q_ref.shape[1]
    rows = n_pairs * SUBLANES
    lane = lax.broadcasted_iota(jnp.int32, (rows, LANES), 1)
    rowid = lax.broadcasted_iota(jnp.int32, (rows, LANES), 0)
    t = rowid & (GROUP - 1)
    first = (rowid & GROUP) == 0
    first8 = lax.broadcasted_iota(jnp.int32, (SUBLANES, 1), 0) < GROUP
    down = lambda y, j: pltpu.roll(y, j, 0)
    up = lambda y, j: pltpu.roll(y, rows - j, 0)

    g = g_ref[...] + bias_ref[...]
    ig = _lane_pick(g, lane, h)
    lf = _lane_pick(jax.nn.log_sigmoid(g), lane, heads + h)
    m_prev = _lane_pick(m_ref[...], lane, h)
    b = lf
    for dlt in range(1, GROUP):
        b = b + jnp.where(t >= dlt, down(lf, dlt), 0.0)
    q = q_ref[...] * (qk ** -0.5)
    k = k_ref[...]
    v = v_ref[...]
    dms = [ig] + [jnp.where(t >= dlt, b - down(b, dlt) + down(ig, dlt), -jnp.inf) for dlt in range(1, GROUP)]
    mt = jnp.maximum(b + m_prev, functools.reduce(jnp.maximum, dms))
    inter = jnp.exp(b + m_prev - mt)[:, 0:1]
    num = jnp.zeros(v.shape, F32)
    den = jnp.zeros((rows, 1), F32)
    for dlt in range(GROUP):
        kd = k if dlt == 0 else down(k, dlt)
        vd_ = v if dlt == 0 else down(v, dlt)
        s = jnp.sum(q * kd, axis=1, keepdims=True) * jnp.exp(dms[dlt] - mt)[:, 0:1]
        num = num + s * vd_
        den = den + s
    m_last = _group_last(mt, t, up)
    b_last = _group_last(b, t, up)
    dec = jnp.exp(b_last + m_prev - m_last)
    kw = jnp.exp(b_last - b + ig - m_last) * k
    n0 = n_ref[...]
    no_ref[...] = dec * n0 + kw + down(kw, 1) + down(kw, 2) + down(kw, 3)
    mo_ref[...] = m_last
    den = den + inter * jnp.sum(q * n0, axis=1, keepdims=True)

    kw_half = (jnp.where(first, kw, 0.0), jnp.where(first, 0.0, kw))
    dec2 = jnp.concatenate([dec, dec], axis=1)
    num_inter = []
    for p in range(n_pairs):
        sl = slice(p * SUBLANES, (p + 1) * SUBLANES)
        qb = q[sl].astype(BF16)
        vb = v[sl].astype(BF16)
        parts = []
        for half in range(2):
            c0 = c_ref[2 * p + half]
            r = p * SUBLANES + half * GROUP
            co_ref[2 * p + half] = dec2[r:r + 1, :] * c0 + _dot_tn(kw_half[half][sl].astype(BF16), vb)
            parts.append(_dot(qb, c0.astype(BF16)))
        num_inter.append(jnp.where(first8, parts[0], parts[1]))
    num = num + inter * jnp.concatenate(num_inter, axis=0)
    hh = num / jnp.maximum(jnp.abs(den), jnp.exp(-mt)[:, 0:1])
    yb = _head_norm(hh, nw_ref[...]) * jax.nn.sigmoid(og_ref[...])
    mg_ref[...] = (ya_ref[...] + jax.nn.sigmoid(gb_ref[...]) * yb).astype(mg_ref.dtype)


def _mlstm_s_call(proj, ya, gates, bias, m_rows, norm_w, c_state, n_state, heads, qk, vd,
                  colq, colk, colv, colo, colg, seqs_per_step):
    n_seq = c_state.shape[0]
    rows = seqs_per_step * GROUP
    kern = functools.partial(_mlstm_s_kernel, heads=heads, n_pairs=seqs_per_step // 2)
    c_spec = pl.BlockSpec((seqs_per_step, None, qk, vd), lambda i, h: (i, h, 0, 0))
    n_spec = pl.BlockSpec((None, rows, qk), lambda i, h: (h, i, 0))
    return pl.pallas_call(
        kern,
        grid=(n_seq // seqs_per_step, heads),
        in_specs=[pl.BlockSpec((rows, qk), lambda i, h: (i, colq // qk + h)),
                  pl.BlockSpec((rows, qk), lambda i, h: (i, colk // qk + h)),
                  pl.BlockSpec((rows, vd), lambda i, h: (i, colv // vd + h)),
                  pl.BlockSpec((rows, vd), lambda i, h: (i, colo // vd + h)),
                  pl.BlockSpec((rows, vd), lambda i, h: (i, colg // vd + h)),
                  pl.BlockSpec((rows, vd), lambda i, h: (i, h)),
                  pl.BlockSpec((rows, LANES), lambda i, h: (i, 0)),
                  pl.BlockSpec((1, LANES), lambda i, h: (0, 0)),
                  pl.BlockSpec((rows, LANES), lambda i, h: (i, 0)),
                  pl.BlockSpec((1, vd), lambda i, h: (0, h)),
                  c_spec, n_spec],
        out_specs=[pl.BlockSpec((rows, vd), lambda i, h: (i, h)),
                   c_spec, n_spec,
                   pl.BlockSpec((None, rows, LANES), lambda i, h: (h, i, 0))],
        out_shape=[jax.ShapeDtypeStruct((n_seq * GROUP, heads * vd), BF16),
                   jax.ShapeDtypeStruct(c_state.shape, F32),
                   jax.ShapeDtypeStruct(n_state.shape, F32),
                   jax.ShapeDtypeStruct((heads, n_seq * GROUP, LANES), F32)],
        compiler_params=_cparams("arbitrary", "arbitrary"),
        name="mlstm_sample",
    )(proj, proj, proj, proj, proj, ya, gates, bias, m_rows, norm_w, c_state, n_state)


def _outproj_kernel(mg_ref, x_ref, g1_ref, w_ref, o_ref):
    o_ref[...] = x_ref[...] + g1_ref[...] * _dot(mg_ref[...], w_ref[...])


def _outproj_call(merged, x, mod, w_out, tm, tiles_per_seq):
    m, d = x.shape
    blk = pl.BlockSpec((tm, d), lambda i: (i, 0))
    return pl.pallas_call(
        _outproj_kernel,
        grid=(m // tm,),
        in_specs=[blk, blk, _mod_spec(mod, tm, 2, tiles_per_seq), pl.BlockSpec((d, d), lambda i: (0, 0))],
        out_specs=blk,
        out_shape=jax.ShapeDtypeStruct((m, d), F32),
        compiler_params=_cparams("arbitrary"),
        name="outproj",
    )(merged, x, mod, w_out)


def _ffn_kernel(*refs, tm, rc, tiles_per_seq, grouped, emit_bf16):
    refs = iter(refs)
    x_ref, sh_ref, sc_ref, g2_ref, wa_ref, wg_ref, wd_ref, cw_ref, cb_ref, fw_ref = (next(refs) for _ in range(10))
    p2_ref = next(refs) if grouped else None
    y_ref = next(refs)
    if grouped:
        cv2_ref, cv3_ref = next(refs), next(refs)
    else:
        a_ref = next(refs)
    wao_ref, wgo_ref, wdo_ref = (next(refs), next(refs), next(refs)) if emit_bf16 else (None, None, None)
    h_s, acc_s = next(refs), next(refs)
    a_s = carry_s = next(refs)
    i = pl.program_id(0)
    j = pl.program_id(1)

    def weight(src, dst):
        w = src[...]
        if emit_bf16:
            w = w.astype(BF16)
            dst[...] = w
        return w

    @pl.when(j == 0)
    def _():
        def body(r, carry):
            r0 = pl.multiple_of(r * rc, rc)
            h = _norm_mod(x_ref[pl.ds(r0, rc), :], _rows(sc_ref, r0, rc), _rows(sh_ref, r0, rc))
            h_s[pl.ds(r0, rc), :] = h.astype(BF16)
            return carry
        lax.fori_loop(0, tm // rc, body, 0)
        acc_s[...] = jnp.zeros_like(acc_s)

    hb = h_s[...]
    a = _dot(hb, weight(wa_ref, wao_ref))
    g = _dot(hb, weight(wg_ref, wgo_ref))
    rowid = lax.broadcasted_iota(jnp.int32, a.shape, 0)
    r1 = pltpu.roll(a, 1, 0)
    r2 = pltpu.roll(a, 2, 0)
    if grouped:
        t = rowid & (GROUP - 1)
        p2 = p2_ref[...]
        prev1 = jnp.where(t == 0, pltpu.roll(p2, tm - 1, 0), r1)
        prev2 = jnp.where(t <= 1, p2, r2)
        for s in range(a.shape[1] // LANES):
            a_s[s] = a[:, s * LANES:(s + 1) * LANES]
        for s in range(a.shape[1] // LANES):
            cv2_ref[:, s * LANES:(s + 1) * LANES] = a_s[s, pl.ds(GROUP - 2, tm // GROUP, stride=GROUP), :]
            cv3_ref[:, s * LANES:(s + 1) * LANES] = a_s[s, pl.ds(GROUP - 1, tm // GROUP, stride=GROUP), :]
    else:
        tail = a[tm - SUBLANES:tm, :]
        car = jnp.where(i % tiles_per_seq == 0, 0.0, carry_s[j])
        c1 = car[SUBLANES - 1:SUBLANES, :]
        c2 = car[SUBLANES - 2:SUBLANES - 1, :]
        prev1 = jnp.where(rowid == 0, c1, r1)
        prev2 = jnp.where(rowid == 0, c2, jnp.where(rowid == 1, c1, r2))
        carry_s[j] = tail
        a_ref[...] = tail
    cw = cw_ref[...]
    ac = cw[0:1, :] * prev2 + cw[1:2, :] * prev1 + cw[2:3, :] * a + cb_ref[...]
    yv = 0.5 * ac * (1.0 + lax.erf(ac * (2.0 ** -0.5))) * g
    acc_s[...] += _dot(yv.astype(BF16), weight(wd_ref, wdo_ref))

    @pl.when(j == pl.num_programs(1) - 1)
    def _():
        def body(r, carry):
            r0 = pl.multiple_of(r * rc, rc)
            x2 = x_ref[pl.ds(r0, rc), :] + _rows(g2_ref, r0, rc) * acc_s[pl.ds(r0, rc), :]
            ms = jnp.mean(x2 * x2, axis=-1, keepdims=True)
            y_ref[pl.ds(r0, rc), :] = x2 * lax.rsqrt(ms + EPS) * fw_ref[...]
            return carry
        lax.fori_loop(0, tm // rc, body, 0)


def _ffn_call(x, mod, w_up, w_down, conv_w, conv_b, final_w, tm, tf, tiles_per_seq, p2=None):
    m, d = x.shape
    f = w_down.shape[0]
    nj = f // tf
    grouped = p2 is not None
    emit_bf16 = not isinstance(w_up, tuple)
    kern = functools.partial(_ffn_kernel, tm=tm, rc=min(tm, 256), tiles_per_seq=tiles_per_seq, grouped=grouped,
                             emit_bf16=emit_bf16)
    half_spec = pl.BlockSpec((d, tf), lambda i, j: (0, j))
    down_spec = pl.BlockSpec((tf, d), lambda i, j: (j, 0))
    if emit_bf16:
        assert m == tm
        up_specs = [half_spec, pl.BlockSpec((d, tf), lambda i, j: (0, nj + j))]
        up_args = [w_up, w_up]
    else:
        up_specs = [half_spec, half_spec]
        up_args = list(w_up)
    in_specs = [pl.BlockSpec((tm, d), lambda i, j: (i, 0)),
                _mod_spec(mod, tm, 3, tiles_per_seq),
                _mod_spec(mod, tm, 4, tiles_per_seq),
                _mod_spec(mod, tm, 5, tiles_per_seq)] + up_specs + [
                down_spec,
                pl.BlockSpec((CONV_W, tf), lambda i, j: (0, j)),
                pl.BlockSpec((1, tf), lambda i, j: (0, j)),
                pl.BlockSpec((1, d), lambda i, j: (0, 0))]
    args = [x, mod, mod, mod] + up_args + [w_down, conv_w, conv_b, final_w]
    scratch = [pltpu.VMEM((tm, d), BF16), pltpu.VMEM((tm, d), F32)]
    if grouped:
        in_specs.append(pl.BlockSpec((tm, tf), lambda i, j: (i, j)))
        args.append(p2)
        scratch.append(pltpu.VMEM((tf // LANES, tm, LANES), F32))
        a_specs = [pl.BlockSpec((tm // GROUP, tf), lambda i, j: (i, j))] * 2
        a_shapes = [jax.ShapeDtypeStruct((m // GROUP, f), F32)] * 2
    else:
        scratch.append(pltpu.VMEM((nj, SUBLANES, tf), F32))
        a_specs = [pl.BlockSpec((SUBLANES, tf), lambda i, j: (i, j))]
        a_shapes = [jax.ShapeDtypeStruct((m // tm * SUBLANES, f), F32)]
    if emit_bf16:
        a_specs += [half_spec, half_spec, down_spec]
        a_shapes += [jax.ShapeDtypeStruct((d, f), BF16), jax.ShapeDtypeStruct((d, f), BF16),
                     jax.ShapeDtypeStruct((f, d), BF16)]
    return pl.pallas_call(
        kern,
        grid=(m // tm, nj),
        in_specs=in_specs,
        out_specs=[pl.BlockSpec((tm, d), lambda i, j: (i, 0))] + a_specs,
        out_shape=[jax.ShapeDtypeStruct((m, d), F32)] + a_shapes,
        scratch_shapes=scratch,
        compiler_params=_cparams("arbitrary", "arbitrary"),
        name="ffn",
    )(*args)


def kernel(x_prompt, x_sample, c_prompt, c_sample, state_hgrn, state_mlstm_C, state_mlstm_n, state_mlstm_m,
           state_conv, ada_w, ada_b, w_in, hg_lb_logits, hg_norm_w, ml_i_bias, ml_f_bias, ml_norm_w, w_out,
           conv_w, conv_b, w_up, w_down, final_norm_w):
    n_p, seq, d = x_prompt.shape
    n_s, dec_seq, _ = x_sample.shape
    depth, _, hg_heads, hg_k, hg_v = state_hgrn.shape
    _, _, ml_heads, ml_qk, ml_v = state_mlstm_C.shape
    f = w_down.shape[1]
    assert depth == 1 and dec_seq == GROUP and hg_k == hg_v == LANES and 2 * ml_heads <= LANES
    assert hg_lb_logits.shape[0] == 2

    hg_w = hg_heads * hg_k
    gate0 = 4 * hg_w + 2 * ml_heads * ml_qk + ml_heads * ml_v
    gate1 = gate0 + 2 * ml_heads
    wt = jnp.swapaxes(w_in[0], 0, 1)
    col_mq = 4 * hg_w
    col_mk = col_mq + ml_heads * ml_qk
    col_mv = col_mk + ml_heads * ml_qk
    col_mo = col_mv + ml_heads * ml_v
    col_ga = col_mo + ml_heads * ml_v
    col_gb = col_ga + d
    assert col_gb + d == wt.shape[0] - (gate1 - gate0)
    gate_bias = jnp.pad(jnp.concatenate([ml_i_bias[0], ml_f_bias[0]]), (0, LANES - 2 * ml_heads))[None, :]

    m_s_rows = n_s * GROUP
    c_all = jnp.concatenate([c_sample, c_prompt], axis=0)
    c_all = jnp.pad(c_all, ((0, (-c_all.shape[0]) % SUBLANES), (0, 0)))
    mod_s = _mod_call(c_all, n_s, ada_w[0], ada_b)
    mod_p = mod_s[m_s_rows:m_s_rows + n_p][:, None, :]

    w_out_b = w_out[0].astype(BF16)
    hg_nw = hg_norm_w
    ml_nw = ml_norm_w
    fw = final_norm_w[None, :]
    tm_o = 256

    xs = x_sample.reshape(n_s * GROUP, d)
    proj_s, gates_s, *wt_b = _inproj_call(xs, mod_s, wt, gate0, gate1 - gate0, m_s_rows, 512, 1)
    ya_s, hg_s = _hgrn_s_call(proj_s, hg_lb_logits, hg_nw, state_hgrn[0], hg_heads, hg_k, 0, col_ga // hg_k, 64)
    m_rows = jnp.pad(jnp.repeat(state_mlstm_m[0], GROUP, axis=0), ((0, 0), (0, LANES - ml_heads)))
    n_hb = jnp.repeat(jnp.transpose(state_mlstm_n[0], (1, 0, 2)), GROUP, axis=1)
    mg_s, c_s, nn_s, m_s = _mlstm_s_call(proj_s, ya_s, gates_s, gate_bias, m_rows, ml_nw, state_mlstm_C[0], n_hb,
                                         ml_heads, ml_qk, ml_v, col_mq, col_mk, col_mv, col_mo, col_gb, 64)
    x1_s = _outproj_call(mg_s, xs, mod_s, w_out_b, tm_o, 1)
    p2 = jnp.pad(state_conv[0], ((0, 0), (0, GROUP - (CONV_W - 1)), (0, 0))).reshape(m_s_rows, f)
    y_s, cv2, cv3, wa_b, wg_b, wd_b = _ffn_call(x1_s, mod_s, w_up[0], w_down[0], conv_w[0], conv_b, fw,
                                                m_s_rows, 256, 1, p2)
    cv_s = jnp.stack([cv2, cv3], axis=1)

    xp = x_prompt.reshape(n_p * seq, d)
    tm_p = 1024
    proj_p, gates_p = _inproj_call(xp, mod_p, tuple(wt_b), gate0, gate1 - gate0, tm_p, 2048, seq // tm_p)
    ya_p, hg_p = _hgrn_p_call(proj_p, hg_lb_logits, hg_nw, n_p, seq, hg_heads, hg_k, 0, col_ga // hg_k, 1024, 2)
    mg_p, c_p, nn_p, m_p = _mlstm_p_call(proj_p, ya_p, gates_p, gate_bias, ml_nw, n_p, seq, ml_heads, ml_qk, ml_v,
                                         col_mq, col_mk, col_mv, col_mo, col_gb)
    x1_p = _outproj_call(mg_p, xp, mod_p, w_out_b, tm_o, seq // tm_o)
    tm_f = 512
    tiles = seq // tm_f
    y_p, atail = _ffn_call(x1_p, mod_p, (wa_b, wg_b), wd_b, conv_w[0], conv_b, fw, tm_f, 512, tiles)
    cv_p = atail.reshape(n_p, tiles, SUBLANES, f)[:, tiles - 1, SUBLANES - (CONV_W - 1):]

    return (y_p.reshape(n_p, seq, d), y_s.reshape(n_s, GROUP, d),
            hg_p[None], hg_s[None],
            c_p[None], c_s[None],
            nn_p.reshape(1, n_p, ml_heads, ml_qk), jnp.transpose(nn_s[:, GROUP - 1::GROUP], (1, 0, 2))[None],
            m_p[:, :, 0, 0][None], jnp.transpose(m_s[:, ::GROUP, 0])[None],
            cv_p[None], cv_s[None])
```

```python
import functools

import numpy as np
import jax
import jax.numpy as jnp
from jax import lax
from jax.experimental import pallas as pl
from jax.experimental.pallas import tpu as pltpu

F32 = jnp.float32
BF16 = jnp.bfloat16

EPS = 1e-6
CONV_W = 3
LANES = 128
SUBLANES = 8
VMEM_LIMIT_BYTES = 58 * 1024 * 1024

MXU_DEPTH = 256
HG_CHUNK = 64
HG_SAFE_EXP = 60.0
HG_LEVELS = (32, 16, 8, 4, 2, 1)
ML_CHUNK = 128
ML_HEADS_INTERLEAVED = 8
GROUP = 4


def _cparams(*sem):
    return pltpu.CompilerParams(dimension_semantics=sem, vmem_limit_bytes=VMEM_LIMIT_BYTES)


def _dot(a, b):
    return jnp.dot(a, b, preferred_element_type=F32)


def _dot_nt(a, b):
    return lax.dot_general(a, b, (((1,), (1,)), ((), ())), preferred_element_type=F32)


def _dot_tn(a, b):
    return lax.dot_general(a, b, (((0,), (0,)), ((), ())), preferred_element_type=F32)


def _split3(x):
    x1 = x.astype(BF16)
    r1 = x - x1.astype(F32)
    x2 = r1.astype(BF16)
    x3 = (r1 - x2.astype(F32)).astype(BF16)
    return x1, x2, x3


def _exact_left_mul(w, x):
    x1, x2, x3 = _split3(x)
    return _dot(w, x1) + _dot(w, x2) + _dot(w, x3)


def _exact_left_mul3(w3, x):
    return _dot(w3, jnp.concatenate(_split3(x), axis=0))


def _rows(ref, r0, n):
    if ref.shape[0] == 1:
        return ref[...]
    return ref[pl.ds(r0, n), :]


def _norm_mod(x, sc, sh):
    ms = jnp.mean(x * x, axis=-1, keepdims=True)
    return x * lax.rsqrt(ms + EPS) * (1.0 + sc) + sh


def _head_norm(o, w):
    return o * lax.rsqrt(jnp.mean(o * o, axis=-1, keepdims=True) + EPS) * w


def _lower_bound(lg):
    l0, l1 = lg[0:1, :], lg[1:2, :]
    m = jnp.maximum(l0, l1)
    e0, e1 = jnp.exp(l0 - m), jnp.exp(l1 - m)
    return e0 / (e0 + e1)


def _group_last(x, t, up):
    return jnp.where(t == 3, x, jnp.where(t == 2, up(x, 1), jnp.where(t == 1, up(x, 2), up(x, 3))))


def _mod_kernel(c_ref, w_ref, b_ref, o_ref, rep_s, *, n_rep):
    c = c_ref[...]
    s = (c * jax.nn.sigmoid(c)).astype(BF16)
    res = _dot(s, w_ref[...].astype(BF16)) + b_ref[...]
    for k in range(res.shape[1] // LANES):
        slab = res[:, k * LANES:(k + 1) * LANES]
        for t in range(GROUP):
            rep_s[k, pl.ds(t, n_rep, stride=GROUP), :] = slab[:n_rep]
        rep_s[k, n_rep * GROUP:, :] = slab[n_rep:]
    for k in range(res.shape[1] // LANES):
        o_ref[:, k * LANES:(k + 1) * LANES] = rep_s[k]


def _mod_call(c_all, n_rep, ada_w, ada_b):
    mp, d = c_all.shape
    n = ada_w.shape[1]
    tn = 512
    mo = n_rep * GROUP + (mp - n_rep)
    return pl.pallas_call(
        functools.partial(_mod_kernel, n_rep=n_rep),
        grid=(n // tn,),
        in_specs=[pl.BlockSpec((mp, d), lambda j: (0, 0)),
                  pl.BlockSpec((d, tn), lambda j: (0, j)),
                  pl.BlockSpec((1, tn), lambda j: (0, j))],
        out_specs=pl.BlockSpec((mo, tn), lambda j: (0, j)),
        out_shape=jax.ShapeDtypeStruct((mo, n), F32),
        scratch_shapes=[pltpu.VMEM((tn // LANES, mo, LANES), F32)],
        compiler_params=_cparams("arbitrary"),
        name="mod",
    )(c_all, ada_w, ada_b)


def _inproj_kernel(*refs, tm, rc, na, emit_bf16):
    if emit_bf16:
        x_ref, sh_ref, sc_ref, wa_ref, wb_ref, wg_ref, o_ref, og_ref, wo_ref, wgo_ref, h_ref = refs
    else:
        x_ref, sh_ref, sc_ref, wa_ref, wg_ref, o_ref, og_ref, h_ref = refs
        wb_ref = wo_ref = wgo_ref = None
    j = pl.program_id(1)

    def weight(src, dst):
        w = src[...]
        if emit_bf16:
            w = w.astype(BF16)
            dst[...] = w
        return w

    @pl.when(j == 0)
    def _():
        wg = weight(wg_ref, wgo_ref)

        def body(r, carry):
            r0 = pl.multiple_of(r * rc, rc)
            h = _norm_mod(x_ref[pl.ds(r0, rc), :], _rows(sc_ref, r0, rc), _rows(sh_ref, r0, rc))
            hb = h.astype(BF16)
            h_ref[pl.ds(r0, rc), :] = hb
            gg = _dot_nt(hb, wg)
            og_ref[pl.ds(r0, rc), :] = jnp.concatenate(
                [gg, jnp.zeros((rc, og_ref.shape[1] - gg.shape[1]), F32)], axis=1)
            return carry
        lax.fori_loop(0, tm // rc, body, 0)

    if emit_bf16:
        @pl.when(j < na)
        def _():
            o_ref[...] = _dot_nt(h_ref[...], weight(wa_ref, wo_ref))

        @pl.when(j >= na)
        def _():
            o_ref[...] = _dot_nt(h_ref[...], weight(wb_ref, wo_ref))
    else:
        o_ref[...] = _dot_nt(h_ref[...], wa_ref[...])


def _mod_spec(mod, tm, col, tiles_per_seq):
    d = mod.shape[-1] // 6
    if mod.ndim == 3:
        return pl.BlockSpec((None, 1, d), lambda i, *_: (i // tiles_per_seq, 0, col))
    return pl.BlockSpec((tm, d), lambda i, *_: (i, col))


def _inproj_call(x, mod, w, gate0, n_gate, tm, tn, tiles_per_seq):
    m, d = x.shape
    emit_bf16 = not isinstance(w, tuple)
    gate1 = gate0 + n_gate
    n = (w.shape[0] - n_gate) if emit_bf16 else w[0].shape[0]
    assert n_gate == 2 * SUBLANES and gate0 % tn == 0 and n % tn == 0
    na = gate0 // tn
    tile = pl.BlockSpec((tn, d), lambda i, j: (j, 0))
    seg_g = pl.BlockSpec((n_gate, d), lambda i, j: (0, 0))
    out_specs = [pl.BlockSpec((tm, tn), lambda i, j: (i, j)), pl.BlockSpec((tm, LANES), lambda i, j: (i, 0))]
    out_shape = [jax.ShapeDtypeStruct((m, n), F32), jax.ShapeDtypeStruct((m, LANES), F32)]
    if emit_bf16:
        assert m == tm
        w_specs = [pl.BlockSpec((tn, d), lambda i, j: (jnp.minimum(j, na - 1), 0)),
                   pl.BlockSpec((pl.Element(tn), pl.Element(d)),
                                lambda i, j: (pl.multiple_of(gate1 + jnp.maximum(j - na, 0) * tn, SUBLANES), 0)),
                   pl.BlockSpec((n_gate, d), lambda i, j: (gate0 // n_gate, 0))]
        w_args = [w, w, w]
        out_specs += [tile, seg_g]
        out_shape += [jax.ShapeDtypeStruct((n, d), BF16), jax.ShapeDtypeStruct((n_gate, d), BF16)]
    else:
        w_specs = [tile, seg_g]
        w_args = list(w)
    kern = functools.partial(_inproj_kernel, tm=tm, rc=256, na=na, emit_bf16=emit_bf16)
    return pl.pallas_call(
        kern,
        grid=(m // tm, n // tn),
        in_specs=[pl.BlockSpec((tm, d), lambda i, j: (i, 0)),
                  _mod_spec(mod, tm, 0, tiles_per_seq),
                  _mod_spec(mod, tm, 1, tiles_per_seq)] + w_specs,
        out_specs=out_specs,
        out_shape=out_shape,
        scratch_shapes=[pltpu.VMEM((tm, d), BF16)],
        compiler_params=_cparams("arbitrary", "arbitrary"),
        name="inproj",
    )(x, mod, mod, *w_args)


def _hgrn_constants():
    L = HG_CHUNK
    tri = np.tril(np.ones((L, L), np.float32))
    t = np.arange(L)
    blocks = []
    masks = []
    for h in HG_LEVELS:
        mid = (t // (2 * h)) * (2 * h) + h - 1
        blocks.append(tri - tri[mid])
        masks.append((t[:, None] // (2 * h) == t[None, :] // (2 * h)).astype(np.float32))
    blocks.append(tri[L - 1][None, :] - tri)
    masks.append(np.eye(L, dtype=np.float32))
    return jnp.asarray(np.concatenate(blocks, 0), BF16), jnp.asarray(np.stack(masks, 0), F32)


def _hgrn_gates(hq, z, lb, kdim):
    omlb = 1.0 - lb
    sz = jax.nn.sigmoid(z)
    logf = jnp.log(lb + omlb * sz)
    kk = omlb * (1.0 - sz)
    q = hq * jax.nn.sigmoid(hq) * (kdim ** -0.5)
    return q, kk, logf


def _hgrn_p_kernel(hq_ref, hf_ref, hi_ref, hg_ref, ga_ref, lg_ref, nw_ref, trib_ref, wall_ref, mask_ref,
                   ya_ref, so_ref, st_ref, *, n_chunks, kdim):
    L = HG_CHUNK
    c = pl.program_id(2)
    hps = hq_ref.shape[1] // kdim
    tb = trib_ref.shape[0]
    heads_chunks = [(hd, ci) for ci in range(n_chunks) for hd in range(hps)]
    rows = lambda ci: slice(ci * L, (ci + 1) * L)
    lanes = lambda hd: slice(hd * kdim, (hd + 1) * kdim)

    @pl.when(c == 0)
    def _():
        st_ref[...] = jnp.zeros_like(st_ref)

    lb = _lower_bound(lg_ref[...])
    nw = nw_ref[...]
    q, kk, logf = _hgrn_gates(hq_ref[...], hf_ref[...], lb, kdim)
    trib = trib_ref[...]
    b = jnp.concatenate([_exact_left_mul3(trib, logf[r:r + tb]) for r in range(0, n_chunks * L, tb)], axis=0)
    worst = -jnp.min(jnp.concatenate([b[r:r + 1, :] for r in range(L - 1, n_chunks * L, L)], axis=0))

    def finish(hd, ci, o):
        hg = hg_ref[rows(ci), lanes(hd)]
        ya = _head_norm(o, nw[:, lanes(hd)]) * (hg * jax.nn.sigmoid(hg))
        ya_ref[rows(ci), lanes(hd)] = jax.nn.sigmoid(ga_ref[rows(ci), lanes(hd)]) * ya

    @pl.when(worst <= HG_SAFE_EXP)
    def _():
        eb = jnp.exp(b)
        qt = (q * eb).astype(BF16)
        kn = kk * (1.0 / eb)
        knb = kn.astype(BF16)
        row = lax.broadcasted_iota(jnp.int32, (L, L), 0)
        colid = lax.broadcasted_iota(jnp.int32, (L, L), 1)
        causal = colid <= row
        o_intra, incr, dec = {}, {}, {}
        for hd, ci in heads_chunks:
            vb = hi_ref[rows(ci), lanes(hd)].astype(BF16)
            a = jnp.where(causal, _dot_nt(qt[rows(ci), lanes(hd)], knb[rows(ci), lanes(hd)]), 0.0)
            o_intra[hd, ci] = _dot(a.astype(BF16), vb)
            dec[hd, ci] = eb[(ci + 1) * L - 1:(ci + 1) * L, lanes(hd)]
            incr[hd, ci] = _dot_tn(vb, (kn[rows(ci), lanes(hd)] * dec[hd, ci]).astype(BF16))
        sts = {(hd, 0): st_ref[hd] for hd in range(hps)}
        for hd, ci in heads_chunks:
            sts[hd, ci + 1] = sts[hd, ci] * dec[hd, ci] + incr[hd, ci]
        for hd in range(hps):
            st_ref[hd] = sts[hd, n_chunks]
        for hd, ci in heads_chunks:
            finish(hd, ci, o_intra[hd, ci] + _dot_nt(qt[rows(ci), lanes(hd)], sts[hd, ci].astype(BF16)))

    @pl.when(jnp.logical_not(worst <= HG_SAFE_EXP))
    def _():
        wall = wall_ref[...]
        rowid = lax.broadcasted_iota(jnp.int32, (L, kdim), 0)
        n_lv = len(HG_LEVELS)
        for hd, ci in heads_chunks:
            qc, kc, bc = q[rows(ci), lanes(hd)], kk[rows(ci), lanes(hd)], b[rows(ci), lanes(hd)]
            vb = hi_ref[rows(ci), lanes(hd)].astype(BF16)
            d = _exact_left_mul(wall, logf[rows(ci), lanes(hd)])
            a = mask_ref[n_lv] * _dot_nt(qc.astype(BF16), kc.astype(BF16))
            for li, h in enumerate(HG_LEVELS):
                e = jnp.exp(-jnp.abs(d[li * L:(li + 1) * L]))
                second = (rowid & h) != 0
                p = jnp.where(second, qc, kc) * e
                qh = jnp.where(second, p, 0.0).astype(BF16)
                kh = jnp.where(second, 0.0, p).astype(BF16)
                a = a + mask_ref[li] * _dot_nt(qh, kh)
            qt = (qc * jnp.exp(bc)).astype(BF16)
            kt = (kc * jnp.exp(d[n_lv * L:(n_lv + 1) * L])).astype(BF16)
            st = st_ref[hd]
            finish(hd, ci, _dot(a.astype(BF16), vb) + _dot_nt(qt, st.astype(BF16)))
            st_ref[hd] = st * jnp.exp(bc[L - 1:L, :]) + _dot_tn(vb, kt)

    @pl.when(c == pl.num_programs(2) - 1)
    def _():
        for hd in range(hps):
            so_ref[hd] = st_ref[hd].T


def _hgrn_p_call(proj, lb_logits, norm_w, n_seq, seq_len, heads, kdim, col0, col_ga, rows_per_step, hps):
    wall, masks = _hgrn_constants()
    per_mat = min(rows_per_step, MXU_DEPTH) // HG_CHUNK
    trib = np.kron(np.eye(per_mat, dtype=np.float32), np.tril(np.ones((HG_CHUNK, HG_CHUNK), np.float32)))
    trib = jnp.asarray(np.concatenate([trib, trib, trib], axis=1), BF16)
    nc = seq_len // rows_per_step
    assert heads % hps == 0 and col0 % hps == 0 and col_ga % hps == 0
    kern = functools.partial(_hgrn_p_kernel, n_chunks=rows_per_step // HG_CHUNK, kdim=kdim)
    wide = hps * kdim

    def col(first):
        return pl.BlockSpec((rows_per_step, wide), lambda b, h, c: (b * nc + c, first // hps + h))

    return pl.pallas_call(
        kern,
        grid=(n_seq, heads // hps, nc),
        in_specs=[col(col0), col(col0 + heads), col(col0 + 2 * heads), col(col0 + 3 * heads), col(col_ga),
                  pl.BlockSpec((2, wide), lambda b, h, c: (0, h)),
                  pl.BlockSpec((1, wide), lambda b, h, c: (0, h)),
                  pl.BlockSpec(trib.shape, lambda b, h, c: (0, 0)),
                  pl.BlockSpec(wall.shape, lambda b, h, c: (0, 0)),
                  pl.BlockSpec(masks.shape, lambda b, h, c: (0, 0, 0))],
        out_specs=[pl.BlockSpec((rows_per_step, wide), lambda b, h, c: (b * nc + c, h)),
                   pl.BlockSpec((None, hps, kdim, kdim), lambda b, h, c: (b, h, 0, 0))],
        out_shape=[jax.ShapeDtypeStruct((n_seq * seq_len, heads * kdim), F32),
                   jax.ShapeDtypeStruct((n_seq, heads, kdim, kdim), F32)],
        scratch_shapes=[pltpu.VMEM((hps, kdim, kdim), F32)],
        compiler_params=_cparams("arbitrary", "arbitrary", "arbitrary"),
        name="hgrn_prompt",
    )(proj, proj, proj, proj, proj, lb_logits, norm_w, trib, wall, masks)


def _hgrn_s_kernel(hq_ref, hf_ref, hi_ref, hg_ref, ga_ref, lg_ref, nw_ref, s_ref, ya_ref, so_ref, *, n_pairs):
    kdim = hq_ref.shape[1]
    rows = n_pairs * SUBLANES
    lb = _lower_bound(lg_ref[...])
    rowid = lax.broadcasted_iota(jnp.int32, (rows, kdim), 0)
    t = rowid & (GROUP - 1)
    first = (rowid & GROUP) == 0
    first8 = lax.broadcasted_iota(jnp.int32, (SUBLANES, kdim), 0) < GROUP
    down = lambda y, j: pltpu.roll(y, j, 0)
    up = lambda y, j: pltpu.roll(y, rows - j, 0)

    q, kk, logf = _hgrn_gates(hq_ref[...], hf_ref[...], lb, kdim)
    v = hi_ref[...]
    b = logf
    for dlt in range(1, GROUP):
        b = b + jnp.where(t >= dlt, down(logf, dlt), 0.0)
    o = jnp.sum(q * kk, axis=1, keepdims=True) * v
    for dlt in range(1, GROUP):
        x = q * down(kk, dlt) * jnp.exp(b - down(b, dlt))
        a = jnp.sum(jnp.where(t >= dlt, x, 0.0), axis=1, keepdims=True)
        o = o + a * down(v, dlt)
    b_last = _group_last(b, t, up)
    qt = q * jnp.exp(b)
    kt = kk * jnp.exp(b_last - b)
    d1, d2, d3 = _split3(jnp.exp(b_last))
    swap = lambda y: jnp.where(first, up(y.astype(F32), GROUP), down(y.astype(F32), GROUP))
    dsplit = jnp.where(t == 0, swap(d1), jnp.where(t == 1, swap(d2), jnp.where(t == 2, swap(d3), 0.0)))
    ones = jnp.where(t <= 2, 1.0, 0.0)
    lhs = (jnp.where(first, kt, dsplit), jnp.where(first, dsplit, kt))
    rhs = (jnp.concatenate([jnp.where(first, v, 0.0), jnp.where(first, 0.0, ones)], axis=1),
           jnp.concatenate([jnp.where(first, 0.0, v), jnp.where(first, ones, 0.0)], axis=1))
    o_inter = []
    for p in range(n_pairs):
        sl = slice(p * SUBLANES, (p + 1) * SUBLANES)
        qb = qt[sl].astype(BF16)
        parts = []
        for half in range(2):
            s0 = s_ref[2 * p + half]
            upd = _dot_tn(lhs[half][sl].astype(BF16), rhs[half][sl].astype(BF16))
            so_ref[2 * p + half] = s0 * upd[:, kdim:] + upd[:, :kdim]
            parts.append(_dot(qb, s0.astype(BF16)))
        o_inter.append(jnp.where(first8, parts[0], parts[1]))
    o = o + jnp.concatenate(o_inter, axis=0)
    hg = hg_ref[...]
    ya_ref[...] = jax.nn.sigmoid(ga_ref[...]) * (_head_norm(o, nw_ref[...]) * (hg * jax.nn.sigmoid(hg)))


def _hgrn_s_call(proj, lb_logits, norm_w, state, heads, kdim, col0, col_ga, seqs_per_step):
    n_seq = state.shape[0]
    rows = seqs_per_step * GROUP
    kern = functools.partial(_hgrn_s_kernel, n_pairs=seqs_per_step // 2)

    def col(k):
        return pl.BlockSpec((rows, kdim), lambda i, h: (i, col0 + k * heads + h))

    st_spec = pl.BlockSpec((seqs_per_step, None, kdim, kdim), lambda i, h: (i, h, 0, 0))
    return pl.pallas_call(
        kern,
        grid=(n_seq // seqs_per_step, heads),
        in_specs=[col(0), col(1), col(2), col(3),
                  pl.BlockSpec((rows, kdim), lambda i, h: (i, col_ga + h)),
                  pl.BlockSpec((2, kdim), lambda i, h: (0, h)),
                  pl.BlockSpec((1, kdim), lambda i, h: (0, h)),
                  st_spec],
        out_specs=[pl.BlockSpec((rows, kdim), lambda i, h: (i, h)), st_spec],
        out_shape=[jax.ShapeDtypeStruct((n_seq * GROUP, heads * kdim), F32),
                   jax.ShapeDtypeStruct(state.shape, F32)],
        compiler_params=_cparams("arbitrary", "arbitrary"),
        name="hgrn_sample",
    )(proj, proj, proj, proj, proj, lb_logits, norm_w, state)


def _lane_pick(x, lane, idx):
    return jnp.broadcast_to(jnp.sum(jnp.where(lane == idx, x, 0.0), axis=1, keepdims=True), x.shape)


def _interleave(chains):
    live = list(chains)
    while live:
        still = []
        for g in live:
            try:
                next(g)
                still.append(g)
            except StopIteration:
                pass
        live = still


def _mlstm_p_kernel(q_ref, k_ref, v_ref, og_ref, gb_ref, ya_ref, g_ref, bias_ref, nw_ref, tri_ref,
                    mg_ref, co_ref, no_ref, mo_ref, c_s, n_s, m_s, *, heads, qk, vd):
    L = ML_CHUNK
    c = pl.program_id(1)

    @pl.when(c == 0)
    def _():
        c_s[...] = jnp.zeros_like(c_s)
        n_s[...] = jnp.zeros_like(n_s)
        m_s[...] = jnp.zeros_like(m_s)

    lane = lax.broadcasted_iota(jnp.int32, (L, LANES), 1)
    row = lax.broadcasted_iota(jnp.int32, (L, LANES), 0)

    def head(hd, r, g, b_all):
        qs = slice(hd * qk, (hd + 1) * qk)
        vs = slice(hd * vd, (hd + 1) * vd)
        ig = _lane_pick(g, lane, hd)
        b = _lane_pick(b_all, lane, heads + hd)
        m_prev = m_s[hd]
        gs = (ig - b).T
        yield
        dm = jnp.where(lane <= row, b + gs, -jnp.inf)
        mt = jnp.maximum(b + m_prev, jnp.max(dm, axis=1, keepdims=True))
        inter = jnp.exp(b + m_prev - mt)
        q = q_ref[r, qs] * (qk ** -0.5)
        k = k_ref[r, qs]
        vb = v_ref[r, vs].astype(BF16)
        qb = q.astype(BF16)
        sc = _dot_nt(qb, k.astype(BF16)) * jnp.exp(dm - mt)
        yield
        c0 = c_s[hd]
        n0 = n_s[hd]
        num = inter[:, 0:1] * _dot(qb, c0.astype(BF16)) + _dot(sc.astype(BF16), vb)
        den = inter[:, 0:1] * jnp.sum(q * n0, axis=1, keepdims=True) + jnp.sum(sc, axis=1, keepdims=True)
        yield
        hh = num / jnp.maximum(jnp.abs(den), jnp.exp(-mt[:, 0:1]))
        yb = _head_norm(hh, nw_ref[:, vs]) * jax.nn.sigmoid(og_ref[r, vs])
        mg_ref[r, vs] = (ya_ref[r, vs] + jax.nn.sigmoid(gb_ref[r, vs]) * yb).astype(mg_ref.dtype)
        yield
        m_last = mt[L - 1:L, :]
        b_last = b[L - 1:L, :]
        dec = jnp.exp(b_last + m_prev - m_last)
        kw = jnp.exp(b_last - b + ig - m_last) * k
        c_s[hd] = jnp.concatenate([dec, dec], axis=1) * c0 + _dot_tn(kw.astype(BF16), vb)
        n_s[hd] = dec * n0 + jnp.sum(kw, axis=0, keepdims=True)
        m_s[hd] = m_last

    for r0 in range(0, q_ref.shape[0], L):
        r = slice(r0, r0 + L)
        g = g_ref[r, :] + bias_ref[...]
        b_all = _exact_left_mul(tri_ref[...], jax.nn.log_sigmoid(g))
        for h0 in range(0, heads, ML_HEADS_INTERLEAVED):
            _interleave([head(hd, r, g, b_all) for hd in range(h0, min(h0 + ML_HEADS_INTERLEAVED, heads))])

    @pl.when(c == pl.num_programs(1) - 1)
    def _():
        co_ref[...] = c_s[...]
        no_ref[...] = n_s[...]
        mo_ref[...] = m_s[...]


def _mlstm_p_call(proj, ya, gates, bias, norm_w, n_seq, seq_len, heads, qk, vd, colq, colk, colv, colo, colg,
                  rows_per_step):
    assert qk == ML_CHUNK and qk == LANES and rows_per_step % ML_CHUNK == 0
    L = rows_per_step
    nc = seq_len // L
    qw, vw = heads * qk, heads * vd
    assert colq % qw == 0 and colk % qw == 0 and colv % vw == 0 and colo % vw == 0 and colg % vw == 0
    tri = jnp.asarray(np.tril(np.ones((ML_CHUNK, ML_CHUNK), np.float32)), BF16)
    kern = functools.partial(_mlstm_p_kernel, heads=heads, qk=qk, vd=vd)
    return pl.pallas_call(
        kern,
        grid=(n_seq, nc),
        in_specs=[pl.BlockSpec((L, qw), lambda b, c: (b * nc + c, colq // qw)),
                  pl.BlockSpec((L, qw), lambda b, c: (b * nc + c, colk // qw)),
                  pl.BlockSpec((L, vw), lambda b, c: (b * nc + c, colv // vw)),
                  pl.BlockSpec((L, vw), lambda b, c: (b * nc + c, colo // vw)),
                  pl.BlockSpec((L, vw), lambda b, c: (b * nc + c, colg // vw)),
                  pl.BlockSpec((L, vw), lambda b, c: (b * nc + c, 0)),
                  pl.BlockSpec((L, LANES), lambda b, c: (b * nc + c, 0)),
                  pl.BlockSpec((1, LANES), lambda b, c: (0, 0)),
                  pl.BlockSpec((1, vw), lambda b, c: (0, 0)),
                  pl.BlockSpec(tri.shape, lambda b, c: (0, 0))],
        out_specs=[pl.BlockSpec((L, vw), lambda b, c: (b * nc + c, 0)),
                   pl.BlockSpec((None, heads, qk, vd), lambda b, c: (b, 0, 0, 0)),
                   pl.BlockSpec((None, heads, 1, qk), lambda b, c: (b, 0, 0, 0)),
                   pl.BlockSpec((None, heads, 1, LANES), lambda b, c: (b, 0, 0, 0))],
        out_shape=[jax.ShapeDtypeStruct((n_seq * seq_len, vw), BF16),
                   jax.ShapeDtypeStruct((n_seq, heads, qk, vd), F32),
                   jax.ShapeDtypeStruct((n_seq, heads, 1, qk), F32),
                   jax.ShapeDtypeStruct((n_seq, heads, 1, LANES), F32)],
        scratch_shapes=[pltpu.VMEM((heads, qk, vd), F32), pltpu.VMEM((heads, 1, qk), F32),
                        pltpu.VMEM((heads, 1, LANES), F32)],
        compiler_params=_cparams("arbitrary", "arbitrary"),
        name="mlstm_prompt",
    )(proj, proj, proj, proj, proj, ya, gates, bias, norm_w, tri)


def _mlstm_s_kernel(q_ref, k_ref, v_ref, og_ref, gb_ref, ya_ref, g_ref, bias_ref, m_ref, nw_ref, c_ref, n_ref,
                    mg_ref, co_ref, no_ref, mo_ref, *, heads, n_pairs):
    h = pl.program_id(1)
    qk = q_ref.shape[1]
    rows = n_pairs * SUBLANES
    lane = lax.broadcasted_iota(jnp.int32, (rows, LANES), 1)
    rowid = lax.broadcasted_iota(jnp.int32, (rows, LANES), 0)
    t = rowid & (GROUP - 1)
    first = (rowid & GROUP) == 0
    first8 = lax.broadcasted_iota(jnp.int32, (SUBLANES, 1), 0) < GROUP
    down = lambda y, j: pltpu.roll(y, j, 0)
    up = lambda y, j: pltpu.roll(y, rows - j, 0)

    g = g_ref[...] + bias_ref[...]
    ig = _lane_pick(g, lane, h)
    lf = _lane_pick(jax.nn.log_sigmoid(g), lane, heads + h)
    m_prev = _lane_pick(m_ref[...], lane, h)
    b = lf
    for dlt in range(1, GROUP):
        b = b + jnp.where(t >= dlt, down(lf, dlt), 0.0)
    q = q_ref[...] * (qk ** -0.5)
    k = k_ref[...]
    v = v_ref[...]
    dms = [ig] + [jnp.where(t >= dlt, b - down(b, dlt) + down(ig, dlt), -jnp.inf) for dlt in range(1, GROUP)]
    mt = jnp.maximum(b + m_prev, functools.reduce(jnp.maximum, dms))
    inter = jnp.exp(b + m_prev - mt)[:, 0:1]
    num = jnp.zeros(v.shape, F32)
    den = jnp.zeros((rows, 1), F32)
    for dlt in range(GROUP):
        kd = k if dlt == 0 else down(k, dlt)
        vd_ = v if dlt == 0 else down(v, dlt)
        s = jnp.sum(q * kd, axis=1, keepdims=True) * jnp.exp(dms[dlt] - mt)[:, 0:1]
        num = num + s * vd_
        den = den + s
    m_last = _group_last(mt, t, up)
    b_last = _group_last(b, t, up)
    dec = jnp.exp(b_last + m_prev - m_last)
    kw = jnp.exp(b_last - b + ig - m_last) * k
    n0 = n_ref[...]
    no_ref[...] = dec * n0 + kw + down(kw, 1) + down(kw, 2) + down(kw, 3)
    mo_ref[...] = m_last
    den = den + inter * jnp.sum(q * n0, axis=1, keepdims=True)

    kw_half = (jnp.where(first, kw, 0.0), jnp.where(first, 0.0, kw))
    dec2 = jnp.concatenate([dec, dec], axis=1)
    num_inter = []
    for p in range(n_pairs):
        sl = slice(p * SUBLANES, (p + 1) * SUBLANES)
        qb = q[sl].astype(BF16)
        vb = v[sl].astype(BF16)
        parts = []
        for half in range(2):
            c0 = c_ref[2 * p + half]
            r = p * SUBLANES + half * GROUP
            co_ref[2 * p + half] = dec2[r:r + 1, :] * c0 + _dot_tn(kw_half[half][sl].astype(BF16), vb)
            parts.append(_dot(qb, c0.astype(BF16)))
        num_inter.append(jnp.where(first8, parts[0], parts[1]))
    num = num + inter * jnp.concatenate(num_inter, axis=0)
    hh = num / jnp.maximum(jnp.abs(den), jnp.exp(-mt)[:, 0:1])
    yb = _head_norm(hh, nw_ref[...]) * jax.nn.sigmoid(og_ref[...])
    mg_ref[...] = (ya_ref[...] + jax.nn.sigmoid(gb_ref[...]) * yb).astype(mg_ref.dtype)


def _mlstm_s_call(proj, ya, gates, bias, m_rows, norm_w, c_state, n_state, heads, qk, vd,
                  colq, colk, colv, colo, colg, seqs_per_step):
    n_seq = c_state.shape[0]
    rows = seqs_per_step * GROUP
    kern = functools.partial(_mlstm_s_kernel, heads=heads, n_pairs=seqs_per_step // 2)
    c_spec = pl.BlockSpec((seqs_per_step, None, qk, vd), lambda i, h: (i, h, 0, 0))
    n_spec = pl.BlockSpec((None, rows, qk), lambda i, h: (h, i, 0))
    return pl.pallas_call(
        kern,
        grid=(n_seq // seqs_per_step, heads),
        in_specs=[pl.BlockSpec((rows, qk), lambda i, h: (i, colq // qk + h)),
                  pl.BlockSpec((rows, qk), lambda i, h: (i, colk // qk + h)),
                  pl.BlockSpec((rows, vd), lambda i, h: (i, colv // vd + h)),
                  pl.BlockSpec((rows, vd), lambda i, h: (i, colo // vd + h)),
                  pl.BlockSpec((rows, vd), lambda i, h: (i, colg // vd + h)),
                  pl.BlockSpec((rows, vd), lambda i, h: (i, h)),
                  pl.BlockSpec((rows, LANES), lambda i, h: (i, 0)),
                  pl.BlockSpec((1, LANES), lambda i, h: (0, 0)),
                  pl.BlockSpec((rows, LANES), lambda i, h: (i, 0)),
                  pl.BlockSpec((1, vd), lambda i, h: (0, h)),
                  c_spec, n_spec],
        out_specs=[pl.BlockSpec((rows, vd), lambda i, h: (i, h)),
                   c_spec, n_spec,
                   pl.BlockSpec((None, rows, LANES), lambda i, h: (h, i, 0))],
        out_shape=[jax.ShapeDtypeStruct((n_seq * GROUP, heads * vd), BF16),
                   jax.ShapeDtypeStruct(c_state.shape, F32),
                   jax.ShapeDtypeStruct(n_state.shape, F32),
                   jax.ShapeDtypeStruct((heads, n_seq * GROUP, LANES), F32)],
        compiler_params=_cparams("arbitrary", "arbitrary"),
        name="mlstm_sample",
    )(proj, proj, proj, proj, proj, ya, gates, bias, m_rows, norm_w, c_state, n_state)


def _outproj_kernel(mg_ref, x_ref, g1_ref, w_ref, o_ref):
    o_ref[...] = x_ref[...] + g1_ref[...] * _dot(mg_ref[...], w_ref[...])


def _outproj_call(merged, x, mod, w_out, tm, tiles_per_seq):
    m, d = x.shape
    blk = pl.BlockSpec((tm, d), lambda i: (i, 0))
    return pl.pallas_call(
        _outproj_kernel,
        grid=(m // tm,),
        in_specs=[blk, blk, _mod_spec(mod, tm, 2, tiles_per_seq), pl.BlockSpec((d, d), lambda i: (0, 0))],
        out_specs=blk,
        out_shape=jax.ShapeDtypeStruct((m, d), F32),
        compiler_params=_cparams("arbitrary"),
        name="outproj",
    )(merged, x, mod, w_out)


def _ffn_kernel(*refs, tm, rc, tiles_per_seq, grouped, emit_bf16):
    refs = iter(refs)
    x_ref, sh_ref, sc_ref, g2_ref, wa_ref, wg_ref, wd_ref, cw_ref, cb_ref, fw_ref = (next(refs) for _ in range(10))
    hist_refs = (next(refs), next(refs)) if grouped else None
    y_ref = next(refs)
    if grouped:
        cv2_ref, cv3_ref = next(refs), next(refs)
    else:
        a_ref = next(refs)
    wao_ref, wgo_ref, wdo_ref = (next(refs), next(refs), next(refs)) if emit_bf16 else (None, None, None)
    h_s, acc_s = next(refs), next(refs)
    a_s = carry_s = next(refs)
    i = pl.program_id(0)
    j = pl.program_id(1)

    def weight(src, dst):
        w = src[...]
        if emit_bf16:
            w = w.astype(BF16)
            dst[...] = w
        return w

    @pl.when(j == 0)
    def _():
        def body(r, carry):
            r0 = pl.multiple_of(r * rc, rc)
            h = _norm_mod(x_ref[pl.ds(r0, rc), :], _rows(sc_ref, r0, rc), _rows(sh_ref, r0, rc))
            h_s[pl.ds(r0, rc), :] = h.astype(BF16)
            return carry
        lax.fori_loop(0, tm // rc, body, 0)
        acc_s[...] = jnp.zeros_like(acc_s)

    hb = h_s[...]
    a = _dot(hb, weight(wa_ref, wao_ref))
    g = _dot(hb, weight(wg_ref, wgo_ref))
    rowid = lax.broadcasted_iota(jnp.int32, a.shape, 0)
    r1 = pltpu.roll(a, 1, 0)
    r2 = pltpu.roll(a, 2, 0)
    if grouped:
        t = rowid & (GROUP - 1)
        slabs = []
        for s in range(a.shape[1] // LANES):
            a_s[s] = jnp.zeros((tm, LANES), F32)
            for u, hist in enumerate(hist_refs):
                a_s[s, pl.ds(u, tm // GROUP, stride=GROUP), :] = hist[:, s * LANES:(s + 1) * LANES]
            slabs.append(a_s[s])
        p2 = jnp.concatenate(slabs, axis=1)
        prev1 = jnp.where(t == 0, pltpu.roll(p2, tm - 1, 0), r1)
        prev2 = jnp.where(t <= 1, p2, r2)
        for s in range(a.shape[1] // LANES):
            a_s[s] = a[:, s * LANES:(s + 1) * LANES]
        for s in range(a.shape[1] // LANES):
            cv2_ref[:, s * LANES:(s + 1) * LANES] = a_s[s, pl.ds(GROUP - 2, tm // GROUP, stride=GROUP), :]
            cv3_ref[:, s * LANES:(s + 1) * LANES] = a_s[s, pl.ds(GROUP - 1, tm // GROUP, stride=GROUP), :]
    else:
        tail = a[tm - SUBLANES:tm, :]
        car = jnp.where(i % tiles_per_seq == 0, 0.0, carry_s[j])
        c1 = car[SUBLANES - 1:SUBLANES, :]
        c2 = car[SUBLANES - 2:SUBLANES - 1, :]
        prev1 = jnp.where(rowid == 0, c1, r1)
        prev2 = jnp.where(rowid == 0, c2, jnp.where(rowid == 1, c1, r2))
        carry_s[j] = tail
        a_ref[...] = tail
    cw = cw_ref[...]
    ac = cw[0:1, :] * prev2 + cw[1:2, :] * prev1 + cw[2:3, :] * a + cb_ref[...]
    yv = 0.5 * ac * (1.0 + lax.erf(ac * (2.0 ** -0.5))) * g
    acc_s[...] += _dot(yv.astype(BF16), weight(wd_ref, wdo_ref))

    @pl.when(j == pl.num_programs(1) - 1)
    def _():
        def body(r, carry):
            r0 = pl.multiple_of(r * rc, rc)
            x2 = x_ref[pl.ds(r0, rc), :] + _rows(g2_ref, r0, rc) * acc_s[pl.ds(r0, rc), :]
            ms = jnp.mean(x2 * x2, axis=-1, keepdims=True)
            y_ref[pl.ds(r0, rc), :] = x2 * lax.rsqrt(ms + EPS) * fw_ref[...]
            return carry
        lax.fori_loop(0, tm // rc, body, 0)


def _ffn_call(x, mod, w_up, w_down, conv_w, conv_b, final_w, tm, tf, tiles_per_seq, hist=None):
    m, d = x.shape
    f = w_down.shape[0]
    nj = f // tf
    grouped = hist is not None
    emit_bf16 = not isinstance(w_up, tuple)
    kern = functools.partial(_ffn_kernel, tm=tm, rc=min(tm, 256), tiles_per_seq=tiles_per_seq, grouped=grouped,
                             emit_bf16=emit_bf16)
    half_spec = pl.BlockSpec((d, tf), lambda i, j: (0, j))
    down_spec = pl.BlockSpec((tf, d), lambda i, j: (j, 0))
    if emit_bf16:
        assert m == tm
        up_specs = [half_spec, pl.BlockSpec((d, tf), lambda i, j: (0, nj + j))]
        up_args = [w_up, w_up]
    else:
        up_specs = [half_spec, half_spec]
        up_args = list(w_up)
    in_specs = [pl.BlockSpec((tm, d), lambda i, j: (i, 0)),
                _mod_spec(mod, tm, 3, tiles_per_seq),
                _mod_spec(mod, tm, 4, tiles_per_seq),
                _mod_spec(mod, tm, 5, tiles_per_seq)] + up_specs + [
                down_spec,
                pl.BlockSpec((CONV_W, tf), lambda i, j: (0, j)),
                pl.BlockSpec((1, tf), lambda i, j: (0, j)),
                pl.BlockSpec((1, d), lambda i, j: (0, 0))]
    args = [x, mod, mod, mod] + up_args + [w_down, conv_w, conv_b, final_w]
    scratch = [pltpu.VMEM((tm, d), BF16), pltpu.VMEM((tm, d), F32)]
    if grouped:
        in_specs += [pl.BlockSpec((tm // GROUP, tf), lambda i, j: (i, j))] * 2
        args += list(hist)
        scratch.append(pltpu.VMEM((tf // LANES, tm, LANES), F32))
        a_specs = [pl.BlockSpec((tm // GROUP, tf), lambda i, j: (i, j))] * 2
        a_shapes = [jax.ShapeDtypeStruct((m // GROUP, f), F32)] * 2
    else:
        scratch.append(pltpu.VMEM((nj, SUBLANES, tf), F32))
        a_specs = [pl.BlockSpec((SUBLANES, tf), lambda i, j: (i, j))]
        a_shapes = [jax.ShapeDtypeStruct((m // tm * SUBLANES, f), F32)]
    if emit_bf16:
        a_specs += [half_spec, half_spec, down_spec]
        a_shapes += [jax.ShapeDtypeStruct((d, f), BF16), jax.ShapeDtypeStruct((d, f), BF16),
                     jax.ShapeDtypeStruct((f, d), BF16)]
    return pl.pallas_call(
        kern,
        grid=(m // tm, nj),
        in_specs=in_specs,
        out_specs=[pl.BlockSpec((tm, d), lambda i, j: (i, 0))] + a_specs,
        out_shape=[jax.ShapeDtypeStruct((m, d), F32)] + a_shapes,
        scratch_shapes=scratch,
        compiler_params=_cparams("arbitrary", "arbitrary"),
        name="ffn",
    )(*args)


def kernel(x_prompt, x_sample, c_prompt, c_sample, state_hgrn, state_mlstm_C, state_mlstm_n, state_mlstm_m,
           state_conv, ada_w, ada_b, w_in, hg_lb_logits, hg_norm_w, ml_i_bias, ml_f_bias, ml_norm_w, w_out,
           conv_w, conv_b, w_up, w_down, final_norm_w):
    n_p, seq, d = x_prompt.shape
    n_s, dec_seq, _ = x_sample.shape
    depth, _, hg_heads, hg_k, hg_v = state_hgrn.shape
    _, _, ml_heads, ml_qk, ml_v = state_mlstm_C.shape
    f = w_down.shape[1]
    assert depth == 1 and dec_seq == GROUP and hg_k == hg_v == LANES and 2 * ml_heads <= LANES
    assert hg_lb_logits.shape[0] == 2

    hg_w = hg_heads * hg_k
    gate0 = 4 * hg_w + 2 * ml_heads * ml_qk + ml_heads * ml_v
    gate1 = gate0 + 2 * ml_heads
    wt = jnp.swapaxes(w_in[0], 0, 1)
    col_mq = 4 * hg_w
    col_mk = col_mq + ml_heads * ml_qk
    col_mv = col_mk + ml_heads * ml_qk
    col_mo = col_mv + ml_heads * ml_v
    col_ga = col_mo + ml_heads * ml_v
    col_gb = col_ga + d
    assert col_gb + d == wt.shape[0] - (gate1 - gate0)
    gate_bias = jnp.pad(jnp.concatenate([ml_i_bias[0], ml_f_bias[0]]), (0, LANES - 2 * ml_heads))[None, :]

    m_s_rows = n_s * GROUP
    c_all = jnp.concatenate([c_sample, c_prompt], axis=0)
    c_all = jnp.pad(c_all, ((0, (-c_all.shape[0]) % SUBLANES), (0, 0)))
    mod_s = _mod_call(c_all, n_s, ada_w[0], ada_b)
    mod_p = mod_s[m_s_rows:m_s_rows + n_p][:, None, :]

    w_out_b = w_out[0].astype(BF16)
    hg_nw = hg_norm_w
    ml_nw = ml_norm_w
    fw = final_norm_w[None, :]
    tm_o = 256

    xs = x_sample.reshape(n_s * GROUP, d)
    proj_s, gates_s, *wt_b = _inproj_call(xs, mod_s, wt, gate0, gate1 - gate0, m_s_rows, 512, 1)
    ya_s, hg_s = _hgrn_s_call(proj_s, hg_lb_logits, hg_nw, state_hgrn[0], hg_heads, hg_k, 0, col_ga // hg_k, 64)
    m_rows = jnp.pad(jnp.repeat(state_mlstm_m[0], GROUP, axis=0), ((0, 0), (0, LANES - ml_heads)))
    n_hb = jnp.repeat(jnp.transpose(state_mlstm_n[0], (1, 0, 2)), GROUP, axis=1)
    mg_s, c_s, nn_s, m_s = _mlstm_s_call(proj_s, ya_s, gates_s, gate_bias, m_rows, ml_nw, state_mlstm_C[0], n_hb,
                                         ml_heads, ml_qk, ml_v, col_mq, col_mk, col_mv, col_mo, col_gb, 64)
    x1_s = _outproj_call(mg_s, xs, mod_s, w_out_b, tm_o, 1)
    hist = (state_conv[0][:, 0], state_conv[0][:, 1])
    y_s, cv2, cv3, wa_b, wg_b, wd_b = _ffn_call(x1_s, mod_s, w_up[0], w_down[0], conv_w[0], conv_b, fw,
                                                m_s_rows, 256, 1, hist)
    cv_s = jnp.stack([cv2, cv3], axis=1)

    xp = x_prompt.reshape(n_p * seq, d)
    tm_p = 1024
    proj_p, gates_p = _inproj_call(xp, mod_p, tuple(wt_b), gate0, gate1 - gate0, tm_p, 2048, seq // tm_p)
    ya_p, hg_p = _hgrn_p_call(proj_p, hg_lb_logits, hg_nw, n_p, seq, hg_heads, hg_k, 0, col_ga // hg_k, 1024, 2)
    mg_p, c_p, nn_p, m_p = _mlstm_p_call(proj_p, ya_p, gates_p, gate_bias, ml_nw, n_p, seq, ml_heads, ml_qk, ml_v,
                                         col_mq, col_mk, col_mv, col_mo, col_gb, 2 * ML_CHUNK)
    x1_p = _outproj_call(mg_p, xp, mod_p, w_out_b, tm_o, seq // tm_o)
    tm_f = 512
    tiles = seq // tm_f
    y_p, atail = _ffn_call(x1_p, mod_p, (wa_b, wg_b), wd_b, conv_w[0], conv_b, fw, tm_f, 512, tiles)
    cv_p = atail.reshape(n_p, tiles, SUBLANES, f)[:, tiles - 1, SUBLANES - (CONV_W - 1):]

    return (y_p.reshape(n_p, seq, d), y_s.reshape(n_s, GROUP, d),
            hg_p[None], hg_s[None],
            c_p[None], c_s[None],
            nn_p.reshape(1, n_p, ml_heads, ml_qk), jnp.transpose(nn_s[:, GROUP - 1::GROUP], (1, 0, 2))[None],
            m_p[:, :, 0, 0][None], jnp.transpose(m_s[:, ::GROUP, 0])[None],
            cv_p[None], cv_s[None])
```

```python
import functools

import numpy as np
import jax
import jax.numpy as jnp
from jax import lax
from jax.experimental import pallas as pl
from jax.experimental.pallas import tpu as pltpu

F32 = jnp.float32
BF16 = jnp.bfloat16

EPS = 1e-6
CONV_W = 3
LANES = 128
SUBLANES = 8
VMEM_LIMIT_BYTES = 58 * 1024 * 1024

MXU_DEPTH = 256
HG_CHUNK = 64
HG_SAFE_EXP = 60.0
HG_LEVELS = (32, 16, 8, 4, 2, 1)
ML_CHUNK = 128
ML_HEADS_INTERLEAVED = 8
GROUP = 4


def _cparams(*sem):
    return pltpu.CompilerParams(dimension_semantics=sem, vmem_limit_bytes=VMEM_LIMIT_BYTES)


def _dot(a, b):
    return jnp.dot(a, b, preferred_element_type=F32)


def _dot_nt(a, b):
    return lax.dot_general(a, b, (((1,), (1,)), ((), ())), preferred_element_type=F32)


def _dot_tn(a, b):
    return lax.dot_general(a, b, (((0,), (0,)), ((), ())), preferred_element_type=F32)


def _split3(x):
    x1 = x.astype(BF16)
    r1 = x - x1.astype(F32)
    x2 = r1.astype(BF16)
    x3 = (r1 - x2.astype(F32)).astype(BF16)
    return x1, x2, x3


def _exact_left_mul(w, x):
    x1, x2, x3 = _split3(x)
    return _dot(w, x1) + _dot(w, x2) + _dot(w, x3)


def _exact_left_mul3(w3, x):
    return _dot(w3, jnp.concatenate(_split3(x), axis=0))


def _rows(ref, r0, n):
    if ref.shape[0] == 1:
        return ref[...]
    return ref[pl.ds(r0, n), :]


def _norm_mod(x, sc, sh):
    ms = jnp.mean(x * x, axis=-1, keepdims=True)
    return x * lax.rsqrt(ms + EPS) * (1.0 + sc) + sh


def _head_norm(o, w):
    return o * lax.rsqrt(jnp.mean(o * o, axis=-1, keepdims=True) + EPS) * w


def _lower_bound(lg):
    l0, l1 = lg[0:1, :], lg[1:2, :]
    m = jnp.maximum(l0, l1)
    e0, e1 = jnp.exp(l0 - m), jnp.exp(l1 - m)
    return e0 / (e0 + e1)


def _group_last(x, t, up):
    return jnp.where(t == 3, x, jnp.where(t == 2, up(x, 1), jnp.where(t == 1, up(x, 2), up(x, 3))))


def _mod_kernel(c_ref, w_ref, b_ref, o_ref, rep_s, *, n_rep):
    c = c_ref[...]
    s = (c * jax.nn.sigmoid(c)).astype(BF16)
    res = _dot(s, w_ref[...].astype(BF16)) + b_ref[...]
    for k in range(res.shape[1] // LANES):
        slab = res[:, k * LANES:(k + 1) * LANES]
        for t in range(GROUP):
            rep_s[k, pl.ds(t, n_rep, stride=GROUP), :] = slab[:n_rep]
        rep_s[k, n_rep * GROUP:, :] = slab[n_rep:]
    for k in range(res.shape[1] // LANES):
        o_ref[:, k * LANES:(k + 1) * LANES] = rep_s[k]


def _mod_call(c_all, n_rep, ada_w, ada_b):
    mp, d = c_all.shape
    n = ada_w.shape[1]
    tn = 1024
    mo = n_rep * GROUP + (mp - n_rep)
    return pl.pallas_call(
        functools.partial(_mod_kernel, n_rep=n_rep),
        grid=(n // tn,),
        in_specs=[pl.BlockSpec((mp, d), lambda j: (0, 0)),
                  pl.BlockSpec((d, tn), lambda j: (0, j)),
                  pl.BlockSpec((1, tn), lambda j: (0, j))],
        out_specs=pl.BlockSpec((mo, tn), lambda j: (0, j)),
        out_shape=jax.ShapeDtypeStruct((mo, n), F32),
        scratch_shapes=[pltpu.VMEM((tn // LANES, mo, LANES), F32)],
        compiler_params=_cparams("arbitrary"),
        name="mod",
    )(c_all, ada_w, ada_b)


def _inproj_kernel(*refs, tm, rc, na, emit_bf16):
    if emit_bf16:
        x_ref, sh_ref, sc_ref, wa_ref, wb_ref, wg_ref, o_ref, og_ref, wo_ref, wgo_ref, h_ref = refs
    else:
        x_ref, sh_ref, sc_ref, wa_ref, wg_ref, o_ref, og_ref, h_ref = refs
        wb_ref = wo_ref = wgo_ref = None
    j = pl.program_id(1)

    def weight(src, dst):
        w = src[...]
        if emit_bf16:
            w = w.astype(BF16)
            dst[...] = w
        return w

    @pl.when(j == 0)
    def _():
        wg = weight(wg_ref, wgo_ref)

        def body(r, carry):
            r0 = pl.multiple_of(r * rc, rc)
            h = _norm_mod(x_ref[pl.ds(r0, rc), :], _rows(sc_ref, r0, rc), _rows(sh_ref, r0, rc))
            hb = h.astype(BF16)
            h_ref[pl.ds(r0, rc), :] = hb
            gg = _dot_nt(hb, wg)
            og_ref[pl.ds(r0, rc), :] = jnp.concatenate(
                [gg, jnp.zeros((rc, og_ref.shape[1] - gg.shape[1]), F32)], axis=1)
            return carry
        lax.fori_loop(0, tm // rc, body, 0)

    if emit_bf16:
        @pl.when(j < na)
        def _():
            o_ref[...] = _dot_nt(h_ref[...], weight(wa_ref, wo_ref))

        @pl.when(j >= na)
        def _():
            o_ref[...] = _dot_nt(h_ref[...], weight(wb_ref, wo_ref))
    else:
        o_ref[...] = _dot_nt(h_ref[...], wa_ref[...])


def _mod_spec(mod, tm, col, tiles_per_seq):
    d = mod.shape[-1] // 6
    if mod.ndim == 3:
        return pl.BlockSpec((None, 1, d), lambda i, *_: (i // tiles_per_seq, 0, col))
    return pl.BlockSpec((tm, d), lambda i, *_: (i, col))


def _inproj_call(x, mod, w, gate0, n_gate, tm, tn, tiles_per_seq):
    m, d = x.shape
    emit_bf16 = not isinstance(w, tuple)
    gate1 = gate0 + n_gate
    n = (w.shape[0] - n_gate) if emit_bf16 else w[0].shape[0]
    assert n_gate == 2 * SUBLANES and gate0 % tn == 0 and n % tn == 0
    na = gate0 // tn
    tile = pl.BlockSpec((tn, d), lambda i, j: (j, 0))
    seg_g = pl.BlockSpec((n_gate, d), lambda i, j: (0, 0))
    out_specs = [pl.BlockSpec((tm, tn), lambda i, j: (i, j)), pl.BlockSpec((tm, LANES), lambda i, j: (i, 0))]
    out_shape = [jax.ShapeDtypeStruct((m, n), F32), jax.ShapeDtypeStruct((m, LANES), F32)]
    if emit_bf16:
        assert m == tm
        w_specs = [pl.BlockSpec((tn, d), lambda i, j: (jnp.minimum(j, na - 1), 0)),
                   pl.BlockSpec((pl.Element(tn), pl.Element(d)),
                                lambda i, j: (pl.multiple_of(gate1 + jnp.maximum(j - na, 0) * tn, SUBLANES), 0)),
                   pl.BlockSpec((n_gate, d), lambda i, j: (gate0 // n_gate, 0))]
        w_args = [w, w, w]
        out_specs += [tile, seg_g]
        out_shape += [jax.ShapeDtypeStruct((n, d), BF16), jax.ShapeDtypeStruct((n_gate, d), BF16)]
    else:
        w_specs = [tile, seg_g]
        w_args = list(w)
    kern = functools.partial(_inproj_kernel, tm=tm, rc=256, na=na, emit_bf16=emit_bf16)
    return pl.pallas_call(
        kern,
        grid=(m // tm, n // tn),
        in_specs=[pl.BlockSpec((tm, d), lambda i, j: (i, 0)),
                  _mod_spec(mod, tm, 0, tiles_per_seq),
                  _mod_spec(mod, tm, 1, tiles_per_seq)] + w_specs,
        out_specs=out_specs,
        out_shape=out_shape,
        scratch_shapes=[pltpu.VMEM((tm, d), BF16)],
        compiler_params=_cparams("arbitrary", "arbitrary"),
        name="inproj",
    )(x, mod, mod, *w_args)


def _hgrn_constants():
    L = HG_CHUNK
    tri = np.tril(np.ones((L, L), np.float32))
    t = np.arange(L)
    blocks = []
    masks = []
    for h in HG_LEVELS:
        mid = (t // (2 * h)) * (2 * h) + h - 1
        blocks.append(tri - tri[mid])
        masks.append((t[:, None] // (2 * h) == t[None, :] // (2 * h)).astype(np.float32))
    blocks.append(tri[L - 1][None, :] - tri)
    masks.append(np.eye(L, dtype=np.float32))
    return jnp.asarray(np.concatenate(blocks, 0), BF16), jnp.asarray(np.stack(masks, 0), F32)


def _hgrn_gates(hq, z, lb, kdim):
    omlb = 1.0 - lb
    sz = jax.nn.sigmoid(z)
    logf = jnp.log(lb + omlb * sz)
    kk = omlb * (1.0 - sz)
    q = hq * jax.nn.sigmoid(hq) * (kdim ** -0.5)
    return q, kk, logf


def _hgrn_p_kernel(hq_ref, hf_ref, hi_ref, hg_ref, ga_ref, lg_ref, nw_ref, trib_ref, wall_ref, mask_ref,
                   ya_ref, so_ref, st_ref, *, n_chunks, kdim):
    L = HG_CHUNK
    c = pl.program_id(2)
    hps = hq_ref.shape[1] // kdim
    tb = trib_ref.shape[0]
    heads_chunks = [(hd, ci) for ci in range(n_chunks) for hd in range(hps)]
    rows = lambda ci: slice(ci * L, (ci + 1) * L)
    lanes = lambda hd: slice(hd * kdim, (hd + 1) * kdim)

    @pl.when(c == 0)
    def _():
        st_ref[...] = jnp.zeros_like(st_ref)

    lb = _lower_bound(lg_ref[...])
    nw = nw_ref[...]
    q, kk, logf = _hgrn_gates(hq_ref[...], hf_ref[...], lb, kdim)
    trib = trib_ref[...]
    b = jnp.concatenate([_exact_left_mul3(trib, logf[r:r + tb]) for r in range(0, n_chunks * L, tb)], axis=0)
    worst = -jnp.min(jnp.concatenate([b[r:r + 1, :] for r in range(L - 1, n_chunks * L, L)], axis=0))

    def finish(hd, ci, o):
        hg = hg_ref[rows(ci), lanes(hd)]
        ya = _head_norm(o, nw[:, lanes(hd)]) * (hg * jax.nn.sigmoid(hg))
        ya_ref[rows(ci), lanes(hd)] = jax.nn.sigmoid(ga_ref[rows(ci), lanes(hd)]) * ya

    @pl.when(worst <= HG_SAFE_EXP)
    def _():
        eb = jnp.exp(b)
        qt = (q * eb).astype(BF16)
        kn = kk * (1.0 / eb)
        knb = kn.astype(BF16)
        row = lax.broadcasted_iota(jnp.int32, (L, L), 0)
        colid = lax.broadcasted_iota(jnp.int32, (L, L), 1)
        causal = colid <= row
        o_intra, incr, dec = {}, {}, {}
        for hd, ci in heads_chunks:
            vb = hi_ref[rows(ci), lanes(hd)].astype(BF16)
            a = jnp.where(causal, _dot_nt(qt[rows(ci), lanes(hd)], knb[rows(ci), lanes(hd)]), 0.0)
            o_intra[hd, ci] = _dot(a.astype(BF16), vb)
            dec[hd, ci] = eb[(ci + 1) * L - 1:(ci + 1) * L, lanes(hd)]
            incr[hd, ci] = _dot_tn(vb, (kn[rows(ci), lanes(hd)] * dec[hd, ci]).astype(BF16))
        sts = {(hd, 0): st_ref[hd] for hd in range(hps)}
        for hd, ci in heads_chunks:
            sts[hd, ci + 1] = sts[hd, ci] * dec[hd, ci] + incr[hd, ci]
        for hd in range(hps):
            st_ref[hd] = sts[hd, n_chunks]
        for hd, ci in heads_chunks:
            finish(hd, ci, o_intra[hd, ci] + _dot_nt(qt[rows(ci), lanes(hd)], sts[hd, ci].astype(BF16)))

    @pl.when(jnp.logical_not(worst <= HG_SAFE_EXP))
    def _():
        wall = wall_ref[...]
        rowid = lax.broadcasted_iota(jnp.int32, (L, kdim), 0)
        n_lv = len(HG_LEVELS)
        for hd, ci in heads_chunks:
            qc, kc, bc = q[rows(ci), lanes(hd)], kk[rows(ci), lanes(hd)], b[rows(ci), lanes(hd)]
            vb = hi_ref[rows(ci), lanes(hd)].astype(BF16)
            d = _exact_left_mul(wall, logf[rows(ci), lanes(hd)])
            a = mask_ref[n_lv] * _dot_nt(qc.astype(BF16), kc.astype(BF16))
            for li, h in enumerate(HG_LEVELS):
                e = jnp.exp(-jnp.abs(d[li * L:(li + 1) * L]))
                second = (rowid & h) != 0
                p = jnp.where(second, qc, kc) * e
                qh = jnp.where(second, p, 0.0).astype(BF16)
                kh = jnp.where(second, 0.0, p).astype(BF16)
                a = a + mask_ref[li] * _dot_nt(qh, kh)
            qt = (qc * jnp.exp(bc)).astype(BF16)
            kt = (kc * jnp.exp(d[n_lv * L:(n_lv + 1) * L])).astype(BF16)
            st = st_ref[hd]
            finish(hd, ci, _dot(a.astype(BF16), vb) + _dot_nt(qt, st.astype(BF16)))
            st_ref[hd] = st * jnp.exp(bc[L - 1:L, :]) + _dot_tn(vb, kt)

    @pl.when(c == pl.num_programs(2) - 1)
    def _():
        for hd in range(hps):
            so_ref[hd] = st_ref[hd].T


def _hgrn_p_call(proj, lb_logits, norm_w, n_seq, seq_len, heads, kdim, col0, col_ga, rows_per_step, hps):
    wall, masks = _hgrn_constants()
    per_mat = min(rows_per_step, MXU_DEPTH) // HG_CHUNK
    trib = np.kron(np.eye(per_mat, dtype=np.float32), np.tril(np.ones((HG_CHUNK, HG_CHUNK), np.float32)))
    trib = jnp.asarray(np.concatenate([trib, trib, trib], axis=1), BF16)
    nc = seq_len // rows_per_step
    assert heads % hps == 0 and col0 % hps == 0 and col_ga % hps == 0
    kern = functools.partial(_hgrn_p_kernel, n_chunks=rows_per_step // HG_CHUNK, kdim=kdim)
    wide = hps * kdim

    def col(first):
        return pl.BlockSpec((rows_per_step, wide), lambda b, h, c: (b * nc + c, first // hps + h))

    return pl.pallas_call(
        kern,
        grid=(n_seq, heads // hps, nc),
        in_specs=[col(col0), col(col0 + heads), col(col0 + 2 * heads), col(col0 + 3 * heads), col(col_ga),
                  pl.BlockSpec((2, wide), lambda b, h, c: (0, h)),
                  pl.BlockSpec((1, wide), lambda b, h, c: (0, h)),
                  pl.BlockSpec(trib.shape, lambda b, h, c: (0, 0)),
                  pl.BlockSpec(wall.shape, lambda b, h, c: (0, 0)),
                  pl.BlockSpec(masks.shape, lambda b, h, c: (0, 0, 0))],
        out_specs=[pl.BlockSpec((rows_per_step, wide), lambda b, h, c: (b * nc + c, h)),
                   pl.BlockSpec((None, hps, kdim, kdim), lambda b, h, c: (b, h, 0, 0))],
        out_shape=[jax.ShapeDtypeStruct((n_seq * seq_len, heads * kdim), F32),
                   jax.ShapeDtypeStruct((n_seq, heads, kdim, kdim), F32)],
        scratch_shapes=[pltpu.VMEM((hps, kdim, kdim), F32)],
        compiler_params=_cparams("arbitrary", "arbitrary", "arbitrary"),
        name="hgrn_prompt",
    )(proj, proj, proj, proj, proj, lb_logits, norm_w, trib, wall, masks)


def _hgrn_s_kernel(hq_ref, hf_ref, hi_ref, hg_ref, ga_ref, lg_ref, nw_ref, s_ref, ya_ref, so_ref, *, n_pairs):
    kdim = hq_ref.shape[1]
    rows = n_pairs * SUBLANES
    lb = _lower_bound(lg_ref[...])
    rowid = lax.broadcasted_iota(jnp.int32, (rows, kdim), 0)
    t = rowid & (GROUP - 1)
    first = (rowid & GROUP) == 0
    first8 = lax.broadcasted_iota(jnp.int32, (SUBLANES, kdim), 0) < GROUP
    down = lambda y, j: pltpu.roll(y, j, 0)
    up = lambda y, j: pltpu.roll(y, rows - j, 0)

    q, kk, logf = _hgrn_gates(hq_ref[...], hf_ref[...], lb, kdim)
    v = hi_ref[...]
    b = logf
    for dlt in range(1, GROUP):
        b = b + jnp.where(t >= dlt, down(logf, dlt), 0.0)
    o = jnp.sum(q * kk, axis=1, keepdims=True) * v
    for dlt in range(1, GROUP):
        x = q * down(kk, dlt) * jnp.exp(b - down(b, dlt))
        a = jnp.sum(jnp.where(t >= dlt, x, 0.0), axis=1, keepdims=True)
        o = o + a * down(v, dlt)
    b_last = _group_last(b, t, up)
    qt = q * jnp.exp(b)
    kt = kk * jnp.exp(b_last - b)
    d1, d2, d3 = _split3(jnp.exp(b_last))
    swap = lambda y: jnp.where(first, up(y.astype(F32), GROUP), down(y.astype(F32), GROUP))
    dsplit = jnp.where(t == 0, swap(d1), jnp.where(t == 1, swap(d2), jnp.where(t == 2, swap(d3), 0.0)))
    ones = jnp.where(t <= 2, 1.0, 0.0)
    lhs = (jnp.where(first, kt, dsplit), jnp.where(first, dsplit, kt))
    rhs = (jnp.concatenate([jnp.where(first, v, 0.0), jnp.where(first, 0.0, ones)], axis=1),
           jnp.concatenate([jnp.where(first, 0.0, v), jnp.where(first, ones, 0.0)], axis=1))
    o_inter = []
    for p in range(n_pairs):
        sl = slice(p * SUBLANES, (p + 1) * SUBLANES)
        qb = qt[sl].astype(BF16)
        parts = []
        for half in range(2):
            s0 = s_ref[2 * p + half]
            upd = _dot_tn(lhs[half][sl].astype(BF16), rhs[half][sl].astype(BF16))
            so_ref[2 * p + half] = s0 * upd[:, kdim:] + upd[:, :kdim]
            parts.append(_dot(qb, s0.astype(BF16)))
        o_inter.append(jnp.where(first8, parts[0], parts[1]))
    o = o + jnp.concatenate(o_inter, axis=0)
    hg = hg_ref[...]
    ya_ref[...] = jax.nn.sigmoid(ga_ref[...]) * (_head_norm(o, nw_ref[...]) * (hg * jax.nn.sigmoid(hg)))


def _hgrn_s_call(proj, lb_logits, norm_w, state, heads, kdim, col0, col_ga, seqs_per_step):
    n_seq = state.shape[0]
    rows = seqs_per_step * GROUP
    kern = functools.partial(_hgrn_s_kernel, n_pairs=seqs_per_step // 2)

    def col(k):
        return pl.BlockSpec((rows, kdim), lambda i, h: (i, col0 + k * heads + h))

    st_spec = pl.BlockSpec((seqs_per_step, None, kdim, kdim), lambda i, h: (i, h, 0, 0))
    return pl.pallas_call(
        kern,
        grid=(n_seq // seqs_per_step, heads),
        in_specs=[col(0), col(1), col(2), col(3),
                  pl.BlockSpec((rows, kdim), lambda i, h: (i, col_ga + h)),
                  pl.BlockSpec((2, kdim), lambda i, h: (0, h)),
                  pl.BlockSpec((1, kdim), lambda i, h: (0, h)),
                  st_spec],
        out_specs=[pl.BlockSpec((rows, kdim), lambda i, h: (i, h)), st_spec],
        out_shape=[jax.ShapeDtypeStruct((n_seq * GROUP, heads * kdim), F32),
                   jax.ShapeDtypeStruct(state.shape, F32)],
        compiler_params=_cparams("arbitrary", "arbitrary"),
        name="hgrn_sample",
    )(proj, proj, proj, proj, proj, lb_logits, norm_w, state)


def _lane_pick(x, lane, idx):
    return jnp.broadcast_to(jnp.sum(jnp.where(lane == idx, x, 0.0), axis=1, keepdims=True), x.shape)


def _interleave(chains):
    live = list(chains)
    while live:
        still = []
        for g in live:
            try:
                next(g)
                still.append(g)
            except StopIteration:
                pass
        live = still


def _mlstm_p_kernel(q_ref, k_ref, v_ref, og_ref, gb_ref, ya_ref, g_ref, bias_ref, nw_ref, tri_ref,
                    mg_ref, co_ref, no_ref, mo_ref, c_s, n_s, m_s, *, heads, qk, vd):
    L = ML_CHUNK
    c = pl.program_id(1)

    @pl.when(c == 0)
    def _():
        c_s[...] = jnp.zeros_like(c_s)
        n_s[...] = jnp.zeros_like(n_s)
        m_s[...] = jnp.zeros_like(m_s)

    lane = lax.broadcasted_iota(jnp.int32, (L, LANES), 1)
    row = lax.broadcasted_iota(jnp.int32, (L, LANES), 0)

    def head(hd, r, g, b_all):
        qs = slice(hd * qk, (hd + 1) * qk)
        vs = slice(hd * vd, (hd + 1) * vd)
        ig = _lane_pick(g, lane, hd)
        b = _lane_pick(b_all, lane, heads + hd)
        m_prev = m_s[hd]
        gs = (ig - b).T
        yield
        dm = jnp.where(lane <= row, b + gs, -jnp.inf)
        mt = jnp.maximum(b + m_prev, jnp.max(dm, axis=1, keepdims=True))
        inter = jnp.exp(b + m_prev - mt)
        q = q_ref[r, qs] * (qk ** -0.5)
        k = k_ref[r, qs]
        vb = v_ref[r, vs].astype(BF16)
        qb = q.astype(BF16)
        sc = _dot_nt(qb, k.astype(BF16)) * jnp.exp(dm - mt)
        yield
        c0 = c_s[hd]
        n0 = n_s[hd]
        num = inter[:, 0:1] * _dot(qb, c0.astype(BF16)) + _dot(sc.astype(BF16), vb)
        den = inter[:, 0:1] * jnp.sum(q * n0, axis=1, keepdims=True) + jnp.sum(sc, axis=1, keepdims=True)
        yield
        hh = num / jnp.maximum(jnp.abs(den), jnp.exp(-mt[:, 0:1]))
        yb = _head_norm(hh, nw_ref[:, vs]) * jax.nn.sigmoid(og_ref[r, vs])
        mg_ref[r, vs] = (ya_ref[r, vs] + jax.nn.sigmoid(gb_ref[r, vs]) * yb).astype(mg_ref.dtype)
        yield
        m_last = mt[L - 1:L, :]
        b_last = b[L - 1:L, :]
        dec = jnp.exp(b_last + m_prev - m_last)
        kw = jnp.exp(b_last - b + ig - m_last) * k
        c_s[hd] = jnp.concatenate([dec, dec], axis=1) * c0 + _dot_tn(kw.astype(BF16), vb)
        n_s[hd] = dec * n0 + jnp.sum(kw, axis=0, keepdims=True)
        m_s[hd] = m_last

    for r0 in range(0, q_ref.shape[0], L):
        r = slice(r0, r0 + L)
        g = g_ref[r, :] + bias_ref[...]
        b_all = _exact_left_mul(tri_ref[...], jax.nn.log_sigmoid(g))
        for h0 in range(0, heads, ML_HEADS_INTERLEAVED):
            _interleave([head(hd, r, g, b_all) for hd in range(h0, min(h0 + ML_HEADS_INTERLEAVED, heads))])

    @pl.when(c == pl.num_programs(1) - 1)
    def _():
        co_ref[...] = c_s[...]
        no_ref[...] = n_s[...]
        mo_ref[...] = m_s[...]


def _mlstm_p_call(proj, ya, gates, bias, norm_w, n_seq, seq_len, heads, qk, vd, colq, colk, colv, colo, colg,
                  rows_per_step):
    assert qk == ML_CHUNK and qk == LANES and rows_per_step % ML_CHUNK == 0
    L = rows_per_step
    nc = seq_len // L
    qw, vw = heads * qk, heads * vd
    assert colq % qw == 0 and colk % qw == 0 and colv % vw == 0 and colo % vw == 0 and colg % vw == 0
    tri = jnp.asarray(np.tril(np.ones((ML_CHUNK, ML_CHUNK), np.float32)), BF16)
    kern = functools.partial(_mlstm_p_kernel, heads=heads, qk=qk, vd=vd)
    return pl.pallas_call(
        kern,
        grid=(n_seq, nc),
        in_specs=[pl.BlockSpec((L, qw), lambda b, c: (b * nc + c, colq // qw)),
                  pl.BlockSpec((L, qw), lambda b, c: (b * nc + c, colk // qw)),
                  pl.BlockSpec((L, vw), lambda b, c: (b * nc + c, colv // vw)),
                  pl.BlockSpec((L, vw), lambda b, c: (b * nc + c, colo // vw)),
                  pl.BlockSpec((L, vw), lambda b, c: (b * nc + c, colg // vw)),
                  pl.BlockSpec((L, vw), lambda b, c: (b * nc + c, 0)),
                  pl.BlockSpec((L, LANES), lambda b, c: (b * nc + c, 0)),
                  pl.BlockSpec((1, LANES), lambda b, c: (0, 0)),
                  pl.BlockSpec((1, vw), lambda b, c: (0, 0)),
                  pl.BlockSpec(tri.shape, lambda b, c: (0, 0))],
        out_specs=[pl.BlockSpec((L, vw), lambda b, c: (b * nc + c, 0)),
                   pl.BlockSpec((None, heads, qk, vd), lambda b, c: (b, 0, 0, 0)),
                   pl.BlockSpec((None, heads, 1, qk), lambda b, c: (b, 0, 0, 0)),
                   pl.BlockSpec((None, heads, 1, LANES), lambda b, c: (b, 0, 0, 0))],
        out_shape=[jax.ShapeDtypeStruct((n_seq * seq_len, vw), BF16),
                   jax.ShapeDtypeStruct((n_seq, heads, qk, vd), F32),
                   jax.ShapeDtypeStruct((n_seq, heads, 1, qk), F32),
                   jax.ShapeDtypeStruct((n_seq, heads, 1, LANES), F32)],
        scratch_shapes=[pltpu.VMEM((heads, qk, vd), F32), pltpu.VMEM((heads, 1, qk), F32),
                        pltpu.VMEM((heads, 1, LANES), F32)],
        compiler_params=_cparams("arbitrary", "arbitrary"),
        name="mlstm_prompt",
    )(proj, proj, proj, proj, proj, ya, gates, bias, norm_w, tri)


def _mlstm_s_kernel(q_ref, k_ref, v_ref, og_ref, gb_ref, ya_ref, g_ref, bias_ref, m_ref, nw_ref, c_ref, n_ref,
                    mg_ref, co_ref, no_ref, mo_ref, *, heads, n_pairs):
    h = pl.program_id(1)
    qk = q_ref.shape[1]
    rows = n_pairs * SUBLANES
    lane = lax.broadcasted_iota(jnp.int32, (rows, LANES), 1)
    rowid = lax.broadcasted_iota(jnp.int32, (rows, LANES), 0)
    t = rowid & (GROUP - 1)
    first = (rowid & GROUP) == 0
    first8 = lax.broadcasted_iota(jnp.int32, (SUBLANES, 1), 0) < GROUP
    down = lambda y, j: pltpu.roll(y, j, 0)
    up = lambda y, j: pltpu.roll(y, rows - j, 0)

    g = g_ref[...] + bias_ref[...]
    ig = _lane_pick(g, lane, h)
    lf = _lane_pick(jax.nn.log_sigmoid(g), lane, heads + h)
    m_prev = _lane_pick(m_ref[...], lane, h)
    b = lf
    for dlt in range(1, GROUP):
        b = b + jnp.where(t >= dlt, down(lf, dlt), 0.0)
    q = q_ref[...] * (qk ** -0.5)
    k = k_ref[...]
    v = v_ref[...]
    dms = [ig] + [jnp.where(t >= dlt, b - down(b, dlt) + down(ig, dlt), -jnp.inf) for dlt in range(1, GROUP)]
    mt = jnp.maximum(b + m_prev, functools.reduce(jnp.maximum, dms))
    inter = jnp.exp(b + m_prev - mt)[:, 0:1]
    num = jnp.zeros(v.shape, F32)
    den = jnp.zeros((rows, 1), F32)
    for dlt in range(GROUP):
        kd = k if dlt == 0 else down(k, dlt)
        vd_ = v if dlt == 0 else down(v, dlt)
        s = jnp.sum(q * kd, axis=1, keepdims=True) * jnp.exp(dms[dlt] - mt)[:, 0:1]
        num = num + s * vd_
        den = den + s
    m_last = _group_last(mt, t, up)
    b_last = _group_last(b, t, up)
    dec = jnp.exp(b_last + m_prev - m_last)
    kw = jnp.exp(b_last - b + ig - m_last) * k
    n0 = n_ref[...]
    no_ref[...] = dec * n0 + kw + down(kw, 1) + down(kw, 2) + down(kw, 3)
    mo_ref[...] = m_last
    den = den + inter * jnp.sum(q * n0, axis=1, keepdims=True)

    kw_half = (jnp.where(first, kw, 0.0), jnp.where(first, 0.0, kw))
    dec2 = jnp.concatenate([dec, dec], axis=1)
    num_inter = []
    for p in range(n_pairs):
        sl = slice(p * SUBLANES, (p + 1) * SUBLANES)
        qb = q[sl].astype(BF16)
        vb = v[sl].astype(BF16)
        parts = []
        for half in range(2):
            c0 = c_ref[2 * p + half]
            r = p * SUBLANES + half * GROUP
            co_ref[2 * p + half] = dec2[r:r + 1, :] * c0 + _dot_tn(kw_half[half][sl].astype(BF16), vb)
            parts.append(_dot(qb, c0.astype(BF16)))
        num_inter.append(jnp.where(first8, parts[0], parts[1]))
    num = num + inter * jnp.concatenate(num_inter, axis=0)
    hh = num / jnp.maximum(jnp.abs(den), jnp.exp(-mt)[:, 0:1])
    yb = _head_norm(hh, nw_ref[...]) * jax.nn.sigmoid(og_ref[...])
    mg_ref[...] = (ya_ref[...] + jax.nn.sigmoid(gb_ref[...]) * yb).astype(mg_ref.dtype)


def _mlstm_s_call(proj, ya, gates, bias, m_rows, norm_w, c_state, n_state, heads, qk, vd,
                  colq, colk, colv, colo, colg, seqs_per_step):
    n_seq = c_state.shape[0]
    rows = seqs_per_step * GROUP
    kern = functools.partial(_mlstm_s_kernel, heads=heads, n_pairs=seqs_per_step // 2)
    c_spec = pl.BlockSpec((seqs_per_step, None, qk, vd), lambda i, h: (i, h, 0, 0))
    n_spec = pl.BlockSpec((None, rows, qk), lambda i, h: (h, i, 0))
    return pl.pallas_call(
        kern,
        grid=(n_seq // seqs_per_step, heads),
        in_specs=[pl.BlockSpec((rows, qk), lambda i, h: (i, colq // qk + h)),
                  pl.BlockSpec((rows, qk), lambda i, h: (i, colk // qk + h)),
                  pl.BlockSpec((rows, vd), lambda i, h: (i, colv // vd + h)),
                  pl.BlockSpec((rows, vd), lambda i, h: (i, colo // vd + h)),
                  pl.BlockSpec((rows, vd), lambda i, h: (i, colg // vd + h)),
                  pl.BlockSpec((rows, vd), lambda i, h: (i, h)),
                  pl.BlockSpec((rows, LANES), lambda i, h: (i, 0)),
                  pl.BlockSpec((1, LANES), lambda i, h: (0, 0)),
                  pl.BlockSpec((rows, LANES), lambda i, h: (i, 0)),
                  pl.BlockSpec((1, vd), lambda i, h: (0, h)),
                  c_spec, n_spec],
        out_specs=[pl.BlockSpec((rows, vd), lambda i, h: (i, h)),
                   c_spec, n_spec,
                   pl.BlockSpec((None, rows, LANES), lambda i, h: (h, i, 0))],
        out_shape=[jax.ShapeDtypeStruct((n_seq * GROUP, heads * vd), BF16),
                   jax.ShapeDtypeStruct(c_state.shape, F32),
                   jax.ShapeDtypeStruct(n_state.shape, F32),
                   jax.ShapeDtypeStruct((heads, n_seq * GROUP, LANES), F32)],
        compiler_params=_cparams("arbitrary", "arbitrary"),
        name="mlstm_sample",
    )(proj, proj, proj, proj, proj, ya, gates, bias, m_rows, norm_w, c_state, n_state)


def _outproj_kernel(mg_ref, x_ref, g1_ref, w_ref, o_ref):
    o_ref[...] = x_ref[...] + g1_ref[...] * _dot(mg_ref[...], w_ref[...])


def _outproj_call(merged, x, mod, w_out, tm, tiles_per_seq):
    m, d = x.shape
    blk = pl.BlockSpec((tm, d), lambda i: (i, 0))
    return pl.pallas_call(
        _outproj_kernel,
        grid=(m // tm,),
        in_specs=[blk, blk, _mod_spec(mod, tm, 2, tiles_per_seq), pl.BlockSpec((d, d), lambda i: (0, 0))],
        out_specs=blk,
        out_shape=jax.ShapeDtypeStruct((m, d), F32),
        compiler_params=_cparams("arbitrary"),
        name="outproj",
    )(merged, x, mod, w_out)


def _ffn_kernel(*refs, tm, rc, tiles_per_seq, grouped, emit_bf16):
    refs = iter(refs)
    x_ref, sh_ref, sc_ref, g2_ref, wa_ref, wg_ref, wd_ref, cw_ref, cb_ref, fw_ref = (next(refs) for _ in range(10))
    hist_refs = (next(refs), next(refs)) if grouped else None
    y_ref = next(refs)
    if grouped:
        cv2_ref, cv3_ref = next(refs), next(refs)
    else:
        a_ref = next(refs)
    wao_ref, wgo_ref, wdo_ref = (next(refs), next(refs), next(refs)) if emit_bf16 else (None, None, None)
    h_s, acc_s = next(refs), next(refs)
    a_s = carry_s = next(refs)
    i = pl.program_id(0)
    j = pl.program_id(1)

    def weight(src, dst):
        w = src[...]
        if emit_bf16:
            w = w.astype(BF16)
            dst[...] = w
        return w

    @pl.when(j == 0)
    def _():
        def body(r, carry):
            r0 = pl.multiple_of(r * rc, rc)
            h = _norm_mod(x_ref[pl.ds(r0, rc), :], _rows(sc_ref, r0, rc), _rows(sh_ref, r0, rc))
            h_s[pl.ds(r0, rc), :] = h.astype(BF16)
            return carry
        lax.fori_loop(0, tm // rc, body, 0)
        acc_s[...] = jnp.zeros_like(acc_s)

    hb = h_s[...]
    a = _dot(hb, weight(wa_ref, wao_ref))
    g = _dot(hb, weight(wg_ref, wgo_ref))
    rowid = lax.broadcasted_iota(jnp.int32, a.shape, 0)
    r1 = pltpu.roll(a, 1, 0)
    r2 = pltpu.roll(a, 2, 0)
    if grouped:
        t = rowid & (GROUP - 1)
        slabs = []
        for s in range(a.shape[1] // LANES):
            a_s[s] = jnp.zeros((tm, LANES), F32)
            for u, hist in enumerate(hist_refs):
                a_s[s, pl.ds(u, tm // GROUP, stride=GROUP), :] = hist[:, s * LANES:(s + 1) * LANES]
            slabs.append(a_s[s])
        p2 = jnp.concatenate(slabs, axis=1)
        prev1 = jnp.where(t == 0, pltpu.roll(p2, tm - 1, 0), r1)
        prev2 = jnp.where(t <= 1, p2, r2)
        for s in range(a.shape[1] // LANES):
            a_s[s] = a[:, s * LANES:(s + 1) * LANES]
        for s in range(a.shape[1] // LANES):
            cv2_ref[:, s * LANES:(s + 1) * LANES] = a_s[s, pl.ds(GROUP - 2, tm // GROUP, stride=GROUP), :]
            cv3_ref[:, s * LANES:(s + 1) * LANES] = a_s[s, pl.ds(GROUP - 1, tm // GROUP, stride=GROUP), :]
    else:
        tail = a[tm - SUBLANES:tm, :]
        car = jnp.where(i % tiles_per_seq == 0, 0.0, carry_s[j])
        c1 = car[SUBLANES - 1:SUBLANES, :]
        c2 = car[SUBLANES - 2:SUBLANES - 1, :]
        prev1 = jnp.where(rowid == 0, c1, r1)
        prev2 = jnp.where(rowid == 0, c2, jnp.where(rowid == 1, c1, r2))
        carry_s[j] = tail
        a_ref[...] = tail
    cw = cw_ref[...]
    ac = cw[0:1, :] * prev2 + cw[1:2, :] * prev1 + cw[2:3, :] * a + cb_ref[...]
    yv = 0.5 * ac * (1.0 + lax.erf(ac * (2.0 ** -0.5))) * g
    acc_s[...] += _dot(yv.astype(BF16), weight(wd_ref, wdo_ref))

    @pl.when(j == pl.num_programs(1) - 1)
    def _():
        def body(r, carry):
            r0 = pl.multiple_of(r * rc, rc)
            x2 = x_ref[pl.ds(r0, rc), :] + _rows(g2_ref, r0, rc) * acc_s[pl.ds(r0, rc), :]
            ms = jnp.mean(x2 * x2, axis=-1, keepdims=True)
            y_ref[pl.ds(r0, rc), :] = x2 * lax.rsqrt(ms + EPS) * fw_ref[...]
            return carry
        lax.fori_loop(0, tm // rc, body, 0)


def _ffn_call(x, mod, w_up, w_down, conv_w, conv_b, final_w, tm, tf, tiles_per_seq, hist=None):
    m, d = x.shape
    f = w_down.shape[0]
    nj = f // tf
    grouped = hist is not None
    emit_bf16 = not isinstance(w_up, tuple)
    kern = functools.partial(_ffn_kernel, tm=tm, rc=min(tm, 256), tiles_per_seq=tiles_per_seq, grouped=grouped,
                             emit_bf16=emit_bf16)
    half_spec = pl.BlockSpec((d, tf), lambda i, j: (0, j))
    down_spec = pl.BlockSpec((tf, d), lambda i, j: (j, 0))
    if emit_bf16:
        assert m == tm
        up_specs = [half_spec, pl.BlockSpec((d, tf), lambda i, j: (0, nj + j))]
        up_args = [w_up, w_up]
    else:
        up_specs = [half_spec, half_spec]
        up_args = list(w_up)
    in_specs = [pl.BlockSpec((tm, d), lambda i, j: (i, 0)),
                _mod_spec(mod, tm, 3, tiles_per_seq),
                _mod_spec(mod, tm, 4, tiles_per_seq),
                _mod_spec(mod, tm, 5, tiles_per_seq)] + up_specs + [
                down_spec,
                pl.BlockSpec((CONV_W, tf), lambda i, j: (0, j)),
                pl.BlockSpec((1, tf), lambda i, j: (0, j)),
                pl.BlockSpec((1, d), lambda i, j: (0, 0))]
    args = [x, mod, mod, mod] + up_args + [w_down, conv_w, conv_b, final_w]
    scratch = [pltpu.VMEM((tm, d), BF16), pltpu.VMEM((tm, d), F32)]
    if grouped:
        in_specs += [pl.BlockSpec((tm // GROUP, tf), lambda i, j: (i, j))] * 2
        args += list(hist)
        scratch.append(pltpu.VMEM((tf // LANES, tm, LANES), F32))
        a_specs = [pl.BlockSpec((tm // GROUP, tf), lambda i, j: (i, j))] * 2
        a_shapes = [jax.ShapeDtypeStruct((m // GROUP, f), F32)] * 2
    else:
        scratch.append(pltpu.VMEM((nj, SUBLANES, tf), F32))
        a_specs = [pl.BlockSpec((SUBLANES, tf), lambda i, j: (i, j))]
        a_shapes = [jax.ShapeDtypeStruct((m // tm * SUBLANES, f), F32)]
    if emit_bf16:
        a_specs += [half_spec, half_spec, down_spec]
        a_shapes += [jax.ShapeDtypeStruct((d, f), BF16), jax.ShapeDtypeStruct((d, f), BF16),
                     jax.ShapeDtypeStruct((f, d), BF16)]
    return pl.pallas_call(
        kern,
        grid=(m // tm, nj),
        in_specs=in_specs,
        out_specs=[pl.BlockSpec((tm, d), lambda i, j: (i, 0))] + a_specs,
        out_shape=[jax.ShapeDtypeStruct((m, d), F32)] + a_shapes,
        scratch_shapes=scratch,
        compiler_params=_cparams("arbitrary", "arbitrary"),
        name="ffn",
    )(*args)


def kernel(x_prompt, x_sample, c_prompt, c_sample, state_hgrn, state_mlstm_C, state_mlstm_n, state_mlstm_m,
           state_conv, ada_w, ada_b, w_in, hg_lb_logits, hg_norm_w, ml_i_bias, ml_f_bias, ml_norm_w, w_out,
           conv_w, conv_b, w_up, w_down, final_norm_w):
    n_p, seq, d = x_prompt.shape
    n_s, dec_seq, _ = x_sample.shape
    depth, _, hg_heads, hg_k, hg_v = state_hgrn.shape
    _, _, ml_heads, ml_qk, ml_v = state_mlstm_C.shape
    f = w_down.shape[1]
    assert depth == 1 and dec_seq == GROUP and hg_k == hg_v == LANES and 2 * ml_heads <= LANES
    assert hg_lb_logits.shape[0] == 2

    hg_w = hg_heads * hg_k
    gate0 = 4 * hg_w + 2 * ml_heads * ml_qk + ml_heads * ml_v
    gate1 = gate0 + 2 * ml_heads
    wt = jnp.swapaxes(w_in[0], 0, 1)
    col_mq = 4 * hg_w
    col_mk = col_mq + ml_heads * ml_qk
    col_mv = col_mk + ml_heads * ml_qk
    col_mo = col_mv + ml_heads * ml_v
    col_ga = col_mo + ml_heads * ml_v
    col_gb = col_ga + d
    assert col_gb + d == wt.shape[0] - (gate1 - gate0)
    gate_bias = jnp.pad(jnp.concatenate([ml_i_bias[0], ml_f_bias[0]]), (0, LANES - 2 * ml_heads))[None, :]

    m_s_rows = n_s * GROUP
    c_all = jnp.concatenate([c_sample, c_prompt], axis=0)
    c_all = jnp.pad(c_all, ((0, (-c_all.shape[0]) % SUBLANES), (0, 0)))
    mod_s = _mod_call(c_all, n_s, ada_w[0], ada_b)
    mod_p = mod_s[m_s_rows:m_s_rows + n_p][:, None, :]

    w_out_b = w_out[0].astype(BF16)
    hg_nw = hg_norm_w
    ml_nw = ml_norm_w
    fw = final_norm_w[None, :]
    tm_o = 256

    xs = x_sample.reshape(n_s * GROUP, d)
    proj_s, gates_s, *wt_b = _inproj_call(xs, mod_s, wt, gate0, gate1 - gate0, m_s_rows, 512, 1)
    ya_s, hg_s = _hgrn_s_call(proj_s, hg_lb_logits, hg_nw, state_hgrn[0], hg_heads, hg_k, 0, col_ga // hg_k, 128)
    m_rows = jnp.pad(jnp.repeat(state_mlstm_m[0], GROUP, axis=0), ((0, 0), (0, LANES - ml_heads)))
    n_hb = jnp.repeat(jnp.transpose(state_mlstm_n[0], (1, 0, 2)), GROUP, axis=1)
    mg_s, c_s, nn_s, m_s = _mlstm_s_call(proj_s, ya_s, gates_s, gate_bias, m_rows, ml_nw, state_mlstm_C[0], n_hb,
                                         ml_heads, ml_qk, ml_v, col_mq, col_mk, col_mv, col_mo, col_gb, 64)
    x1_s = _outproj_call(mg_s, xs, mod_s, w_out_b, tm_o, 1)
    hist = (state_conv[0][:, 0], state_conv[0][:, 1])
    y_s, cv2, cv3, wa_b, wg_b, wd_b = _ffn_call(x1_s, mod_s, w_up[0], w_down[0], conv_w[0], conv_b, fw,
                                                m_s_rows, 256, 1, hist)
    cv_s = jnp.stack([cv2, cv3], axis=1)

    xp = x_prompt.reshape(n_p * seq, d)
    tm_p = 1024
    proj_p, gates_p = _inproj_call(xp, mod_p, tuple(wt_b), gate0, gate1 - gate0, tm_p, 2048, seq // tm_p)
    ya_p, hg_p = _hgrn_p_call(proj_p, hg_lb_logits, hg_nw, n_p, seq, hg_heads, hg_k, 0, col_ga // hg_k, 1024, 4)
    mg_p, c_p, nn_p, m_p = _mlstm_p_call(proj_p, ya_p, gates_p, gate_bias, ml_nw, n_p, seq, ml_heads, ml_qk, ml_v,
                                         col_mq, col_mk, col_mv, col_mo, col_gb, 2 * ML_CHUNK)
    x1_p = _outproj_call(mg_p, xp, mod_p, w_out_b, tm_o, seq // tm_o)
    tm_f = 512
    tiles = seq // tm_f
    y_p, atail = _ffn_call(x1_p, mod_p, (wa_b, wg_b), wd_b, conv_w[0], conv_b, fw, tm_f, 512, tiles)
    cv_p = atail.reshape(n_p, tiles, SUBLANES, f)[:, tiles - 1, SUBLANES - (CONV_W - 1):]

    return (y_p.reshape(n_p, seq, d), y_s.reshape(n_s, GROUP, d),
            hg_p[None], hg_s[None],
            c_p[None], c_s[None],
            nn_p.reshape(1, n_p, ml_heads, ml_qk), jnp.transpose(nn_s[:, GROUP - 1::GROUP], (1, 0, 2))[None],
            m_p[:, :, 0, 0][None], jnp.transpose(m_s[:, ::GROUP, 0])[None],
            cv_p[None], cv_s[None])
```

```python
import functools

import numpy as np
import jax
import jax.numpy as jnp
from jax import lax
from jax.experimental import pallas as pl
from jax.experimental.pallas import tpu as pltpu

F32 = jnp.float32
BF16 = jnp.bfloat16

EPS = 1e-6
CONV_W = 3
LANES = 128
SUBLANES = 8
VMEM_LIMIT_BYTES = 58 * 1024 * 1024

MXU_DEPTH = 256
HG_CHUNK = 64
HG_SAFE_EXP = 60.0
HG_LEVELS = (32, 16, 8, 4, 2, 1)
ML_CHUNK = 128
ML_HEADS_INTERLEAVED = 8
GROUP = 4


def _cparams(*sem):
    return pltpu.CompilerParams(dimension_semantics=sem, vmem_limit_bytes=VMEM_LIMIT_BYTES)


def _dot(a, b):
    return jnp.dot(a, b, preferred_element_type=F32)


def _dot_nt(a, b):
    return lax.dot_general(a, b, (((1,), (1,)), ((), ())), preferred_element_type=F32)


def _dot_tn(a, b):
    return lax.dot_general(a, b, (((0,), (0,)), ((), ())), preferred_element_type=F32)


def _split3(x):
    x1 = x.astype(BF16)
    r1 = x - x1.astype(F32)
    x2 = r1.astype(BF16)
    x3 = (r1 - x2.astype(F32)).astype(BF16)
    return x1, x2, x3


def _exact_left_mul(w, x):
    x1, x2, x3 = _split3(x)
    return _dot(w, x1) + _dot(w, x2) + _dot(w, x3)


def _exact_left_mul3(w3, x):
    return _dot(w3, jnp.concatenate(_split3(x), axis=0))


def _rows(ref, r0, n):
    if ref.shape[0] == 1:
        return ref[...]
    return ref[pl.ds(r0, n), :]


def _norm_mod(x, sc, sh):
    ms = jnp.mean(x * x, axis=-1, keepdims=True)
    return x * lax.rsqrt(ms + EPS) * (1.0 + sc) + sh


def _head_norm(o, w):
    return o * lax.rsqrt(jnp.mean(o * o, axis=-1, keepdims=True) + EPS) * w


def _lower_bound(lg):
    l0, l1 = lg[0:1, :], lg[1:2, :]
    m = jnp.maximum(l0, l1)
    e0, e1 = jnp.exp(l0 - m), jnp.exp(l1 - m)
    return e0 / (e0 + e1)


def _group_last(x, t, up):
    return jnp.where(t == 3, x, jnp.where(t == 2, up(x, 1), jnp.where(t == 1, up(x, 2), up(x, 3))))


def _mod_kernel(c_ref, w_ref, b_ref, o_ref, rep_s, *, n_rep):
    c = c_ref[...]
    s = (c * jax.nn.sigmoid(c)).astype(BF16)
    res = _dot(s, w_ref[...].astype(BF16)) + b_ref[...]
    for k in range(res.shape[1] // LANES):
        slab = res[:, k * LANES:(k + 1) * LANES]
        for t in range(GROUP):
            rep_s[k, pl.ds(t, n_rep, stride=GROUP), :] = slab[:n_rep]
        rep_s[k, n_rep * GROUP:, :] = slab[n_rep:]
    for k in range(res.shape[1] // LANES):
        o_ref[:, k * LANES:(k + 1) * LANES] = rep_s[k]


def _mod_call(c_all, n_rep, ada_w, ada_b):
    mp, d = c_all.shape
    n = ada_w.shape[1]
    tn = 1024
    mo = n_rep * GROUP + (mp - n_rep)
    return pl.pallas_call(
        functools.partial(_mod_kernel, n_rep=n_rep),
        grid=(n // tn,),
        in_specs=[pl.BlockSpec((mp, d), lambda j: (0, 0)),
                  pl.BlockSpec((d, tn), lambda j: (0, j)),
                  pl.BlockSpec((1, tn), lambda j: (0, j))],
        out_specs=pl.BlockSpec((mo, tn), lambda j: (0, j)),
        out_shape=jax.ShapeDtypeStruct((mo, n), F32),
        scratch_shapes=[pltpu.VMEM((tn // LANES, mo, LANES), F32)],
        compiler_params=_cparams("arbitrary"),
        name="mod",
    )(c_all, ada_w, ada_b)


def _inproj_kernel(*refs, tm, rc, emit_bf16):
    if emit_bf16:
        x_ref, sh_ref, sc_ref, w_ref, wg_ref, o_ref, og_ref, wo_ref, wgo_ref, h_ref = refs
    else:
        x_ref, sh_ref, sc_ref, w_ref, wg_ref, o_ref, og_ref, h_ref = refs
        wo_ref = wgo_ref = None
    j = pl.program_id(1)

    def weight(src, dst):
        w = src[...]
        if emit_bf16:
            w = w.astype(BF16)
            dst[...] = w
        return w

    @pl.when(j == 0)
    def _():
        wg = weight(wg_ref, wgo_ref)

        def body(r, carry):
            r0 = pl.multiple_of(r * rc, rc)
            h = _norm_mod(x_ref[pl.ds(r0, rc), :], _rows(sc_ref, r0, rc), _rows(sh_ref, r0, rc))
            hb = h.astype(BF16)
            h_ref[pl.ds(r0, rc), :] = hb
            gg = _dot_nt(hb, wg)
            og_ref[pl.ds(r0, rc), :] = jnp.concatenate(
                [gg, jnp.zeros((rc, og_ref.shape[1] - gg.shape[1]), F32)], axis=1)
            return carry
        lax.fori_loop(0, tm // rc, body, 0)

    o_ref[...] = _dot_nt(h_ref[...], weight(w_ref, wo_ref))


def _mod_spec(mod, tm, col, tiles_per_seq):
    d = mod.shape[-1] // 6
    if mod.ndim == 3:
        return pl.BlockSpec((None, 1, d), lambda i, *_: (i // tiles_per_seq, 0, col))
    return pl.BlockSpec((tm, d), lambda i, *_: (i, col))


def _inproj_call(x, mod, w, gate0, n_gate, tm, tn, tiles_per_seq):
    m, d = x.shape
    emit_bf16 = not isinstance(w, tuple)
    n = (w.shape[0] - n_gate) if emit_bf16 else w[0].shape[0]
    assert n_gate == 2 * SUBLANES and gate0 % tn == 0 and n % tn == 0
    na = gate0 // tn
    tile = pl.BlockSpec((tn, d), lambda i, j: (j, 0))
    seg_g = pl.BlockSpec((n_gate, d), lambda i, j: (0, 0))
    out_specs = [pl.BlockSpec((tm, tn), lambda i, j: (i, j)), pl.BlockSpec((tm, LANES), lambda i, j: (i, 0))]
    out_shape = [jax.ShapeDtypeStruct((m, n), F32), jax.ShapeDtypeStruct((m, LANES), F32)]
    if emit_bf16:
        assert m == tm
        w_specs = [pl.BlockSpec((pl.Element(tn), pl.Element(d)),
                                lambda i, j: (pl.multiple_of(j * tn + jnp.where(j < na, 0, n_gate), SUBLANES), 0)),
                   pl.BlockSpec((n_gate, d), lambda i, j: (gate0 // n_gate, 0))]
        w_args = [w, w]
        out_specs += [tile, seg_g]
        out_shape += [jax.ShapeDtypeStruct((n, d), BF16), jax.ShapeDtypeStruct((n_gate, d), BF16)]
    else:
        w_specs = [tile, seg_g]
        w_args = list(w)
    kern = functools.partial(_inproj_kernel, tm=tm, rc=256, emit_bf16=emit_bf16)
    return pl.pallas_call(
        kern,
        grid=(m // tm, n // tn),
        in_specs=[pl.BlockSpec((tm, d), lambda i, j: (i, 0)),
                  _mod_spec(mod, tm, 0, tiles_per_seq),
                  _mod_spec(mod, tm, 1, tiles_per_seq)] + w_specs,
        out_specs=out_specs,
        out_shape=out_shape,
        scratch_shapes=[pltpu.VMEM((tm, d), BF16)],
        compiler_params=_cparams("arbitrary", "arbitrary"),
        name="inproj",
    )(x, mod, mod, *w_args)


def _hgrn_constants():
    L = HG_CHUNK
    tri = np.tril(np.ones((L, L), np.float32))
    t = np.arange(L)
    blocks = []
    masks = []
    for h in HG_LEVELS:
        mid = (t // (2 * h)) * (2 * h) + h - 1
        blocks.append(tri - tri[mid])
        masks.append((t[:, None] // (2 * h) == t[None, :] // (2 * h)).astype(np.float32))
    blocks.append(tri[L - 1][None, :] - tri)
    masks.append(np.eye(L, dtype=np.float32))
    return jnp.asarray(np.concatenate(blocks, 0), BF16), jnp.asarray(np.stack(masks, 0), F32)


def _hgrn_gates(hq, z, lb, kdim):
    omlb = 1.0 - lb
    sz = jax.nn.sigmoid(z)
    logf = jnp.log(lb + omlb * sz)
    kk = omlb * (1.0 - sz)
    q = hq * jax.nn.sigmoid(hq) * (kdim ** -0.5)
    return q, kk, logf


def _hgrn_p_kernel(hq_ref, hf_ref, hi_ref, hg_ref, ga_ref, lg_ref, nw_ref, trib_ref, wall_ref, mask_ref,
                   ya_ref, so_ref, st_ref, *, n_chunks, kdim):
    L = HG_CHUNK
    c = pl.program_id(2)
    hps = hq_ref.shape[1] // kdim
    tb = trib_ref.shape[0]
    heads_chunks = [(hd, ci) for ci in range(n_chunks) for hd in range(hps)]
    rows = lambda ci: slice(ci * L, (ci + 1) * L)
    lanes = lambda hd: slice(hd * kdim, (hd + 1) * kdim)

    @pl.when(c == 0)
    def _():
        st_ref[...] = jnp.zeros_like(st_ref)

    lb = _lower_bound(lg_ref[...])
    nw = nw_ref[...]
    q, kk, logf = _hgrn_gates(hq_ref[...], hf_ref[...], lb, kdim)
    trib = trib_ref[...]
    b = jnp.concatenate([_exact_left_mul3(trib, logf[r:r + tb]) for r in range(0, n_chunks * L, tb)], axis=0)
    worst = -jnp.min(jnp.concatenate([b[r:r + 1, :] for r in range(L - 1, n_chunks * L, L)], axis=0))

    def finish(hd, ci, o):
        hg = hg_ref[rows(ci), lanes(hd)]
        ya = _head_norm(o, nw[:, lanes(hd)]) * (hg * jax.nn.sigmoid(hg))
        ya_ref[rows(ci), lanes(hd)] = jax.nn.sigmoid(ga_ref[rows(ci), lanes(hd)]) * ya

    @pl.when(worst <= HG_SAFE_EXP)
    def _():
        eb = jnp.exp(b)
        qt = (q * eb).astype(BF16)
        kn = kk * (1.0 / eb)
        knb = kn.astype(BF16)
        row = lax.broadcasted_iota(jnp.int32, (L, L), 0)
        colid = lax.broadcasted_iota(jnp.int32, (L, L), 1)
        causal = colid <= row
        o_intra, incr, dec = {}, {}, {}
        for hd, ci in heads_chunks:
            vb = hi_ref[rows(ci), lanes(hd)].astype(BF16)
            a = jnp.where(causal, _dot_nt(qt[rows(ci), lanes(hd)], knb[rows(ci), lanes(hd)]), 0.0)
            o_intra[hd, ci] = _dot(a.astype(BF16), vb)
            dec[hd, ci] = eb[(ci + 1) * L - 1:(ci + 1) * L, lanes(hd)]
            incr[hd, ci] = _dot_tn(vb, (kn[rows(ci), lanes(hd)] * dec[hd, ci]).astype(BF16))
        sts = {(hd, 0): st_ref[hd] for hd in range(hps)}
        for hd, ci in heads_chunks:
            sts[hd, ci + 1] = sts[hd, ci] * dec[hd, ci] + incr[hd, ci]
        for hd in range(hps):
            st_ref[hd] = sts[hd, n_chunks]
        for hd, ci in heads_chunks:
            finish(hd, ci, o_intra[hd, ci] + _dot_nt(qt[rows(ci), lanes(hd)], sts[hd, ci].astype(BF16)))

    @pl.when(jnp.logical_not(worst <= HG_SAFE_EXP))
    def _():
        wall = wall_ref[...]
        rowid = lax.broadcasted_iota(jnp.int32, (L, kdim), 0)
        n_lv = len(HG_LEVELS)
        for hd, ci in heads_chunks:
            qc, kc, bc = q[rows(ci), lanes(hd)], kk[rows(ci), lanes(hd)], b[rows(ci), lanes(hd)]
            vb = hi_ref[rows(ci), lanes(hd)].astype(BF16)
            d = _exact_left_mul(wall, logf[rows(ci), lanes(hd)])
            a = mask_ref[n_lv] * _dot_nt(qc.astype(BF16), kc.astype(BF16))
            for li, h in enumerate(HG_LEVELS):
                e = jnp.exp(-jnp.abs(d[li * L:(li + 1) * L]))
                second = (rowid & h) != 0
                p = jnp.where(second, qc, kc) * e
                qh = jnp.where(second, p, 0.0).astype(BF16)
                kh = jnp.where(second, 0.0, p).astype(BF16)
                a = a + mask_ref[li] * _dot_nt(qh, kh)
            qt = (qc * jnp.exp(bc)).astype(BF16)
            kt = (kc * jnp.exp(d[n_lv * L:(n_lv + 1) * L])).astype(BF16)
            st = st_ref[hd]
            finish(hd, ci, _dot(a.astype(BF16), vb) + _dot_nt(qt, st.astype(BF16)))
            st_ref[hd] = st * jnp.exp(bc[L - 1:L, :]) + _dot_tn(vb, kt)

    @pl.when(c == pl.num_programs(2) - 1)
    def _():
        for hd in range(hps):
            so_ref[hd] = st_ref[hd].T


def _hgrn_p_call(proj, lb_logits, norm_w, n_seq, seq_len, heads, kdim, col0, col_ga, rows_per_step, hps):
    wall, masks = _hgrn_constants()
    per_mat = min(rows_per_step, MXU_DEPTH) // HG_CHUNK
    trib = np.kron(np.eye(per_mat, dtype=np.float32), np.tril(np.ones((HG_CHUNK, HG_CHUNK), np.float32)))
    trib = jnp.asarray(np.concatenate([trib, trib, trib], axis=1), BF16)
    nc = seq_len // rows_per_step
    assert heads % hps == 0 and col0 % hps == 0 and col_ga % hps == 0
    kern = functools.partial(_hgrn_p_kernel, n_chunks=rows_per_step // HG_CHUNK, kdim=kdim)
    wide = hps * kdim

    def col(first):
        return pl.BlockSpec((rows_per_step, wide), lambda b, h, c: (b * nc + c, first // hps + h))

    return pl.pallas_call(
        kern,
        grid=(n_seq, heads // hps, nc),
        in_specs=[col(col0), col(col0 + heads), col(col0 + 2 * heads), col(col0 + 3 * heads), col(col_ga),
                  pl.BlockSpec((2, wide), lambda b, h, c: (0, h)),
                  pl.BlockSpec((1, wide), lambda b, h, c: (0, h)),
                  pl.BlockSpec(trib.shape, lambda b, h, c: (0, 0)),
                  pl.BlockSpec(wall.shape, lambda b, h, c: (0, 0)),
                  pl.BlockSpec(masks.shape, lambda b, h, c: (0, 0, 0))],
        out_specs=[pl.BlockSpec((rows_per_step, wide), lambda b, h, c: (b * nc + c, h)),
                   pl.BlockSpec((None, hps, kdim, kdim), lambda b, h, c: (b, h, 0, 0))],
        out_shape=[jax.ShapeDtypeStruct((n_seq * seq_len, heads * kdim), F32),
                   jax.ShapeDtypeStruct((n_seq, heads, kdim, kdim), F32)],
        scratch_shapes=[pltpu.VMEM((hps, kdim, kdim), F32)],
        compiler_params=_cparams("arbitrary", "arbitrary", "arbitrary"),
        name="hgrn_prompt",
    )(proj, proj, proj, proj, proj, lb_logits, norm_w, trib, wall, masks)


def _hgrn_s_kernel(hq_ref, hf_ref, hi_ref, hg_ref, ga_ref, lg_ref, nw_ref, s_ref, ya_ref, so_ref, *, n_pairs):
    kdim = hq_ref.shape[1]
    rows = n_pairs * SUBLANES
    lb = _lower_bound(lg_ref[...])
    rowid = lax.broadcasted_iota(jnp.int32, (rows, kdim), 0)
    t = rowid & (GROUP - 1)
    first = (rowid & GROUP) == 0
    first8 = lax.broadcasted_iota(jnp.int32, (SUBLANES, kdim), 0) < GROUP
    down = lambda y, j: pltpu.roll(y, j, 0)
    up = lambda y, j: pltpu.roll(y, rows - j, 0)

    q, kk, logf = _hgrn_gates(hq_ref[...], hf_ref[...], lb, kdim)
    v = hi_ref[...]
    b = logf
    for dlt in range(1, GROUP):
        b = b + jnp.where(t >= dlt, down(logf, dlt), 0.0)
    o = jnp.sum(q * kk, axis=1, keepdims=True) * v
    for dlt in range(1, GROUP):
        x = q * down(kk, dlt) * jnp.exp(b - down(b, dlt))
        a = jnp.sum(jnp.where(t >= dlt, x, 0.0), axis=1, keepdims=True)
        o = o + a * down(v, dlt)
    b_last = _group_last(b, t, up)
    qt = q * jnp.exp(b)
    kt = kk * jnp.exp(b_last - b)
    d1, d2, d3 = _split3(jnp.exp(b_last))
    swap = lambda y: jnp.where(first, up(y.astype(F32), GROUP), down(y.astype(F32), GROUP))
    dsplit = jnp.where(t == 0, swap(d1), jnp.where(t == 1, swap(d2), jnp.where(t == 2, swap(d3), 0.0)))
    ones = jnp.where(t <= 2, 1.0, 0.0)
    lhs = (jnp.where(first, kt, dsplit), jnp.where(first, dsplit, kt))
    rhs = (jnp.concatenate([jnp.where(first, v, 0.0), jnp.where(first, 0.0, ones)], axis=1),
           jnp.concatenate([jnp.where(first, 0.0, v), jnp.where(first, ones, 0.0)], axis=1))
    o_inter = []
    for p in range(n_pairs):
        sl = slice(p * SUBLANES, (p + 1) * SUBLANES)
        qb = qt[sl].astype(BF16)
        parts = []
        for half in range(2):
            s0 = s_ref[2 * p + half]
            upd = _dot_tn(lhs[half][sl].astype(BF16), rhs[half][sl].astype(BF16))
            so_ref[2 * p + half] = s0 * upd[:, kdim:] + upd[:, :kdim]
            parts.append(_dot(qb, s0.astype(BF16)))
        o_inter.append(jnp.where(first8, parts[0], parts[1]))
    o = o + jnp.concatenate(o_inter, axis=0)
    hg = hg_ref[...]
    ya_ref[...] = jax.nn.sigmoid(ga_ref[...]) * (_head_norm(o, nw_ref[...]) * (hg * jax.nn.sigmoid(hg)))


def _hgrn_s_call(proj, lb_logits, norm_w, state, heads, kdim, col0, col_ga, seqs_per_step):
    n_seq = state.shape[0]
    rows = seqs_per_step * GROUP
    kern = functools.partial(_hgrn_s_kernel, n_pairs=seqs_per_step // 2)

    def col(k):
        return pl.BlockSpec((rows, kdim), lambda i, h: (i, col0 + k * heads + h))

    st_spec = pl.BlockSpec((seqs_per_step, None, kdim, kdim), lambda i, h: (i, h, 0, 0))
    return pl.pallas_call(
        kern,
        grid=(n_seq // seqs_per_step, heads),
        in_specs=[col(0), col(1), col(2), col(3),
                  pl.BlockSpec((rows, kdim), lambda i, h: (i, col_ga + h)),
                  pl.BlockSpec((2, kdim), lambda i, h: (0, h)),
                  pl.BlockSpec((1, kdim), lambda i, h: (0, h)),
                  st_spec],
        out_specs=[pl.BlockSpec((rows, kdim), lambda i, h: (i, h)), st_spec],
        out_shape=[jax.ShapeDtypeStruct((n_seq * GROUP, heads * kdim), F32),
                   jax.ShapeDtypeStruct(state.shape, F32)],
        compiler_params=_cparams("arbitrary", "arbitrary"),
        name="hgrn_sample",
    )(proj, proj, proj, proj, proj, lb_logits, norm_w, state)


def _lane_pick(x, lane, idx):
    return jnp.broadcast_to(jnp.sum(jnp.where(lane == idx, x, 0.0), axis=1, keepdims=True), x.shape)


def _interleave(chains):
    live = list(chains)
    while live:
        still = []
        for g in live:
            try:
                next(g)
                still.append(g)
            except StopIteration:
                pass
        live = still


def _mlstm_p_kernel(q_ref, k_ref, v_ref, og_ref, gb_ref, ya_ref, g_ref, bias_ref, nw_ref, tri_ref,
                    mg_ref, co_ref, no_ref, mo_ref, c_s, n_s, m_s, *, heads, qk, vd):
    L = ML_CHUNK
    c = pl.program_id(1)

    @pl.when(c == 0)
    def _():
        c_s[...] = jnp.zeros_like(c_s)
        n_s[...] = jnp.zeros_like(n_s)
        m_s[...] = jnp.zeros_like(m_s)

    lane = lax.broadcasted_iota(jnp.int32, (L, LANES), 1)
    row = lax.broadcasted_iota(jnp.int32, (L, LANES), 0)

    def head(hd, r, g, b_all):
        qs = slice(hd * qk, (hd + 1) * qk)
        vs = slice(hd * vd, (hd + 1) * vd)
        ig = _lane_pick(g, lane, hd)
        b = _lane_pick(b_all, lane, heads + hd)
        m_prev = m_s[hd]
        gs = (ig - b).T
        yield
        dm = jnp.where(lane <= row, b + gs, -jnp.inf)
        mt = jnp.maximum(b + m_prev, jnp.max(dm, axis=1, keepdims=True))
        inter = jnp.exp(b + m_prev - mt)
        q = q_ref[r, qs] * (qk ** -0.5)
        k = k_ref[r, qs]
        vb = v_ref[r, vs].astype(BF16)
        qb = q.astype(BF16)
        sc = _dot_nt(qb, k.astype(BF16)) * jnp.exp(dm - mt)
        yield
        c0 = c_s[hd]
        n0 = n_s[hd]
        num = inter[:, 0:1] * _dot(qb, c0.astype(BF16)) + _dot(sc.astype(BF16), vb)
        den = inter[:, 0:1] * jnp.sum(q * n0, axis=1, keepdims=True) + jnp.sum(sc, axis=1, keepdims=True)
        yield
        hh = num / jnp.maximum(jnp.abs(den), jnp.exp(-mt[:, 0:1]))
        yb = _head_norm(hh, nw_ref[:, vs]) * jax.nn.sigmoid(og_ref[r, vs])
        mg_ref[r, vs] = (ya_ref[r, vs] + jax.nn.sigmoid(gb_ref[r, vs]) * yb).astype(mg_ref.dtype)
        yield
        m_last = mt[L - 1:L, :]
        b_last = b[L - 1:L, :]
        dec = jnp.exp(b_last + m_prev - m_last)
        kw = jnp.exp(b_last - b + ig - m_last) * k
        c_s[hd] = jnp.concatenate([dec, dec], axis=1) * c0 + _dot_tn(kw.astype(BF16), vb)
        n_s[hd] = dec * n0 + jnp.sum(kw, axis=0, keepdims=True)
        m_s[hd] = m_last

    for r0 in range(0, q_ref.shape[0], L):
        r = slice(r0, r0 + L)
        g = g_ref[r, :] + bias_ref[...]
        b_all = _exact_left_mul(tri_ref[...], jax.nn.log_sigmoid(g))
        for h0 in range(0, heads, ML_HEADS_INTERLEAVED):
            _interleave([head(hd, r, g, b_all) for hd in range(h0, min(h0 + ML_HEADS_INTERLEAVED, heads))])

    @pl.when(c == pl.num_programs(1) - 1)
    def _():
        co_ref[...] = c_s[...]
        no_ref[...] = n_s[...]
        mo_ref[...] = m_s[...]


def _mlstm_p_call(proj, ya, gates, bias, norm_w, n_seq, seq_len, heads, qk, vd, colq, colk, colv, colo, colg,
                  rows_per_step):
    assert qk == ML_CHUNK and qk == LANES and rows_per_step % ML_CHUNK == 0
    L = rows_per_step
    nc = seq_len // L
    qw, vw = heads * qk, heads * vd
    assert colq % qw == 0 and colk % qw == 0 and colv % vw == 0 and colo % vw == 0 and colg % vw == 0
    tri = jnp.asarray(np.tril(np.ones((ML_CHUNK, ML_CHUNK), np.float32)), BF16)
    kern = functools.partial(_mlstm_p_kernel, heads=heads, qk=qk, vd=vd)
    return pl.pallas_call(
        kern,
        grid=(n_seq, nc),
        in_specs=[pl.BlockSpec((L, qw), lambda b, c: (b * nc + c, colq // qw)),
                  pl.BlockSpec((L, qw), lambda b, c: (b * nc + c, colk // qw)),
                  pl.BlockSpec((L, vw), lambda b, c: (b * nc + c, colv // vw)),
                  pl.BlockSpec((L, vw), lambda b, c: (b * nc + c, colo // vw)),
                  pl.BlockSpec((L, vw), lambda b, c: (b * nc + c, colg // vw)),
                  pl.BlockSpec((L, vw), lambda b, c: (b * nc + c, 0)),
                  pl.BlockSpec((L, LANES), lambda b, c: (b * nc + c, 0)),
                  pl.BlockSpec((1, LANES), lambda b, c: (0, 0)),
                  pl.BlockSpec((1, vw), lambda b, c: (0, 0)),
                  pl.BlockSpec(tri.shape, lambda b, c: (0, 0))],
        out_specs=[pl.BlockSpec((L, vw), lambda b, c: (b * nc + c, 0)),
                   pl.BlockSpec((None, heads, qk, vd), lambda b, c: (b, 0, 0, 0)),
                   pl.BlockSpec((None, heads, 1, qk), lambda b, c: (b, 0, 0, 0)),
                   pl.BlockSpec((None, heads, 1, LANES), lambda b, c: (b, 0, 0, 0))],
        out_shape=[jax.ShapeDtypeStruct((n_seq * seq_len, vw), BF16),
                   jax.ShapeDtypeStruct((n_seq, heads, qk, vd), F32),
                   jax.ShapeDtypeStruct((n_seq, heads, 1, qk), F32),
                   jax.ShapeDtypeStruct((n_seq, heads, 1, LANES), F32)],
        scratch_shapes=[pltpu.VMEM((heads, qk, vd), F32), pltpu.VMEM((heads, 1, qk), F32),
                        pltpu.VMEM((heads, 1, LANES), F32)],
        compiler_params=_cparams("arbitrary", "arbitrary"),
        name="mlstm_prompt",
    )(proj, proj, proj, proj, proj, ya, gates, bias, norm_w, tri)


def _mlstm_s_kernel(q_ref, k_ref, v_ref, og_ref, gb_ref, ya_ref, g_ref, bias_ref, m_ref, nw_ref, c_ref, n_ref,
                    mg_ref, co_ref, no_ref, mo_ref, *, heads, n_pairs):
    h = pl.program_id(1)
    qk = q_ref.shape[1]
    rows = n_pairs * SUBLANES
    lane = lax.broadcasted_iota(jnp.int32, (rows, LANES), 1)
    rowid = lax.broadcasted_iota(jnp.int32, (rows, LANES), 0)
    t = rowid & (GROUP - 1)
    first = (rowid & GROUP) == 0
    first8 = lax.broadcasted_iota(jnp.int32, (SUBLANES, 1), 0) < GROUP
    down = lambda y, j: pltpu.roll(y, j, 0)
    up = lambda y, j: pltpu.roll(y, rows - j, 0)

    g = g_ref[...] + bias_ref[...]
    ig = _lane_pick(g, lane, h)
    lf = _lane_pick(jax.nn.log_sigmoid(g), lane, heads + h)
    m_prev = _lane_pick(m_ref[...], lane, h)
    b = lf
    for dlt in range(1, GROUP):
        b = b + jnp.where(t >= dlt, down(lf, dlt), 0.0)
    q = q_ref[...] * (qk ** -0.5)
    k = k_ref[...]
    v = v_ref[...]
    dms = [ig] + [jnp.where(t >= dlt, b - down(b, dlt) + down(ig, dlt), -jnp.inf) for dlt in range(1, GROUP)]
    mt = jnp.maximum(b + m_prev, functools.reduce(jnp.maximum, dms))
    inter = jnp.exp(b + m_prev - mt)[:, 0:1]
    num = jnp.zeros(v.shape, F32)
    den = jnp.zeros((rows, 1), F32)
    for dlt in range(GROUP):
        kd = k if dlt == 0 else down(k, dlt)
        vd_ = v if dlt == 0 else down(v, dlt)
        s = jnp.sum(q * kd, axis=1, keepdims=True) * jnp.exp(dms[dlt] - mt)[:, 0:1]
        num = num + s * vd_
        den = den + s
    m_last = _group_last(mt, t, up)
    b_last = _group_last(b, t, up)
    dec = jnp.exp(b_last + m_prev - m_last)
    kw = jnp.exp(b_last - b + ig - m_last) * k
    n0 = n_ref[...]
    no_ref[...] = dec * n0 + kw + down(kw, 1) + down(kw, 2) + down(kw, 3)
    mo_ref[...] = m_last
    den = den + inter * jnp.sum(q * n0, axis=1, keepdims=True)

    kw_half = (jnp.where(first, kw, 0.0), jnp.where(first, 0.0, kw))
    dec2 = jnp.concatenate([dec, dec], axis=1)
    num_inter = []
    for p in range(n_pairs):
        sl = slice(p * SUBLANES, (p + 1) * SUBLANES)
        qb = q[sl].astype(BF16)
        vb = v[sl].astype(BF16)
        parts = []
        for half in range(2):
            c0 = c_ref[2 * p + half]
            r = p * SUBLANES + half * GROUP
            co_ref[2 * p + half] = dec2[r:r + 1, :] * c0 + _dot_tn(kw_half[half][sl].astype(BF16), vb)
            parts.append(_dot(qb, c0.astype(BF16)))
        num_inter.append(jnp.where(first8, parts[0], parts[1]))
    num = num + inter * jnp.concatenate(num_inter, axis=0)
    hh = num / jnp.maximum(jnp.abs(den), jnp.exp(-mt)[:, 0:1])
    yb = _head_norm(hh, nw_ref[...]) * jax.nn.sigmoid(og_ref[...])
    mg_ref[...] = (ya_ref[...] + jax.nn.sigmoid(gb_ref[...]) * yb).astype(mg_ref.dtype)


def _mlstm_s_call(proj, ya, gates, bias, m_rows, norm_w, c_state, n_state, heads, qk, vd,
                  colq, colk, colv, colo, colg, seqs_per_step):
    n_seq = c_state.shape[0]
    rows = seqs_per_step * GROUP
    kern = functools.partial(_mlstm_s_kernel, heads=heads, n_pairs=seqs_per_step // 2)
    c_spec = pl.BlockSpec((seqs_per_step, None, qk, vd), lambda i, h: (i, h, 0, 0))
    n_spec = pl.BlockSpec((None, rows, qk), lambda i, h: (h, i, 0))
    return pl.pallas_call(
        kern,
        grid=(n_seq // seqs_per_step, heads),
        in_specs=[pl.BlockSpec((rows, qk), lambda i, h: (i, colq // qk + h)),
                  pl.BlockSpec((rows, qk), lambda i, h: (i, colk // qk + h)),
                  pl.BlockSpec((rows, vd), lambda i, h: (i, colv // vd + h)),
                  pl.BlockSpec((rows, vd), lambda i, h: (i, colo // vd + h)),
                  pl.BlockSpec((rows, vd), lambda i, h: (i, colg // vd + h)),
                  pl.BlockSpec((rows, vd), lambda i, h: (i, h)),
                  pl.BlockSpec((rows, LANES), lambda i, h: (i, 0)),
                  pl.BlockSpec((1, LANES), lambda i, h: (0, 0)),
                  pl.BlockSpec((rows, LANES), lambda i, h: (i, 0)),
                  pl.BlockSpec((1, vd), lambda i, h: (0, h)),
                  c_spec, n_spec],
        out_specs=[pl.BlockSpec((rows, vd), lambda i, h: (i, h)),
                   c_spec, n_spec,
                   pl.BlockSpec((None, rows, LANES), lambda i, h: (h, i, 0))],
        out_shape=[jax.ShapeDtypeStruct((n_seq * GROUP, heads * vd), BF16),
                   jax.ShapeDtypeStruct(c_state.shape, F32),
                   jax.ShapeDtypeStruct(n_state.shape, F32),
                   jax.ShapeDtypeStruct((heads, n_seq * GROUP, LANES), F32)],
        compiler_params=_cparams("arbitrary", "arbitrary"),
        name="mlstm_sample",
    )(proj, proj, proj, proj, proj, ya, gates, bias, m_rows, norm_w, c_state, n_state)


def _outproj_kernel(mg_ref, x_ref, g1_ref, w_ref, o_ref):
    o_ref[...] = x_ref[...] + g1_ref[...] * _dot(mg_ref[...], w_ref[...])


def _outproj_call(merged, x, mod, w_out, tm, tiles_per_seq):
    m, d = x.shape
    blk = pl.BlockSpec((tm, d), lambda i: (i, 0))
    return pl.pallas_call(
        _outproj_kernel,
        grid=(m // tm,),
        in_specs=[blk, blk, _mod_spec(mod, tm, 2, tiles_per_seq), pl.BlockSpec((d, d), lambda i: (0, 0))],
        out_specs=blk,
        out_shape=jax.ShapeDtypeStruct((m, d), F32),
        compiler_params=_cparams("arbitrary"),
        name="outproj",
    )(merged, x, mod, w_out)


def _ffn_kernel(*refs, tm, rc, tiles_per_seq, grouped, emit_bf16):
    refs = iter(refs)
    x_ref, sh_ref, sc_ref, g2_ref, wa_ref, wg_ref, wd_ref, cw_ref, cb_ref, fw_ref = (next(refs) for _ in range(10))
    hist_refs = (next(refs), next(refs)) if grouped else None
    y_ref = next(refs)
    if grouped:
        cv2_ref, cv3_ref = next(refs), next(refs)
    else:
        a_ref = next(refs)
    wao_ref, wgo_ref, wdo_ref = (next(refs), next(refs), next(refs)) if emit_bf16 else (None, None, None)
    h_s, acc_s = next(refs), next(refs)
    a_s = carry_s = next(refs)
    i = pl.program_id(0)
    j = pl.program_id(1)

    def weight(src, dst):
        w = src[...]
        if emit_bf16:
            w = w.astype(BF16)
            dst[...] = w
        return w

    @pl.when(j == 0)
    def _():
        def body(r, carry):
            r0 = pl.multiple_of(r * rc, rc)
            h = _norm_mod(x_ref[pl.ds(r0, rc), :], _rows(sc_ref, r0, rc), _rows(sh_ref, r0, rc))
            h_s[pl.ds(r0, rc), :] = h.astype(BF16)
            return carry
        lax.fori_loop(0, tm // rc, body, 0)
        acc_s[...] = jnp.zeros_like(acc_s)

    hb = h_s[...]
    a = _dot(hb, weight(wa_ref, wao_ref))
    g = _dot(hb, weight(wg_ref, wgo_ref))
    rowid = lax.broadcasted_iota(jnp.int32, a.shape, 0)
    r1 = pltpu.roll(a, 1, 0)
    r2 = pltpu.roll(a, 2, 0)
    if grouped:
        t = rowid & (GROUP - 1)
        slabs = []
        for s in range(a.shape[1] // LANES):
            a_s[s] = jnp.zeros((tm, LANES), F32)
            for u, hist in enumerate(hist_refs):
                a_s[s, pl.ds(u, tm // GROUP, stride=GROUP), :] = hist[:, s * LANES:(s + 1) * LANES]
            slabs.append(a_s[s])
        p2 = jnp.concatenate(slabs, axis=1)
        prev1 = jnp.where(t == 0, pltpu.roll(p2, tm - 1, 0), r1)
        prev2 = jnp.where(t <= 1, p2, r2)
        for s in range(a.shape[1] // LANES):
            a_s[s] = a[:, s * LANES:(s + 1) * LANES]
        for s in range(a.shape[1] // LANES):
            cv2_ref[:, s * LANES:(s + 1) * LANES] = a_s[s, pl.ds(GROUP - 2, tm // GROUP, stride=GROUP), :]
            cv3_ref[:, s * LANES:(s + 1) * LANES] = a_s[s, pl.ds(GROUP - 1, tm // GROUP, stride=GROUP), :]
    else:
        tail = a[tm - SUBLANES:tm, :]
        car = jnp.where(i % tiles_per_seq == 0, 0.0, carry_s[j])
        c1 = car[SUBLANES - 1:SUBLANES, :]
        c2 = car[SUBLANES - 2:SUBLANES - 1, :]
        prev1 = jnp.where(rowid == 0, c1, r1)
        prev2 = jnp.where(rowid == 0, c2, jnp.where(rowid == 1, c1, r2))
        carry_s[j] = tail
        a_ref[...] = tail
    cw = cw_ref[...]
    ac = cw[0:1, :] * prev2 + cw[1:2, :] * prev1 + cw[2:3, :] * a + cb_ref[...]
    yv = 0.5 * ac * (1.0 + lax.erf(ac * (2.0 ** -0.5))) * g
    acc_s[...] += _dot(yv.astype(BF16), weight(wd_ref, wdo_ref))

    @pl.when(j == pl.num_programs(1) - 1)
    def _():
        def body(r, carry):
            r0 = pl.multiple_of(r * rc, rc)
            x2 = x_ref[pl.ds(r0, rc), :] + _rows(g2_ref, r0, rc) * acc_s[pl.ds(r0, rc), :]
            ms = jnp.mean(x2 * x2, axis=-1, keepdims=True)
            y_ref[pl.ds(r0, rc), :] = x2 * lax.rsqrt(ms + EPS) * fw_ref[...]
            return carry
        lax.fori_loop(0, tm // rc, body, 0)


def _ffn_call(x, mod, w_up, w_down, conv_w, conv_b, final_w, tm, tf, tiles_per_seq, hist=None):
    m, d = x.shape
    f = w_down.shape[0]
    nj = f // tf
    grouped = hist is not None
    emit_bf16 = not isinstance(w_up, tuple)
    kern = functools.partial(_ffn_kernel, tm=tm, rc=min(tm, 256), tiles_per_seq=tiles_per_seq, grouped=grouped,
                             emit_bf16=emit_bf16)
    half_spec = pl.BlockSpec((d, tf), lambda i, j: (0, j))
    down_spec = pl.BlockSpec((tf, d), lambda i, j: (j, 0))
    if emit_bf16:
        assert m == tm
        up_specs = [half_spec, pl.BlockSpec((d, tf), lambda i, j: (0, nj + j))]
        up_args = [w_up, w_up]
    else:
        up_specs = [half_spec, half_spec]
        up_args = list(w_up)
    in_specs = [pl.BlockSpec((tm, d), lambda i, j: (i, 0)),
                _mod_spec(mod, tm, 3, tiles_per_seq),
                _mod_spec(mod, tm, 4, tiles_per_seq),
                _mod_spec(mod, tm, 5, tiles_per_seq)] + up_specs + [
                down_spec,
                pl.BlockSpec((CONV_W, tf), lambda i, j: (0, j)),
                pl.BlockSpec((1, tf), lambda i, j: (0, j)),
                pl.BlockSpec((1, d), lambda i, j: (0, 0))]
    args = [x, mod, mod, mod] + up_args + [w_down, conv_w, conv_b, final_w]
    scratch = [pltpu.VMEM((tm, d), BF16), pltpu.VMEM((tm, d), F32)]
    if grouped:
        in_specs += [pl.BlockSpec((tm // GROUP, tf), lambda i, j: (i, j))] * 2
        args += list(hist)
        scratch.append(pltpu.VMEM((tf // LANES, tm, LANES), F32))
        a_specs = [pl.BlockSpec((tm // GROUP, tf), lambda i, j: (i, j))] * 2
        a_shapes = [jax.ShapeDtypeStruct((m // GROUP, f), F32)] * 2
    else:
        scratch.append(pltpu.VMEM((nj, SUBLANES, tf), F32))
        a_specs = [pl.BlockSpec((SUBLANES, tf), lambda i, j: (i, j))]
        a_shapes = [jax.ShapeDtypeStruct((m // tm * SUBLANES, f), F32)]
    if emit_bf16:
        a_specs += [half_spec, half_spec, down_spec]
        a_shapes += [jax.ShapeDtypeStruct((d, f), BF16), jax.ShapeDtypeStruct((d, f), BF16),
                     jax.ShapeDtypeStruct((f, d), BF16)]
    return pl.pallas_call(
        kern,
        grid=(m // tm, nj),
        in_specs=in_specs,
        out_specs=[pl.BlockSpec((tm, d), lambda i, j: (i, 0))] + a_specs,
        out_shape=[jax.ShapeDtypeStruct((m, d), F32)] + a_shapes,
        scratch_shapes=scratch,
        compiler_params=_cparams("arbitrary", "arbitrary"),
        name="ffn",
    )(*args)


def kernel(x_prompt, x_sample, c_prompt, c_sample, state_hgrn, state_mlstm_C, state_mlstm_n, state_mlstm_m,
           state_conv, ada_w, ada_b, w_in, hg_lb_logits, hg_norm_w, ml_i_bias, ml_f_bias, ml_norm_w, w_out,
           conv_w, conv_b, w_up, w_down, final_norm_w):
    n_p, seq, d = x_prompt.shape
    n_s, dec_seq, _ = x_sample.shape
    depth, _, hg_heads, hg_k, hg_v = state_hgrn.shape
    _, _, ml_heads, ml_qk, ml_v = state_mlstm_C.shape
    f = w_down.shape[1]
    assert depth == 1 and dec_seq == GROUP and hg_k == hg_v == LANES and 2 * ml_heads <= LANES
    assert hg_lb_logits.shape[0] == 2

    hg_w = hg_heads * hg_k
    gate0 = 4 * hg_w + 2 * ml_heads * ml_qk + ml_heads * ml_v
    gate1 = gate0 + 2 * ml_heads
    wt = jnp.swapaxes(w_in[0], 0, 1)
    col_mq = 4 * hg_w
    col_mk = col_mq + ml_heads * ml_qk
    col_mv = col_mk + ml_heads * ml_qk
    col_mo = col_mv + ml_heads * ml_v
    col_ga = col_mo + ml_heads * ml_v
    col_gb = col_ga + d
    assert col_gb + d == wt.shape[0] - (gate1 - gate0)
    gate_bias = jnp.pad(jnp.concatenate([ml_i_bias[0], ml_f_bias[0]]), (0, LANES - 2 * ml_heads))[None, :]

    m_s_rows = n_s * GROUP
    c_all = jnp.concatenate([c_sample, c_prompt], axis=0)
    c_all = jnp.pad(c_all, ((0, (-c_all.shape[0]) % SUBLANES), (0, 0)))
    mod_s = _mod_call(c_all, n_s, ada_w[0], ada_b)
    mod_p = mod_s[m_s_rows:m_s_rows + n_p][:, None, :]

    w_out_b = w_out[0].astype(BF16)
    hg_nw = hg_norm_w
    ml_nw = ml_norm_w
    fw = final_norm_w[None, :]
    tm_o = 512

    xs = x_sample.reshape(n_s * GROUP, d)
    proj_s, gates_s, *wt_b = _inproj_call(xs, mod_s, wt, gate0, gate1 - gate0, m_s_rows, 1024, 1)
    ya_s, hg_s = _hgrn_s_call(proj_s, hg_lb_logits, hg_nw, state_hgrn[0], hg_heads, hg_k, 0, col_ga // hg_k, 128)
    m_rows = jnp.pad(jnp.repeat(state_mlstm_m[0], GROUP, axis=0), ((0, 0), (0, LANES - ml_heads)))
    n_hb = jnp.repeat(jnp.transpose(state_mlstm_n[0], (1, 0, 2)), GROUP, axis=1)
    mg_s, c_s, nn_s, m_s = _mlstm_s_call(proj_s, ya_s, gates_s, gate_bias, m_rows, ml_nw, state_mlstm_C[0], n_hb,
                                         ml_heads, ml_qk, ml_v, col_mq, col_mk, col_mv, col_mo, col_gb, 64)
    x1_s = _outproj_call(mg_s, xs, mod_s, w_out_b, tm_o, 1)
    hist = (state_conv[0][:, 0], state_conv[0][:, 1])
    y_s, cv2, cv3, wa_b, wg_b, wd_b = _ffn_call(x1_s, mod_s, w_up[0], w_down[0], conv_w[0], conv_b, fw,
                                                m_s_rows, 256, 1, hist)
    cv_s = jnp.stack([cv2, cv3], axis=1)

    xp = x_prompt.reshape(n_p * seq, d)
    tm_p = 1024
    proj_p, gates_p = _inproj_call(xp, mod_p, tuple(wt_b), gate0, gate1 - gate0, tm_p, 2048, seq // tm_p)
    ya_p, hg_p = _hgrn_p_call(proj_p, hg_lb_logits, hg_nw, n_p, seq, hg_heads, hg_k, 0, col_ga // hg_k, 1024, 4)
    mg_p, c_p, nn_p, m_p = _mlstm_p_call(proj_p, ya_p, gates_p, gate_bias, ml_nw, n_p, seq, ml_heads, ml_qk, ml_v,
                                         col_mq, col_mk, col_mv, col_mo, col_gb, 2 * ML_CHUNK)
    x1_p = _outproj_call(mg_p, xp, mod_p, w_out_b, tm_o, seq // tm_o)
    tm_f = 512
    tiles = seq // tm_f
    y_p, atail = _ffn_call(x1_p, mod_p, (wa_b, wg_b), wd_b, conv_w[0], conv_b, fw, tm_f, 512, tiles)
    cv_p = atail.reshape(n_p, tiles, SUBLANES, f)[:, tiles - 1, SUBLANES - (CONV_W - 1):]

    return (y_p.reshape(n_p, seq, d), y_s.reshape(n_s, GROUP, d),
            hg_p[None], hg_s[None],
            c_p[None], c_s[None],
            nn_p.reshape(1, n_p, ml_heads, ml_qk), jnp.transpose(nn_s[:, GROUP - 1::GROUP], (1, 0, 2))[None],
            m_p[:, :, 0, 0][None], jnp.transpose(m_s[:, ::GROUP, 0])[None],
            cv_p[None], cv_s[None])
```

```python
import functools

import numpy as np
import jax
import jax.numpy as jnp
from jax import lax
from jax.experimental import pallas as pl
from jax.experimental.pallas import tpu as pltpu

F32 = jnp.float32
BF16 = jnp.bfloat16

EPS = 1e-6
CONV_W = 3
LANES = 128
SUBLANES = 8
VMEM_LIMIT_BYTES = 58 * 1024 * 1024

MXU_DEPTH = 256
HG_CHUNK = 64
HG_SAFE_EXP = 60.0
HG_LEVELS = (32, 16, 8, 4, 2, 1)
ML_CHUNK = 128
ML_HEADS_INTERLEAVED = 8
GROUP = 4


def _cparams(*sem):
    return pltpu.CompilerParams(dimension_semantics=sem, vmem_limit_bytes=VMEM_LIMIT_BYTES)


def _dot(a, b):
    return jnp.dot(a, b, preferred_element_type=F32)


def _dot_nt(a, b):
    return lax.dot_general(a, b, (((1,), (1,)), ((), ())), preferred_element_type=F32)


def _dot_tn(a, b):
    return lax.dot_general(a, b, (((0,), (0,)), ((), ())), preferred_element_type=F32)


def _split3(x):
    x1 = x.astype(BF16)
    r1 = x - x1.astype(F32)
    x2 = r1.astype(BF16)
    x3 = (r1 - x2.astype(F32)).astype(BF16)
    return x1, x2, x3


def _exact_left_mul(w, x):
    x1, x2, x3 = _split3(x)
    return _dot(w, x1) + _dot(w, x2) + _dot(w, x3)


def _exact_left_mul3(w3, x):
    return _dot(w3, jnp.concatenate(_split3(x), axis=0))


def _rows(ref, r0, n):
    if ref.shape[0] == 1:
        return ref[...]
    return ref[pl.ds(r0, n), :]


def _norm_mod(x, sc, sh):
    ms = jnp.mean(x * x, axis=-1, keepdims=True)
    return x * lax.rsqrt(ms + EPS) * (1.0 + sc) + sh


def _head_norm(o, w):
    return o * lax.rsqrt(jnp.mean(o * o, axis=-1, keepdims=True) + EPS) * w


def _lower_bound(lg):
    l0, l1 = lg[0:1, :], lg[1:2, :]
    m = jnp.maximum(l0, l1)
    e0, e1 = jnp.exp(l0 - m), jnp.exp(l1 - m)
    return e0 / (e0 + e1)


def _group_last(x, t, up):
    return jnp.where(t == 3, x, jnp.where(t == 2, up(x, 1), jnp.where(t == 1, up(x, 2), up(x, 3))))


def _mod_kernel(c_ref, w_ref, b_ref, o_ref, rep_s, *, n_rep):
    c = c_ref[...]
    s = (c * jax.nn.sigmoid(c)).astype(BF16)
    res = _dot(s, w_ref[...].astype(BF16)) + b_ref[...]
    for k in range(res.shape[1] // LANES):
        slab = res[:, k * LANES:(k + 1) * LANES]
        for t in range(GROUP):
            rep_s[k, pl.ds(t, n_rep, stride=GROUP), :] = slab[:n_rep]
        rep_s[k, n_rep * GROUP:, :] = slab[n_rep:]
    for k in range(res.shape[1] // LANES):
        o_ref[:, k * LANES:(k + 1) * LANES] = rep_s[k]


def _mod_call(c_all, n_rep, ada_w, ada_b):
    mp, d = c_all.shape
    n = ada_w.shape[1]
    tn = 2048
    mo = n_rep * GROUP + (mp - n_rep)
    return pl.pallas_call(
        functools.partial(_mod_kernel, n_rep=n_rep),
        grid=(n // tn,),
        in_specs=[pl.BlockSpec((mp, d), lambda j: (0, 0)),
                  pl.BlockSpec((d, tn), lambda j: (0, j)),
                  pl.BlockSpec((1, tn), lambda j: (0, j))],
        out_specs=pl.BlockSpec((mo, tn), lambda j: (0, j)),
        out_shape=jax.ShapeDtypeStruct((mo, n), F32),
        scratch_shapes=[pltpu.VMEM((tn // LANES, mo, LANES), F32)],
        compiler_params=_cparams("arbitrary"),
        name="mod",
    )(c_all, ada_w, ada_b)


def _inproj_kernel(*refs, tm, rc, emit_bf16):
    if emit_bf16:
        x_ref, sh_ref, sc_ref, w_ref, wg_ref, o_ref, og_ref, wo_ref, wgo_ref, h_ref = refs
    else:
        x_ref, sh_ref, sc_ref, w_ref, wg_ref, o_ref, og_ref, h_ref = refs
        wo_ref = wgo_ref = None
    j = pl.program_id(1)

    def weight(src, dst):
        w = src[...]
        if emit_bf16:
            w = w.astype(BF16)
            dst[...] = w
        return w

    @pl.when(j == 0)
    def _():
        wg = weight(wg_ref, wgo_ref)

        def body(r, carry):
            r0 = pl.multiple_of(r * rc, rc)
            h = _norm_mod(x_ref[pl.ds(r0, rc), :], _rows(sc_ref, r0, rc), _rows(sh_ref, r0, rc))
            hb = h.astype(BF16)
            h_ref[pl.ds(r0, rc), :] = hb
            gg = _dot_nt(hb, wg)
            og_ref[pl.ds(r0, rc), :] = jnp.concatenate(
                [gg, jnp.zeros((rc, og_ref.shape[1] - gg.shape[1]), F32)], axis=1)
            return carry
        lax.fori_loop(0, tm // rc, body, 0)

    o_ref[...] = _dot_nt(h_ref[...], weight(w_ref, wo_ref))


def _mod_spec(mod, tm, col, tiles_per_seq):
    d = mod.shape[-1] // 6
    if mod.ndim == 3:
        return pl.BlockSpec((None, 1, d), lambda i, *_: (i // tiles_per_seq, 0, col))
    return pl.BlockSpec((tm, d), lambda i, *_: (i, col))


def _inproj_call(x, mod, w, gate0, n_gate, tm, tn, tiles_per_seq):
    m, d = x.shape
    emit_bf16 = not isinstance(w, tuple)
    n = (w.shape[0] - n_gate) if emit_bf16 else w[0].shape[0]
    assert n_gate == 2 * SUBLANES and gate0 % tn == 0 and n % tn == 0
    na = gate0 // tn
    tile = pl.BlockSpec((tn, d), lambda i, j: (j, 0))
    seg_g = pl.BlockSpec((n_gate, d), lambda i, j: (0, 0))
    out_specs = [pl.BlockSpec((tm, tn), lambda i, j: (i, j)), pl.BlockSpec((tm, LANES), lambda i, j: (i, 0))]
    out_shape = [jax.ShapeDtypeStruct((m, n), F32), jax.ShapeDtypeStruct((m, LANES), F32)]
    if emit_bf16:
        assert m == tm
        w_specs = [pl.BlockSpec((pl.Element(tn), pl.Element(d)),
                                lambda i, j: (pl.multiple_of(j * tn + jnp.where(j < na, 0, n_gate), SUBLANES), 0)),
                   pl.BlockSpec((n_gate, d), lambda i, j: (gate0 // n_gate, 0))]
        w_args = [w, w]
        out_specs += [tile, seg_g]
        out_shape += [jax.ShapeDtypeStruct((n, d), BF16), jax.ShapeDtypeStruct((n_gate, d), BF16)]
    else:
        w_specs = [tile, seg_g]
        w_args = list(w)
    kern = functools.partial(_inproj_kernel, tm=tm, rc=256, emit_bf16=emit_bf16)
    return pl.pallas_call(
        kern,
        grid=(m // tm, n // tn),
        in_specs=[pl.BlockSpec((tm, d), lambda i, j: (i, 0)),
                  _mod_spec(mod, tm, 0, tiles_per_seq),
                  _mod_spec(mod, tm, 1, tiles_per_seq)] + w_specs,
        out_specs=out_specs,
        out_shape=out_shape,
        scratch_shapes=[pltpu.VMEM((tm, d), BF16)],
        compiler_params=_cparams("arbitrary", "arbitrary"),
        name="inproj",
    )(x, mod, mod, *w_args)


def _hgrn_constants():
    L = HG_CHUNK
    tri = np.tril(np.ones((L, L), np.float32))
    t = np.arange(L)
    blocks = []
    masks = []
    for h in HG_LEVELS:
        mid = (t // (2 * h)) * (2 * h) + h - 1
        blocks.append(tri - tri[mid])
        masks.append((t[:, None] // (2 * h) == t[None, :] // (2 * h)).astype(np.float32))
    blocks.append(tri[L - 1][None, :] - tri)
    masks.append(np.eye(L, dtype=np.float32))
    return jnp.asarray(np.concatenate(blocks, 0), BF16), jnp.asarray(np.stack(masks, 0), F32)


def _hgrn_gates(hq, z, lb, kdim):
    omlb = 1.0 - lb
    sz = jax.nn.sigmoid(z)
    logf = jnp.log(lb + omlb * sz)
    kk = omlb * (1.0 - sz)
    q = hq * jax.nn.sigmoid(hq) * (kdim ** -0.5)
    return q, kk, logf


def _hgrn_p_kernel(hq_ref, hf_ref, hi_ref, hg_ref, ga_ref, lg_ref, nw_ref, trib_ref, wall_ref, mask_ref,
                   ya_ref, so_ref, st_ref, *, n_chunks, kdim):
    L = HG_CHUNK
    c = pl.program_id(2)
    hps = hq_ref.shape[1] // kdim
    tb = trib_ref.shape[0]
    heads_chunks = [(hd, ci) for ci in range(n_chunks) for hd in range(hps)]
    rows = lambda ci: slice(ci * L, (ci + 1) * L)
    lanes = lambda hd: slice(hd * kdim, (hd + 1) * kdim)

    @pl.when(c == 0)
    def _():
        st_ref[...] = jnp.zeros_like(st_ref)

    lb = _lower_bound(lg_ref[...])
    nw = nw_ref[...]
    q, kk, logf = _hgrn_gates(hq_ref[...], hf_ref[...], lb, kdim)
    trib = trib_ref[...]
    b = jnp.concatenate([_exact_left_mul3(trib, logf[r:r + tb]) for r in range(0, n_chunks * L, tb)], axis=0)
    worst = -jnp.min(jnp.concatenate([b[r:r + 1, :] for r in range(L - 1, n_chunks * L, L)], axis=0))

    def finish(hd, ci, o):
        hg = hg_ref[rows(ci), lanes(hd)]
        ya = _head_norm(o, nw[:, lanes(hd)]) * (hg * jax.nn.sigmoid(hg))
        ya_ref[rows(ci), lanes(hd)] = jax.nn.sigmoid(ga_ref[rows(ci), lanes(hd)]) * ya

    @pl.when(worst <= HG_SAFE_EXP)
    def _():
        eb = jnp.exp(b)
        qt = (q * eb).astype(BF16)
        kn = kk * (1.0 / eb)
        knb = kn.astype(BF16)
        row = lax.broadcasted_iota(jnp.int32, (L, L), 0)
        colid = lax.broadcasted_iota(jnp.int32, (L, L), 1)
        causal = colid <= row
        o_intra, incr, dec = {}, {}, {}
        for hd, ci in heads_chunks:
            vb = hi_ref[rows(ci), lanes(hd)].astype(BF16)
            a = jnp.where(causal, _dot_nt(qt[rows(ci), lanes(hd)], knb[rows(ci), lanes(hd)]), 0.0)
            o_intra[hd, ci] = _dot(a.astype(BF16), vb)
            dec[hd, ci] = eb[(ci + 1) * L - 1:(ci + 1) * L, lanes(hd)]
            incr[hd, ci] = _dot_tn(vb, (kn[rows(ci), lanes(hd)] * dec[hd, ci]).astype(BF16))
        sts = {(hd, 0): st_ref[hd] for hd in range(hps)}
        for hd, ci in heads_chunks:
            sts[hd, ci + 1] = sts[hd, ci] * dec[hd, ci] + incr[hd, ci]
        for hd in range(hps):
            st_ref[hd] = sts[hd, n_chunks]
        for hd, ci in heads_chunks:
            finish(hd, ci, o_intra[hd, ci] + _dot_nt(qt[rows(ci), lanes(hd)], sts[hd, ci].astype(BF16)))

    @pl.when(jnp.logical_not(worst <= HG_SAFE_EXP))
    def _():
        wall = wall_ref[...]
        rowid = lax.broadcasted_iota(jnp.int32, (L, kdim), 0)
        n_lv = len(HG_LEVELS)
        for hd, ci in heads_chunks:
            qc, kc, bc = q[rows(ci), lanes(hd)], kk[rows(ci), lanes(hd)], b[rows(ci), lanes(hd)]
            vb = hi_ref[rows(ci), lanes(hd)].astype(BF16)
            d = _exact_left_mul(wall, logf[rows(ci), lanes(hd)])
            a = mask_ref[n_lv] * _dot_nt(qc.astype(BF16), kc.astype(BF16))
            for li, h in enumerate(HG_LEVELS):
                e = jnp.exp(-jnp.abs(d[li * L:(li + 1) * L]))
                second = (rowid & h) != 0
                p = jnp.where(second, qc, kc) * e
                qh = jnp.where(second, p, 0.0).astype(BF16)
                kh = jnp.where(second, 0.0, p).astype(BF16)
                a = a + mask_ref[li] * _dot_nt(qh, kh)
            qt = (qc * jnp.exp(bc)).astype(BF16)
            kt = (kc * jnp.exp(d[n_lv * L:(n_lv + 1) * L])).astype(BF16)
            st = st_ref[hd]
            finish(hd, ci, _dot(a.astype(BF16), vb) + _dot_nt(qt, st.astype(BF16)))
            st_ref[hd] = st * jnp.exp(bc[L - 1:L, :]) + _dot_tn(vb, kt)

    @pl.when(c == pl.num_programs(2) - 1)
    def _():
        for hd in range(hps):
            so_ref[hd] = st_ref[hd].T


def _hgrn_p_call(proj, lb_logits, norm_w, n_seq, seq_len, heads, kdim, col0, col_ga, rows_per_step, hps):
    wall, masks = _hgrn_constants()
    per_mat = min(rows_per_step, MXU_DEPTH) // HG_CHUNK
    trib = np.kron(np.eye(per_mat, dtype=np.float32), np.tril(np.ones((HG_CHUNK, HG_CHUNK), np.float32)))
    trib = jnp.asarray(np.concatenate([trib, trib, trib], axis=1), BF16)
    nc = seq_len // rows_per_step
    assert heads % hps == 0 and col0 % hps == 0 and col_ga % hps == 0
    kern = functools.partial(_hgrn_p_kernel, n_chunks=rows_per_step // HG_CHUNK, kdim=kdim)
    wide = hps * kdim

    def col(first):
        return pl.BlockSpec((rows_per_step, wide), lambda b, h, c: (b * nc + c, first // hps + h))

    return pl.pallas_call(
        kern,
        grid=(n_seq, heads // hps, nc),
        in_specs=[col(col0), col(col0 + heads), col(col0 + 2 * heads), col(col0 + 3 * heads), col(col_ga),
                  pl.BlockSpec((2, wide), lambda b, h, c: (0, h)),
                  pl.BlockSpec((1, wide), lambda b, h, c: (0, h)),
                  pl.BlockSpec(trib.shape, lambda b, h, c: (0, 0)),
                  pl.BlockSpec(wall.shape, lambda b, h, c: (0, 0)),
                  pl.BlockSpec(masks.shape, lambda b, h, c: (0, 0, 0))],
        out_specs=[pl.BlockSpec((rows_per_step, wide), lambda b, h, c: (b * nc + c, h)),
                   pl.BlockSpec((None, hps, kdim, kdim), lambda b, h, c: (b, h, 0, 0))],
        out_shape=[jax.ShapeDtypeStruct((n_seq * seq_len, heads * kdim), F32),
                   jax.ShapeDtypeStruct((n_seq, heads, kdim, kdim), F32)],
        scratch_shapes=[pltpu.VMEM((hps, kdim, kdim), F32)],
        compiler_params=_cparams("arbitrary", "arbitrary", "arbitrary"),
        name="hgrn_prompt",
    )(proj, proj, proj, proj, proj, lb_logits, norm_w, trib, wall, masks)


def _hgrn_s_kernel(hq_ref, hf_ref, hi_ref, hg_ref, ga_ref, lg_ref, nw_ref, s_ref, ya_ref, so_ref, *, n_pairs):
    kdim = hq_ref.shape[1]
    rows = n_pairs * SUBLANES
    lb = _lower_bound(lg_ref[...])
    rowid = lax.broadcasted_iota(jnp.int32, (rows, kdim), 0)
    t = rowid & (GROUP - 1)
    first = (rowid & GROUP) == 0
    first8 = lax.broadcasted_iota(jnp.int32, (SUBLANES, kdim), 0) < GROUP
    down = lambda y, j: pltpu.roll(y, j, 0)
    up = lambda y, j: pltpu.roll(y, rows - j, 0)

    q, kk, logf = _hgrn_gates(hq_ref[...], hf_ref[...], lb, kdim)
    v = hi_ref[...]
    b = logf
    for dlt in range(1, GROUP):
        b = b + jnp.where(t >= dlt, down(logf, dlt), 0.0)
    o = jnp.sum(q * kk, axis=1, keepdims=True) * v
    for dlt in range(1, GROUP):
        x = q * down(kk, dlt) * jnp.exp(b - down(b, dlt))
        a = jnp.sum(jnp.where(t >= dlt, x, 0.0), axis=1, keepdims=True)
        o = o + a * down(v, dlt)
    b_last = _group_last(b, t, up)
    qt = q * jnp.exp(b)
    kt = kk * jnp.exp(b_last - b)
    d1, d2, d3 = _split3(jnp.exp(b_last))
    swap = lambda y: jnp.where(first, up(y.astype(F32), GROUP), down(y.astype(F32), GROUP))
    dsplit = jnp.where(t == 0, swap(d1), jnp.where(t == 1, swap(d2), jnp.where(t == 2, swap(d3), 0.0)))
    ones = jnp.where(t <= 2, 1.0, 0.0)
    lhs = (jnp.where(first, kt, dsplit), jnp.where(first, dsplit, kt))
    rhs = (jnp.concatenate([jnp.where(first, v, 0.0), jnp.where(first, 0.0, ones)], axis=1),
           jnp.concatenate([jnp.where(first, 0.0, v), jnp.where(first, ones, 0.0)], axis=1))
    o_inter = []
    for p in range(n_pairs):
        sl = slice(p * SUBLANES, (p + 1) * SUBLANES)
        qb = qt[sl].astype(BF16)
        parts = []
        for half in range(2):
            s0 = s_ref[2 * p + half]
            upd = _dot_tn(lhs[half][sl].astype(BF16), rhs[half][sl].astype(BF16))
            so_ref[2 * p + half] = s0 * upd[:, kdim:] + upd[:, :kdim]
            parts.append(_dot(qb, s0.astype(BF16)))
        o_inter.append(jnp.where(first8, parts[0], parts[1]))
    o = o + jnp.concatenate(o_inter, axis=0)
    hg = hg_ref[...]
    ya_ref[...] = jax.nn.sigmoid(ga_ref[...]) * (_head_norm(o, nw_ref[...]) * (hg * jax.nn.sigmoid(hg)))


def _hgrn_s_call(proj, lb_logits, norm_w, state, heads, kdim, col0, col_ga, seqs_per_step):
    n_seq = state.shape[0]
    rows = seqs_per_step * GROUP
    kern = functools.partial(_hgrn_s_kernel, n_pairs=seqs_per_step // 2)

    def col(k):
        return pl.BlockSpec((rows, kdim), lambda i, h: (i, col0 + k * heads + h))

    st_spec = pl.BlockSpec((seqs_per_step, None, kdim, kdim), lambda i, h: (i, h, 0, 0))
    return pl.pallas_call(
        kern,
        grid=(n_seq // seqs_per_step, heads),
        in_specs=[col(0), col(1), col(2), col(3),
                  pl.BlockSpec((rows, kdim), lambda i, h: (i, col_ga + h)),
                  pl.BlockSpec((2, kdim), lambda i, h: (0, h)),
                  pl.BlockSpec((1, kdim), lambda i, h: (0, h)),
                  st_spec],
        out_specs=[pl.BlockSpec((rows, kdim), lambda i, h: (i, h)), st_spec],
        out_shape=[jax.ShapeDtypeStruct((n_seq * GROUP, heads * kdim), F32),
                   jax.ShapeDtypeStruct(state.shape, F32)],
        compiler_params=_cparams("arbitrary", "arbitrary"),
        name="hgrn_sample",
    )(proj, proj, proj, proj, proj, lb_logits, norm_w, state)


def _lane_pick(x, lane, idx):
    return jnp.broadcast_to(jnp.sum(jnp.where(lane == idx, x, 0.0), axis=1, keepdims=True), x.shape)


def _interleave(chains):
    live = list(chains)
    while live:
        still = []
        for g in live:
            try:
                next(g)
                still.append(g)
            except StopIteration:
                pass
        live = still


def _mlstm_p_kernel(q_ref, k_ref, v_ref, og_ref, gb_ref, ya_ref, g_ref, bias_ref, nw_ref, tri_ref,
                    mg_ref, co_ref, no_ref, mo_ref, c_s, n_s, m_s, *, heads, qk, vd):
    L = ML_CHUNK
    c = pl.program_id(1)

    @pl.when(c == 0)
    def _():
        c_s[...] = jnp.zeros_like(c_s)
        n_s[...] = jnp.zeros_like(n_s)
        m_s[...] = jnp.zeros_like(m_s)

    lane = lax.broadcasted_iota(jnp.int32, (L, LANES), 1)
    row = lax.broadcasted_iota(jnp.int32, (L, LANES), 0)

    def head(hd, r, g, b_all):
        qs = slice(hd * qk, (hd + 1) * qk)
        vs = slice(hd * vd, (hd + 1) * vd)
        ig = _lane_pick(g, lane, hd)
        b = _lane_pick(b_all, lane, heads + hd)
        m_prev = m_s[hd]
        gs = (ig - b).T
        yield
        dm = jnp.where(lane <= row, b + gs, -jnp.inf)
        mt = jnp.maximum(b + m_prev, jnp.max(dm, axis=1, keepdims=True))
        inter = jnp.exp(b + m_prev - mt)
        q = q_ref[r, qs] * (qk ** -0.5)
        k = k_ref[r, qs]
        vb = v_ref[r, vs].astype(BF16)
        qb = q.astype(BF16)
        sc = _dot_nt(qb, k.astype(BF16)) * jnp.exp(dm - mt)
        yield
        c0 = c_s[hd]
        n0 = n_s[hd]
        num = inter[:, 0:1] * _dot(qb, c0.astype(BF16)) + _dot(sc.astype(BF16), vb)
        den = inter[:, 0:1] * jnp.sum(q * n0, axis=1, keepdims=True) + jnp.sum(sc, axis=1, keepdims=True)
        yield
        hh = num / jnp.maximum(jnp.abs(den), jnp.exp(-mt[:, 0:1]))
        yb = _head_norm(hh, nw_ref[:, vs]) * jax.nn.sigmoid(og_ref[r, vs])
        mg_ref[r, vs] = (ya_ref[r, vs] + jax.nn.sigmoid(gb_ref[r, vs]) * yb).astype(mg_ref.dtype)
        yield
        m_last = mt[L - 1:L, :]
        b_last = b[L - 1:L, :]
        dec = jnp.exp(b_last + m_prev - m_last)
        kw = jnp.exp(b_last - b + ig - m_last) * k
        c_s[hd] = jnp.concatenate([dec, dec], axis=1) * c0 + _dot_tn(kw.astype(BF16), vb)
        n_s[hd] = dec * n0 + jnp.sum(kw, axis=0, keepdims=True)
        m_s[hd] = m_last

    for r0 in range(0, q_ref.shape[0], L):
        r = slice(r0, r0 + L)
        g = g_ref[r, :] + bias_ref[...]
        b_all = _exact_left_mul(tri_ref[...], jax.nn.log_sigmoid(g))
        for h0 in range(0, heads, ML_HEADS_INTERLEAVED):
            _interleave([head(hd, r, g, b_all) for hd in range(h0, min(h0 + ML_HEADS_INTERLEAVED, heads))])

    @pl.when(c == pl.num_programs(1) - 1)
    def _():
        co_ref[...] = c_s[...]
        no_ref[...] = n_s[...]
        mo_ref[...] = m_s[...]


def _mlstm_p_call(proj, ya, gates, bias, norm_w, n_seq, seq_len, heads, qk, vd, colq, colk, colv, colo, colg,
                  rows_per_step):
    assert qk == ML_CHUNK and qk == LANES and rows_per_step % ML_CHUNK == 0
    L = rows_per_step
    nc = seq_len // L
    qw, vw = heads * qk, heads * vd
    assert colq % qw == 0 and colk % qw == 0 and colv % vw == 0 and colo % vw == 0 and colg % vw == 0
    tri = jnp.asarray(np.tril(np.ones((ML_CHUNK, ML_CHUNK), np.float32)), BF16)
    kern = functools.partial(_mlstm_p_kernel, heads=heads, qk=qk, vd=vd)
    return pl.pallas_call(
        kern,
        grid=(n_seq, nc),
        in_specs=[pl.BlockSpec((L, qw), lambda b, c: (b * nc + c, colq // qw)),
                  pl.BlockSpec((L, qw), lambda b, c: (b * nc + c, colk // qw)),
                  pl.BlockSpec((L, vw), lambda b, c: (b * nc + c, colv // vw)),
                  pl.BlockSpec((L, vw), lambda b, c: (b * nc + c, colo // vw)),
                  pl.BlockSpec((L, vw), lambda b, c: (b * nc + c, colg // vw)),
                  pl.BlockSpec((L, vw), lambda b, c: (b * nc + c, 0)),
                  pl.BlockSpec((L, LANES), lambda b, c: (b * nc + c, 0)),
                  pl.BlockSpec((1, LANES), lambda b, c: (0, 0)),
                  pl.BlockSpec((1, vw), lambda b, c: (0, 0)),
                  pl.BlockSpec(tri.shape, lambda b, c: (0, 0))],
        out_specs=[pl.BlockSpec((L, vw), lambda b, c: (b * nc + c, 0)),
                   pl.BlockSpec((None, heads, qk, vd), lambda b, c: (b, 0, 0, 0)),
                   pl.BlockSpec((None, heads, 1, qk), lambda b, c: (b, 0, 0, 0)),
                   pl.BlockSpec((None, heads, 1, LANES), lambda b, c: (b, 0, 0, 0))],
        out_shape=[jax.ShapeDtypeStruct((n_seq * seq_len, vw), BF16),
                   jax.ShapeDtypeStruct((n_seq, heads, qk, vd), F32),
                   jax.ShapeDtypeStruct((n_seq, heads, 1, qk), F32),
                   jax.ShapeDtypeStruct((n_seq, heads, 1, LANES), F32)],
        scratch_shapes=[pltpu.VMEM((heads, qk, vd), F32), pltpu.VMEM((heads, 1, qk), F32),
                        pltpu.VMEM((heads, 1, LANES), F32)],
        compiler_params=_cparams("arbitrary", "arbitrary"),
        name="mlstm_prompt",
    )(proj, proj, proj, proj, proj, ya, gates, bias, norm_w, tri)


def _mlstm_s_kernel(q_ref, k_ref, v_ref, og_ref, gb_ref, ya_ref, g_ref, bias_ref, m_ref, nw_ref, c_ref, n_ref,
                    mg_ref, co_ref, no_ref, mo_ref, *, heads, n_pairs):
    h = pl.program_id(1)
    qk = q_ref.shape[1]
    rows = n_pairs * SUBLANES
    lane = lax.broadcasted_iota(jnp.int32, (rows, LANES), 1)
    rowid = lax.broadcasted_iota(jnp.int32, (rows, LANES), 0)
    t = rowid & (GROUP - 1)
    first = (rowid & GROUP) == 0
    first8 = lax.broadcasted_iota(jnp.int32, (SUBLANES, 1), 0) < GROUP
    down = lambda y, j: pltpu.roll(y, j, 0)
    up = lambda y, j: pltpu.roll(y, rows - j, 0)

    g = g_ref[...] + bias_ref[...]
    ig = _lane_pick(g, lane, h)
    lf = _lane_pick(jax.nn.log_sigmoid(g), lane, heads + h)
    m_prev = _lane_pick(m_ref[...], lane, h)
    b = lf
    for dlt in range(1, GROUP):
        b = b + jnp.where(t >= dlt, down(lf, dlt), 0.0)
    q = q_ref[...] * (qk ** -0.5)
    k = k_ref[...]
    v = v_ref[...]
    dms = [ig] + [jnp.where(t >= dlt, b - down(b, dlt) + down(ig, dlt), -jnp.inf) for dlt in range(1, GROUP)]
    mt = jnp.maximum(b + m_prev, functools.reduce(jnp.maximum, dms))
    inter = jnp.exp(b + m_prev - mt)[:, 0:1]
    num = jnp.zeros(v.shape, F32)
    den = jnp.zeros((rows, 1), F32)
    for dlt in range(GROUP):
        kd = k if dlt == 0 else down(k, dlt)
        vd_ = v if dlt == 0 else down(v, dlt)
        s = jnp.sum(q * kd, axis=1, keepdims=True) * jnp.exp(dms[dlt] - mt)[:, 0:1]
        num = num + s * vd_
        den = den + s
    m_last = _group_last(mt, t, up)
    b_last = _group_last(b, t, up)
    dec = jnp.exp(b_last + m_prev - m_last)
    kw = jnp.exp(b_last - b + ig - m_last) * k
    n0 = n_ref[...]
    no_ref[...] = dec * n0 + kw + down(kw, 1) + down(kw, 2) + down(kw, 3)
    mo_ref[...] = m_last
    den = den + inter * jnp.sum(q * n0, axis=1, keepdims=True)

    kw_half = (jnp.where(first, kw, 0.0), jnp.where(first, 0.0, kw))
    dec2 = jnp.concatenate([dec, dec], axis=1)
    num_inter = []
    for p in range(n_pairs):
        sl = slice(p * SUBLANES, (p + 1) * SUBLANES)
        qb = q[sl].astype(BF16)
        vb = v[sl].astype(BF16)
        parts = []
        for half in range(2):
            c0 = c_ref[2 * p + half]
            r = p * SUBLANES + half * GROUP
            co_ref[2 * p + half] = dec2[r:r + 1, :] * c0 + _dot_tn(kw_half[half][sl].astype(BF16), vb)
            parts.append(_dot(qb, c0.astype(BF16)))
        num_inter.append(jnp.where(first8, parts[0], parts[1]))
    num = num + inter * jnp.concatenate(num_inter, axis=0)
    hh = num / jnp.maximum(jnp.abs(den), jnp.exp(-mt)[:, 0:1])
    yb = _head_norm(hh, nw_ref[...]) * jax.nn.sigmoid(og_ref[...])
    mg_ref[...] = (ya_ref[...] + jax.nn.sigmoid(gb_ref[...]) * yb).astype(mg_ref.dtype)


def _mlstm_s_call(proj, ya, gates, bias, m_rows, norm_w, c_state, n_state, heads, qk, vd,
                  colq, colk, colv, colo, colg, seqs_per_step):
    n_seq = c_state.shape[0]
    rows = seqs_per_step * GROUP
    kern = functools.partial(_mlstm_s_kernel, heads=heads, n_pairs=seqs_per_step // 2)
    c_spec = pl.BlockSpec((seqs_per_step, None, qk, vd), lambda i, h: (i, h, 0, 0))
    n_spec = pl.BlockSpec((None, rows, qk), lambda i, h: (h, i, 0))
    return pl.pallas_call(
        kern,
        grid=(n_seq // seqs_per_step, heads),
        in_specs=[pl.BlockSpec((rows, qk), lambda i, h: (i, colq // qk + h)),
                  pl.BlockSpec((rows, qk), lambda i, h: (i, colk // qk + h)),
                  pl.BlockSpec((rows, vd), lambda i, h: (i, colv // vd + h)),
                  pl.BlockSpec((rows, vd), lambda i, h: (i, colo // vd + h)),
                  pl.BlockSpec((rows, vd), lambda i, h: (i, colg // vd + h)),
                  pl.BlockSpec((rows, vd), lambda i, h: (i, h)),
                  pl.BlockSpec((rows, LANES), lambda i, h: (i, 0)),
                  pl.BlockSpec((1, LANES), lambda i, h: (0, 0)),
                  pl.BlockSpec((rows, LANES), lambda i, h: (i, 0)),
                  pl.BlockSpec((1, vd), lambda i, h: (0, h)),
                  c_spec, n_spec],
        out_specs=[pl.BlockSpec((rows, vd), lambda i, h: (i, h)),
                   c_spec, n_spec,
                   pl.BlockSpec((None, rows, LANES), lambda i, h: (h, i, 0))],
        out_shape=[jax.ShapeDtypeStruct((n_seq * GROUP, heads * vd), BF16),
                   jax.ShapeDtypeStruct(c_state.shape, F32),
                   jax.ShapeDtypeStruct(n_state.shape, F32),
                   jax.ShapeDtypeStruct((heads, n_seq * GROUP, LANES), F32)],
        compiler_params=_cparams("arbitrary", "arbitrary"),
        name="mlstm_sample",
    )(proj, proj, proj, proj, proj, ya, gates, bias, m_rows, norm_w, c_state, n_state)


def _outproj_kernel(mg_ref, x_ref, g1_ref, w_ref, o_ref):
    o_ref[...] = x_ref[...] + g1_ref[...] * _dot(mg_ref[...], w_ref[...])


def _outproj_call(merged, x, mod, w_out, tm, tiles_per_seq):
    m, d = x.shape
    blk = pl.BlockSpec((tm, d), lambda i: (i, 0))
    return pl.pallas_call(
        _outproj_kernel,
        grid=(m // tm,),
        in_specs=[blk, blk, _mod_spec(mod, tm, 2, tiles_per_seq), pl.BlockSpec((d, d), lambda i: (0, 0))],
        out_specs=blk,
        out_shape=jax.ShapeDtypeStruct((m, d), F32),
        compiler_params=_cparams("arbitrary"),
        name="outproj",
    )(merged, x, mod, w_out)


def _ffn_kernel(*refs, tm, rc, tiles_per_seq, grouped, emit_bf16):
    refs = iter(refs)
    x_ref, sh_ref, sc_ref, g2_ref, wa_ref, wg_ref, wd_ref, cw_ref, cb_ref, fw_ref = (next(refs) for _ in range(10))
    hist_refs = (next(refs), next(refs)) if grouped else None
    y_ref = next(refs)
    if grouped:
        cv2_ref, cv3_ref = next(refs), next(refs)
    else:
        a_ref = next(refs)
    wao_ref, wgo_ref, wdo_ref = (next(refs), next(refs), next(refs)) if emit_bf16 else (None, None, None)
    h_s, acc_s = next(refs), next(refs)
    a_s = carry_s = next(refs)
    i = pl.program_id(0)
    j = pl.program_id(1)

    def weight(src, dst):
        w = src[...]
        if emit_bf16:
            w = w.astype(BF16)
            dst[...] = w
        return w

    @pl.when(j == 0)
    def _():
        def body(r, carry):
            r0 = pl.multiple_of(r * rc, rc)
            h = _norm_mod(x_ref[pl.ds(r0, rc), :], _rows(sc_ref, r0, rc), _rows(sh_ref, r0, rc))
            h_s[pl.ds(r0, rc), :] = h.astype(BF16)
            return carry
        lax.fori_loop(0, tm // rc, body, 0)
        acc_s[...] = jnp.zeros_like(acc_s)

    hb = h_s[...]
    a = _dot(hb, weight(wa_ref, wao_ref))
    g = _dot(hb, weight(wg_ref, wgo_ref))
    rowid = lax.broadcasted_iota(jnp.int32, a.shape, 0)
    r1 = pltpu.roll(a, 1, 0)
    r2 = pltpu.roll(a, 2, 0)
    if grouped:
        t = rowid & (GROUP - 1)
        slabs = []
        for s in range(a.shape[1] // LANES):
            a_s[s] = jnp.zeros((tm, LANES), F32)
            for u, hist in enumerate(hist_refs):
                a_s[s, pl.ds(u, tm // GROUP, stride=GROUP), :] = hist[:, s * LANES:(s + 1) * LANES]
            slabs.append(a_s[s])
        p2 = jnp.concatenate(slabs, axis=1)
        prev1 = jnp.where(t == 0, pltpu.roll(p2, tm - 1, 0), r1)
        prev2 = jnp.where(t <= 1, p2, r2)
        for s in range(a.shape[1] // LANES):
            a_s[s] = a[:, s * LANES:(s + 1) * LANES]
        for s in range(a.shape[1] // LANES):
            cv2_ref[:, s * LANES:(s + 1) * LANES] = a_s[s, pl.ds(GROUP - 2, tm // GROUP, stride=GROUP), :]
            cv3_ref[:, s * LANES:(s + 1) * LANES] = a_s[s, pl.ds(GROUP - 1, tm // GROUP, stride=GROUP), :]
    else:
        tail = a[tm - SUBLANES:tm, :]
        car = jnp.where(i % tiles_per_seq == 0, 0.0, carry_s[j])
        c1 = car[SUBLANES - 1:SUBLANES, :]
        c2 = car[SUBLANES - 2:SUBLANES - 1, :]
        prev1 = jnp.where(rowid == 0, c1, r1)
        prev2 = jnp.where(rowid == 0, c2, jnp.where(rowid == 1, c1, r2))
        carry_s[j] = tail
        a_ref[...] = tail
    cw = cw_ref[...]
    ac = cw[0:1, :] * prev2 + cw[1:2, :] * prev1 + cw[2:3, :] * a + cb_ref[...]
    yv = 0.5 * ac * (1.0 + lax.erf(ac * (2.0 ** -0.5))) * g
    acc_s[...] += _dot(yv.astype(BF16), weight(wd_ref, wdo_ref))

    @pl.when(j == pl.num_programs(1) - 1)
    def _():
        def body(r, carry):
            r0 = pl.multiple_of(r * rc, rc)
            x2 = x_ref[pl.ds(r0, rc), :] + _rows(g2_ref, r0, rc) * acc_s[pl.ds(r0, rc), :]
            ms = jnp.mean(x2 * x2, axis=-1, keepdims=True)
            y_ref[pl.ds(r0, rc), :] = x2 * lax.rsqrt(ms + EPS) * fw_ref[...]
            return carry
        lax.fori_loop(0, tm // rc, body, 0)


def _ffn_call(x, mod, w_up, w_down, conv_w, conv_b, final_w, tm, tf, tiles_per_seq, hist=None):
    m, d = x.shape
    f = w_down.shape[0]
    nj = f // tf
    grouped = hist is not None
    emit_bf16 = not isinstance(w_up, tuple)
    kern = functools.partial(_ffn_kernel, tm=tm, rc=min(tm, 256), tiles_per_seq=tiles_per_seq, grouped=grouped,
                             emit_bf16=emit_bf16)
    half_spec = pl.BlockSpec((d, tf), lambda i, j: (0, j))
    down_spec = pl.BlockSpec((tf, d), lambda i, j: (j, 0))
    if emit_bf16:
        assert m == tm
        up_specs = [half_spec, pl.BlockSpec((d, tf), lambda i, j: (0, nj + j))]
        up_args = [w_up, w_up]
    else:
        up_specs = [half_spec, half_spec]
        up_args = list(w_up)
    in_specs = [pl.BlockSpec((tm, d), lambda i, j: (i, 0)),
                _mod_spec(mod, tm, 3, tiles_per_seq),
                _mod_spec(mod, tm, 4, tiles_per_seq),
                _mod_spec(mod, tm, 5, tiles_per_seq)] + up_specs + [
                down_spec,
                pl.BlockSpec((CONV_W, tf), lambda i, j: (0, j)),
                pl.BlockSpec((1, tf), lambda i, j: (0, j)),
                pl.BlockSpec((1, d), lambda i, j: (0, 0))]
    args = [x, mod, mod, mod] + up_args + [w_down, conv_w, conv_b, final_w]
    scratch = [pltpu.VMEM((tm, d), BF16), pltpu.VMEM((tm, d), F32)]
    if grouped:
        in_specs += [pl.BlockSpec((tm // GROUP, tf), lambda i, j: (i, j))] * 2
        args += list(hist)
        scratch.append(pltpu.VMEM((tf // LANES, tm, LANES), F32))
        a_specs = [pl.BlockSpec((tm // GROUP, tf), lambda i, j: (i, j))] * 2
        a_shapes = [jax.ShapeDtypeStruct((m // GROUP, f), F32)] * 2
    else:
        scratch.append(pltpu.VMEM((nj, SUBLANES, tf), F32))
        a_specs = [pl.BlockSpec((SUBLANES, tf), lambda i, j: (i, j))]
        a_shapes = [jax.ShapeDtypeStruct((m // tm * SUBLANES, f), F32)]
    if emit_bf16:
        a_specs += [half_spec, half_spec, down_spec]
        a_shapes += [jax.ShapeDtypeStruct((d, f), BF16), jax.ShapeDtypeStruct((d, f), BF16),
                     jax.ShapeDtypeStruct((f, d), BF16)]
    return pl.pallas_call(
        kern,
        grid=(m // tm, nj),
        in_specs=in_specs,
        out_specs=[pl.BlockSpec((tm, d), lambda i, j: (i, 0))] + a_specs,
        out_shape=[jax.ShapeDtypeStruct((m, d), F32)] + a_shapes,
        scratch_shapes=scratch,
        compiler_params=_cparams("arbitrary", "arbitrary"),
        name="ffn",
    )(*args)


def kernel(x_prompt, x_sample, c_prompt, c_sample, state_hgrn, state_mlstm_C, state_mlstm_n, state_mlstm_m,
           state_conv, ada_w, ada_b, w_in, hg_lb_logits, hg_norm_w, ml_i_bias, ml_f_bias, ml_norm_w, w_out,
           conv_w, conv_b, w_up, w_down, final_norm_w):
    n_p, seq, d = x_prompt.shape
    n_s, dec_seq, _ = x_sample.shape
    depth, _, hg_heads, hg_k, hg_v = state_hgrn.shape
    _, _, ml_heads, ml_qk, ml_v = state_mlstm_C.shape
    f = w_down.shape[1]
    assert depth == 1 and dec_seq == GROUP and hg_k == hg_v == LANES and 2 * ml_heads <= LANES
    assert hg_lb_logits.shape[0] == 2

    hg_w = hg_heads * hg_k
    gate0 = 4 * hg_w + 2 * ml_heads * ml_qk + ml_heads * ml_v
    gate1 = gate0 + 2 * ml_heads
    wt = jnp.swapaxes(w_in[0], 0, 1)
    col_mq = 4 * hg_w
    col_mk = col_mq + ml_heads * ml_qk
    col_mv = col_mk + ml_heads * ml_qk
    col_mo = col_mv + ml_heads * ml_v
    col_ga = col_mo + ml_heads * ml_v
    col_gb = col_ga + d
    assert col_gb + d == wt.shape[0] - (gate1 - gate0)
    gate_bias = jnp.pad(jnp.concatenate([ml_i_bias[0], ml_f_bias[0]]), (0, LANES - 2 * ml_heads))[None, :]

    m_s_rows = n_s * GROUP
    c_all = jnp.concatenate([c_sample, c_prompt], axis=0)
    c_all = jnp.pad(c_all, ((0, (-c_all.shape[0]) % SUBLANES), (0, 0)))
    mod_s = _mod_call(c_all, n_s, ada_w[0], ada_b)
    mod_p = mod_s[m_s_rows:m_s_rows + n_p][:, None, :]

    w_out_b = w_out[0].astype(BF16)
    hg_nw = hg_norm_w
    ml_nw = ml_norm_w
    fw = final_norm_w[None, :]
    tm_o = 512

    xs = x_sample.reshape(n_s * GROUP, d)
    proj_s, gates_s, *wt_b = _inproj_call(xs, mod_s, wt, gate0, gate1 - gate0, m_s_rows, 1024, 1)
    ya_s, hg_s = _hgrn_s_call(proj_s, hg_lb_logits, hg_nw, state_hgrn[0], hg_heads, hg_k, 0, col_ga // hg_k, 128)
    m_rows = jnp.pad(jnp.repeat(state_mlstm_m[0], GROUP, axis=0), ((0, 0), (0, LANES - ml_heads)))
    n_hb = jnp.repeat(jnp.transpose(state_mlstm_n[0], (1, 0, 2)), GROUP, axis=1)
    mg_s, c_s, nn_s, m_s = _mlstm_s_call(proj_s, ya_s, gates_s, gate_bias, m_rows, ml_nw, state_mlstm_C[0], n_hb,
                                         ml_heads, ml_qk, ml_v, col_mq, col_mk, col_mv, col_mo, col_gb, 64)
    x1_s = _outproj_call(mg_s, xs, mod_s, w_out_b, tm_o, 1)
    hist = (state_conv[0][:, 0], state_conv[0][:, 1])
    y_s, cv2, cv3, wa_b, wg_b, wd_b = _ffn_call(x1_s, mod_s, w_up[0], w_down[0], conv_w[0], conv_b, fw,
                                                m_s_rows, 256, 1, hist)
    cv_s = jnp.stack([cv2, cv3], axis=1)

    xp = x_prompt.reshape(n_p * seq, d)
    tm_p = 1024
    proj_p, gates_p = _inproj_call(xp, mod_p, tuple(wt_b), gate0, gate1 - gate0, tm_p, 2048, seq // tm_p)
    ya_p, hg_p = _hgrn_p_call(proj_p, hg_lb_logits, hg_nw, n_p, seq, hg_heads, hg_k, 0, col_ga // hg_k, 1024, 4)
    mg_p, c_p, nn_p, m_p = _mlstm_p_call(proj_p, ya_p, gates_p, gate_bias, ml_nw, n_p, seq, ml_heads, ml_qk, ml_v,
                                         col_mq, col_mk, col_mv, col_mo, col_gb, 4 * ML_CHUNK)
    x1_p = _outproj_call(mg_p, xp, mod_p, w_out_b, tm_o, seq // tm_o)
    tm_f = 512
    tiles = seq // tm_f
    y_p, atail = _ffn_call(x1_p, mod_p, (wa_b, wg_b), wd_b, conv_w[0], conv_b, fw, tm_f, 512, tiles)
    cv_p = atail.reshape(n_p, tiles, SUBLANES, f)[:, tiles - 1, SUBLANES - (CONV_W - 1):]

    return (y_p.reshape(n_p, seq, d), y_s.reshape(n_s, GROUP, d),
            hg_p[None], hg_s[None],
            c_p[None], c_s[None],
            nn_p.reshape(1, n_p, ml_heads, ml_qk), jnp.transpose(nn_s[:, GROUP - 1::GROUP], (1, 0, 2))[None],
            m_p[:, :, 0, 0][None], jnp.transpose(m_s[:, ::GROUP, 0])[None],
            cv_p[None], cv_s[None])
```

```python
import functools

import numpy as np
import jax
import jax.numpy as jnp
from jax import lax
from jax.experimental import pallas as pl
from jax.experimental.pallas import tpu as pltpu

F32 = jnp.float32
BF16 = jnp.bfloat16

EPS = 1e-6
CONV_W = 3
LANES = 128
SUBLANES = 8
VMEM_LIMIT_BYTES = 58 * 1024 * 1024

MXU_DEPTH = 256
HG_CHUNK = 64
HG_SAFE_EXP = 60.0
HG_LEVELS = (32, 16, 8, 4, 2, 1)
ML_CHUNK = 128
ML_HEADS_INTERLEAVED = 8
GROUP = 4

ROW_CHUNK = 256
MOD_COLS = 1024
INPROJ_ROWS_PROMPT, INPROJ_COLS_PROMPT, INPROJ_COLS_SAMPLE = 1024, 2048, 1024
OUTPROJ_ROWS = 512
FFN_ROWS_PROMPT, FFN_COLS_PROMPT, FFN_COLS_SAMPLE = 512, 512, 256
HG_ROWS_PER_STEP, HG_HEADS_PER_STEP, HG_SAMPLE_SEQS_PER_STEP = 1024, 4, 128
ML_ROWS_PER_STEP, ML_SAMPLE_SEQS_PER_STEP = 2 * ML_CHUNK, 64


def _cparams(*sem):
    return pltpu.CompilerParams(dimension_semantics=sem, vmem_limit_bytes=VMEM_LIMIT_BYTES)


def _dot(a, b):
    return jnp.dot(a, b, preferred_element_type=F32)


def _dot_nt(a, b):
    return lax.dot_general(a, b, (((1,), (1,)), ((), ())), preferred_element_type=F32)


def _dot_tn(a, b):
    return lax.dot_general(a, b, (((0,), (0,)), ((), ())), preferred_element_type=F32)


def _split3(x):
    x1 = x.astype(BF16)
    r1 = x - x1.astype(F32)
    x2 = r1.astype(BF16)
    x3 = (r1 - x2.astype(F32)).astype(BF16)
    return x1, x2, x3


def _exact_left_mul(w, x):
    x1, x2, x3 = _split3(x)
    return _dot(w, x1) + _dot(w, x2) + _dot(w, x3)


def _exact_left_mul3(w3, x):
    return _dot(w3, jnp.concatenate(_split3(x), axis=0))


def _rows(ref, r0, n):
    if ref.shape[0] == 1:
        return ref[...]
    return ref[pl.ds(r0, n), :]


def _norm_mod(x, sc, sh):
    ms = jnp.mean(x * x, axis=-1, keepdims=True)
    return x * lax.rsqrt(ms + EPS) * (1.0 + sc) + sh


def _head_norm(o, w):
    return o * lax.rsqrt(jnp.mean(o * o, axis=-1, keepdims=True) + EPS) * w


def _lower_bound(lg):
    l0, l1 = lg[0:1, :], lg[1:2, :]
    m = jnp.maximum(l0, l1)
    e0, e1 = jnp.exp(l0 - m), jnp.exp(l1 - m)
    return e0 / (e0 + e1)


def _group_last(x, t, up):
    return jnp.where(t == 3, x, jnp.where(t == 2, up(x, 1), jnp.where(t == 1, up(x, 2), up(x, 3))))


def _mod_kernel(c_ref, w_ref, b_ref, o_ref, rep_s, *, n_rep):
    c = c_ref[...]
    s = (c * jax.nn.sigmoid(c)).astype(BF16)
    res = _dot(s, w_ref[...].astype(BF16)) + b_ref[...]
    for k in range(res.shape[1] // LANES):
        slab = res[:, k * LANES:(k + 1) * LANES]
        for t in range(GROUP):
            rep_s[k, pl.ds(t, n_rep, stride=GROUP), :] = slab[:n_rep]
        rep_s[k, n_rep * GROUP:, :] = slab[n_rep:]
    for k in range(res.shape[1] // LANES):
        o_ref[:, k * LANES:(k + 1) * LANES] = rep_s[k]


def _mod_call(c_all, n_rep, ada_w, ada_b):
    mp, d = c_all.shape
    n = ada_w.shape[1]
    tn = MOD_COLS
    mo = n_rep * GROUP + (mp - n_rep)
    return pl.pallas_call(
        functools.partial(_mod_kernel, n_rep=n_rep),
        grid=(n // tn,),
        in_specs=[pl.BlockSpec((mp, d), lambda j: (0, 0)),
                  pl.BlockSpec((d, tn), lambda j: (0, j)),
                  pl.BlockSpec((1, tn), lambda j: (0, j))],
        out_specs=pl.BlockSpec((mo, tn), lambda j: (0, j)),
        out_shape=jax.ShapeDtypeStruct((mo, n), F32),
        scratch_shapes=[pltpu.VMEM((tn // LANES, mo, LANES), F32)],
        compiler_params=_cparams("arbitrary"),
        name="mod",
    )(c_all, ada_w, ada_b)


def _inproj_kernel(*refs, tm, rc, emit_bf16):
    if emit_bf16:
        x_ref, sh_ref, sc_ref, w_ref, wg_ref, o_ref, og_ref, wo_ref, wgo_ref, h_ref = refs
    else:
        x_ref, sh_ref, sc_ref, w_ref, wg_ref, o_ref, og_ref, h_ref = refs
        wo_ref = wgo_ref = None
    j = pl.program_id(1)

    def weight(src, dst):
        w = src[...]
        if emit_bf16:
            w = w.astype(BF16)
            dst[...] = w
        return w

    @pl.when(j == 0)
    def _():
        wg = weight(wg_ref, wgo_ref)

        def body(r, carry):
            r0 = pl.multiple_of(r * rc, rc)
            h = _norm_mod(x_ref[pl.ds(r0, rc), :], _rows(sc_ref, r0, rc), _rows(sh_ref, r0, rc))
            hb = h.astype(BF16)
            h_ref[pl.ds(r0, rc), :] = hb
            gg = _dot_nt(hb, wg)
            og_ref[pl.ds(r0, rc), :] = jnp.concatenate(
                [gg, jnp.zeros((rc, og_ref.shape[1] - gg.shape[1]), F32)], axis=1)
            return carry
        lax.fori_loop(0, tm // rc, body, 0)

    o_ref[...] = _dot_nt(h_ref[...], weight(w_ref, wo_ref))


def _mod_spec(mod, tm, col, tiles_per_seq):
    d = mod.shape[-1] // 6
    if mod.ndim == 3:
        return pl.BlockSpec((None, 1, d), lambda i, *_: (i // tiles_per_seq, 0, col))
    return pl.BlockSpec((tm, d), lambda i, *_: (i, col))


def _inproj_call(x, mod, w, gate0, n_gate, tm, tn, tiles_per_seq):
    m, d = x.shape
    emit_bf16 = not isinstance(w, tuple)
    n = (w.shape[0] - n_gate) if emit_bf16 else w[0].shape[0]
    assert n_gate == 2 * SUBLANES and gate0 % tn == 0 and n % tn == 0
    na = gate0 // tn
    tile = pl.BlockSpec((tn, d), lambda i, j: (j, 0))
    seg_g = pl.BlockSpec((n_gate, d), lambda i, j: (0, 0))
    out_specs = [pl.BlockSpec((tm, tn), lambda i, j: (i, j)), pl.BlockSpec((tm, LANES), lambda i, j: (i, 0))]
    out_shape = [jax.ShapeDtypeStruct((m, n), F32), jax.ShapeDtypeStruct((m, LANES), F32)]
    if emit_bf16:
        assert m == tm
        w_specs = [pl.BlockSpec((pl.Element(tn), pl.Element(d)),
                                lambda i, j: (pl.multiple_of(j * tn + jnp.where(j < na, 0, n_gate), SUBLANES), 0)),
                   pl.BlockSpec((n_gate, d), lambda i, j: (gate0 // n_gate, 0))]
        w_args = [w, w]
        out_specs += [tile, seg_g]
        out_shape += [jax.ShapeDtypeStruct((n, d), BF16), jax.ShapeDtypeStruct((n_gate, d), BF16)]
    else:
        w_specs = [tile, seg_g]
        w_args = list(w)
    kern = functools.partial(_inproj_kernel, tm=tm, rc=ROW_CHUNK, emit_bf16=emit_bf16)
    return pl.pallas_call(
        kern,
        grid=(m // tm, n // tn),
        in_specs=[pl.BlockSpec((tm, d), lambda i, j: (i, 0)),
                  _mod_spec(mod, tm, 0, tiles_per_seq),
                  _mod_spec(mod, tm, 1, tiles_per_seq)] + w_specs,
        out_specs=out_specs,
        out_shape=out_shape,
        scratch_shapes=[pltpu.VMEM((tm, d), BF16)],
        compiler_params=_cparams("arbitrary", "arbitrary"),
        name="inproj",
    )(x, mod, mod, *w_args)


def _hgrn_constants():
    L = HG_CHUNK
    tri = np.tril(np.ones((L, L), np.float32))
    t = np.arange(L)
    blocks = []
    masks = []
    for h in HG_LEVELS:
        mid = (t // (2 * h)) * (2 * h) + h - 1
        blocks.append(tri - tri[mid])
        masks.append((t[:, None] // (2 * h) == t[None, :] // (2 * h)).astype(np.float32))
    blocks.append(tri[L - 1][None, :] - tri)
    masks.append(np.eye(L, dtype=np.float32))
    return jnp.asarray(np.concatenate(blocks, 0), BF16), jnp.asarray(np.stack(masks, 0), F32)


def _hgrn_gates(hq, z, lb, kdim):
    omlb = 1.0 - lb
    sz = jax.nn.sigmoid(z)
    logf = jnp.log(lb + omlb * sz)
    kk = omlb * (1.0 - sz)
    q = hq * jax.nn.sigmoid(hq) * (kdim ** -0.5)
    return q, kk, logf


def _hgrn_p_kernel(hq_ref, hf_ref, hi_ref, hg_ref, ga_ref, lg_ref, nw_ref, trib_ref, wall_ref, mask_ref,
                   ya_ref, so_ref, st_ref, *, n_chunks, kdim):
    L = HG_CHUNK
    c = pl.program_id(2)
    hps = hq_ref.shape[1] // kdim
    tb = trib_ref.shape[0]
    heads_chunks = [(hd, ci) for ci in range(n_chunks) for hd in range(hps)]
    rows = lambda ci: slice(ci * L, (ci + 1) * L)
    lanes = lambda hd: slice(hd * kdim, (hd + 1) * kdim)

    @pl.when(c == 0)
    def _():
        st_ref[...] = jnp.zeros_like(st_ref)

    lb = _lower_bound(lg_ref[...])
    nw = nw_ref[...]
    q, kk, logf = _hgrn_gates(hq_ref[...], hf_ref[...], lb, kdim)
    trib = trib_ref[...]
    b = jnp.concatenate([_exact_left_mul3(trib, logf[r:r + tb]) for r in range(0, n_chunks * L, tb)], axis=0)
    worst = -jnp.min(jnp.concatenate([b[r:r + 1, :] for r in range(L - 1, n_chunks * L, L)], axis=0))

    def finish(hd, ci, o):
        hg = hg_ref[rows(ci), lanes(hd)]
        ya = _head_norm(o, nw[:, lanes(hd)]) * (hg * jax.nn.sigmoid(hg))
        ya_ref[rows(ci), lanes(hd)] = jax.nn.sigmoid(ga_ref[rows(ci), lanes(hd)]) * ya

    @pl.when(worst <= HG_SAFE_EXP)
    def _():
        eb = jnp.exp(b)
        qt = (q * eb).astype(BF16)
        kn = kk * (1.0 / eb)
        knb = kn.astype(BF16)
        row = lax.broadcasted_iota(jnp.int32, (L, L), 0)
        colid = lax.broadcasted_iota(jnp.int32, (L, L), 1)
        causal = colid <= row
        o_intra, incr, dec = {}, {}, {}
        for hd, ci in heads_chunks:
            vb = hi_ref[rows(ci), lanes(hd)].astype(BF16)
            a = jnp.where(causal, _dot_nt(qt[rows(ci), lanes(hd)], knb[rows(ci), lanes(hd)]), 0.0)
            o_intra[hd, ci] = _dot(a.astype(BF16), vb)
            dec[hd, ci] = eb[(ci + 1) * L - 1:(ci + 1) * L, lanes(hd)]
            incr[hd, ci] = _dot_tn(vb, (kn[rows(ci), lanes(hd)] * dec[hd, ci]).astype(BF16))
        sts = {(hd, 0): st_ref[hd] for hd in range(hps)}
        for hd, ci in heads_chunks:
            sts[hd, ci + 1] = sts[hd, ci] * dec[hd, ci] + incr[hd, ci]
        for hd in range(hps):
            st_ref[hd] = sts[hd, n_chunks]
        for hd, ci in heads_chunks:
            finish(hd, ci, o_intra[hd, ci] + _dot_nt(qt[rows(ci), lanes(hd)], sts[hd, ci].astype(BF16)))

    @pl.when(jnp.logical_not(worst <= HG_SAFE_EXP))
    def _():
        wall = wall_ref[...]
        rowid = lax.broadcasted_iota(jnp.int32, (L, kdim), 0)
        n_lv = len(HG_LEVELS)
        for hd, ci in heads_chunks:
            qc, kc, bc = q[rows(ci), lanes(hd)], kk[rows(ci), lanes(hd)], b[rows(ci), lanes(hd)]
            vb = hi_ref[rows(ci), lanes(hd)].astype(BF16)
            d = _exact_left_mul(wall, logf[rows(ci), lanes(hd)])
            a = mask_ref[n_lv] * _dot_nt(qc.astype(BF16), kc.astype(BF16))
            for li, h in enumerate(HG_LEVELS):
                e = jnp.exp(-jnp.abs(d[li * L:(li + 1) * L]))
                second = (rowid & h) != 0
                p = jnp.where(second, qc, kc) * e
                qh = jnp.where(second, p, 0.0).astype(BF16)
                kh = jnp.where(second, 0.0, p).astype(BF16)
                a = a + mask_ref[li] * _dot_nt(qh, kh)
            qt = (qc * jnp.exp(bc)).astype(BF16)
            kt = (kc * jnp.exp(d[n_lv * L:(n_lv + 1) * L])).astype(BF16)
            st = st_ref[hd]
            finish(hd, ci, _dot(a.astype(BF16), vb) + _dot_nt(qt, st.astype(BF16)))
            st_ref[hd] = st * jnp.exp(bc[L - 1:L, :]) + _dot_tn(vb, kt)

    @pl.when(c == pl.num_programs(2) - 1)
    def _():
        for hd in range(hps):
            so_ref[hd] = st_ref[hd].T


def _hgrn_p_call(proj, lb_logits, norm_w, n_seq, seq_len, heads, kdim, col0, col_ga, rows_per_step, hps):
    wall, masks = _hgrn_constants()
    per_mat = min(rows_per_step, MXU_DEPTH) // HG_CHUNK
    trib = np.kron(np.eye(per_mat, dtype=np.float32), np.tril(np.ones((HG_CHUNK, HG_CHUNK), np.float32)))
    trib = jnp.asarray(np.concatenate([trib, trib, trib], axis=1), BF16)
    nc = seq_len // rows_per_step
    assert heads % hps == 0 and col0 % hps == 0 and col_ga % hps == 0
    kern = functools.partial(_hgrn_p_kernel, n_chunks=rows_per_step // HG_CHUNK, kdim=kdim)
    wide = hps * kdim

    def col(first):
        return pl.BlockSpec((rows_per_step, wide), lambda b, h, c: (b * nc + c, first // hps + h))

    return pl.pallas_call(
        kern,
        grid=(n_seq, heads // hps, nc),
        in_specs=[col(col0), col(col0 + heads), col(col0 + 2 * heads), col(col0 + 3 * heads), col(col_ga),
                  pl.BlockSpec((2, wide), lambda b, h, c: (0, h)),
                  pl.BlockSpec((1, wide), lambda b, h, c: (0, h)),
                  pl.BlockSpec(trib.shape, lambda b, h, c: (0, 0)),
                  pl.BlockSpec(wall.shape, lambda b, h, c: (0, 0)),
                  pl.BlockSpec(masks.shape, lambda b, h, c: (0, 0, 0))],
        out_specs=[pl.BlockSpec((rows_per_step, wide), lambda b, h, c: (b * nc + c, h)),
                   pl.BlockSpec((None, hps, kdim, kdim), lambda b, h, c: (b, h, 0, 0))],
        out_shape=[jax.ShapeDtypeStruct((n_seq * seq_len, heads * kdim), F32),
                   jax.ShapeDtypeStruct((n_seq, heads, kdim, kdim), F32)],
        scratch_shapes=[pltpu.VMEM((hps, kdim, kdim), F32)],
        compiler_params=_cparams("arbitrary", "arbitrary", "arbitrary"),
        name="hgrn_prompt",
    )(proj, proj, proj, proj, proj, lb_logits, norm_w, trib, wall, masks)


def _hgrn_s_kernel(hq_ref, hf_ref, hi_ref, hg_ref, ga_ref, lg_ref, nw_ref, s_ref, ya_ref, so_ref, *, n_pairs):
    kdim = hq_ref.shape[1]
    rows = n_pairs * SUBLANES
    lb = _lower_bound(lg_ref[...])
    rowid = lax.broadcasted_iota(jnp.int32, (rows, kdim), 0)
    t = rowid & (GROUP - 1)
    first = (rowid & GROUP) == 0
    first8 = lax.broadcasted_iota(jnp.int32, (SUBLANES, kdim), 0) < GROUP
    down = lambda y, j: pltpu.roll(y, j, 0)
    up = lambda y, j: pltpu.roll(y, rows - j, 0)

    q, kk, logf = _hgrn_gates(hq_ref[...], hf_ref[...], lb, kdim)
    v = hi_ref[...]
    b = logf
    for dlt in range(1, GROUP):
        b = b + jnp.where(t >= dlt, down(logf, dlt), 0.0)
    o = jnp.sum(q * kk, axis=1, keepdims=True) * v
    for dlt in range(1, GROUP):
        x = q * down(kk, dlt) * jnp.exp(b - down(b, dlt))
        a = jnp.sum(jnp.where(t >= dlt, x, 0.0), axis=1, keepdims=True)
        o = o + a * down(v, dlt)
    b_last = _group_last(b, t, up)
    qt = q * jnp.exp(b)
    kt = kk * jnp.exp(b_last - b)
    d1, d2, d3 = _split3(jnp.exp(b_last))
    swap = lambda y: jnp.where(first, up(y.astype(F32), GROUP), down(y.astype(F32), GROUP))
    dsplit = jnp.where(t == 0, swap(d1), jnp.where(t == 1, swap(d2), jnp.where(t == 2, swap(d3), 0.0)))
    ones = jnp.where(t <= 2, 1.0, 0.0)
    lhs = (jnp.where(first, kt, dsplit), jnp.where(first, dsplit, kt))
    rhs = (jnp.concatenate([jnp.where(first, v, 0.0), jnp.where(first, 0.0, ones)], axis=1),
           jnp.concatenate([jnp.where(first, 0.0, v), jnp.where(first, ones, 0.0)], axis=1))
    o_inter = []
    for p in range(n_pairs):
        sl = slice(p * SUBLANES, (p + 1) * SUBLANES)
        qb = qt[sl].astype(BF16)
        parts = []
        for half in range(2):
            s0 = s_ref[2 * p + half]
            upd = _dot_tn(lhs[half][sl].astype(BF16), rhs[half][sl].astype(BF16))
            so_ref[2 * p + half] = s0 * upd[:, kdim:] + upd[:, :kdim]
            parts.append(_dot(qb, s0.astype(BF16)))
        o_inter.append(jnp.where(first8, parts[0], parts[1]))
    o = o + jnp.concatenate(o_inter, axis=0)
    hg = hg_ref[...]
    ya_ref[...] = jax.nn.sigmoid(ga_ref[...]) * (_head_norm(o, nw_ref[...]) * (hg * jax.nn.sigmoid(hg)))


def _hgrn_s_call(proj, lb_logits, norm_w, state, heads, kdim, col0, col_ga, seqs_per_step):
    n_seq = state.shape[0]
    rows = seqs_per_step * GROUP
    kern = functools.partial(_hgrn_s_kernel, n_pairs=seqs_per_step // 2)

    def col(k):
        return pl.BlockSpec((rows, kdim), lambda i, h: (i, col0 + k * heads + h))

    st_spec = pl.BlockSpec((seqs_per_step, None, kdim, kdim), lambda i, h: (i, h, 0, 0))
    return pl.pallas_call(
        kern,
        grid=(n_seq // seqs_per_step, heads),
        in_specs=[col(0), col(1), col(2), col(3),
                  pl.BlockSpec((rows, kdim), lambda i, h: (i, col_ga + h)),
                  pl.BlockSpec((2, kdim), lambda i, h: (0, h)),
                  pl.BlockSpec((1, kdim), lambda i, h: (0, h)),
                  st_spec],
        out_specs=[pl.BlockSpec((rows, kdim), lambda i, h: (i, h)), st_spec],
        out_shape=[jax.ShapeDtypeStruct((n_seq * GROUP, heads * kdim), F32),
                   jax.ShapeDtypeStruct(state.shape, F32)],
        compiler_params=_cparams("arbitrary", "arbitrary"),
        name="hgrn_sample",
    )(proj, proj, proj, proj, proj, lb_logits, norm_w, state)


def _lane_pick(x, lane, idx):
    return jnp.broadcast_to(jnp.sum(jnp.where(lane == idx, x, 0.0), axis=1, keepdims=True), x.shape)


def _interleave(chains):
    live = list(chains)
    while live:
        still = []
        for g in live:
            try:
                next(g)
                still.append(g)
            except StopIteration:
                pass
        live = still


def _mlstm_p_kernel(q_ref, k_ref, v_ref, og_ref, gb_ref, ya_ref, g_ref, bias_ref, nw_ref, tri_ref,
                    mg_ref, co_ref, no_ref, mo_ref, c_s, n_s, m_s, *, heads, qk, vd):
    L = ML_CHUNK
    c = pl.program_id(1)

    @pl.when(c == 0)
    def _():
        c_s[...] = jnp.zeros_like(c_s)
        n_s[...] = jnp.zeros_like(n_s)
        m_s[...] = jnp.zeros_like(m_s)

    lane = lax.broadcasted_iota(jnp.int32, (L, LANES), 1)
    row = lax.broadcasted_iota(jnp.int32, (L, LANES), 0)

    def head(hd, r, g, b_all):
        qs = slice(hd * qk, (hd + 1) * qk)
        vs = slice(hd * vd, (hd + 1) * vd)
        ig = _lane_pick(g, lane, hd)
        b = _lane_pick(b_all, lane, heads + hd)
        m_prev = m_s[hd]
        gs = (ig - b).T
        yield
        dm = jnp.where(lane <= row, b + gs, -jnp.inf)
        mt = jnp.maximum(b + m_prev, jnp.max(dm, axis=1, keepdims=True))
        inter = jnp.exp(b + m_prev - mt)
        q = q_ref[r, qs] * (qk ** -0.5)
        k = k_ref[r, qs]
        vb = v_ref[r, vs].astype(BF16)
        qb = q.astype(BF16)
        sc = _dot_nt(qb, k.astype(BF16)) * jnp.exp(dm - mt)
        yield
        c0 = c_s[hd]
        n0 = n_s[hd]
        num = inter[:, 0:1] * _dot(qb, c0.astype(BF16)) + _dot(sc.astype(BF16), vb)
        den = inter[:, 0:1] * jnp.sum(q * n0, axis=1, keepdims=True) + jnp.sum(sc, axis=1, keepdims=True)
        yield
        hh = num / jnp.maximum(jnp.abs(den), jnp.exp(-mt[:, 0:1]))
        yb = _head_norm(hh, nw_ref[:, vs]) * jax.nn.sigmoid(og_ref[r, vs])
        mg_ref[r, vs] = (ya_ref[r, vs] + jax.nn.sigmoid(gb_ref[r, vs]) * yb).astype(mg_ref.dtype)
        yield
        m_last = mt[L - 1:L, :]
        b_last = b[L - 1:L, :]
        dec = jnp.exp(b_last + m_prev - m_last)
        kw = jnp.exp(b_last - b + ig - m_last) * k
        c_s[hd] = jnp.concatenate([dec, dec], axis=1) * c0 + _dot_tn(kw.astype(BF16), vb)
        n_s[hd] = dec * n0 + jnp.sum(kw, axis=0, keepdims=True)
        m_s[hd] = m_last

    for r0 in range(0, q_ref.shape[0], L):
        r = slice(r0, r0 + L)
        g = g_ref[r, :] + bias_ref[...]
        b_all = _exact_left_mul(tri_ref[...], jax.nn.log_sigmoid(g))
        for h0 in range(0, heads, ML_HEADS_INTERLEAVED):
            _interleave([head(hd, r, g, b_all) for hd in range(h0, min(h0 + ML_HEADS_INTERLEAVED, heads))])

    @pl.when(c == pl.num_programs(1) - 1)
    def _():
        co_ref[...] = c_s[...]
        no_ref[...] = n_s[...]
        mo_ref[...] = m_s[...]


def _mlstm_p_call(proj, ya, gates, bias, norm_w, n_seq, seq_len, heads, qk, vd, colq, colk, colv, colo, colg,
                  rows_per_step):
    assert qk == ML_CHUNK and qk == LANES and rows_per_step % ML_CHUNK == 0
    L = rows_per_step
    nc = seq_len // L
    qw, vw = heads * qk, heads * vd
    assert colq % qw == 0 and colk % qw == 0 and colv % vw == 0 and colo % vw == 0 and colg % vw == 0
    tri = jnp.asarray(np.tril(np.ones((ML_CHUNK, ML_CHUNK), np.float32)), BF16)
    kern = functools.partial(_mlstm_p_kernel, heads=heads, qk=qk, vd=vd)
    return pl.pallas_call(
        kern,
        grid=(n_seq, nc),
        in_specs=[pl.BlockSpec((L, qw), lambda b, c: (b * nc + c, colq // qw)),
                  pl.BlockSpec((L, qw), lambda b, c: (b * nc + c, colk // qw)),
                  pl.BlockSpec((L, vw), lambda b, c: (b * nc + c, colv // vw)),
                  pl.BlockSpec((L, vw), lambda b, c: (b * nc + c, colo // vw)),
                  pl.BlockSpec((L, vw), lambda b, c: (b * nc + c, colg // vw)),
                  pl.BlockSpec((L, vw), lambda b, c: (b * nc + c, 0)),
                  pl.BlockSpec((L, LANES), lambda b, c: (b * nc + c, 0)),
                  pl.BlockSpec((1, LANES), lambda b, c: (0, 0)),
                  pl.BlockSpec((1, vw), lambda b, c: (0, 0)),
                  pl.BlockSpec(tri.shape, lambda b, c: (0, 0))],
        out_specs=[pl.BlockSpec((L, vw), lambda b, c: (b * nc + c, 0)),
                   pl.BlockSpec((None, heads, qk, vd), lambda b, c: (b, 0, 0, 0)),
                   pl.BlockSpec((None, heads, 1, qk), lambda b, c: (b, 0, 0, 0)),
                   pl.BlockSpec((None, heads, 1, LANES), lambda b, c: (b, 0, 0, 0))],
        out_shape=[jax.ShapeDtypeStruct((n_seq * seq_len, vw), BF16),
                   jax.ShapeDtypeStruct((n_seq, heads, qk, vd), F32),
                   jax.ShapeDtypeStruct((n_seq, heads, 1, qk), F32),
                   jax.ShapeDtypeStruct((n_seq, heads, 1, LANES), F32)],
        scratch_shapes=[pltpu.VMEM((heads, qk, vd), F32), pltpu.VMEM((heads, 1, qk), F32),
                        pltpu.VMEM((heads, 1, LANES), F32)],
        compiler_params=_cparams("arbitrary", "arbitrary"),
        name="mlstm_prompt",
    )(proj, proj, proj, proj, proj, ya, gates, bias, norm_w, tri)


def _mlstm_s_kernel(q_ref, k_ref, v_ref, og_ref, gb_ref, ya_ref, g_ref, bias_ref, m_ref, nw_ref, c_ref, n_ref,
                    mg_ref, co_ref, no_ref, mo_ref, *, heads, n_pairs):
    h = pl.program_id(1)
    qk = q_ref.shape[1]
    rows = n_pairs * SUBLANES
    lane = lax.broadcasted_iota(jnp.int32, (rows, LANES), 1)
    rowid = lax.broadcasted_iota(jnp.int32, (rows, LANES), 0)
    t = rowid & (GROUP - 1)
    first = (rowid & GROUP) == 0
    first8 = lax.broadcasted_iota(jnp.int32, (SUBLANES, 1), 0) < GROUP
    down = lambda y, j: pltpu.roll(y, j, 0)
    up = lambda y, j: pltpu.roll(y, rows - j, 0)

    g = g_ref[...] + bias_ref[...]
    ig = _lane_pick(g, lane, h)
    lf = _lane_pick(jax.nn.log_sigmoid(g), lane, heads + h)
    m_prev = _lane_pick(m_ref[...], lane, h)
    b = lf
    for dlt in range(1, GROUP):
        b = b + jnp.where(t >= dlt, down(lf, dlt), 0.0)
    q = q_ref[...] * (qk ** -0.5)
    k = k_ref[...]
    v = v_ref[...]
    dms = [ig] + [jnp.where(t >= dlt, b - down(b, dlt) + down(ig, dlt), -jnp.inf) for dlt in range(1, GROUP)]
    mt = jnp.maximum(b + m_prev, functools.reduce(jnp.maximum, dms))
    inter = jnp.exp(b + m_prev - mt)[:, 0:1]
    num = jnp.zeros(v.shape, F32)
    den = jnp.zeros((rows, 1), F32)
    for dlt in range(GROUP):
        kd = k if dlt == 0 else down(k, dlt)
        vd_ = v if dlt == 0 else down(v, dlt)
        s = jnp.sum(q * kd, axis=1, keepdims=True) * jnp.exp(dms[dlt] - mt)[:, 0:1]
        num = num + s * vd_
        den = den + s
    m_last = _group_last(mt, t, up)
    b_last = _group_last(b, t, up)
    dec = jnp.exp(b_last + m_prev - m_last)
    kw = jnp.exp(b_last - b + ig - m_last) * k
    n0 = n_ref[...]
    no_ref[...] = dec * n0 + kw + down(kw, 1) + down(kw, 2) + down(kw, 3)
    mo_ref[...] = m_last
    den = den + inter * jnp.sum(q * n0, axis=1, keepdims=True)

    kw_half = (jnp.where(first, kw, 0.0), jnp.where(first, 0.0, kw))
    dec2 = jnp.concatenate([dec, dec], axis=1)
    num_inter = []
    for p in range(n_pairs):
        sl = slice(p * SUBLANES, (p + 1) * SUBLANES)
        qb = q[sl].astype(BF16)
        vb = v[sl].astype(BF16)
        parts = []
        for half in range(2):
            c0 = c_ref[2 * p + half]
            r = p * SUBLANES + half * GROUP
            co_ref[2 * p + half] = dec2[r:r + 1, :] * c0 + _dot_tn(kw_half[half][sl].astype(BF16), vb)
            parts.append(_dot(qb, c0.astype(BF16)))
        num_inter.append(jnp.where(first8, parts[0], parts[1]))
    num = num + inter * jnp.concatenate(num_inter, axis=0)
    hh = num / jnp.maximum(jnp.abs(den), jnp.exp(-mt)[:, 0:1])
    yb = _head_norm(hh, nw_ref[...]) * jax.nn.sigmoid(og_ref[...])
    mg_ref[...] = (ya_ref[...] + jax.nn.sigmoid(gb_ref[...]) * yb).astype(mg_ref.dtype)


def _mlstm_s_call(proj, ya, gates, bias, m_rows, norm_w, c_state, n_state, heads, qk, vd,
                  colq, colk, colv, colo, colg, seqs_per_step):
    n_seq = c_state.shape[0]
    rows = seqs_per_step * GROUP
    kern = functools.partial(_mlstm_s_kernel, heads=heads, n_pairs=seqs_per_step // 2)
    c_spec = pl.BlockSpec((seqs_per_step, None, qk, vd), lambda i, h: (i, h, 0, 0))
    n_spec = pl.BlockSpec((None, rows, qk), lambda i, h: (h, i, 0))
    return pl.pallas_call(
        kern,
        grid=(n_seq // seqs_per_step, heads),
        in_specs=[pl.BlockSpec((rows, qk), lambda i, h: (i, colq // qk + h)),
                  pl.BlockSpec((rows, qk), lambda i, h: (i, colk // qk + h)),
                  pl.BlockSpec((rows, vd), lambda i, h: (i, colv // vd + h)),
                  pl.BlockSpec((rows, vd), lambda i, h: (i, colo // vd + h)),
                  pl.BlockSpec((rows, vd), lambda i, h: (i, colg // vd + h)),
                  pl.BlockSpec((rows, vd), lambda i, h: (i, h)),
                  pl.BlockSpec((rows, LANES), lambda i, h: (i, 0)),
                  pl.BlockSpec((1, LANES), lambda i, h: (0, 0)),
                  pl.BlockSpec((rows, LANES), lambda i, h: (i, 0)),
                  pl.BlockSpec((1, vd), lambda i, h: (0, h)),
                  c_spec, n_spec],
        out_specs=[pl.BlockSpec((rows, vd), lambda i, h: (i, h)),
                   c_spec, n_spec,
                   pl.BlockSpec((None, rows, LANES), lambda i, h: (h, i, 0))],
        out_shape=[jax.ShapeDtypeStruct((n_seq * GROUP, heads * vd), BF16),
                   jax.ShapeDtypeStruct(c_state.shape, F32),
                   jax.ShapeDtypeStruct(n_state.shape, F32),
                   jax.ShapeDtypeStruct((heads, n_seq * GROUP, LANES), F32)],
        compiler_params=_cparams("arbitrary", "arbitrary"),
        name="mlstm_sample",
    )(proj, proj, proj, proj, proj, ya, gates, bias, m_rows, norm_w, c_state, n_state)


def _outproj_kernel(mg_ref, x_ref, g1_ref, w_ref, o_ref):
    o_ref[...] = x_ref[...] + g1_ref[...] * _dot(mg_ref[...], w_ref[...])


def _outproj_call(merged, x, mod, w_out, tm, tiles_per_seq):
    m, d = x.shape
    blk = pl.BlockSpec((tm, d), lambda i: (i, 0))
    return pl.pallas_call(
        _outproj_kernel,
        grid=(m // tm,),
        in_specs=[blk, blk, _mod_spec(mod, tm, 2, tiles_per_seq), pl.BlockSpec((d, d), lambda i: (0, 0))],
        out_specs=blk,
        out_shape=jax.ShapeDtypeStruct((m, d), F32),
        compiler_params=_cparams("arbitrary"),
        name="outproj",
    )(merged, x, mod, w_out)


def _ffn_kernel(*refs, tm, rc, tiles_per_seq, grouped, emit_bf16):
    refs = iter(refs)
    x_ref, sh_ref, sc_ref, g2_ref, wa_ref, wg_ref, wd_ref, cw_ref, cb_ref, fw_ref = (next(refs) for _ in range(10))
    hist_refs = (next(refs), next(refs)) if grouped else None
    y_ref = next(refs)
    if grouped:
        cv2_ref, cv3_ref = next(refs), next(refs)
    else:
        a_ref = next(refs)
    wao_ref, wgo_ref, wdo_ref = (next(refs), next(refs), next(refs)) if emit_bf16 else (None, None, None)
    h_s, acc_s = next(refs), next(refs)
    a_s = carry_s = next(refs)
    i = pl.program_id(0)
    j = pl.program_id(1)

    def weight(src, dst):
        w = src[...]
        if emit_bf16:
            w = w.astype(BF16)
            dst[...] = w
        return w

    @pl.when(j == 0)
    def _():
        def body(r, carry):
            r0 = pl.multiple_of(r * rc, rc)
            h = _norm_mod(x_ref[pl.ds(r0, rc), :], _rows(sc_ref, r0, rc), _rows(sh_ref, r0, rc))
            h_s[pl.ds(r0, rc), :] = h.astype(BF16)
            return carry
        lax.fori_loop(0, tm // rc, body, 0)
        acc_s[...] = jnp.zeros_like(acc_s)

    hb = h_s[...]
    a = _dot(hb, weight(wa_ref, wao_ref))
    g = _dot(hb, weight(wg_ref, wgo_ref))
    rowid = lax.broadcasted_iota(jnp.int32, a.shape, 0)
    r1 = pltpu.roll(a, 1, 0)
    r2 = pltpu.roll(a, 2, 0)
    if grouped:
        t = rowid & (GROUP - 1)
        slabs = []
        for s in range(a.shape[1] // LANES):
            a_s[s] = jnp.zeros((tm, LANES), F32)
            for u, hist in enumerate(hist_refs):
                a_s[s, pl.ds(u, tm // GROUP, stride=GROUP), :] = hist[:, s * LANES:(s + 1) * LANES]
            slabs.append(a_s[s])
        p2 = jnp.concatenate(slabs, axis=1)
        prev1 = jnp.where(t == 0, pltpu.roll(p2, tm - 1, 0), r1)
        prev2 = jnp.where(t <= 1, p2, r2)
        for s in range(a.shape[1] // LANES):
            a_s[s] = a[:, s * LANES:(s + 1) * LANES]
        for s in range(a.shape[1] // LANES):
            cv2_ref[:, s * LANES:(s + 1) * LANES] = a_s[s, pl.ds(GROUP - 2, tm // GROUP, stride=GROUP), :]
            cv3_ref[:, s * LANES:(s + 1) * LANES] = a_s[s, pl.ds(GROUP - 1, tm // GROUP, stride=GROUP), :]
    else:
        tail = a[tm - SUBLANES:tm, :]
        car = jnp.where(i % tiles_per_seq == 0, 0.0, carry_s[j])
        c1 = car[SUBLANES - 1:SUBLANES, :]
        c2 = car[SUBLANES - 2:SUBLANES - 1, :]
        prev1 = jnp.where(rowid == 0, c1, r1)
        prev2 = jnp.where(rowid == 0, c2, jnp.where(rowid == 1, c1, r2))
        carry_s[j] = tail
        a_ref[...] = tail
    cw = cw_ref[...]
    ac = cw[0:1, :] * prev2 + cw[1:2, :] * prev1 + cw[2:3, :] * a + cb_ref[...]
    yv = 0.5 * ac * (1.0 + lax.erf(ac * (2.0 ** -0.5))) * g
    acc_s[...] += _dot(yv.astype(BF16), weight(wd_ref, wdo_ref))

    @pl.when(j == pl.num_programs(1) - 1)
    def _():
        def body(r, carry):
            r0 = pl.multiple_of(r * rc, rc)
            x2 = x_ref[pl.ds(r0, rc), :] + _rows(g2_ref, r0, rc) * acc_s[pl.ds(r0, rc), :]
            ms = jnp.mean(x2 * x2, axis=-1, keepdims=True)
            y_ref[pl.ds(r0, rc), :] = x2 * lax.rsqrt(ms + EPS) * fw_ref[...]
            return carry
        lax.fori_loop(0, tm // rc, body, 0)


def _ffn_call(x, mod, w_up, w_down, conv_w, conv_b, final_w, tm, tf, tiles_per_seq, hist=None):
    m, d = x.shape
    f = w_down.shape[0]
    nj = f // tf
    grouped = hist is not None
    emit_bf16 = not isinstance(w_up, tuple)
    kern = functools.partial(_ffn_kernel, tm=tm, rc=min(tm, ROW_CHUNK), tiles_per_seq=tiles_per_seq, grouped=grouped,
                             emit_bf16=emit_bf16)
    half_spec = pl.BlockSpec((d, tf), lambda i, j: (0, j))
    down_spec = pl.BlockSpec((tf, d), lambda i, j: (j, 0))
    if emit_bf16:
        assert m == tm
        up_specs = [half_spec, pl.BlockSpec((d, tf), lambda i, j: (0, nj + j))]
        up_args = [w_up, w_up]
    else:
        up_specs = [half_spec, half_spec]
        up_args = list(w_up)
    in_specs = [pl.BlockSpec((tm, d), lambda i, j: (i, 0)),
                _mod_spec(mod, tm, 3, tiles_per_seq),
                _mod_spec(mod, tm, 4, tiles_per_seq),
                _mod_spec(mod, tm, 5, tiles_per_seq)] + up_specs + [
                down_spec,
                pl.BlockSpec((CONV_W, tf), lambda i, j: (0, j)),
                pl.BlockSpec((1, tf), lambda i, j: (0, j)),
                pl.BlockSpec((1, d), lambda i, j: (0, 0))]
    args = [x, mod, mod, mod] + up_args + [w_down, conv_w, conv_b, final_w]
    scratch = [pltpu.VMEM((tm, d), BF16), pltpu.VMEM((tm, d), F32)]
    if grouped:
        in_specs += [pl.BlockSpec((tm // GROUP, tf), lambda i, j: (i, j))] * 2
        args += list(hist)
        scratch.append(pltpu.VMEM((tf // LANES, tm, LANES), F32))
        a_specs = [pl.BlockSpec((tm // GROUP, tf), lambda i, j: (i, j))] * 2
        a_shapes = [jax.ShapeDtypeStruct((m // GROUP, f), F32)] * 2
    else:
        scratch.append(pltpu.VMEM((nj, SUBLANES, tf), F32))
        a_specs = [pl.BlockSpec((SUBLANES, tf), lambda i, j: (i, j))]
        a_shapes = [jax.ShapeDtypeStruct((m // tm * SUBLANES, f), F32)]
    if emit_bf16:
        a_specs += [half_spec, half_spec, down_spec]
        a_shapes += [jax.ShapeDtypeStruct((d, f), BF16), jax.ShapeDtypeStruct((d, f), BF16),
                     jax.ShapeDtypeStruct((f, d), BF16)]
    return pl.pallas_call(
        kern,
        grid=(m // tm, nj),
        in_specs=in_specs,
        out_specs=[pl.BlockSpec((tm, d), lambda i, j: (i, 0))] + a_specs,
        out_shape=[jax.ShapeDtypeStruct((m, d), F32)] + a_shapes,
        scratch_shapes=scratch,
        compiler_params=_cparams("arbitrary", "arbitrary"),
        name="ffn",
    )(*args)


def kernel(x_prompt, x_sample, c_prompt, c_sample, state_hgrn, state_mlstm_C, state_mlstm_n, state_mlstm_m,
           state_conv, ada_w, ada_b, w_in, hg_lb_logits, hg_norm_w, ml_i_bias, ml_f_bias, ml_norm_w, w_out,
           conv_w, conv_b, w_up, w_down, final_norm_w):
    n_p, seq, d = x_prompt.shape
    n_s, dec_seq, _ = x_sample.shape
    depth, _, hg_heads, hg_k, hg_v = state_hgrn.shape
    _, _, ml_heads, ml_qk, ml_v = state_mlstm_C.shape
    f = w_down.shape[1]
    assert depth == 1 and dec_seq == GROUP and hg_k == hg_v == LANES and 2 * ml_heads <= LANES
    assert hg_lb_logits.shape[0] == 2

    hg_w = hg_heads * hg_k
    gate0 = 4 * hg_w + 2 * ml_heads * ml_qk + ml_heads * ml_v
    gate1 = gate0 + 2 * ml_heads
    wt = jnp.swapaxes(w_in[0], 0, 1)
    col_mq = 4 * hg_w
    col_mk = col_mq + ml_heads * ml_qk
    col_mv = col_mk + ml_heads * ml_qk
    col_mo = col_mv + ml_heads * ml_v
    col_ga = col_mo + ml_heads * ml_v
    col_gb = col_ga + d
    assert col_gb + d == wt.shape[0] - (gate1 - gate0)
    gate_bias = jnp.pad(jnp.concatenate([ml_i_bias[0], ml_f_bias[0]]), (0, LANES - 2 * ml_heads))[None, :]

    m_s_rows = n_s * GROUP
    c_all = jnp.concatenate([c_sample, c_prompt], axis=0)
    c_all = jnp.pad(c_all, ((0, (-c_all.shape[0]) % SUBLANES), (0, 0)))
    mod_s = _mod_call(c_all, n_s, ada_w[0], ada_b)
    mod_p = mod_s[m_s_rows:m_s_rows + n_p][:, None, :]

    w_out_b = w_out[0].astype(BF16)
    hg_nw = hg_norm_w
    ml_nw = ml_norm_w
    fw = final_norm_w[None, :]

    xs = x_sample.reshape(n_s * GROUP, d)
    proj_s, gates_s, *wt_b = _inproj_call(xs, mod_s, wt, gate0, gate1 - gate0, m_s_rows, INPROJ_COLS_SAMPLE, 1)
    ya_s, hg_s = _hgrn_s_call(proj_s, hg_lb_logits, hg_nw, state_hgrn[0], hg_heads, hg_k, 0, col_ga // hg_k,
                              HG_SAMPLE_SEQS_PER_STEP)
    m_rows = jnp.pad(jnp.repeat(state_mlstm_m[0], GROUP, axis=0), ((0, 0), (0, LANES - ml_heads)))
    n_hb = jnp.repeat(jnp.transpose(state_mlstm_n[0], (1, 0, 2)), GROUP, axis=1)
    mg_s, c_s, nn_s, m_s = _mlstm_s_call(proj_s, ya_s, gates_s, gate_bias, m_rows, ml_nw, state_mlstm_C[0], n_hb,
                                         ml_heads, ml_qk, ml_v, col_mq, col_mk, col_mv, col_mo, col_gb,
                                         ML_SAMPLE_SEQS_PER_STEP)
    x1_s = _outproj_call(mg_s, xs, mod_s, w_out_b, OUTPROJ_ROWS, 1)
    hist = (state_conv[0][:, 0], state_conv[0][:, 1])
    y_s, cv2, cv3, wa_b, wg_b, wd_b = _ffn_call(x1_s, mod_s, w_up[0], w_down[0], conv_w[0], conv_b, fw,
                                                m_s_rows, FFN_COLS_SAMPLE, 1, hist)
    cv_s = jnp.stack([cv2, cv3], axis=1)

    xp = x_prompt.reshape(n_p * seq, d)
    proj_p, gates_p = _inproj_call(xp, mod_p, tuple(wt_b), gate0, gate1 - gate0, INPROJ_ROWS_PROMPT,
                                   INPROJ_COLS_PROMPT, seq // INPROJ_ROWS_PROMPT)
    ya_p, hg_p = _hgrn_p_call(proj_p, hg_lb_logits, hg_nw, n_p, seq, hg_heads, hg_k, 0, col_ga // hg_k,
                              HG_ROWS_PER_STEP, HG_HEADS_PER_STEP)
    mg_p, c_p, nn_p, m_p = _mlstm_p_call(proj_p, ya_p, gates_p, gate_bias, ml_nw, n_p, seq, ml_heads, ml_qk, ml_v,
                                         col_mq, col_mk, col_mv, col_mo, col_gb, ML_ROWS_PER_STEP)
    x1_p = _outproj_call(mg_p, xp, mod_p, w_out_b, OUTPROJ_ROWS, seq // OUTPROJ_ROWS)
    tiles = seq // FFN_ROWS_PROMPT
    y_p, atail = _ffn_call(x1_p, mod_p, (wa_b, wg_b), wd_b, conv_w[0], conv_b, fw, FFN_ROWS_PROMPT,
                           FFN_COLS_PROMPT, tiles)
    cv_p = atail.reshape(n_p, tiles, SUBLANES, f)[:, tiles - 1, SUBLANES - (CONV_W - 1):]

    return (y_p.reshape(n_p, seq, d), y_s.reshape(n_s, GROUP, d),
            hg_p[None], hg_s[None],
            c_p[None], c_s[None],
            nn_p.reshape(1, n_p, ml_heads, ml_qk), jnp.transpose(nn_s[:, GROUP - 1::GROUP], (1, 0, 2))[None],
            m_p[:, :, 0, 0][None], jnp.transpose(m_s[:, ::GROUP, 0])[None],
            cv_p[None], cv_s[None])
```

```python
import functools

import numpy as np
import jax
import jax.numpy as jnp
from jax import lax
from jax.experimental import pallas as pl
from jax.experimental.pallas import tpu as pltpu

F32 = jnp.float32
BF16 = jnp.bfloat16

EPS = 1e-6
CONV_W = 3
LANES = 128
SUBLANES = 8
VMEM_LIMIT_BYTES = 58 * 1024 * 1024

MXU_DEPTH = 256
HG_CHUNK = 64
HG_SAFE_EXP = 60.0
HG_LEVELS = (32, 16, 8, 4, 2, 1)
ML_CHUNK = 128
ML_HEADS_INTERLEAVED = 8
GROUP = 4

ROW_CHUNK = 256
MOD_COLS = 1024
INPROJ_ROWS_PROMPT, INPROJ_COLS_PROMPT, INPROJ_COLS_SAMPLE = 1024, 2048, 1024
OUTPROJ_ROWS = 1024
FFN_ROWS_PROMPT, FFN_COLS_PROMPT, FFN_COLS_SAMPLE = 512, 512, 256
HG_ROWS_PER_STEP, HG_HEADS_PER_STEP, HG_SAMPLE_SEQS_PER_STEP = 1024, 4, 128
ML_ROWS_PER_STEP, ML_SAMPLE_SEQS_PER_STEP = 2 * ML_CHUNK, 64


def _cparams(*sem):
    return pltpu.CompilerParams(dimension_semantics=sem, vmem_limit_bytes=VMEM_LIMIT_BYTES)


def _dot(a, b):
    return jnp.dot(a, b, preferred_element_type=F32)


def _dot_nt(a, b):
    return lax.dot_general(a, b, (((1,), (1,)), ((), ())), preferred_element_type=F32)


def _dot_tn(a, b):
    return lax.dot_general(a, b, (((0,), (0,)), ((), ())), preferred_element_type=F32)


def _split3(x):
    x1 = x.astype(BF16)
    r1 = x - x1.astype(F32)
    x2 = r1.astype(BF16)
    x3 = (r1 - x2.astype(F32)).astype(BF16)
    return x1, x2, x3


def _exact_left_mul(w, x):
    x1, x2, x3 = _split3(x)
    return _dot(w, x1) + _dot(w, x2) + _dot(w, x3)


def _exact_left_mul3(w3, x):
    return _dot(w3, jnp.concatenate(_split3(x), axis=0))


def _rows(ref, r0, n):
    if ref.shape[0] == 1:
        return ref[...]
    return ref[pl.ds(r0, n), :]


def _norm_mod(x, sc, sh):
    ms = jnp.mean(x * x, axis=-1, keepdims=True)
    return x * lax.rsqrt(ms + EPS) * (1.0 + sc) + sh


def _head_norm(o, w):
    return o * lax.rsqrt(jnp.mean(o * o, axis=-1, keepdims=True) + EPS) * w


def _lower_bound(lg):
    l0, l1 = lg[0:1, :], lg[1:2, :]
    m = jnp.maximum(l0, l1)
    e0, e1 = jnp.exp(l0 - m), jnp.exp(l1 - m)
    return e0 / (e0 + e1)


def _group_last(x, t, up):
    return jnp.where(t == 3, x, jnp.where(t == 2, up(x, 1), jnp.where(t == 1, up(x, 2), up(x, 3))))


def _mod_kernel(c_ref, w_ref, b_ref, o_ref, rep_s, *, n_rep):
    c = c_ref[...]
    s = (c * jax.nn.sigmoid(c)).astype(BF16)
    res = _dot(s, w_ref[...].astype(BF16)) + b_ref[...]
    for k in range(res.shape[1] // LANES):
        slab = res[:, k * LANES:(k + 1) * LANES]
        for t in range(GROUP):
            rep_s[k, pl.ds(t, n_rep, stride=GROUP), :] = slab[:n_rep]
        rep_s[k, n_rep * GROUP:, :] = slab[n_rep:]
    for k in range(res.shape[1] // LANES):
        o_ref[:, k * LANES:(k + 1) * LANES] = rep_s[k]


def _mod_call(c_all, n_rep, ada_w, ada_b):
    mp, d = c_all.shape
    n = ada_w.shape[1]
    tn = MOD_COLS
    mo = n_rep * GROUP + (mp - n_rep)
    return pl.pallas_call(
        functools.partial(_mod_kernel, n_rep=n_rep),
        grid=(n // tn,),
        in_specs=[pl.BlockSpec((mp, d), lambda j: (0, 0)),
                  pl.BlockSpec((d, tn), lambda j: (0, j)),
                  pl.BlockSpec((1, tn), lambda j: (0, j))],
        out_specs=pl.BlockSpec((mo, tn), lambda j: (0, j)),
        out_shape=jax.ShapeDtypeStruct((mo, n), F32),
        scratch_shapes=[pltpu.VMEM((tn // LANES, mo, LANES), F32)],
        compiler_params=_cparams("arbitrary"),
        name="mod",
    )(c_all, ada_w, ada_b)


def _inproj_kernel(*refs, tm, rc, emit_bf16):
    if emit_bf16:
        x_ref, sh_ref, sc_ref, w_ref, wg_ref, o_ref, og_ref, wo_ref, wgo_ref, h_ref = refs
    else:
        x_ref, sh_ref, sc_ref, w_ref, wg_ref, o_ref, og_ref, h_ref = refs
        wo_ref = wgo_ref = None
    j = pl.program_id(1)

    def weight(src, dst):
        w = src[...]
        if emit_bf16:
            w = w.astype(BF16)
            dst[...] = w
        return w

    @pl.when(j == 0)
    def _():
        wg = weight(wg_ref, wgo_ref)

        def body(r, carry):
            r0 = pl.multiple_of(r * rc, rc)
            h = _norm_mod(x_ref[pl.ds(r0, rc), :], _rows(sc_ref, r0, rc), _rows(sh_ref, r0, rc))
            hb = h.astype(BF16)
            h_ref[pl.ds(r0, rc), :] = hb
            gg = _dot_nt(hb, wg)
            og_ref[pl.ds(r0, rc), :] = jnp.concatenate(
                [gg, jnp.zeros((rc, og_ref.shape[1] - gg.shape[1]), F32)], axis=1)
            return carry
        lax.fori_loop(0, tm // rc, body, 0)

    o_ref[...] = _dot_nt(h_ref[...], weight(w_ref, wo_ref))


def _mod_spec(mod, tm, col, tiles_per_seq):
    d = mod.shape[-1] // 6
    if mod.ndim == 3:
        return pl.BlockSpec((None, 1, d), lambda i, *_: (i // tiles_per_seq, 0, col))
    return pl.BlockSpec((tm, d), lambda i, *_: (i, col))


def _inproj_call(x, mod, w, gate0, n_gate, tm, tn, tiles_per_seq):
    m, d = x.shape
    emit_bf16 = not isinstance(w, tuple)
    n = (w.shape[0] - n_gate) if emit_bf16 else w[0].shape[0]
    assert n_gate == 2 * SUBLANES and gate0 % tn == 0 and n % tn == 0
    na = gate0 // tn
    tile = pl.BlockSpec((tn, d), lambda i, j: (j, 0))
    seg_g = pl.BlockSpec((n_gate, d), lambda i, j: (0, 0))
    out_specs = [pl.BlockSpec((tm, tn), lambda i, j: (i, j)), pl.BlockSpec((tm, LANES), lambda i, j: (i, 0))]
    out_shape = [jax.ShapeDtypeStruct((m, n), F32), jax.ShapeDtypeStruct((m, LANES), F32)]
    if emit_bf16:
        assert m == tm
        w_specs = [pl.BlockSpec((pl.Element(tn), pl.Element(d)),
                                lambda i, j: (pl.multiple_of(j * tn + jnp.where(j < na, 0, n_gate), SUBLANES), 0)),
                   pl.BlockSpec((n_gate, d), lambda i, j: (gate0 // n_gate, 0))]
        w_args = [w, w]
        out_specs += [tile, seg_g]
        out_shape += [jax.ShapeDtypeStruct((n, d), BF16), jax.ShapeDtypeStruct((n_gate, d), BF16)]
    else:
        w_specs = [tile, seg_g]
        w_args = list(w)
    kern = functools.partial(_inproj_kernel, tm=tm, rc=ROW_CHUNK, emit_bf16=emit_bf16)
    return pl.pallas_call(
        kern,
        grid=(m // tm, n // tn),
        in_specs=[pl.BlockSpec((tm, d), lambda i, j: (i, 0)),
                  _mod_spec(mod, tm, 0, tiles_per_seq),
                  _mod_spec(mod, tm, 1, tiles_per_seq)] + w_specs,
        out_specs=out_specs,
        out_shape=out_shape,
        scratch_shapes=[pltpu.VMEM((tm, d), BF16)],
        compiler_params=_cparams("arbitrary", "arbitrary"),
        name="inproj",
    )(x, mod, mod, *w_args)


def _hgrn_constants():
    L = HG_CHUNK
    tri = np.tril(np.ones((L, L), np.float32))
    t = np.arange(L)
    blocks = []
    masks = []
    for h in HG_LEVELS:
        mid = (t // (2 * h)) * (2 * h) + h - 1
        blocks.append(tri - tri[mid])
        masks.append((t[:, None] // (2 * h) == t[None, :] // (2 * h)).astype(np.float32))
    blocks.append(tri[L - 1][None, :] - tri)
    masks.append(np.eye(L, dtype=np.float32))
    return jnp.asarray(np.concatenate(blocks, 0), BF16), jnp.asarray(np.stack(masks, 0), F32)


def _hgrn_gates(hq, z, lb, kdim):
    omlb = 1.0 - lb
    sz = jax.nn.sigmoid(z)
    logf = jnp.log(lb + omlb * sz)
    kk = omlb * (1.0 - sz)
    q = hq * jax.nn.sigmoid(hq) * (kdim ** -0.5)
    return q, kk, logf


def _hgrn_p_kernel(hq_ref, hf_ref, hi_ref, hg_ref, ga_ref, lg_ref, nw_ref, trib_ref, wall_ref, mask_ref,
                   ya_ref, so_ref, st_ref, *, n_chunks, kdim):
    L = HG_CHUNK
    c = pl.program_id(2)
    hps = hq_ref.shape[1] // kdim
    tb = trib_ref.shape[0]
    heads_chunks = [(hd, ci) for ci in range(n_chunks) for hd in range(hps)]
    rows = lambda ci: slice(ci * L, (ci + 1) * L)
    lanes = lambda hd: slice(hd * kdim, (hd + 1) * kdim)

    @pl.when(c == 0)
    def _():
        st_ref[...] = jnp.zeros_like(st_ref)

    lb = _lower_bound(lg_ref[...])
    nw = nw_ref[...]
    q, kk, logf = _hgrn_gates(hq_ref[...], hf_ref[...], lb, kdim)
    trib = trib_ref[...]
    b = jnp.concatenate([_exact_left_mul3(trib, logf[r:r + tb]) for r in range(0, n_chunks * L, tb)], axis=0)
    worst = -jnp.min(jnp.concatenate([b[r:r + 1, :] for r in range(L - 1, n_chunks * L, L)], axis=0))

    def finish(hd, ci, o):
        hg = hg_ref[rows(ci), lanes(hd)]
        ya = _head_norm(o, nw[:, lanes(hd)]) * (hg * jax.nn.sigmoid(hg))
        ya_ref[rows(ci), lanes(hd)] = jax.nn.sigmoid(ga_ref[rows(ci), lanes(hd)]) * ya

    @pl.when(worst <= HG_SAFE_EXP)
    def _():
        eb = jnp.exp(b)
        qt = (q * eb).astype(BF16)
        kn = kk * (1.0 / eb)
        knb = kn.astype(BF16)
        row = lax.broadcasted_iota(jnp.int32, (L, L), 0)
        colid = lax.broadcasted_iota(jnp.int32, (L, L), 1)
        causal = colid <= row
        o_intra, incr, dec = {}, {}, {}
        for hd, ci in heads_chunks:
            vb = hi_ref[rows(ci), lanes(hd)].astype(BF16)
            a = jnp.where(causal, _dot_nt(qt[rows(ci), lanes(hd)], knb[rows(ci), lanes(hd)]), 0.0)
            o_intra[hd, ci] = _dot(a.astype(BF16), vb)
            dec[hd, ci] = eb[(ci + 1) * L - 1:(ci + 1) * L, lanes(hd)]
            incr[hd, ci] = _dot_tn(vb, (kn[rows(ci), lanes(hd)] * dec[hd, ci]).astype(BF16))
        sts = {(hd, 0): st_ref[hd] for hd in range(hps)}
        for hd, ci in heads_chunks:
            sts[hd, ci + 1] = sts[hd, ci] * dec[hd, ci] + incr[hd, ci]
        for hd in range(hps):
            st_ref[hd] = sts[hd, n_chunks]
        for hd, ci in heads_chunks:
            finish(hd, ci, o_intra[hd, ci] + _dot_nt(qt[rows(ci), lanes(hd)], sts[hd, ci].astype(BF16)))

    @pl.when(jnp.logical_not(worst <= HG_SAFE_EXP))
    def _():
        wall = wall_ref[...]
        rowid = lax.broadcasted_iota(jnp.int32, (L, kdim), 0)
        n_lv = len(HG_LEVELS)
        for hd, ci in heads_chunks:
            qc, kc, bc = q[rows(ci), lanes(hd)], kk[rows(ci), lanes(hd)], b[rows(ci), lanes(hd)]
            vb = hi_ref[rows(ci), lanes(hd)].astype(BF16)
            d = _exact_left_mul(wall, logf[rows(ci), lanes(hd)])
            a = mask_ref[n_lv] * _dot_nt(qc.astype(BF16), kc.astype(BF16))
            for li, h in enumerate(HG_LEVELS):
                e = jnp.exp(-jnp.abs(d[li * L:(li + 1) * L]))
                second = (rowid & h) != 0
                p = jnp.where(second, qc, kc) * e
                qh = jnp.where(second, p, 0.0).astype(BF16)
                kh = jnp.where(second, 0.0, p).astype(BF16)
                a = a + mask_ref[li] * _dot_nt(qh, kh)
            qt = (qc * jnp.exp(bc)).astype(BF16)
            kt = (kc * jnp.exp(d[n_lv * L:(n_lv + 1) * L])).astype(BF16)
            st = st_ref[hd]
            finish(hd, ci, _dot(a.astype(BF16), vb) + _dot_nt(qt, st.astype(BF16)))
            st_ref[hd] = st * jnp.exp(bc[L - 1:L, :]) + _dot_tn(vb, kt)

    @pl.when(c == pl.num_programs(2) - 1)
    def _():
        for hd in range(hps):
            so_ref[hd] = st_ref[hd].T


def _hgrn_p_call(proj, lb_logits, norm_w, n_seq, seq_len, heads, kdim, col0, col_ga, rows_per_step, hps):
    wall, masks = _hgrn_constants()
    per_mat = min(rows_per_step, MXU_DEPTH) // HG_CHUNK
    trib = np.kron(np.eye(per_mat, dtype=np.float32), np.tril(np.ones((HG_CHUNK, HG_CHUNK), np.float32)))
    trib = jnp.asarray(np.concatenate([trib, trib, trib], axis=1), BF16)
    nc = seq_len // rows_per_step
    assert heads % hps == 0 and col0 % hps == 0 and col_ga % hps == 0
    kern = functools.partial(_hgrn_p_kernel, n_chunks=rows_per_step // HG_CHUNK, kdim=kdim)
    wide = hps * kdim

    def col(first):
        return pl.BlockSpec((rows_per_step, wide), lambda b, h, c: (b * nc + c, first // hps + h))

    return pl.pallas_call(
        kern,
        grid=(n_seq, heads // hps, nc),
        in_specs=[col(col0), col(col0 + heads), col(col0 + 2 * heads), col(col0 + 3 * heads), col(col_ga),
                  pl.BlockSpec((2, wide), lambda b, h, c: (0, h)),
                  pl.BlockSpec((1, wide), lambda b, h, c: (0, h)),
                  pl.BlockSpec(trib.shape, lambda b, h, c: (0, 0)),
                  pl.BlockSpec(wall.shape, lambda b, h, c: (0, 0)),
                  pl.BlockSpec(masks.shape, lambda b, h, c: (0, 0, 0))],
        out_specs=[pl.BlockSpec((rows_per_step, wide), lambda b, h, c: (b * nc + c, h)),
                   pl.BlockSpec((None, hps, kdim, kdim), lambda b, h, c: (b, h, 0, 0))],
        out_shape=[jax.ShapeDtypeStruct((n_seq * seq_len, heads * kdim), F32),
                   jax.ShapeDtypeStruct((n_seq, heads, kdim, kdim), F32)],
        scratch_shapes=[pltpu.VMEM((hps, kdim, kdim), F32)],
        compiler_params=_cparams("arbitrary", "arbitrary", "arbitrary"),
        name="hgrn_prompt",
    )(proj, proj, proj, proj, proj, lb_logits, norm_w, trib, wall, masks)


def _hgrn_s_kernel(hq_ref, hf_ref, hi_ref, hg_ref, ga_ref, lg_ref, nw_ref, s_ref, ya_ref, so_ref, *, n_pairs):
    kdim = hq_ref.shape[1]
    rows = n_pairs * SUBLANES
    lb = _lower_bound(lg_ref[...])
    rowid = lax.broadcasted_iota(jnp.int32, (rows, kdim), 0)
    t = rowid & (GROUP - 1)
    first = (rowid & GROUP) == 0
    first8 = lax.broadcasted_iota(jnp.int32, (SUBLANES, kdim), 0) < GROUP
    down = lambda y, j: pltpu.roll(y, j, 0)
    up = lambda y, j: pltpu.roll(y, rows - j, 0)

    q, kk, logf = _hgrn_gates(hq_ref[...], hf_ref[...], lb, kdim)
    v = hi_ref[...]
    b = logf
    for dlt in range(1, GROUP):
        b = b + jnp.where(t >= dlt, down(logf, dlt), 0.0)
    o = jnp.sum(q * kk, axis=1, keepdims=True) * v
    for dlt in range(1, GROUP):
        x = q * down(kk, dlt) * jnp.exp(b - down(b, dlt))
        a = jnp.sum(jnp.where(t >= dlt, x, 0.0), axis=1, keepdims=True)
        o = o + a * down(v, dlt)
    b_last = _group_last(b, t, up)
    qt = q * jnp.exp(b)
    kt = kk * jnp.exp(b_last - b)
    d1, d2, d3 = _split3(jnp.exp(b_last))
    swap = lambda y: jnp.where(first, up(y.astype(F32), GROUP), down(y.astype(F32), GROUP))
    dsplit = jnp.where(t == 0, swap(d1), jnp.where(t == 1, swap(d2), jnp.where(t == 2, swap(d3), 0.0)))
    ones = jnp.where(t <= 2, 1.0, 0.0)
    lhs = (jnp.where(first, kt, dsplit), jnp.where(first, dsplit, kt))
    rhs = (jnp.concatenate([jnp.where(first, v, 0.0), jnp.where(first, 0.0, ones)], axis=1),
           jnp.concatenate([jnp.where(first, 0.0, v), jnp.where(first, ones, 0.0)], axis=1))
    o_inter = []
    for p in range(n_pairs):
        sl = slice(p * SUBLANES, (p + 1) * SUBLANES)
        qb = qt[sl].astype(BF16)
        parts = []
        for half in range(2):
            s0 = s_ref[2 * p + half]
            upd = _dot_tn(lhs[half][sl].astype(BF16), rhs[half][sl].astype(BF16))
            so_ref[2 * p + half] = s0 * upd[:, kdim:] + upd[:, :kdim]
            parts.append(_dot(qb, s0.astype(BF16)))
        o_inter.append(jnp.where(first8, parts[0], parts[1]))
    o = o + jnp.concatenate(o_inter, axis=0)
    hg = hg_ref[...]
    ya_ref[...] = jax.nn.sigmoid(ga_ref[...]) * (_head_norm(o, nw_ref[...]) * (hg * jax.nn.sigmoid(hg)))


def _hgrn_s_call(proj, lb_logits, norm_w, state, heads, kdim, col0, col_ga, seqs_per_step):
    n_seq = state.shape[0]
    rows = seqs_per_step * GROUP
    kern = functools.partial(_hgrn_s_kernel, n_pairs=seqs_per_step // 2)

    def col(k):
        return pl.BlockSpec((rows, kdim), lambda i, h: (i, col0 + k * heads + h))

    st_spec = pl.BlockSpec((seqs_per_step, None, kdim, kdim), lambda i, h: (i, h, 0, 0))
    return pl.pallas_call(
        kern,
        grid=(n_seq // seqs_per_step, heads),
        in_specs=[col(0), col(1), col(2), col(3),
                  pl.BlockSpec((rows, kdim), lambda i, h: (i, col_ga + h)),
                  pl.BlockSpec((2, kdim), lambda i, h: (0, h)),
                  pl.BlockSpec((1, kdim), lambda i, h: (0, h)),
                  st_spec],
        out_specs=[pl.BlockSpec((rows, kdim), lambda i, h: (i, h)), st_spec],
        out_shape=[jax.ShapeDtypeStruct((n_seq * GROUP, heads * kdim), F32),
                   jax.ShapeDtypeStruct(state.shape, F32)],
        compiler_params=_cparams("arbitrary", "arbitrary"),
        name="hgrn_sample",
    )(proj, proj, proj, proj, proj, lb_logits, norm_w, state)


def _lane_pick(x, lane, idx):
    return jnp.broadcast_to(jnp.sum(jnp.where(lane == idx, x, 0.0), axis=1, keepdims=True), x.shape)


def _interleave(chains):
    live = list(chains)
    while live:
        still = []
        for g in live:
            try:
                next(g)
                still.append(g)
            except StopIteration:
                pass
        live = still


def _mlstm_p_kernel(q_ref, k_ref, v_ref, og_ref, gb_ref, ya_ref, g_ref, bias_ref, nw_ref, tri_ref,
                    mg_ref, co_ref, no_ref, mo_ref, c_s, n_s, m_s, *, heads, qk, vd):
    L = ML_CHUNK
    c = pl.program_id(1)

    @pl.when(c == 0)
    def _():
        c_s[...] = jnp.zeros_like(c_s)
        n_s[...] = jnp.zeros_like(n_s)
        m_s[...] = jnp.zeros_like(m_s)

    lane = lax.broadcasted_iota(jnp.int32, (L, LANES), 1)
    row = lax.broadcasted_iota(jnp.int32, (L, LANES), 0)

    def head(hd, r, g, b_all):
        qs = slice(hd * qk, (hd + 1) * qk)
        vs = slice(hd * vd, (hd + 1) * vd)
        ig = _lane_pick(g, lane, hd)
        b = _lane_pick(b_all, lane, heads + hd)
        m_prev = m_s[hd]
        gs = (ig - b).T
        yield
        dm = jnp.where(lane <= row, b + gs, -jnp.inf)
        mt = jnp.maximum(b + m_prev, jnp.max(dm, axis=1, keepdims=True))
        inter = jnp.exp(b + m_prev - mt)
        q = q_ref[r, qs] * (qk ** -0.5)
        k = k_ref[r, qs]
        vb = v_ref[r, vs].astype(BF16)
        qb = q.astype(BF16)
        sc = _dot_nt(qb, k.astype(BF16)) * jnp.exp(dm - mt)
        yield
        c0 = c_s[hd]
        n0 = n_s[hd]
        num = inter[:, 0:1] * _dot(qb, c0.astype(BF16)) + _dot(sc.astype(BF16), vb)
        den = inter[:, 0:1] * jnp.sum(q * n0, axis=1, keepdims=True) + jnp.sum(sc, axis=1, keepdims=True)
        yield
        hh = num / jnp.maximum(jnp.abs(den), jnp.exp(-mt[:, 0:1]))
        yb = _head_norm(hh, nw_ref[:, vs]) * jax.nn.sigmoid(og_ref[r, vs])
        mg_ref[r, vs] = (ya_ref[r, vs] + jax.nn.sigmoid(gb_ref[r, vs]) * yb).astype(mg_ref.dtype)
        yield
        m_last = mt[L - 1:L, :]
        b_last = b[L - 1:L, :]
        dec = jnp.exp(b_last + m_prev - m_last)
        kw = jnp.exp(b_last - b + ig - m_last) * k
        c_s[hd] = jnp.concatenate([dec, dec], axis=1) * c0 + _dot_tn(kw.astype(BF16), vb)
        n_s[hd] = dec * n0 + jnp.sum(kw, axis=0, keepdims=True)
        m_s[hd] = m_last

    for r0 in range(0, q_ref.shape[0], L):
        r = slice(r0, r0 + L)
        g = g_ref[r, :] + bias_ref[...]
        b_all = _exact_left_mul(tri_ref[...], jax.nn.log_sigmoid(g))
        for h0 in range(0, heads, ML_HEADS_INTERLEAVED):
            _interleave([head(hd, r, g, b_all) for hd in range(h0, min(h0 + ML_HEADS_INTERLEAVED, heads))])

    @pl.when(c == pl.num_programs(1) - 1)
    def _():
        co_ref[...] = c_s[...]
        no_ref[...] = n_s[...]
        mo_ref[...] = m_s[...]


def _mlstm_p_call(proj, ya, gates, bias, norm_w, n_seq, seq_len, heads, qk, vd, colq, colk, colv, colo, colg,
                  rows_per_step):
    assert qk == ML_CHUNK and qk == LANES and rows_per_step % ML_CHUNK == 0
    L = rows_per_step
    nc = seq_len // L
    qw, vw = heads * qk, heads * vd
    assert colq % qw == 0 and colk % qw == 0 and colv % vw == 0 and colo % vw == 0 and colg % vw == 0
    tri = jnp.asarray(np.tril(np.ones((ML_CHUNK, ML_CHUNK), np.float32)), BF16)
    kern = functools.partial(_mlstm_p_kernel, heads=heads, qk=qk, vd=vd)
    return pl.pallas_call(
        kern,
        grid=(n_seq, nc),
        in_specs=[pl.BlockSpec((L, qw), lambda b, c: (b * nc + c, colq // qw)),
                  pl.BlockSpec((L, qw), lambda b, c: (b * nc + c, colk // qw)),
                  pl.BlockSpec((L, vw), lambda b, c: (b * nc + c, colv // vw)),
                  pl.BlockSpec((L, vw), lambda b, c: (b * nc + c, colo // vw)),
                  pl.BlockSpec((L, vw), lambda b, c: (b * nc + c, colg // vw)),
                  pl.BlockSpec((L, vw), lambda b, c: (b * nc + c, 0)),
                  pl.BlockSpec((L, LANES), lambda b, c: (b * nc + c, 0)),
                  pl.BlockSpec((1, LANES), lambda b, c: (0, 0)),
                  pl.BlockSpec((1, vw), lambda b, c: (0, 0)),
                  pl.BlockSpec(tri.shape, lambda b, c: (0, 0))],
        out_specs=[pl.BlockSpec((L, vw), lambda b, c: (b * nc + c, 0)),
                   pl.BlockSpec((None, heads, qk, vd), lambda b, c: (b, 0, 0, 0)),
                   pl.BlockSpec((None, heads, 1, qk), lambda b, c: (b, 0, 0, 0)),
                   pl.BlockSpec((None, heads, 1, LANES), lambda b, c: (b, 0, 0, 0))],
        out_shape=[jax.ShapeDtypeStruct((n_seq * seq_len, vw), BF16),
                   jax.ShapeDtypeStruct((n_seq, heads, qk, vd), F32),
                   jax.ShapeDtypeStruct((n_seq, heads, 1, qk), F32),
                   jax.ShapeDtypeStruct((n_seq, heads, 1, LANES), F32)],
        scratch_shapes=[pltpu.VMEM((heads, qk, vd), F32), pltpu.VMEM((heads, 1, qk), F32),
                        pltpu.VMEM((heads, 1, LANES), F32)],
        compiler_params=_cparams("arbitrary", "arbitrary"),
        name="mlstm_prompt",
    )(proj, proj, proj, proj, proj, ya, gates, bias, norm_w, tri)


def _mlstm_s_kernel(q_ref, k_ref, v_ref, og_ref, gb_ref, ya_ref, g_ref, bias_ref, m_ref, nw_ref, c_ref, n_ref,
                    mg_ref, co_ref, no_ref, mo_ref, *, heads, n_pairs):
    h = pl.program_id(1)
    qk = q_ref.shape[1]
    rows = n_pairs * SUBLANES
    lane = lax.broadcasted_iota(jnp.int32, (rows, LANES), 1)
    rowid = lax.broadcasted_iota(jnp.int32, (rows, LANES), 0)
    t = rowid & (GROUP - 1)
    first = (rowid & GROUP) == 0
    first8 = lax.broadcasted_iota(jnp.int32, (SUBLANES, 1), 0) < GROUP
    down = lambda y, j: pltpu.roll(y, j, 0)
    up = lambda y, j: pltpu.roll(y, rows - j, 0)

    g = g_ref[...] + bias_ref[...]
    ig = _lane_pick(g, lane, h)
    lf = _lane_pick(jax.nn.log_sigmoid(g), lane, heads + h)
    m_prev = _lane_pick(m_ref[...], lane, h)
    b = lf
    for dlt in range(1, GROUP):
        b = b + jnp.where(t >= dlt, down(lf, dlt), 0.0)
    q = q_ref[...] * (qk ** -0.5)
    k = k_ref[...]
    v = v_ref[...]
    dms = [ig] + [jnp.where(t >= dlt, b - down(b, dlt) + down(ig, dlt), -jnp.inf) for dlt in range(1, GROUP)]
    mt = jnp.maximum(b + m_prev, functools.reduce(jnp.maximum, dms))
    inter = jnp.exp(b + m_prev - mt)[:, 0:1]
    num = jnp.zeros(v.shape, F32)
    den = jnp.zeros((rows, 1), F32)
    for dlt in range(GROUP):
        kd = k if dlt == 0 else down(k, dlt)
        vd_ = v if dlt == 0 else down(v, dlt)
        s = jnp.sum(q * kd, axis=1, keepdims=True) * jnp.exp(dms[dlt] - mt)[:, 0:1]
        num = num + s * vd_
        den = den + s
    m_last = _group_last(mt, t, up)
    b_last = _group_last(b, t, up)
    dec = jnp.exp(b_last + m_prev - m_last)
    kw = jnp.exp(b_last - b + ig - m_last) * k
    n0 = n_ref[...]
    no_ref[...] = dec * n0 + kw + down(kw, 1) + down(kw, 2) + down(kw, 3)
    mo_ref[...] = m_last
    den = den + inter * jnp.sum(q * n0, axis=1, keepdims=True)

    kw_half = (jnp.where(first, kw, 0.0), jnp.where(first, 0.0, kw))
    dec2 = jnp.concatenate([dec, dec], axis=1)
    num_inter = []
    for p in range(n_pairs):
        sl = slice(p * SUBLANES, (p + 1) * SUBLANES)
        qb = q[sl].astype(BF16)
        vb = v[sl].astype(BF16)
        parts = []
        for half in range(2):
            c0 = c_ref[2 * p + half]
            r = p * SUBLANES + half * GROUP
            co_ref[2 * p + half] = dec2[r:r + 1, :] * c0 + _dot_tn(kw_half[half][sl].astype(BF16), vb)
            parts.append(_dot(qb, c0.astype(BF16)))
        num_inter.append(jnp.where(first8, parts[0], parts[1]))
    num = num + inter * jnp.concatenate(num_inter, axis=0)
    hh = num / jnp.maximum(jnp.abs(den), jnp.exp(-mt)[:, 0:1])
    yb = _head_norm(hh, nw_ref[...]) * jax.nn.sigmoid(og_ref[...])
    mg_ref[...] = (ya_ref[...] + jax.nn.sigmoid(gb_ref[...]) * yb).astype(mg_ref.dtype)


def _mlstm_s_call(proj, ya, gates, bias, m_rows, norm_w, c_state, n_state, heads, qk, vd,
                  colq, colk, colv, colo, colg, seqs_per_step):
    n_seq = c_state.shape[0]
    rows = seqs_per_step * GROUP
    kern = functools.partial(_mlstm_s_kernel, heads=heads, n_pairs=seqs_per_step // 2)
    c_spec = pl.BlockSpec((seqs_per_step, None, qk, vd), lambda i, h: (i, h, 0, 0))
    n_spec = pl.BlockSpec((None, rows, qk), lambda i, h: (h, i, 0))
    return pl.pallas_call(
        kern,
        grid=(n_seq // seqs_per_step, heads),
        in_specs=[pl.BlockSpec((rows, qk), lambda i, h: (i, colq // qk + h)),
                  pl.BlockSpec((rows, qk), lambda i, h: (i, colk // qk + h)),
                  pl.BlockSpec((rows, vd), lambda i, h: (i, colv // vd + h)),
                  pl.BlockSpec((rows, vd), lambda i, h: (i, colo // vd + h)),
                  pl.BlockSpec((rows, vd), lambda i, h: (i, colg // vd + h)),
                  pl.BlockSpec((rows, vd), lambda i, h: (i, h)),
                  pl.BlockSpec((rows, LANES), lambda i, h: (i, 0)),
                  pl.BlockSpec((1, LANES), lambda i, h: (0, 0)),
                  pl.BlockSpec((rows, LANES), lambda i, h: (i, 0)),
                  pl.BlockSpec((1, vd), lambda i, h: (0, h)),
                  c_spec, n_spec],
        out_specs=[pl.BlockSpec((rows, vd), lambda i, h: (i, h)),
                   c_spec, n_spec,
                   pl.BlockSpec((None, rows, LANES), lambda i, h: (h, i, 0))],
        out_shape=[jax.ShapeDtypeStruct((n_seq * GROUP, heads * vd), BF16),
                   jax.ShapeDtypeStruct(c_state.shape, F32),
                   jax.ShapeDtypeStruct(n_state.shape, F32),
                   jax.ShapeDtypeStruct((heads, n_seq * GROUP, LANES), F32)],
        compiler_params=_cparams("arbitrary", "arbitrary"),
        name="mlstm_sample",
    )(proj, proj, proj, proj, proj, ya, gates, bias, m_rows, norm_w, c_state, n_state)


def _outproj_kernel(mg_ref, x_ref, g1_ref, w_ref, o_ref):
    o_ref[...] = x_ref[...] + g1_ref[...] * _dot(mg_ref[...], w_ref[...])


def _outproj_call(merged, x, mod, w_out, tm, tiles_per_seq):
    m, d = x.shape
    blk = pl.BlockSpec((tm, d), lambda i: (i, 0))
    return pl.pallas_call(
        _outproj_kernel,
        grid=(m // tm,),
        in_specs=[blk, blk, _mod_spec(mod, tm, 2, tiles_per_seq),
                  pl.BlockSpec((d, d), lambda i: (0, 0), pipeline_mode=pl.Buffered(1))],
        out_specs=blk,
        out_shape=jax.ShapeDtypeStruct((m, d), F32),
        compiler_params=_cparams("arbitrary"),
        name="outproj",
    )(merged, x, mod, w_out)


def _ffn_kernel(*refs, tm, rc, tiles_per_seq, grouped, emit_bf16):
    refs = iter(refs)
    x_ref, sh_ref, sc_ref, g2_ref, wa_ref, wg_ref, wd_ref, cw_ref, cb_ref, fw_ref = (next(refs) for _ in range(10))
    hist_refs = (next(refs), next(refs)) if grouped else None
    y_ref = next(refs)
    if grouped:
        cv2_ref, cv3_ref = next(refs), next(refs)
    else:
        a_ref = next(refs)
    wao_ref, wgo_ref, wdo_ref = (next(refs), next(refs), next(refs)) if emit_bf16 else (None, None, None)
    h_s, acc_s = next(refs), next(refs)
    a_s = carry_s = next(refs)
    i = pl.program_id(0)
    j = pl.program_id(1)

    def weight(src, dst):
        w = src[...]
        if emit_bf16:
            w = w.astype(BF16)
            dst[...] = w
        return w

    @pl.when(j == 0)
    def _():
        def body(r, carry):
            r0 = pl.multiple_of(r * rc, rc)
            h = _norm_mod(x_ref[pl.ds(r0, rc), :], _rows(sc_ref, r0, rc), _rows(sh_ref, r0, rc))
            h_s[pl.ds(r0, rc), :] = h.astype(BF16)
            return carry
        lax.fori_loop(0, tm // rc, body, 0)
        acc_s[...] = jnp.zeros_like(acc_s)

    hb = h_s[...]
    a = _dot(hb, weight(wa_ref, wao_ref))
    g = _dot(hb, weight(wg_ref, wgo_ref))
    rowid = lax.broadcasted_iota(jnp.int32, a.shape, 0)
    r1 = pltpu.roll(a, 1, 0)
    r2 = pltpu.roll(a, 2, 0)
    if grouped:
        t = rowid & (GROUP - 1)
        slabs = []
        for s in range(a.shape[1] // LANES):
            a_s[s] = jnp.zeros((tm, LANES), F32)
            for u, hist in enumerate(hist_refs):
                a_s[s, pl.ds(u, tm // GROUP, stride=GROUP), :] = hist[:, s * LANES:(s + 1) * LANES]
            slabs.append(a_s[s])
        p2 = jnp.concatenate(slabs, axis=1)
        prev1 = jnp.where(t == 0, pltpu.roll(p2, tm - 1, 0), r1)
        prev2 = jnp.where(t <= 1, p2, r2)
        for s in range(a.shape[1] // LANES):
            a_s[s] = a[:, s * LANES:(s + 1) * LANES]
        for s in range(a.shape[1] // LANES):
            cv2_ref[:, s * LANES:(s + 1) * LANES] = a_s[s, pl.ds(GROUP - 2, tm // GROUP, stride=GROUP), :]
            cv3_ref[:, s * LANES:(s + 1) * LANES] = a_s[s, pl.ds(GROUP - 1, tm // GROUP, stride=GROUP), :]
    else:
        tail = a[tm - SUBLANES:tm, :]
        car = jnp.where(i % tiles_per_seq == 0, 0.0, carry_s[j])
        c1 = car[SUBLANES - 1:SUBLANES, :]
        c2 = car[SUBLANES - 2:SUBLANES - 1, :]
        prev1 = jnp.where(rowid == 0, c1, r1)
        prev2 = jnp.where(rowid == 0, c2, jnp.where(rowid == 1, c1, r2))
        carry_s[j] = tail
        a_ref[...] = tail
    cw = cw_ref[...]
    ac = cw[0:1, :] * prev2 + cw[1:2, :] * prev1 + cw[2:3, :] * a + cb_ref[...]
    yv = 0.5 * ac * (1.0 + lax.erf(ac * (2.0 ** -0.5))) * g
    acc_s[...] += _dot(yv.astype(BF16), weight(wd_ref, wdo_ref))

    @pl.when(j == pl.num_programs(1) - 1)
    def _():
        def body(r, carry):
            r0 = pl.multiple_of(r * rc, rc)
            x2 = x_ref[pl.ds(r0, rc), :] + _rows(g2_ref, r0, rc) * acc_s[pl.ds(r0, rc), :]
            ms = jnp.mean(x2 * x2, axis=-1, keepdims=True)
            y_ref[pl.ds(r0, rc), :] = x2 * lax.rsqrt(ms + EPS) * fw_ref[...]
            return carry
        lax.fori_loop(0, tm // rc, body, 0)


def _ffn_call(x, mod, w_up, w_down, conv_w, conv_b, final_w, tm, tf, tiles_per_seq, hist=None):
    m, d = x.shape
    f = w_down.shape[0]
    nj = f // tf
    grouped = hist is not None
    emit_bf16 = not isinstance(w_up, tuple)
    kern = functools.partial(_ffn_kernel, tm=tm, rc=min(tm, ROW_CHUNK), tiles_per_seq=tiles_per_seq, grouped=grouped,
                             emit_bf16=emit_bf16)
    half_spec = pl.BlockSpec((d, tf), lambda i, j: (0, j))
    down_spec = pl.BlockSpec((tf, d), lambda i, j: (j, 0))
    if emit_bf16:
        assert m == tm
        up_specs = [half_spec, pl.BlockSpec((d, tf), lambda i, j: (0, nj + j))]
        up_args = [w_up, w_up]
    else:
        up_specs = [half_spec, half_spec]
        up_args = list(w_up)
    in_specs = [pl.BlockSpec((tm, d), lambda i, j: (i, 0)),
                _mod_spec(mod, tm, 3, tiles_per_seq),
                _mod_spec(mod, tm, 4, tiles_per_seq),
                _mod_spec(mod, tm, 5, tiles_per_seq)] + up_specs + [
                down_spec,
                pl.BlockSpec((CONV_W, tf), lambda i, j: (0, j)),
                pl.BlockSpec((1, tf), lambda i, j: (0, j)),
                pl.BlockSpec((1, d), lambda i, j: (0, 0))]
    args = [x, mod, mod, mod] + up_args + [w_down, conv_w, conv_b, final_w]
    scratch = [pltpu.VMEM((tm, d), BF16), pltpu.VMEM((tm, d), F32)]
    if grouped:
        in_specs += [pl.BlockSpec((tm // GROUP, tf), lambda i, j: (i, j))] * 2
        args += list(hist)
        scratch.append(pltpu.VMEM((tf // LANES, tm, LANES), F32))
        a_specs = [pl.BlockSpec((tm // GROUP, tf), lambda i, j: (i, j))] * 2
        a_shapes = [jax.ShapeDtypeStruct((m // GROUP, f), F32)] * 2
    else:
        scratch.append(pltpu.VMEM((nj, SUBLANES, tf), F32))
        a_specs = [pl.BlockSpec((SUBLANES, tf), lambda i, j: (i, j))]
        a_shapes = [jax.ShapeDtypeStruct((m // tm * SUBLANES, f), F32)]
    if emit_bf16:
        a_specs += [half_spec, half_spec, down_spec]
        a_shapes += [jax.ShapeDtypeStruct((d, f), BF16), jax.ShapeDtypeStruct((d, f), BF16),
                     jax.ShapeDtypeStruct((f, d), BF16)]
    return pl.pallas_call(
        kern,
        grid=(m // tm, nj),
        in_specs=in_specs,
        out_specs=[pl.BlockSpec((tm, d), lambda i, j: (i, 0))] + a_specs,
        out_shape=[jax.ShapeDtypeStruct((m, d), F32)] + a_shapes,
        scratch_shapes=scratch,
        compiler_params=_cparams("arbitrary", "arbitrary"),
        name="ffn",
    )(*args)


def kernel(x_prompt, x_sample, c_prompt, c_sample, state_hgrn, state_mlstm_C, state_mlstm_n, state_mlstm_m,
           state_conv, ada_w, ada_b, w_in, hg_lb_logits, hg_norm_w, ml_i_bias, ml_f_bias, ml_norm_w, w_out,
           conv_w, conv_b, w_up, w_down, final_norm_w):
    n_p, seq, d = x_prompt.shape
    n_s, dec_seq, _ = x_sample.shape
    depth, _, hg_heads, hg_k, hg_v = state_hgrn.shape
    _, _, ml_heads, ml_qk, ml_v = state_mlstm_C.shape
    f = w_down.shape[1]
    assert depth == 1 and dec_seq == GROUP and hg_k == hg_v == LANES and 2 * ml_heads <= LANES
    assert hg_lb_logits.shape[0] == 2

    hg_w = hg_heads * hg_k
    gate0 = 4 * hg_w + 2 * ml_heads * ml_qk + ml_heads * ml_v
    gate1 = gate0 + 2 * ml_heads
    wt = jnp.swapaxes(w_in[0], 0, 1)
    col_mq = 4 * hg_w
    col_mk = col_mq + ml_heads * ml_qk
    col_mv = col_mk + ml_heads * ml_qk
    col_mo = col_mv + ml_heads * ml_v
    col_ga = col_mo + ml_heads * ml_v
    col_gb = col_ga + d
    assert col_gb + d == wt.shape[0] - (gate1 - gate0)
    gate_bias = jnp.pad(jnp.concatenate([ml_i_bias[0], ml_f_bias[0]]), (0, LANES - 2 * ml_heads))[None, :]

    m_s_rows = n_s * GROUP
    c_all = jnp.concatenate([c_sample, c_prompt], axis=0)
    c_all = jnp.pad(c_all, ((0, (-c_all.shape[0]) % SUBLANES), (0, 0)))
    mod_s = _mod_call(c_all, n_s, ada_w[0], ada_b)
    mod_p = mod_s[m_s_rows:m_s_rows + n_p][:, None, :]

    w_out_b = w_out[0].astype(BF16)
    hg_nw = hg_norm_w
    ml_nw = ml_norm_w
    fw = final_norm_w[None, :]

    xs = x_sample.reshape(n_s * GROUP, d)
    proj_s, gates_s, *wt_b = _inproj_call(xs, mod_s, wt, gate0, gate1 - gate0, m_s_rows, INPROJ_COLS_SAMPLE, 1)
    ya_s, hg_s = _hgrn_s_call(proj_s, hg_lb_logits, hg_nw, state_hgrn[0], hg_heads, hg_k, 0, col_ga // hg_k,
                              HG_SAMPLE_SEQS_PER_STEP)
    m_rows = jnp.pad(jnp.repeat(state_mlstm_m[0], GROUP, axis=0), ((0, 0), (0, LANES - ml_heads)))
    n_hb = jnp.repeat(jnp.transpose(state_mlstm_n[0], (1, 0, 2)), GROUP, axis=1)
    mg_s, c_s, nn_s, m_s = _mlstm_s_call(proj_s, ya_s, gates_s, gate_bias, m_rows, ml_nw, state_mlstm_C[0], n_hb,
                                         ml_heads, ml_qk, ml_v, col_mq, col_mk, col_mv, col_mo, col_gb,
                                         ML_SAMPLE_SEQS_PER_STEP)
    x1_s = _outproj_call(mg_s, xs, mod_s, w_out_b, min(OUTPROJ_ROWS, m_s_rows), 1)
    hist = (state_conv[0][:, 0], state_conv[0][:, 1])
    y_s, cv2, cv3, wa_b, wg_b, wd_b = _ffn_call(x1_s, mod_s, w_up[0], w_down[0], conv_w[0], conv_b, fw,
                                                m_s_rows, FFN_COLS_SAMPLE, 1, hist)
    cv_s = jnp.stack([cv2, cv3], axis=1)

    xp = x_prompt.reshape(n_p * seq, d)
    proj_p, gates_p = _inproj_call(xp, mod_p, tuple(wt_b), gate0, gate1 - gate0, INPROJ_ROWS_PROMPT,
                                   INPROJ_COLS_PROMPT, seq // INPROJ_ROWS_PROMPT)
    ya_p, hg_p = _hgrn_p_call(proj_p, hg_lb_logits, hg_nw, n_p, seq, hg_heads, hg_k, 0, col_ga // hg_k,
                              HG_ROWS_PER_STEP, HG_HEADS_PER_STEP)
    mg_p, c_p, nn_p, m_p = _mlstm_p_call(proj_p, ya_p, gates_p, gate_bias, ml_nw, n_p, seq, ml_heads, ml_qk, ml_v,
                                         col_mq, col_mk, col_mv, col_mo, col_gb, ML_ROWS_PER_STEP)
    x1_p = _outproj_call(mg_p, xp, mod_p, w_out_b, OUTPROJ_ROWS, seq // OUTPROJ_ROWS)
    tiles = seq // FFN_ROWS_PROMPT
    y_p, atail = _ffn_call(x1_p, mod_p, (wa_b, wg_b), wd_b, conv_w[0], conv_b, fw, FFN_ROWS_PROMPT,
                           FFN_COLS_PROMPT, tiles)
    cv_p = atail.reshape(n_p, tiles, SUBLANES, f)[:, tiles - 1, SUBLANES - (CONV_W - 1):]

    return (y_p.reshape(n_p, seq, d), y_s.reshape(n_s, GROUP, d),
            hg_p[None], hg_s[None],
            c_p[None], c_s[None],
            nn_p.reshape(1, n_p, ml_heads, ml_qk), jnp.transpose(nn_s[:, GROUP - 1::GROUP], (1, 0, 2))[None],
            m_p[:, :, 0, 0][None], jnp.transpose(m_s[:, ::GROUP, 0])[None],
            cv_p[None], cv_s[None])
```
